```python
import math
import jax, jax.numpy as jnp
from jax import lax
import numpy as np

D_MODEL = 1024
BATCH = 32
SEQ = 2048
DEPTH = 1

CHUNK = 64
D_CONV = 1024
CONV_WIDTH = 3
HEAD_DIM = 64
N_HEADS = 16
D_ATT = N_HEADS * HEAD_DIM
Q_BLOCK = 128
N_GROUPS = 4
EXPERTS_PER_GROUP = 8
N_EXPERTS = N_GROUPS * EXPERTS_PER_GROUP
TOP_K = 2
D_EXPERT = 512
MOE_BLOCK = 256
EPS = 1e-6

SPLITS = (D_CONV, D_CONV, D_CONV, D_ATT, D_ATT, D_ATT, N_HEADS, D_MODEL, D_MODEL)
D_IN_PROJ = sum(SPLITS)

kernel_name = "hybrid_conv_fox_hiermoe_block"


def rmsnorm(x, g):
    xf = x.astype(jnp.float32)
    y = xf * lax.rsqrt(jnp.mean(xf * xf, axis=-1, keepdims=True) + EPS)
    return (y * g.astype(jnp.float32)).astype(x.dtype)


def split_cols(p):
    idx = np.cumsum(np.array(SPLITS))[:-1].tolist()
    return jnp.split(p, idx, axis=-1)


def short_conv(b, c, u, conv_w, conv_b):
    S = u.shape[1]
    z = c * u
    zp = jnp.pad(z, ((0, 0), (CONV_WIDTH - 1, 0), (0, 0)))
    acc = conv_b
    for i in range(CONV_WIDTH):
        acc = acc + conv_w[i] * zp[:, i:i + S]
    return b * acc


def forgetting_attention(q, k, v, f_logit, b_forget):
    B, S, H, dh = q.shape
    log_f = jax.nn.log_sigmoid(f_logit.astype(jnp.float32) + b_forget.astype(jnp.float32))
    cum = jnp.transpose(jnp.cumsum(log_f, axis=1), (0, 2, 1))
    scale = 1.0 / math.sqrt(dh)
    outs = []
    for qb in range(S // Q_BLOCK):
        qs, qe = qb * Q_BLOCK, (qb + 1) * Q_BLOCK
        qblk = q[:, qs:qe]
        kp, vp = k[:, :qe], v[:, :qe]
        s = jnp.einsum('bqhd,bkhd->bhqk', qblk, kp).astype(jnp.float32) * scale
        s = s + (cum[:, :, qs:qe, None] - cum[:, :, None, :qe])
        t_pos = jnp.arange(qs, qe)[:, None]
        s_pos = jnp.arange(qe)[None, :]
        s = jnp.where(s_pos <= t_pos, s, -jnp.inf)
        p = jax.nn.softmax(s, axis=-1).astype(v.dtype)
        outs.append(jnp.einsum('bhqk,bkhd->bqhd', p, vp))
    return jnp.concatenate(outs, axis=1)


def token_mixer(h, w_in, conv_w, conv_b, b_forget, w_conv_out, w_att_out, w_out):
    B, S, _ = h.shape
    proj = h @ w_in
    cb, cc, cu, q, k, v, f_logit, g_conv, g_att = split_cols(proj)
    y_conv = short_conv(cb, cc, cu, conv_w, conv_b) @ w_conv_out
    o = forgetting_attention(q.reshape(B, S, N_HEADS, HEAD_DIM),
                             k.reshape(B, S, N_HEADS, HEAD_DIM),
                             v.reshape(B, S, N_HEADS, HEAD_DIM),
                             f_logit, b_forget)
    y_att = o.reshape(B, S, D_ATT) @ w_att_out
    m = jax.nn.sigmoid(g_conv) * y_conv + jax.nn.sigmoid(g_att) * y_att
    return m @ w_out


def hier_moe(h, w_router_group, w_router_expert, w_e_gate, w_e_up, w_e_down):
    B, S, D = h.shape
    T = B * S
    hf = h.reshape(T, D)
    pg = jax.nn.softmax((hf @ w_router_group).astype(jnp.float32), axis=-1)
    g_val, g_idx = lax.top_k(pg, 1)
    le = jnp.einsum('td,gde->tge', hf, w_router_expert).astype(jnp.float32)
    le_sel = jnp.take_along_axis(le, g_idx[:, :, None], axis=1)[:, 0]
    pe = jax.nn.softmax(le_sel, axis=-1)
    e_val, e_idx = lax.top_k(pe, TOP_K)
    e_val = e_val / jnp.sum(e_val, axis=-1, keepdims=True)
    w_tok = (g_val * e_val).astype(h.dtype)
    expert = (g_idx * EXPERTS_PER_GROUP + e_idx).astype(jnp.int32)

    A = T * TOP_K
    flat_e = expert.reshape(A)
    flat_w = w_tok.reshape(A)
    flat_tok = jnp.arange(A, dtype=jnp.int32) // TOP_K
    order = jnp.argsort(flat_e)
    sorted_e = flat_e[order]
    counts = jnp.bincount(flat_e, length=N_EXPERTS)
    padded = (counts + MOE_BLOCK - 1) // MOE_BLOCK * MOE_BLOCK
    start_sorted = jnp.cumsum(counts) - counts
    end_padded = jnp.cumsum(padded)
    start_padded = end_padded - padded
    rank = jnp.arange(A, dtype=jnp.int32) - start_sorted[sorted_e]
    dest = start_padded[sorted_e] + rank
    n_rows = A + N_EXPERTS * MOE_BLOCK
    n_blocks = n_rows // MOE_BLOCK
    row_tok = jnp.zeros((n_rows,), jnp.int32).at[dest].set(flat_tok[order])
    row_w = jnp.zeros((n_rows,), h.dtype).at[dest].set(flat_w[order])
    block_e = jnp.minimum(
        jnp.searchsorted(end_padded, jnp.arange(n_blocks) * MOE_BLOCK, side='right'),
        N_EXPERTS - 1).astype(jnp.int32)
    x_rows = hf[row_tok].reshape(n_blocks, MOE_BLOCK, D)

    def expert_block(args):
        xb, e = args
        a = xb @ w_e_gate[e]
        u = xb @ w_e_up[e]
        return (jax.nn.silu(a) * u) @ w_e_down[e]

    y_rows = lax.map(expert_block, (x_rows, block_e)).reshape(n_rows, D)
    y = jax.ops.segment_sum(y_rows * row_w[:, None], row_tok, num_segments=T)
    return y.reshape(B, S, D)


def setup_inputs(seed: int = 0) -> dict:
    key = jax.random.key(seed)
    ks = jax.random.split(key, 20)
    f32 = jnp.float32
    nrm = lambda k, shape, fan: jax.random.normal(k, shape, f32) * (fan ** -0.5)
    return {
        "x": jax.random.normal(ks[0], (BATCH, SEQ, D_MODEL), f32),
        "norm_mix_g": 1.0 + 0.02 * jax.random.normal(ks[1], (D_MODEL,), f32),
        "w_in": nrm(ks[2], (D_MODEL, D_IN_PROJ), D_MODEL),
        "conv_w": nrm(ks[3], (CONV_WIDTH, D_CONV), CONV_WIDTH),
        "conv_b": 0.01 * jax.random.normal(ks[4], (D_CONV,), f32),
        "b_forget": jax.random.uniform(ks[5], (N_HEADS,), f32, 1.0, 6.0),
        "w_conv_out": nrm(ks[6], (D_CONV, D_MODEL), D_CONV),
        "w_att_out": nrm(ks[7], (D_ATT, D_MODEL), D_ATT),
        "w_out": nrm(ks[8], (D_MODEL, D_MODEL), D_MODEL),
        "norm_ffn_g": 1.0 + 0.02 * jax.random.normal(ks[9], (D_MODEL,), f32),
        "w_router_group": nrm(ks[10], (D_MODEL, N_GROUPS), D_MODEL),
        "w_router_expert": nrm(ks[11], (N_GROUPS, D_MODEL, EXPERTS_PER_GROUP), D_MODEL),
        "w_e_gate": nrm(ks[12], (N_EXPERTS, D_MODEL, D_EXPERT), D_MODEL),
        "w_e_up": nrm(ks[13], (N_EXPERTS, D_MODEL, D_EXPERT), D_MODEL),
        "w_e_down": nrm(ks[14], (N_EXPERTS, D_EXPERT, D_MODEL), D_EXPERT),
        "norm_final_g": 1.0 + 0.02 * jax.random.normal(ks[15], (D_MODEL,), f32),
    }


def reference(x, norm_mix_g, w_in, conv_w, conv_b, b_forget, w_conv_out, w_att_out, w_out,
              norm_ffn_g, w_router_group, w_router_expert, w_e_gate, w_e_up, w_e_down,
              norm_final_g):
    for _ in range(DEPTH):
        x = x + token_mixer(rmsnorm(x, norm_mix_g), w_in, conv_w, conv_b, b_forget,
                            w_conv_out, w_att_out, w_out)
        x = x + hier_moe(rmsnorm(x, norm_ffn_g), w_router_group, w_router_expert,
                         w_e_gate, w_e_up, w_e_down)
    return rmsnorm(x, norm_final_g)
```

```python
import functools

import jax
import jax.numpy as jnp
from jax import lax
from jax.experimental import pallas as pl
from jax.experimental.pallas import tpu as pltpu

D_MODEL = 1024
HEAD_DIM = 64
N_HEADS = 16
N_GROUPS = 4
EXPERTS_PER_GROUP = 8
N_EXPERTS = N_GROUPS * EXPERTS_PER_GROUP
D_EXPERT = 512
MOE_BLOCK = 256
CONV_WIDTH = 3
EPS = 1e-6

LANES = 128
HEADS_PER_BLOCK = LANES // HEAD_DIM
NEG_BIG = -1e30

IN_PROJ_COLS = 256
IN_PROJ_ROWS = 512
ATT_TQ = 256
ATT_TK = 256
MIX_ROWS = 512
ROW_TILE = 256
VMEM_LIMIT = 56 * 1024 * 1024

F32 = jnp.float32
BF16 = jnp.bfloat16


def _dot(a, b):
    return jnp.dot(a, b, preferred_element_type=F32)


def _in_proj_kernel(x_ref, g_ref, w_ref, wf_ref, bf_ref, cw_ref, cb_ref,
                    yc_ref, q_ref, k_ref, v_ref, sgc_ref, sga_ref, cum_ref, h_scr):
    j = pl.program_id(1)
    seq = x_ref.shape[0]
    n_chunks = seq // IN_PROJ_ROWS

    @pl.when(j == 0)
    def _():
        for r in range(n_chunks):
            rows = slice(r * IN_PROJ_ROWS, (r + 1) * IN_PROJ_ROWS)
            xs = x_ref[rows, :]
            ms = jnp.mean(xs * xs, axis=-1, keepdims=True)
            h_scr[rows, :] = (xs * lax.rsqrt(ms + EPS) * g_ref[...]).astype(BF16)
        f = _dot(h_scr[...], wf_ref[...]) + bf_ref[...]
        c = jnp.minimum(f, 0.0) - jnp.log(1.0 + jnp.exp(-jnp.abs(f)))
        row = lax.broadcasted_iota(jnp.int32, c.shape, 0)
        d = 1
        while d < seq:
            c = c + jnp.where(row >= d, pltpu.roll(c, d, axis=0), 0.0)
            d *= 2
        cum_ref[0] = c

    cw0 = cw_ref[0:1, :]
    cw1 = cw_ref[1:2, :]
    cw2 = cw_ref[2:3, :]
    cb = cb_ref[...]
    rowc = lax.broadcasted_iota(jnp.int32, (IN_PROJ_ROWS, IN_PROJ_COLS), 0)
    zprev = None
    for r in range(n_chunks):
        rows = slice(r * IN_PROJ_ROWS, (r + 1) * IN_PROJ_ROWS)
        hs = h_scr[rows, :]
        cb_gate = _dot(hs, w_ref[0])
        z = _dot(hs, w_ref[1]) * _dot(hs, w_ref[2])
        z1 = pltpu.roll(z, 1, axis=0)
        z2 = pltpu.roll(z, 2, axis=0)
        if zprev is None:
            p1 = jnp.zeros_like(z)
            p2 = p1
        else:
            p1 = pltpu.roll(zprev, 1, axis=0)
            p2 = pltpu.roll(zprev, 2, axis=0)
        z1 = jnp.where(rowc < 1, p1, z1)
        z2 = jnp.where(rowc < 2, p2, z2)
        acc = cb + cw0 * z2 + cw1 * z1 + cw2 * z
        yc_ref[rows, :] = (cb_gate * acc).astype(BF16)
        zprev = z
        q_ref[rows, :] = (_dot(hs, w_ref[3]) * (HEAD_DIM ** -0.5)).astype(BF16)
        k_ref[rows, :] = _dot(hs, w_ref[4]).astype(BF16)
        v_ref[rows, :] = _dot(hs, w_ref[5]).astype(BF16)
        sgc_ref[rows, :] = jax.nn.sigmoid(_dot(hs, w_ref[6])).astype(BF16)
        sga_ref[rows, :] = jax.nn.sigmoid(_dot(hs, w_ref[7])).astype(BF16)


def _in_proj(x2d, g, w_stack, wf, bfv, conv_w, conv_b, batch, seq):
    tokens = batch * seq
    tn = IN_PROJ_COLS
    nj = D_MODEL // tn
    col_out = pl.BlockSpec((seq, tn), lambda b, j: (b, j))
    out_bf16 = jax.ShapeDtypeStruct((tokens, D_MODEL), BF16)
    return pl.pallas_call(
        _in_proj_kernel,
        grid=(batch, nj),
        in_specs=[
            pl.BlockSpec((seq, D_MODEL), lambda b, j: (b, 0)),
            pl.BlockSpec((1, D_MODEL), lambda b, j: (0, 0)),
            pl.BlockSpec((8, D_MODEL, tn), lambda b, j: (0, 0, j)),
            pl.BlockSpec((D_MODEL, LANES), lambda b, j: (0, 0)),
            pl.BlockSpec((1, LANES), lambda b, j: (0, 0)),
            pl.BlockSpec((CONV_WIDTH, tn), lambda b, j: (0, j)),
            pl.BlockSpec((1, tn), lambda b, j: (0, j)),
        ],
        out_specs=[col_out] * 6 + [pl.BlockSpec((1, seq, LANES), lambda b, j: (b, 0, 0))],
        out_shape=[out_bf16] * 6 + [jax.ShapeDtypeStruct((batch, seq, LANES), F32)],
        scratch_shapes=[pltpu.VMEM((seq, D_MODEL), BF16)],
        compiler_params=pltpu.CompilerParams(
            dimension_semantics=("arbitrary", "arbitrary"), vmem_limit_bytes=VMEM_LIMIT),
        name="in_proj",
    )(x2d, g, w_stack, wf, bfv, conv_w, conv_b)


def _attention_kernel(q_ref, k_ref, v_ref, cq_ref, ck_ref, o_ref):
    hp = pl.program_id(1)
    seq = q_ref.shape[0]
    lane = lax.broadcasted_iota(jnp.int32, (1, LANES), 1)
    head_of_lane = lane // HEAD_DIM
    qpos = lax.broadcasted_iota(jnp.int32, (ATT_TQ, ATT_TK), 0)
    kpos = lax.broadcasted_iota(jnp.int32, (ATT_TQ, ATT_TK), 1)
    causal = kpos <= qpos

    def q_block(qi, _):
        q0 = pl.multiple_of(qi * ATT_TQ, ATT_TQ)
        q = q_ref[pl.ds(q0, ATT_TQ), :]
        cq = cq_ref[0, pl.ds(q0, ATT_TQ), :]
        qm, a = [], []
        for hh in range(HEADS_PER_BLOCK):
            qm.append(jnp.where(head_of_lane == hh, q, jnp.zeros_like(q)))
            sel = lane == hp * HEADS_PER_BLOCK + hh
            a.append(jnp.sum(jnp.where(sel, cq, 0.0), axis=1, keepdims=True))

        def kv_step(j, carry, masked):
            k0 = pl.multiple_of(j * ATT_TK, ATT_TK)
            kb = k_ref[pl.ds(k0, ATT_TK), :]
            vb = v_ref[pl.ds(k0, ATT_TK), :]
            new = []
            for hh in range(HEADS_PER_BLOCK):
                m, l, acc = carry[hh]
                bj = ck_ref[0, pl.ds(hp * HEADS_PER_BLOCK + hh, 1), pl.ds(k0, ATT_TK)]
                s = lax.dot_general(qm[hh], kb, (((1,), (1,)), ((), ())),
                                    preferred_element_type=F32)
                s = s + (a[hh] - bj)
                if masked:
                    s = jnp.where(causal, s, NEG_BIG)
                m_new = jnp.maximum(m, jnp.max(s, axis=1, keepdims=True))
                alpha = jnp.exp(m - m_new)
                p = jnp.exp(s - m_new)
                l = alpha * l + jnp.sum(p, axis=1, keepdims=True)
                acc = alpha * acc + _dot(p.astype(BF16), vb)
                new.append((m_new, l, acc))
            return tuple(new)

        init = tuple((jnp.full((ATT_TQ, 1), NEG_BIG, F32), jnp.zeros((ATT_TQ, 1), F32),
                      jnp.zeros((ATT_TQ, LANES), F32)) for _ in range(HEADS_PER_BLOCK))
        carry = lax.fori_loop(0, qi, lambda j, c: kv_step(j, c, False), init)
        carry = kv_step(qi, carry, True)
        out = jnp.zeros((ATT_TQ, LANES), F32)
        for hh in range(HEADS_PER_BLOCK):
            _, l, acc = carry[hh]
            out = jnp.where(head_of_lane == hh, acc / l, out)
        o_ref[pl.ds(q0, ATT_TQ), :] = out.astype(o_ref.dtype)
        return 0

    lax.fori_loop(0, seq // ATT_TQ, q_block, 0)


def _attention(q, k, v, cum, cum_t, batch, seq):
    tokens = batch * seq
    n_hp = N_HEADS // HEADS_PER_BLOCK
    blk = pl.BlockSpec((seq, LANES), lambda b, hp: (b, hp))
    return pl.pallas_call(
        _attention_kernel,
        grid=(batch, n_hp),
        in_specs=[blk, blk, blk,
                  pl.BlockSpec((1, seq, LANES), lambda b, hp: (b, 0, 0)),
                  pl.BlockSpec((1, N_HEADS, seq), lambda b, hp: (b, 0, 0))],
        out_specs=blk,
        out_shape=jax.ShapeDtypeStruct((tokens, D_MODEL), BF16),
        compiler_params=pltpu.CompilerParams(
            dimension_semantics=("arbitrary", "arbitrary"), vmem_limit_bytes=VMEM_LIMIT),
        name="fox_attention",
    )(q, k, v, cum, cum_t)


def _mixer_out_kernel(yc_ref, o_ref, sgc_ref, sga_ref, x_ref, wco_ref, wao_ref, wo_ref,
                      g_ref, wr_ref, x1_ref, h2_ref, route_ref, counts_ref, carry_scr):
    i = pl.program_id(0)
    tm = x_ref.shape[0]

    @pl.when(i == 0)
    def _():
        carry_scr[...] = jnp.zeros_like(carry_scr)

    y_conv = _dot(yc_ref[...], wco_ref[...])
    y_att = _dot(o_ref[...], wao_ref[...])
    m = sgc_ref[...].astype(F32) * y_conv + sga_ref[...].astype(F32) * y_att
    x1 = x_ref[...] + _dot(m.astype(BF16), wo_ref[...])
    x1_ref[...] = x1
    ms = jnp.mean(x1 * x1, axis=-1, keepdims=True)
    h2 = x1 * lax.rsqrt(ms + EPS) * g_ref[...]
    h2_ref[...] = h2
    logits = _dot(h2.astype(BF16), wr_ref[...])

    lane = lax.broadcasted_iota(jnp.int32, (tm, LANES), 1)
    far = jnp.int32(4 * LANES)

    def first_lane_of_max(vals, vmax):
        return jnp.min(jnp.where(vals == vmax, lane, far), axis=1, keepdims=True)

    lg = jnp.where(lane < N_GROUPS, logits, NEG_BIG)
    gmax = jnp.max(lg, axis=1, keepdims=True)
    g_val = 1.0 / jnp.sum(jnp.exp(lg - gmax), axis=1, keepdims=True)
    g_idx = first_lane_of_max(lg, gmax)
    lo = N_GROUPS + EXPERTS_PER_GROUP * g_idx
    le = jnp.where((lane >= lo) & (lane < lo + EXPERTS_PER_GROUP), logits, NEG_BIG)
    e1max = jnp.max(le, axis=1, keepdims=True)
    e1 = first_lane_of_max(le, e1max)
    le2 = jnp.where(lane == e1, NEG_BIG, le)
    e2max = jnp.max(le2, axis=1, keepdims=True)
    e2 = first_lane_of_max(le2, e2max)
    ratio = jnp.exp(e2max - e1max)
    w1 = g_val / (1.0 + ratio)
    w2 = g_val * ratio / (1.0 + ratio)

    oh1 = lane == e1
    oh2 = lane == e2
    onehot = jnp.where(oh1 | oh2, 1.0, 0.0).astype(BF16)
    r_i = lax.broadcasted_iota(jnp.int32, (tm, tm), 0)
    c_i = lax.broadcasted_iota(jnp.int32, (tm, tm), 1)
    strict_lower = jnp.where(c_i < r_i, 1.0, 0.0).astype(BF16)
    before = _dot(strict_lower, onehot) + carry_scr[...]
    rank1 = jnp.sum(jnp.where(oh1, before, 0.0), axis=1, keepdims=True)
    rank2 = jnp.sum(jnp.where(oh2, before, 0.0), axis=1, keepdims=True)
    total = carry_scr[...] + jnp.sum(onehot.astype(F32), axis=0, keepdims=True)
    carry_scr[...] = total
    counts_ref[...] = total

    e1f = (e1 - N_GROUPS).astype(F32)
    e2f = (e2 - N_GROUPS).astype(F32)
    route = jnp.where(lane == 0, e1f, 0.0)
    route = jnp.where(lane == 1, e2f, route)
    route = jnp.where(lane == 2, w1, route)
    route = jnp.where(lane == 3, w2, route)
    route = jnp.where(lane == 4, rank1, route)
    route = jnp.where(lane == 5, rank2, route)
    route_ref[...] = route


def _mixer_out(yc, o, sgc, sga, x2d, wco, wao, wo, g, wr):
    tokens = x2d.shape[0]
    tm = MIX_ROWS
    row_blk = pl.BlockSpec((tm, D_MODEL), lambda i: (i, 0))
    w_blk = pl.BlockSpec((D_MODEL, D_MODEL), lambda i: (0, 0))
    return pl.pallas_call(
        _mixer_out_kernel,
        grid=(tokens // tm,),
        in_specs=[row_blk, row_blk, row_blk, row_blk, row_blk, w_blk, w_blk, w_blk,
                  pl.BlockSpec((1, D_MODEL), lambda i: (0, 0)),
                  pl.BlockSpec((D_MODEL, LANES), lambda i: (0, 0))],
        out_specs=[row_blk, row_blk,
                   pl.BlockSpec((tm, LANES), lambda i: (i, 0)),
                   pl.BlockSpec((1, LANES), lambda i: (0, 0))],
        out_shape=[jax.ShapeDtypeStruct((tokens, D_MODEL), F32),
                   jax.ShapeDtypeStruct((tokens, D_MODEL), F32),
                   jax.ShapeDtypeStruct((tokens, LANES), F32),
                   jax.ShapeDtypeStruct((1, LANES), F32)],
        scratch_shapes=[pltpu.VMEM((1, LANES), F32)],
        compiler_params=pltpu.CompilerParams(
            dimension_semantics=("arbitrary",), vmem_limit_bytes=VMEM_LIMIT),
        name="mixer_out",
    )(yc, o, sgc, sga, x2d, wco, wao, wo, g, wr)


def _scatter_kernel(zero_row_ref, zero_on_ref, dest_ref, h2_ref, xrows_ref, zero_scr, sem, zsem):
    i = pl.program_id(0)
    ts = h2_ref.shape[0]

    def zero_copy(n):
        start = pl.multiple_of(zero_row_ref[n], MOE_BLOCK)
        return pltpu.make_async_copy(zero_scr, xrows_ref.at[pl.ds(start, MOE_BLOCK)], zsem)

    @pl.when(i == 0)
    def _():
        zero_scr[...] = jnp.zeros_like(zero_scr)
        for n in range(2 * N_EXPERTS):
            @pl.when(zero_on_ref[n] > 0)
            def _():
                zero_copy(n).start()
        for n in range(2 * N_EXPERTS):
            @pl.when(zero_on_ref[n] > 0)
            def _():
                zero_copy(n).wait()

    def issue(t, _):
        for kk in range(2):
            d = dest_ref[0, 0, kk * ts + t]
            pltpu.make_async_copy(h2_ref.at[pl.ds(t, 1)], xrows_ref.at[pl.ds(d, 1)], sem).start()
        return 0

    lax.fori_loop(0, ts, issue, 0)
    pltpu.make_async_copy(xrows_ref.at[pl.ds(0, 2 * ts)], xrows_ref.at[pl.ds(0, 2 * ts)], sem).wait()


def _scatter(zero_row, zero_on, dest_blocks, h2, n_rows):
    tokens = h2.shape[0]
    ts = ROW_TILE
    grid_spec = pltpu.PrefetchScalarGridSpec(
        num_scalar_prefetch=2,
        grid=(tokens // ts,),
        in_specs=[pl.BlockSpec((1, 1, 2 * ts), lambda i, *_: (i, 0, 0), memory_space=pltpu.SMEM),
                  pl.BlockSpec((ts, D_MODEL), lambda i, *_: (i, 0))],
        out_specs=pl.BlockSpec(memory_space=pl.ANY),
        scratch_shapes=[pltpu.VMEM((MOE_BLOCK, D_MODEL), F32),
                        pltpu.SemaphoreType.DMA(()), pltpu.SemaphoreType.DMA(())],
    )
    return pl.pallas_call(
        _scatter_kernel,
        grid_spec=grid_spec,
        out_shape=jax.ShapeDtypeStruct((n_rows, D_MODEL), F32),
        compiler_params=pltpu.CompilerParams(
            dimension_semantics=("arbitrary",), vmem_limit_bytes=VMEM_LIMIT),
        name="moe_scatter",
    )(zero_row, zero_on, dest_blocks, h2)


def _expert_kernel(block_e_ref, n_used_ref, x_ref, wg_ref, wu_ref, wd_ref, y_ref):
    i = pl.program_id(0)

    @pl.when(i < n_used_ref[0])
    def _():
        xb = x_ref[...].astype(BF16)
        a = _dot(xb, wg_ref[0])
        u = _dot(xb, wu_ref[0])
        hmid = (a * jax.nn.sigmoid(a) * u).astype(BF16)
        y_ref[...] = _dot(hmid, wd_ref[0])

    @pl.when(i >= n_used_ref[0])
    def _():
        y_ref[...] = jnp.zeros_like(y_ref)


def _experts(block_e, n_used, x_rows, wg, wu, wd):
    n_rows = x_rows.shape[0]
    n_blocks = n_rows // MOE_BLOCK

    def blk(i, block_e_ref, n_used_ref):
        return jnp.minimum(i, n_used_ref[0] - 1)

    def row_map(i, be, nu):
        return (blk(i, be, nu), 0)

    def w_map(i, be, nu):
        return (be[blk(i, be, nu)], 0, 0)

    grid_spec = pltpu.PrefetchScalarGridSpec(
        num_scalar_prefetch=2,
        grid=(n_blocks,),
        in_specs=[pl.BlockSpec((MOE_BLOCK, D_MODEL), row_map),
                  pl.BlockSpec((1, D_MODEL, D_EXPERT), w_map),
                  pl.BlockSpec((1, D_MODEL, D_EXPERT), w_map),
                  pl.BlockSpec((1, D_EXPERT, D_MODEL), w_map)],
        out_specs=pl.BlockSpec((MOE_BLOCK, D_MODEL), lambda i, be, nu: (i, 0)),
    )
    return pl.pallas_call(
        _expert_kernel,
        grid_spec=grid_spec,
        out_shape=jax.ShapeDtypeStruct((n_rows, D_MODEL), F32),
        compiler_params=pltpu.CompilerParams(
            dimension_semantics=("arbitrary",), vmem_limit_bytes=VMEM_LIMIT),
        name="moe_experts",
    )(block_e, n_used, x_rows, wg, wu, wd)


def _combine_kernel(dest_ref, yrows_ref, x1_ref, route_ref, g_ref, out_ref, ybuf, sem):
    tc = x1_ref.shape[0]

    def issue(t, _):
        for kk in range(2):
            d = dest_ref[0, 0, kk * tc + t]
            pltpu.make_async_copy(yrows_ref.at[pl.ds(d, 1)],
                                  ybuf.at[pl.ds(kk * tc + t, 1)], sem).start()
        return 0

    lax.fori_loop(0, tc, issue, 0)
    pltpu.make_async_copy(yrows_ref.at[pl.ds(0, 2 * tc)], ybuf, sem).wait()

    route = route_ref[...]
    w1 = route[:, 2:3]
    w2 = route[:, 3:4]
    x2 = x1_ref[...] + (ybuf[0:tc, :] * w1 + ybuf[tc:2 * tc, :] * w2)
    ms = jnp.mean(x2 * x2, axis=-1, keepdims=True)
    out_ref[...] = x2 * lax.rsqrt(ms + EPS) * g_ref[...]


def _combine(dest_blocks, y_rows, x1, route, g):
    tokens = x1.shape[0]
    tc = ROW_TILE
    row_blk = pl.BlockSpec((tc, D_MODEL), lambda i: (i, 0))
    return pl.pallas_call(
        _combine_kernel,
        grid=(tokens // tc,),
        in_specs=[pl.BlockSpec((1, 1, 2 * tc), lambda i: (i, 0, 0), memory_space=pltpu.SMEM),
                  pl.BlockSpec(memory_space=pl.ANY),
                  row_blk,
                  pl.BlockSpec((tc, LANES), lambda i: (i, 0)),
                  pl.BlockSpec((1, D_MODEL), lambda i: (0, 0))],
        out_specs=row_blk,
        out_shape=jax.ShapeDtypeStruct((tokens, D_MODEL), F32),
        scratch_shapes=[pltpu.VMEM((2 * tc, D_MODEL), F32), pltpu.SemaphoreType.DMA(())],
        compiler_params=pltpu.CompilerParams(
            dimension_semantics=("arbitrary",), vmem_limit_bytes=VMEM_LIMIT),
        name="moe_combine",
    )(dest_blocks, y_rows, x1, route, g)


def _dest_blocks(dest, tile):
    tokens = dest.shape[0]
    return dest.reshape(tokens // tile, tile, 2).transpose(0, 2, 1).reshape(tokens // tile, 1, 2 * tile)


def kernel(x, norm_mix_g, w_in, conv_w, conv_b, b_forget, w_conv_out, w_att_out, w_out,
           norm_ffn_g, w_router_group, w_router_expert, w_e_gate, w_e_up, w_e_down,
           norm_final_g):
    batch, seq, d = x.shape
    assert d == D_MODEL and seq % max(ATT_TQ, IN_PROJ_ROWS) == 0
    tokens = batch * seq
    assert tokens % MIX_ROWS == 0 and tokens % ROW_TILE == 0
    x2d = x.reshape(tokens, D_MODEL)

    wb = w_in.astype(BF16)
    n_main = 6 * D_MODEL
    gate0 = n_main + N_HEADS
    pieces = [wb[:, i * D_MODEL:(i + 1) * D_MODEL] for i in range(6)]
    pieces += [wb[:, gate0:gate0 + D_MODEL], wb[:, gate0 + D_MODEL:gate0 + 2 * D_MODEL]]
    w_stack = jnp.stack(pieces, axis=0)
    wf = jnp.pad(wb[:, n_main:gate0], ((0, 0), (0, LANES - N_HEADS)))
    bfv = jnp.pad(b_forget.astype(F32), (0, LANES - N_HEADS)).reshape(1, LANES)
    wr = jnp.concatenate(
        [w_router_group, jnp.transpose(w_router_expert, (1, 0, 2)).reshape(D_MODEL, N_EXPERTS)], axis=1)
    wr = jnp.pad(wr, ((0, 0), (0, LANES - wr.shape[1]))).astype(BF16)

    yc, q, k, v, sgc, sga, cum = _in_proj(
        x2d, norm_mix_g.reshape(1, D_MODEL), w_stack, wf, bfv, conv_w,
        conv_b.reshape(1, D_MODEL), batch, seq)
    cum_t = jnp.transpose(cum[:, :, :N_HEADS], (0, 2, 1))
    o = _attention(q, k, v, cum, cum_t, batch, seq)

    x1, h2, route, counts = _mixer_out(
        yc, o, sgc, sga, x2d, w_conv_out.astype(BF16), w_att_out.astype(BF16),
        w_out.astype(BF16), norm_ffn_g.reshape(1, D_MODEL), wr)

    counts = counts[0, N_GROUPS:N_GROUPS + N_EXPERTS].astype(jnp.int32)
    padded = (counts + MOE_BLOCK - 1) // MOE_BLOCK * MOE_BLOCK
    end_padded = jnp.cumsum(padded)
    start_padded = end_padded - padded
    expert = route[:, 0:2].astype(jnp.int32)
    rank = route[:, 4:6].astype(jnp.int32)
    dest = start_padded[expert] + rank
    n_rows = tokens * 2 + N_EXPERTS * MOE_BLOCK
    n_blocks = n_rows // MOE_BLOCK
    n_used = (end_padded[-1] // MOE_BLOCK).astype(jnp.int32).reshape(1)
    block_e = jnp.minimum(
        jnp.searchsorted(end_padded, jnp.arange(n_blocks, dtype=jnp.int32) * MOE_BLOCK, side='right'),
        N_EXPERTS - 1).astype(jnp.int32)
    tail_blk = n_used[0] + jnp.arange(N_EXPERTS, dtype=jnp.int32)
    zero_row = jnp.concatenate([jnp.maximum(end_padded - MOE_BLOCK, 0).astype(jnp.int32),
                                jnp.minimum(tail_blk, n_blocks - 1) * MOE_BLOCK])
    zero_on = jnp.concatenate([counts > 0, tail_blk < n_blocks]).astype(jnp.int32)
    dest_blocks = _dest_blocks(dest.astype(jnp.int32), ROW_TILE)

    x_rows = _scatter(zero_row, zero_on, dest_blocks, h2, n_rows)
    y_rows = _experts(block_e, n_used, x_rows, w_e_gate.astype(BF16), w_e_up.astype(BF16),
                      w_e_down.astype(BF16))
    out = _combine(dest_blocks, y_rows, x1, route, norm_final_g.reshape(1, D_MODEL))
    return out.reshape(batch, seq, D_MODEL)
```

```python
import jax
import jax.numpy as jnp
from jax import lax
from jax.experimental import pallas as pl
from jax.experimental.pallas import tpu as pltpu

D_MODEL = 1024
HEAD_DIM = 64
N_HEADS = 16
N_GROUPS = 4
EXPERTS_PER_GROUP = 8
N_EXPERTS = N_GROUPS * EXPERTS_PER_GROUP
D_EXPERT = 512
MOE_BLOCK = 256
CONV_WIDTH = 3
EPS = 1e-6

LANES = 128
HEADS_PER_BLOCK = LANES // HEAD_DIM
NEG_BIG = -1e30
LOG2E = 1.4426950408889634

IN_PROJ_COLS = 256
IN_PROJ_ROWS = 512
ATT_BLOCK = 512
MIX_ROWS = 512
ROW_TILE = 256
VMEM_LIMIT = 56 * 1024 * 1024

F32 = jnp.float32
BF16 = jnp.bfloat16


def _dot(a, b):
    return jnp.dot(a, b, preferred_element_type=F32)


def _in_proj_kernel(x_ref, g_ref, w_ref, wf_ref, bf_ref, cw_ref, cb_ref,
                    yc_ref, q_ref, k_ref, v_ref, sgc_ref, sga_ref, cq_ref, ck_ref, h_scr):
    j = pl.program_id(1)
    seq = x_ref.shape[0]
    n_chunks = seq // IN_PROJ_ROWS

    @pl.when(j == 0)
    def _():
        for r in range(n_chunks):
            rows = slice(r * IN_PROJ_ROWS, (r + 1) * IN_PROJ_ROWS)
            xs = x_ref[rows, :]
            ms = jnp.mean(xs * xs, axis=-1, keepdims=True)
            h_scr[rows, :] = (xs * lax.rsqrt(ms + EPS) * g_ref[...]).astype(BF16)
        f = _dot(h_scr[...], wf_ref[...]) + bf_ref[...]
        c = jnp.minimum(f, 0.0) - jnp.log(1.0 + jnp.exp(-jnp.abs(f)))
        row = lax.broadcasted_iota(jnp.int32, c.shape, 0)
        d = 1
        while d < seq:
            c = c + jnp.where(row >= d, pltpu.roll(c, d, axis=0), 0.0)
            d *= 2
        c = c * LOG2E
        hi = c.astype(BF16).astype(F32)
        mid = (c - hi).astype(BF16).astype(F32)
        lo = (c - hi - mid).astype(BF16).astype(F32)
        grp = lax.broadcasted_iota(jnp.int32, c.shape, 1) // N_HEADS
        terms = jnp.where(grp % 3 == 0, hi, jnp.where(grp % 3 == 1, mid, lo))
        one = jnp.float32(1.0)
        cq_ref[0] = jnp.where(grp < 3, terms, jnp.where(grp < 6, one, 0.0)).astype(BF16)
        ck_ref[0] = jnp.where(grp < 3, one, jnp.where(grp < 6, -terms, 0.0)).astype(BF16)

    cw0 = cw_ref[0:1, :]
    cw1 = cw_ref[1:2, :]
    cw2 = cw_ref[2:3, :]
    cb = cb_ref[...]
    rowc = lax.broadcasted_iota(jnp.int32, (IN_PROJ_ROWS, IN_PROJ_COLS), 0)
    zprev = None
    for r in range(n_chunks):
        rows = slice(r * IN_PROJ_ROWS, (r + 1) * IN_PROJ_ROWS)
        hs = h_scr[rows, :]
        cb_gate = _dot(hs, w_ref[0])
        z = _dot(hs, w_ref[1]) * _dot(hs, w_ref[2])
        z1 = pltpu.roll(z, 1, axis=0)
        z2 = pltpu.roll(z, 2, axis=0)
        if zprev is None:
            p1 = jnp.zeros_like(z)
            p2 = p1
        else:
            p1 = pltpu.roll(zprev, 1, axis=0)
            p2 = pltpu.roll(zprev, 2, axis=0)
        z1 = jnp.where(rowc < 1, p1, z1)
        z2 = jnp.where(rowc < 2, p2, z2)
        acc = cb + cw0 * z2 + cw1 * z1 + cw2 * z
        yc_ref[rows, :] = (cb_gate * acc).astype(BF16)
        zprev = z
        q_ref[rows, :] = (_dot(hs, w_ref[3]) * (LOG2E * HEAD_DIM ** -0.5)).astype(BF16)
        k_ref[rows, :] = _dot(hs, w_ref[4]).astype(BF16)
        v_ref[rows, :] = _dot(hs, w_ref[5]).astype(BF16)
        sgc_ref[rows, :] = jax.nn.sigmoid(_dot(hs, w_ref[6])).astype(BF16)
        sga_ref[rows, :] = jax.nn.sigmoid(_dot(hs, w_ref[7])).astype(BF16)


def _in_proj(x2d, g, w_stack, wf, bfv, conv_w, conv_b, batch, seq):
    tokens = batch * seq
    tn = IN_PROJ_COLS
    nj = D_MODEL // tn
    col_out = pl.BlockSpec((seq, tn), lambda b, j: (b, j))
    out_bf16 = jax.ShapeDtypeStruct((tokens, D_MODEL), BF16)
    return pl.pallas_call(
        _in_proj_kernel,
        grid=(batch, nj),
        in_specs=[
            pl.BlockSpec((seq, D_MODEL), lambda b, j: (b, 0)),
            pl.BlockSpec((1, D_MODEL), lambda b, j: (0, 0)),
            pl.BlockSpec((8, D_MODEL, tn), lambda b, j: (0, 0, j)),
            pl.BlockSpec((D_MODEL, LANES), lambda b, j: (0, 0)),
            pl.BlockSpec((1, LANES), lambda b, j: (0, 0)),
            pl.BlockSpec((CONV_WIDTH, tn), lambda b, j: (0, j)),
            pl.BlockSpec((1, tn), lambda b, j: (0, j)),
        ],
        out_specs=[col_out] * 6 + [pl.BlockSpec((1, seq, LANES), lambda b, j: (b, 0, 0))] * 2,
        out_shape=[out_bf16] * 6 + [jax.ShapeDtypeStruct((batch, seq, LANES), BF16)] * 2,
        scratch_shapes=[pltpu.VMEM((seq, D_MODEL), BF16)],
        compiler_params=pltpu.CompilerParams(
            dimension_semantics=("arbitrary", "arbitrary"), vmem_limit_bytes=VMEM_LIMIT),
        name="in_proj",
    )(x2d, g, w_stack, wf, bfv, conv_w, conv_b)


def _attention_kernel(q_ref, k_ref, v_ref, cq_ref, ck_ref, o_ref, qa_scr, ka_scr, va_scr):
    hp = pl.program_id(1)
    seq = q_ref.shape[0]
    blk = ATT_BLOCK
    lane = lax.broadcasted_iota(jnp.int32, (1, LANES), 1)
    qpos = lax.broadcasted_iota(jnp.int32, (blk, blk), 0)
    kpos = lax.broadcasted_iota(jnp.int32, (blk, blk), 1)
    causal = kpos <= qpos
    zero = jnp.zeros((), BF16)

    own, base = [], []
    for hh in range(HEADS_PER_BLOCK):
        own.append((lane // HEAD_DIM) == hh)
        base.append(((hh + 1) % HEADS_PER_BLOCK) * HEAD_DIM)
        gate = (lane % N_HEADS) == hp * HEADS_PER_BLOCK + hh
        qa_scr[hh, :, 0:LANES] = jnp.where(own[hh], q_ref[...], zero)
        qa_scr[hh, :, LANES:2 * LANES] = jnp.where(gate, cq_ref[0], zero)
        ka_scr[hh, :, 0:LANES] = jnp.where(own[hh], k_ref[...], zero)
        ka_scr[hh, :, LANES:2 * LANES] = jnp.where(gate, ck_ref[0], zero)
        va_scr[hh] = jnp.where(own[hh], v_ref[...],
                               jnp.where(lane == base[hh], 1.0, 0.0).astype(BF16))

    for rg in range(seq // blk):
        rows = slice(rg * blk, (rg + 1) * blk)
        out = None
        for hh in range(HEADS_PER_BLOCK):
            qa = qa_scr[hh, rows, :]
            m = acc = None
            for c in range(rg + 1):
                keys = slice(c * blk, (c + 1) * blk)
                s = lax.dot_general(qa, ka_scr[hh, keys, :], (((1,), (1,)), ((), ())),
                                    preferred_element_type=F32)
                if c == rg:
                    s = jnp.where(causal, s, NEG_BIG)
                mx = jnp.max(s, axis=1, keepdims=True)
                if m is None:
                    m = mx
                    acc = _dot(jnp.exp2(s - m).astype(BF16), va_scr[hh, keys, :])
                else:
                    m_new = jnp.maximum(m, mx)
                    acc = jnp.exp2(m - m_new) * acc + _dot(jnp.exp2(s - m_new).astype(BF16),
                                                           va_scr[hh, keys, :])
                    m = m_new
            res = acc / acc[:, base[hh]:base[hh] + 1]
            out = res if out is None else jnp.where(own[hh], res, out)
        o_ref[rows, :] = out.astype(o_ref.dtype)


def _attention(q, k, v, cq, ck, batch, seq):
    tokens = batch * seq
    n_hp = N_HEADS // HEADS_PER_BLOCK
    blk = pl.BlockSpec((seq, LANES), lambda b, hp: (b, hp))
    gate_blk = pl.BlockSpec((1, seq, LANES), lambda b, hp: (b, 0, 0))
    wide = pltpu.VMEM((HEADS_PER_BLOCK, seq, 2 * LANES), BF16)
    return pl.pallas_call(
        _attention_kernel,
        grid=(batch, n_hp),
        in_specs=[blk, blk, blk, gate_blk, gate_blk],
        out_specs=blk,
        out_shape=jax.ShapeDtypeStruct((tokens, D_MODEL), BF16),
        scratch_shapes=[wide, wide, pltpu.VMEM((HEADS_PER_BLOCK, seq, LANES), BF16)],
        compiler_params=pltpu.CompilerParams(
            dimension_semantics=("arbitrary", "arbitrary"), vmem_limit_bytes=VMEM_LIMIT),
        name="fox_attention",
    )(q, k, v, cq, ck)


def _mixer_out_kernel(yc_ref, o_ref, sgc_ref, sga_ref, x_ref, wco_ref, wao_ref, wo_ref,
                      g_ref, wr_ref, x1_ref, h2_ref, route_ref, counts_ref, carry_scr):
    i = pl.program_id(0)
    tm = x_ref.shape[0]

    @pl.when(i == 0)
    def _():
        carry_scr[...] = jnp.zeros_like(carry_scr)

    y_conv = _dot(yc_ref[...], wco_ref[...])
    y_att = _dot(o_ref[...], wao_ref[...])
    m = sgc_ref[...].astype(F32) * y_conv + sga_ref[...].astype(F32) * y_att
    x1 = x_ref[...] + _dot(m.astype(BF16), wo_ref[...])
    x1_ref[...] = x1
    ms = jnp.mean(x1 * x1, axis=-1, keepdims=True)
    h2 = x1 * lax.rsqrt(ms + EPS) * g_ref[...]
    h2_ref[...] = h2
    logits = _dot(h2.astype(BF16), wr_ref[...])

    lane = lax.broadcasted_iota(jnp.int32, (tm, LANES), 1)
    far = jnp.int32(4 * LANES)

    def first_lane_of_max(vals, vmax):
        return jnp.min(jnp.where(vals == vmax, lane, far), axis=1, keepdims=True)

    lg = jnp.where(lane < N_GROUPS, logits, NEG_BIG)
    gmax = jnp.max(lg, axis=1, keepdims=True)
    g_val = 1.0 / jnp.sum(jnp.exp(lg - gmax), axis=1, keepdims=True)
    g_idx = first_lane_of_max(lg, gmax)
    lo = N_GROUPS + EXPERTS_PER_GROUP * g_idx
    le = jnp.where((lane >= lo) & (lane < lo + EXPERTS_PER_GROUP), logits, NEG_BIG)
    e1max = jnp.max(le, axis=1, keepdims=True)
    e1 = first_lane_of_max(le, e1max)
    le2 = jnp.where(lane == e1, NEG_BIG, le)
    e2max = jnp.max(le2, axis=1, keepdims=True)
    e2 = first_lane_of_max(le2, e2max)
    ratio = jnp.exp(e2max - e1max)
    w1 = g_val / (1.0 + ratio)
    w2 = g_val * ratio / (1.0 + ratio)

    oh1 = lane == e1
    oh2 = lane == e2
    onehot = jnp.where(oh1 | oh2, 1.0, 0.0).astype(BF16)
    r_i = lax.broadcasted_iota(jnp.int32, (tm, tm), 0)
    c_i = lax.broadcasted_iota(jnp.int32, (tm, tm), 1)
    strict_lower = jnp.where(c_i < r_i, 1.0, 0.0).astype(BF16)
    before = _dot(strict_lower, onehot) + carry_scr[...]
    rank1 = jnp.sum(jnp.where(oh1, before, 0.0), axis=1, keepdims=True)
    rank2 = jnp.sum(jnp.where(oh2, before, 0.0), axis=1, keepdims=True)
    total = carry_scr[...] + jnp.sum(onehot.astype(F32), axis=0, keepdims=True)
    carry_scr[...] = total
    counts_ref[...] = total

    e1f = (e1 - N_GROUPS).astype(F32)
    e2f = (e2 - N_GROUPS).astype(F32)
    route = jnp.where(lane == 0, e1f, 0.0)
    route = jnp.where(lane == 1, e2f, route)
    route = jnp.where(lane == 2, w1, route)
    route = jnp.where(lane == 3, w2, route)
    route = jnp.where(lane == 4, rank1, route)
    route = jnp.where(lane == 5, rank2, route)
    route_ref[...] = route


def _mixer_out(yc, o, sgc, sga, x2d, wco, wao, wo, g, wr):
    tokens = x2d.shape[0]
    tm = MIX_ROWS
    row_blk = pl.BlockSpec((tm, D_MODEL), lambda i: (i, 0))
    w_blk = pl.BlockSpec((D_MODEL, D_MODEL), lambda i: (0, 0))
    return pl.pallas_call(
        _mixer_out_kernel,
        grid=(tokens // tm,),
        in_specs=[row_blk, row_blk, row_blk, row_blk, row_blk, w_blk, w_blk, w_blk,
                  pl.BlockSpec((1, D_MODEL), lambda i: (0, 0)),
                  pl.BlockSpec((D_MODEL, LANES), lambda i: (0, 0))],
        out_specs=[row_blk, row_blk,
                   pl.BlockSpec((tm, LANES), lambda i: (i, 0)),
                   pl.BlockSpec((1, LANES), lambda i: (0, 0))],
        out_shape=[jax.ShapeDtypeStruct((tokens, D_MODEL), F32),
                   jax.ShapeDtypeStruct((tokens, D_MODEL), F32),
                   jax.ShapeDtypeStruct((tokens, LANES), F32),
                   jax.ShapeDtypeStruct((1, LANES), F32)],
        scratch_shapes=[pltpu.VMEM((1, LANES), F32)],
        compiler_params=pltpu.CompilerParams(
            dimension_semantics=("arbitrary",), vmem_limit_bytes=VMEM_LIMIT),
        name="mixer_out",
    )(yc, o, sgc, sga, x2d, wco, wao, wo, g, wr)


def _scatter_kernel(zero_row_ref, zero_on_ref, dest_ref, h2_ref, xrows_ref, zero_scr, sem, zsem):
    i = pl.program_id(0)
    ts = h2_ref.shape[0]

    def zero_copy(n):
        start = pl.multiple_of(zero_row_ref[n], MOE_BLOCK)
        return pltpu.make_async_copy(zero_scr, xrows_ref.at[pl.ds(start, MOE_BLOCK)], zsem)

    @pl.when(i == 0)
    def _():
        zero_scr[...] = jnp.zeros_like(zero_scr)
        for n in range(2 * N_EXPERTS):
            @pl.when(zero_on_ref[n] > 0)
            def _():
                zero_copy(n).start()
        for n in range(2 * N_EXPERTS):
            @pl.when(zero_on_ref[n] > 0)
            def _():
                zero_copy(n).wait()

    def issue(t, _):
        for kk in range(2):
            d = dest_ref[0, 0, kk * ts + t]
            pltpu.make_async_copy(h2_ref.at[pl.ds(t, 1)], xrows_ref.at[pl.ds(d, 1)], sem).start()
        return 0

    lax.fori_loop(0, ts, issue, 0)
    pltpu.make_async_copy(xrows_ref.at[pl.ds(0, 2 * ts)], xrows_ref.at[pl.ds(0, 2 * ts)], sem).wait()


def _scatter(zero_row, zero_on, dest_blocks, h2, n_rows):
    tokens = h2.shape[0]
    ts = ROW_TILE
    grid_spec = pltpu.PrefetchScalarGridSpec(
        num_scalar_prefetch=2,
        grid=(tokens // ts,),
        in_specs=[pl.BlockSpec((1, 1, 2 * ts), lambda i, *_: (i, 0, 0), memory_space=pltpu.SMEM),
                  pl.BlockSpec((ts, D_MODEL), lambda i, *_: (i, 0))],
        out_specs=pl.BlockSpec(memory_space=pl.ANY),
        scratch_shapes=[pltpu.VMEM((MOE_BLOCK, D_MODEL), F32),
                        pltpu.SemaphoreType.DMA(()), pltpu.SemaphoreType.DMA(())],
    )
    return pl.pallas_call(
        _scatter_kernel,
        grid_spec=grid_spec,
        out_shape=jax.ShapeDtypeStruct((n_rows, D_MODEL), F32),
        compiler_params=pltpu.CompilerParams(
            dimension_semantics=("arbitrary",), vmem_limit_bytes=VMEM_LIMIT),
        name="moe_scatter",
    )(zero_row, zero_on, dest_blocks, h2)


def _expert_kernel(block_e_ref, n_used_ref, x_ref, wg_ref, wu_ref, wd_ref, y_ref):
    i = pl.program_id(0)

    @pl.when(i < n_used_ref[0])
    def _():
        xb = x_ref[...].astype(BF16)
        a = _dot(xb, wg_ref[0])
        u = _dot(xb, wu_ref[0])
        hmid = (a * jax.nn.sigmoid(a) * u).astype(BF16)
        y_ref[...] = _dot(hmid, wd_ref[0])

    @pl.when(i >= n_used_ref[0])
    def _():
        y_ref[...] = jnp.zeros_like(y_ref)


def _experts(block_e, n_used, x_rows, wg, wu, wd):
    n_rows = x_rows.shape[0]
    n_blocks = n_rows // MOE_BLOCK

    def blk(i, block_e_ref, n_used_ref):
        return jnp.minimum(i, n_used_ref[0] - 1)

    def row_map(i, be, nu):
        return (blk(i, be, nu), 0)

    def w_map(i, be, nu):
        return (be[blk(i, be, nu)], 0, 0)

    grid_spec = pltpu.PrefetchScalarGridSpec(
        num_scalar_prefetch=2,
        grid=(n_blocks,),
        in_specs=[pl.BlockSpec((MOE_BLOCK, D_MODEL), row_map),
                  pl.BlockSpec((1, D_MODEL, D_EXPERT), w_map),
                  pl.BlockSpec((1, D_MODEL, D_EXPERT), w_map),
                  pl.BlockSpec((1, D_EXPERT, D_MODEL), w_map)],
        out_specs=pl.BlockSpec((MOE_BLOCK, D_MODEL), lambda i, be, nu: (i, 0)),
    )
    return pl.pallas_call(
        _expert_kernel,
        grid_spec=grid_spec,
        out_shape=jax.ShapeDtypeStruct((n_rows, D_MODEL), F32),
        compiler_params=pltpu.CompilerParams(
            dimension_semantics=("arbitrary",), vmem_limit_bytes=VMEM_LIMIT),
        name="moe_experts",
    )(block_e, n_used, x_rows, wg, wu, wd)


def _combine_kernel(dest_ref, yrows_ref, x1_ref, route_ref, g_ref, out_ref, ybuf, sem):
    tc = x1_ref.shape[0]

    def issue(t, _):
        for kk in range(2):
            d = dest_ref[0, 0, kk * tc + t]
            pltpu.make_async_copy(yrows_ref.at[pl.ds(d, 1)],
                                  ybuf.at[pl.ds(kk * tc + t, 1)], sem).start()
        return 0

    lax.fori_loop(0, tc, issue, 0)
    pltpu.make_async_copy(yrows_ref.at[pl.ds(0, 2 * tc)], ybuf, sem).wait()

    route = route_ref[...]
    w1 = route[:, 2:3]
    w2 = route[:, 3:4]
    x2 = x1_ref[...] + (ybuf[0:tc, :] * w1 + ybuf[tc:2 * tc, :] * w2)
    ms = jnp.mean(x2 * x2, axis=-1, keepdims=True)
    out_ref[...] = x2 * lax.rsqrt(ms + EPS) * g_ref[...]


def _combine(dest_blocks, y_rows, x1, route, g):
    tokens = x1.shape[0]
    tc = ROW_TILE
    row_blk = pl.BlockSpec((tc, D_MODEL), lambda i: (i, 0))
    return pl.pallas_call(
        _combine_kernel,
        grid=(tokens // tc,),
        in_specs=[pl.BlockSpec((1, 1, 2 * tc), lambda i: (i, 0, 0), memory_space=pltpu.SMEM),
                  pl.BlockSpec(memory_space=pl.ANY),
                  row_blk,
                  pl.BlockSpec((tc, LANES), lambda i: (i, 0)),
                  pl.BlockSpec((1, D_MODEL), lambda i: (0, 0))],
        out_specs=row_blk,
        out_shape=jax.ShapeDtypeStruct((tokens, D_MODEL), F32),
        scratch_shapes=[pltpu.VMEM((2 * tc, D_MODEL), F32), pltpu.SemaphoreType.DMA(())],
        compiler_params=pltpu.CompilerParams(
            dimension_semantics=("arbitrary",), vmem_limit_bytes=VMEM_LIMIT),
        name="moe_combine",
    )(dest_blocks, y_rows, x1, route, g)


def _dest_blocks(dest, tile):
    tokens = dest.shape[0]
    return dest.reshape(tokens // tile, tile, 2).transpose(0, 2, 1).reshape(tokens // tile, 1, 2 * tile)


def kernel(x, norm_mix_g, w_in, conv_w, conv_b, b_forget, w_conv_out, w_att_out, w_out,
           norm_ffn_g, w_router_group, w_router_expert, w_e_gate, w_e_up, w_e_down,
           norm_final_g):
    batch, seq, d = x.shape
    assert d == D_MODEL and seq % max(ATT_BLOCK, IN_PROJ_ROWS) == 0
    tokens = batch * seq
    assert tokens % MIX_ROWS == 0 and tokens % ROW_TILE == 0
    x2d = x.reshape(tokens, D_MODEL)

    wb = w_in.astype(BF16)
    n_main = 6 * D_MODEL
    gate0 = n_main + N_HEADS
    pieces = [wb[:, i * D_MODEL:(i + 1) * D_MODEL] for i in range(6)]
    pieces += [wb[:, gate0:gate0 + D_MODEL], wb[:, gate0 + D_MODEL:gate0 + 2 * D_MODEL]]
    w_stack = jnp.stack(pieces, axis=0)
    n_rep = 6
    wf = jnp.pad(jnp.tile(wb[:, n_main:gate0], (1, n_rep)), ((0, 0), (0, LANES - n_rep * N_HEADS)))
    bfv = jnp.pad(jnp.tile(b_forget.astype(F32), n_rep), (0, LANES - n_rep * N_HEADS)).reshape(1, LANES)
    wr = jnp.concatenate(
        [w_router_group, jnp.transpose(w_router_expert, (1, 0, 2)).reshape(D_MODEL, N_EXPERTS)], axis=1)
    wr = jnp.pad(wr, ((0, 0), (0, LANES - wr.shape[1]))).astype(BF16)

    yc, q, k, v, sgc, sga, cq, ck = _in_proj(
        x2d, norm_mix_g.reshape(1, D_MODEL), w_stack, wf, bfv, conv_w,
        conv_b.reshape(1, D_MODEL), batch, seq)
    o = _attention(q, k, v, cq, ck, batch, seq)

    x1, h2, route, counts = _mixer_out(
        yc, o, sgc, sga, x2d, w_conv_out.astype(BF16), w_att_out.astype(BF16),
        w_out.astype(BF16), norm_ffn_g.reshape(1, D_MODEL), wr)

    counts = counts[0, N_GROUPS:N_GROUPS + N_EXPERTS].astype(jnp.int32)
    padded = (counts + MOE_BLOCK - 1) // MOE_BLOCK * MOE_BLOCK
    end_padded = jnp.cumsum(padded)
    start_padded = end_padded - padded
    expert = route[:, 0:2].astype(jnp.int32)
    rank = route[:, 4:6].astype(jnp.int32)
    dest = start_padded[expert] + rank
    n_rows = tokens * 2 + N_EXPERTS * MOE_BLOCK
    n_blocks = n_rows // MOE_BLOCK
    n_used = (end_padded[-1] // MOE_BLOCK).astype(jnp.int32).reshape(1)
    block_start = jnp.arange(n_blocks, dtype=jnp.int32) * MOE_BLOCK
    block_e = jnp.minimum(jnp.sum(end_padded[None, :] <= block_start[:, None], axis=1),
                          N_EXPERTS - 1).astype(jnp.int32)
    tail_blk = n_used[0] + jnp.arange(N_EXPERTS, dtype=jnp.int32)
    zero_row = jnp.concatenate([jnp.maximum(end_padded - MOE_BLOCK, 0).astype(jnp.int32),
                                jnp.minimum(tail_blk, n_blocks - 1) * MOE_BLOCK])
    zero_on = jnp.concatenate([counts > 0, tail_blk < n_blocks]).astype(jnp.int32)
    dest_blocks = _dest_blocks(dest.astype(jnp.int32), ROW_TILE)

    x_rows = _scatter(zero_row, zero_on, dest_blocks, h2, n_rows)
    y_rows = _experts(block_e, n_used, x_rows, w_e_gate.astype(BF16), w_e_up.astype(BF16),
                      w_e_down.astype(BF16))
    out = _combine(dest_blocks, y_rows, x1, route, norm_final_g.reshape(1, D_MODEL))
    return out.reshape(batch, seq, D_MODEL)
```

```python
import jax
import jax.numpy as jnp
from jax import lax
from jax.experimental import pallas as pl
from jax.experimental.pallas import tpu as pltpu
from jax.experimental.pallas import tpu_sc as plsc

D_MODEL = 1024
HEAD_DIM = 64
N_HEADS = 16
N_GROUPS = 4
EXPERTS_PER_GROUP = 8
N_EXPERTS = N_GROUPS * EXPERTS_PER_GROUP
D_EXPERT = 512
MOE_BLOCK = 256
CONV_WIDTH = 3
EPS = 1e-6

LANES = 128
HEADS_PER_BLOCK = LANES // HEAD_DIM
NEG_BIG = -1e30
LOG2E = 1.4426950408889634

IN_PROJ_COLS = 256
IN_PROJ_ROWS = 512
ATT_BLOCK = 512
MIX_ROWS = 512
ROW_TILE = 256
SC_WINDOW = 32
VMEM_LIMIT = 56 * 1024 * 1024

F32 = jnp.float32
BF16 = jnp.bfloat16


def _dot(a, b):
    return jnp.dot(a, b, preferred_element_type=F32)


def _in_proj_kernel(x_ref, g_ref, w_ref, wf_ref, bf_ref, cw_ref, cb_ref,
                    yc_ref, q_ref, k_ref, v_ref, sgc_ref, sga_ref, cq_ref, ck_ref, h_scr):
    j = pl.program_id(1)
    seq = x_ref.shape[0]
    n_chunks = seq // IN_PROJ_ROWS

    @pl.when(j == 0)
    def _():
        for r in range(n_chunks):
            rows = slice(r * IN_PROJ_ROWS, (r + 1) * IN_PROJ_ROWS)
            xs = x_ref[rows, :]
            ms = jnp.mean(xs * xs, axis=-1, keepdims=True)
            h_scr[rows, :] = (xs * lax.rsqrt(ms + EPS) * g_ref[...]).astype(BF16)
        f = _dot(h_scr[...], wf_ref[...]) + bf_ref[...]
        c = jnp.minimum(f, 0.0) - jnp.log(1.0 + jnp.exp(-jnp.abs(f)))
        row = lax.broadcasted_iota(jnp.int32, c.shape, 0)
        d = 1
        while d < seq:
            c = c + jnp.where(row >= d, pltpu.roll(c, d, axis=0), 0.0)
            d *= 2
        c = c * LOG2E
        hi = c.astype(BF16).astype(F32)
        mid = (c - hi).astype(BF16).astype(F32)
        lo = (c - hi - mid).astype(BF16).astype(F32)
        grp = lax.broadcasted_iota(jnp.int32, c.shape, 1) // N_HEADS
        terms = jnp.where(grp % 3 == 0, hi, jnp.where(grp % 3 == 1, mid, lo))
        one = jnp.float32(1.0)
        cq_ref[0] = jnp.where(grp < 3, terms, jnp.where(grp < 6, one, 0.0)).astype(BF16)
        ck_ref[0] = jnp.where(grp < 3, one, jnp.where(grp < 6, -terms, 0.0)).astype(BF16)

    cw0 = cw_ref[0:1, :]
    cw1 = cw_ref[1:2, :]
    cw2 = cw_ref[2:3, :]
    cb = cb_ref[...]
    rowc = lax.broadcasted_iota(jnp.int32, (IN_PROJ_ROWS, IN_PROJ_COLS), 0)
    zprev = None
    for r in range(n_chunks):
        rows = slice(r * IN_PROJ_ROWS, (r + 1) * IN_PROJ_ROWS)
        hs = h_scr[rows, :]
        cb_gate = _dot(hs, w_ref[0])
        z = _dot(hs, w_ref[1]) * _dot(hs, w_ref[2])
        z1 = pltpu.roll(z, 1, axis=0)
        z2 = pltpu.roll(z, 2, axis=0)
        if zprev is None:
            p1 = jnp.zeros_like(z)
            p2 = p1
        else:
            p1 = pltpu.roll(zprev, 1, axis=0)
            p2 = pltpu.roll(zprev, 2, axis=0)
        z1 = jnp.where(rowc < 1, p1, z1)
        z2 = jnp.where(rowc < 2, p2, z2)
        acc = cb + cw0 * z2 + cw1 * z1 + cw2 * z
        yc_ref[rows, :] = (cb_gate * acc).astype(BF16)
        zprev = z
        q_ref[rows, :] = (_dot(hs, w_ref[3]) * (LOG2E * HEAD_DIM ** -0.5)).astype(BF16)
        k_ref[rows, :] = _dot(hs, w_ref[4]).astype(BF16)
        v_ref[rows, :] = _dot(hs, w_ref[5]).astype(BF16)
        sgc_ref[rows, :] = jax.nn.sigmoid(_dot(hs, w_ref[6])).astype(BF16)
        sga_ref[rows, :] = jax.nn.sigmoid(_dot(hs, w_ref[7])).astype(BF16)


def _in_proj(x2d, g, w_stack, wf, bfv, conv_w, conv_b, batch, seq):
    tokens = batch * seq
    tn = IN_PROJ_COLS
    nj = D_MODEL // tn
    col_out = pl.BlockSpec((seq, tn), lambda b, j: (b, j))
    out_bf16 = jax.ShapeDtypeStruct((tokens, D_MODEL), BF16)
    return pl.pallas_call(
        _in_proj_kernel,
        grid=(batch, nj),
        in_specs=[
            pl.BlockSpec((seq, D_MODEL), lambda b, j: (b, 0)),
            pl.BlockSpec((1, D_MODEL), lambda b, j: (0, 0)),
            pl.BlockSpec((8, D_MODEL, tn), lambda b, j: (0, 0, j)),
            pl.BlockSpec((D_MODEL, LANES), lambda b, j: (0, 0)),
            pl.BlockSpec((1, LANES), lambda b, j: (0, 0)),
            pl.BlockSpec((CONV_WIDTH, tn), lambda b, j: (0, j)),
            pl.BlockSpec((1, tn), lambda b, j: (0, j)),
        ],
        out_specs=[col_out] * 6 + [pl.BlockSpec((1, seq, LANES), lambda b, j: (b, 0, 0))] * 2,
        out_shape=[out_bf16] * 6 + [jax.ShapeDtypeStruct((batch, seq, LANES), BF16)] * 2,
        scratch_shapes=[pltpu.VMEM((seq, D_MODEL), BF16)],
        compiler_params=pltpu.CompilerParams(
            dimension_semantics=("arbitrary", "arbitrary"), vmem_limit_bytes=VMEM_LIMIT),
        name="in_proj",
    )(x2d, g, w_stack, wf, bfv, conv_w, conv_b)


def _attention_kernel(q_ref, k_ref, v_ref, cq_ref, ck_ref, o_ref, qa_scr, ka_scr, va_scr):
    hp = pl.program_id(1)
    seq = q_ref.shape[0]
    blk = ATT_BLOCK
    lane = lax.broadcasted_iota(jnp.int32, (1, LANES), 1)
    qpos = lax.broadcasted_iota(jnp.int32, (blk, blk), 0)
    kpos = lax.broadcasted_iota(jnp.int32, (blk, blk), 1)
    causal = kpos <= qpos
    zero = jnp.zeros((), BF16)

    own, base = [], []
    for hh in range(HEADS_PER_BLOCK):
        own.append((lane // HEAD_DIM) == hh)
        base.append(((hh + 1) % HEADS_PER_BLOCK) * HEAD_DIM)
        gate = (lane % N_HEADS) == hp * HEADS_PER_BLOCK + hh
        qa_scr[hh, :, 0:LANES] = jnp.where(own[hh], q_ref[...], zero)
        qa_scr[hh, :, LANES:2 * LANES] = jnp.where(gate, cq_ref[0], zero)
        ka_scr[hh, :, 0:LANES] = jnp.where(own[hh], k_ref[...], zero)
        ka_scr[hh, :, LANES:2 * LANES] = jnp.where(gate, ck_ref[0], zero)
        va_scr[hh] = jnp.where(own[hh], v_ref[...],
                               jnp.where(lane == base[hh], 1.0, 0.0).astype(BF16))

    for rg in range(seq // blk):
        rows = slice(rg * blk, (rg + 1) * blk)
        out = None
        for hh in range(HEADS_PER_BLOCK):
            qa = qa_scr[hh, rows, :]
            m = acc = None
            for c in range(rg + 1):
                keys = slice(c * blk, (c + 1) * blk)
                s = lax.dot_general(qa, ka_scr[hh, keys, :], (((1,), (1,)), ((), ())),
                                    preferred_element_type=F32)
                if c == rg:
                    s = jnp.where(causal, s, NEG_BIG)
                mx = jnp.max(s, axis=1, keepdims=True)
                if m is None:
                    m = mx
                    acc = _dot(jnp.exp2(s - m).astype(BF16), va_scr[hh, keys, :])
                else:
                    m_new = jnp.maximum(m, mx)
                    acc = jnp.exp2(m - m_new) * acc + _dot(jnp.exp2(s - m_new).astype(BF16),
                                                           va_scr[hh, keys, :])
                    m = m_new
            res = acc / acc[:, base[hh]:base[hh] + 1]
            out = res if out is None else jnp.where(own[hh], res, out)
        o_ref[rows, :] = out.astype(o_ref.dtype)


def _attention(q, k, v, cq, ck, batch, seq):
    tokens = batch * seq
    n_hp = N_HEADS // HEADS_PER_BLOCK
    blk = pl.BlockSpec((seq, LANES), lambda b, hp: (b, hp))
    gate_blk = pl.BlockSpec((1, seq, LANES), lambda b, hp: (b, 0, 0))
    wide = pltpu.VMEM((HEADS_PER_BLOCK, seq, 2 * LANES), BF16)
    return pl.pallas_call(
        _attention_kernel,
        grid=(batch, n_hp),
        in_specs=[blk, blk, blk, gate_blk, gate_blk],
        out_specs=blk,
        out_shape=jax.ShapeDtypeStruct((tokens, D_MODEL), BF16),
        scratch_shapes=[wide, wide, pltpu.VMEM((HEADS_PER_BLOCK, seq, LANES), BF16)],
        compiler_params=pltpu.CompilerParams(
            dimension_semantics=("arbitrary", "arbitrary"), vmem_limit_bytes=VMEM_LIMIT),
        name="fox_attention",
    )(q, k, v, cq, ck)


def _mixer_out_kernel(yc_ref, o_ref, sgc_ref, sga_ref, x_ref, wco_ref, wao_ref, wo_ref,
                      g_ref, wr_ref, x1_ref, h2_ref, route_ref, counts_ref, carry_scr):
    i = pl.program_id(0)
    tm = x_ref.shape[0]

    @pl.when(i == 0)
    def _():
        carry_scr[...] = jnp.zeros_like(carry_scr)

    y_conv = _dot(yc_ref[...], wco_ref[...])
    y_att = _dot(o_ref[...], wao_ref[...])
    m = sgc_ref[...].astype(F32) * y_conv + sga_ref[...].astype(F32) * y_att
    x1 = x_ref[...] + _dot(m.astype(BF16), wo_ref[...])
    x1_ref[...] = x1
    ms = jnp.mean(x1 * x1, axis=-1, keepdims=True)
    h2 = x1 * lax.rsqrt(ms + EPS) * g_ref[...]
    h2_ref[...] = h2
    logits = _dot(h2.astype(BF16), wr_ref[...])

    lane = lax.broadcasted_iota(jnp.int32, (tm, LANES), 1)
    far = jnp.int32(4 * LANES)

    def first_lane_of_max(vals, vmax):
        return jnp.min(jnp.where(vals == vmax, lane, far), axis=1, keepdims=True)

    lg = jnp.where(lane < N_GROUPS, logits, NEG_BIG)
    gmax = jnp.max(lg, axis=1, keepdims=True)
    g_val = 1.0 / jnp.sum(jnp.exp(lg - gmax), axis=1, keepdims=True)
    g_idx = first_lane_of_max(lg, gmax)
    lo = N_GROUPS + EXPERTS_PER_GROUP * g_idx
    le = jnp.where((lane >= lo) & (lane < lo + EXPERTS_PER_GROUP), logits, NEG_BIG)
    e1max = jnp.max(le, axis=1, keepdims=True)
    e1 = first_lane_of_max(le, e1max)
    le2 = jnp.where(lane == e1, NEG_BIG, le)
    e2max = jnp.max(le2, axis=1, keepdims=True)
    e2 = first_lane_of_max(le2, e2max)
    ratio = jnp.exp(e2max - e1max)
    w1 = g_val / (1.0 + ratio)
    w2 = g_val * ratio / (1.0 + ratio)

    oh1 = lane == e1
    oh2 = lane == e2
    onehot = jnp.where(oh1 | oh2, 1.0, 0.0).astype(BF16)
    r_i = lax.broadcasted_iota(jnp.int32, (tm, tm), 0)
    c_i = lax.broadcasted_iota(jnp.int32, (tm, tm), 1)
    strict_lower = jnp.where(c_i < r_i, 1.0, 0.0).astype(BF16)
    before = _dot(strict_lower, onehot) + carry_scr[...]
    rank1 = jnp.sum(jnp.where(oh1, before, 0.0), axis=1, keepdims=True)
    rank2 = jnp.sum(jnp.where(oh2, before, 0.0), axis=1, keepdims=True)
    total = carry_scr[...] + jnp.sum(onehot.astype(F32), axis=0, keepdims=True)
    carry_scr[...] = total
    counts_ref[...] = total

    e1f = (e1 - N_GROUPS).astype(F32)
    e2f = (e2 - N_GROUPS).astype(F32)
    route = jnp.where(lane == 0, e1f, 0.0)
    route = jnp.where(lane == 1, e2f, route)
    route = jnp.where(lane == 2, w1, route)
    route = jnp.where(lane == 3, w2, route)
    route = jnp.where(lane == 4, rank1, route)
    route = jnp.where(lane == 5, rank2, route)
    route_ref[...] = route


def _mixer_out(yc, o, sgc, sga, x2d, wco, wao, wo, g, wr):
    tokens = x2d.shape[0]
    tm = MIX_ROWS
    row_blk = pl.BlockSpec((tm, D_MODEL), lambda i: (i, 0))
    w_blk = pl.BlockSpec((D_MODEL, D_MODEL), lambda i: (0, 0))
    return pl.pallas_call(
        _mixer_out_kernel,
        grid=(tokens // tm,),
        in_specs=[row_blk, row_blk, row_blk, row_blk, row_blk, w_blk, w_blk, w_blk,
                  pl.BlockSpec((1, D_MODEL), lambda i: (0, 0)),
                  pl.BlockSpec((D_MODEL, LANES), lambda i: (0, 0))],
        out_specs=[row_blk, row_blk,
                   pl.BlockSpec((tm, LANES), lambda i: (i, 0)),
                   pl.BlockSpec((1, LANES), lambda i: (0, 0))],
        out_shape=[jax.ShapeDtypeStruct((tokens, D_MODEL), F32),
                   jax.ShapeDtypeStruct((tokens, D_MODEL), F32),
                   jax.ShapeDtypeStruct((tokens, LANES), F32),
                   jax.ShapeDtypeStruct((1, LANES), F32)],
        scratch_shapes=[pltpu.VMEM((1, LANES), F32)],
        compiler_params=pltpu.CompilerParams(
            dimension_semantics=("arbitrary",), vmem_limit_bytes=VMEM_LIMIT),
        name="mixer_out",
    )(yc, o, sgc, sga, x2d, wco, wao, wo, g, wr)


def _scatter_kernel(zero_row_ref, zero_on_ref, dest_ref, h2_ref, xrows_ref, zero_scr, sem, zsem):
    i = pl.program_id(0)
    ts = h2_ref.shape[0]

    def zero_copy(n):
        start = pl.multiple_of(zero_row_ref[n], MOE_BLOCK)
        return pltpu.make_async_copy(zero_scr, xrows_ref.at[pl.ds(start, MOE_BLOCK)], zsem)

    @pl.when(i == 0)
    def _():
        zero_scr[...] = jnp.zeros_like(zero_scr)
        for n in range(2 * N_EXPERTS):
            @pl.when(zero_on_ref[n] > 0)
            def _():
                zero_copy(n).start()
        for n in range(2 * N_EXPERTS):
            @pl.when(zero_on_ref[n] > 0)
            def _():
                zero_copy(n).wait()

    def issue(t, _):
        for kk in range(2):
            d = dest_ref[0, 0, kk * ts + t]
            pltpu.make_async_copy(h2_ref.at[pl.ds(t, 1)], xrows_ref.at[pl.ds(d, 1)], sem).start()
        return 0

    lax.fori_loop(0, ts, issue, 0)
    pltpu.make_async_copy(xrows_ref.at[pl.ds(0, 2 * ts)], xrows_ref.at[pl.ds(0, 2 * ts)], sem).wait()


def _scatter(zero_row, zero_on, dest_blocks, h2, n_rows):
    tokens = h2.shape[0]
    ts = ROW_TILE
    grid_spec = pltpu.PrefetchScalarGridSpec(
        num_scalar_prefetch=2,
        grid=(tokens // ts,),
        in_specs=[pl.BlockSpec((1, 1, 2 * ts), lambda i, *_: (i, 0, 0), memory_space=pltpu.SMEM),
                  pl.BlockSpec((ts, D_MODEL), lambda i, *_: (i, 0))],
        out_specs=pl.BlockSpec(memory_space=pl.ANY),
        scratch_shapes=[pltpu.VMEM((MOE_BLOCK, D_MODEL), F32),
                        pltpu.SemaphoreType.DMA(()), pltpu.SemaphoreType.DMA(())],
    )
    return pl.pallas_call(
        _scatter_kernel,
        grid_spec=grid_spec,
        out_shape=jax.ShapeDtypeStruct((n_rows, D_MODEL), F32),
        compiler_params=pltpu.CompilerParams(
            dimension_semantics=("arbitrary",), vmem_limit_bytes=VMEM_LIMIT),
        name="moe_scatter",
    )(zero_row, zero_on, dest_blocks, h2)


def _expert_kernel(block_e_ref, n_used_ref, x_ref, wg_ref, wu_ref, wd_ref, y_ref):
    i = pl.program_id(0)

    @pl.when(i < n_used_ref[0])
    def _():
        xb = x_ref[...].astype(BF16)
        a = _dot(xb, wg_ref[0])
        u = _dot(xb, wu_ref[0])
        hmid = (a * jax.nn.sigmoid(a) * u).astype(BF16)
        y_ref[...] = _dot(hmid, wd_ref[0])

    @pl.when(i >= n_used_ref[0])
    def _():
        y_ref[...] = jnp.zeros_like(y_ref)


def _experts(block_e, n_used, x_rows, wg, wu, wd):
    n_rows = x_rows.shape[0]
    n_blocks = n_rows // MOE_BLOCK

    def blk(i, block_e_ref, n_used_ref):
        return jnp.minimum(i, n_used_ref[0] - 1)

    def row_map(i, be, nu):
        return (blk(i, be, nu), 0)

    def w_map(i, be, nu):
        return (be[blk(i, be, nu)], 0, 0)

    grid_spec = pltpu.PrefetchScalarGridSpec(
        num_scalar_prefetch=2,
        grid=(n_blocks,),
        in_specs=[pl.BlockSpec((MOE_BLOCK, D_MODEL), row_map),
                  pl.BlockSpec((1, D_MODEL, D_EXPERT), w_map),
                  pl.BlockSpec((1, D_MODEL, D_EXPERT), w_map),
                  pl.BlockSpec((1, D_EXPERT, D_MODEL), w_map)],
        out_specs=pl.BlockSpec((MOE_BLOCK, D_MODEL), lambda i, be, nu: (i, 0)),
    )
    return pl.pallas_call(
        _expert_kernel,
        grid_spec=grid_spec,
        out_shape=jax.ShapeDtypeStruct((n_rows, D_MODEL), F32),
        compiler_params=pltpu.CompilerParams(
            dimension_semantics=("arbitrary",), vmem_limit_bytes=VMEM_LIMIT),
        name="moe_experts",
    )(block_e, n_used, x_rows, wg, wu, wd)


def _sc_gather_rows(data, indices):
    n = indices.shape[0]
    width = data.shape[1]
    mesh = plsc.VectorSubcoreMesh(core_axis_name="c", subcore_axis_name="s")

    @pl.kernel(out_type=jax.ShapeDtypeStruct((n, width), data.dtype), mesh=mesh)
    def gather_kernel(x_hbm, i_hbm, o_hbm):
        def body(i_vmem, o_vmem):
            pltpu.sync_copy(x_hbm.at[i_vmem.at[0, pl.ds(0, SC_WINDOW)]], o_vmem)

        pltpu.emit_pipeline(
            body,
            grid=(n // SC_WINDOW,),
            in_specs=[pl.BlockSpec((1, LANES), lambda i: (i, 0))],
            out_specs=[pl.BlockSpec((SC_WINDOW, width), lambda i: (i, 0))],
            core_axis_name=("c", "s"),
            dimension_semantics=(pltpu.PARALLEL,),
        )(i_hbm, o_hbm)

    idx = jnp.pad(indices.reshape(n // SC_WINDOW, SC_WINDOW), ((0, 0), (0, LANES - SC_WINDOW)))
    return gather_kernel(data, idx)


def _combine_kernel(ya_ref, yb_ref, x1_ref, route_ref, g_ref, out_ref):
    route = route_ref[...]
    w1 = route[:, 2:3]
    w2 = route[:, 3:4]
    x2 = x1_ref[...] + (ya_ref[...] * w1 + yb_ref[...] * w2)
    ms = jnp.mean(x2 * x2, axis=-1, keepdims=True)
    out_ref[...] = x2 * lax.rsqrt(ms + EPS) * g_ref[...]


def _combine(y_pairs, x1, route, g):
    tokens = x1.shape[0]
    tc = MIX_ROWS
    n_tiles = tokens // tc
    row_blk = pl.BlockSpec((tc, D_MODEL), lambda i: (i, 0))
    return pl.pallas_call(
        _combine_kernel,
        grid=(n_tiles,),
        in_specs=[row_blk,
                  pl.BlockSpec((tc, D_MODEL), lambda i: (i + n_tiles, 0)),
                  row_blk,
                  pl.BlockSpec((tc, LANES), lambda i: (i, 0)),
                  pl.BlockSpec((1, D_MODEL), lambda i: (0, 0))],
        out_specs=row_blk,
        out_shape=jax.ShapeDtypeStruct((tokens, D_MODEL), F32),
        compiler_params=pltpu.CompilerParams(
            dimension_semantics=("arbitrary",), vmem_limit_bytes=VMEM_LIMIT),
        name="moe_combine",
    )(y_pairs, y_pairs, x1, route, g)


def _dest_blocks(dest, tile):
    tokens = dest.shape[0]
    return dest.reshape(tokens // tile, tile, 2).transpose(0, 2, 1).reshape(tokens // tile, 1, 2 * tile)


def kernel(x, norm_mix_g, w_in, conv_w, conv_b, b_forget, w_conv_out, w_att_out, w_out,
           norm_ffn_g, w_router_group, w_router_expert, w_e_gate, w_e_up, w_e_down,
           norm_final_g):
    batch, seq, d = x.shape
    assert d == D_MODEL and seq % max(ATT_BLOCK, IN_PROJ_ROWS) == 0
    tokens = batch * seq
    assert tokens % MIX_ROWS == 0 and tokens % ROW_TILE == 0
    x2d = x.reshape(tokens, D_MODEL)

    wb = w_in.astype(BF16)
    n_main = 6 * D_MODEL
    gate0 = n_main + N_HEADS
    pieces = [wb[:, i * D_MODEL:(i + 1) * D_MODEL] for i in range(6)]
    pieces += [wb[:, gate0:gate0 + D_MODEL], wb[:, gate0 + D_MODEL:gate0 + 2 * D_MODEL]]
    w_stack = jnp.stack(pieces, axis=0)
    n_rep = 6
    wf = jnp.pad(jnp.tile(wb[:, n_main:gate0], (1, n_rep)), ((0, 0), (0, LANES - n_rep * N_HEADS)))
    bfv = jnp.pad(jnp.tile(b_forget.astype(F32), n_rep), (0, LANES - n_rep * N_HEADS)).reshape(1, LANES)
    wr = jnp.concatenate(
        [w_router_group, jnp.transpose(w_router_expert, (1, 0, 2)).reshape(D_MODEL, N_EXPERTS)], axis=1)
    wr = jnp.pad(wr, ((0, 0), (0, LANES - wr.shape[1]))).astype(BF16)

    yc, q, k, v, sgc, sga, cq, ck = _in_proj(
        x2d, norm_mix_g.reshape(1, D_MODEL), w_stack, wf, bfv, conv_w,
        conv_b.reshape(1, D_MODEL), batch, seq)
    o = _attention(q, k, v, cq, ck, batch, seq)

    x1, h2, route, counts = _mixer_out(
        yc, o, sgc, sga, x2d, w_conv_out.astype(BF16), w_att_out.astype(BF16),
        w_out.astype(BF16), norm_ffn_g.reshape(1, D_MODEL), wr)

    counts = counts[0, N_GROUPS:N_GROUPS + N_EXPERTS].astype(jnp.int32)
    padded = (counts + MOE_BLOCK - 1) // MOE_BLOCK * MOE_BLOCK
    end_padded = jnp.cumsum(padded)
    start_padded = end_padded - padded
    expert = route[:, 0:2].astype(jnp.int32)
    rank = route[:, 4:6].astype(jnp.int32)
    dest = start_padded[expert] + rank
    n_rows = tokens * 2 + N_EXPERTS * MOE_BLOCK
    n_blocks = n_rows // MOE_BLOCK
    n_used = (end_padded[-1] // MOE_BLOCK).astype(jnp.int32).reshape(1)
    block_start = jnp.arange(n_blocks, dtype=jnp.int32) * MOE_BLOCK
    block_e = jnp.minimum(jnp.sum(end_padded[None, :] <= block_start[:, None], axis=1),
                          N_EXPERTS - 1).astype(jnp.int32)
    tail_blk = n_used[0] + jnp.arange(N_EXPERTS, dtype=jnp.int32)
    zero_row = jnp.concatenate([jnp.maximum(end_padded - MOE_BLOCK, 0).astype(jnp.int32),
                                jnp.minimum(tail_blk, n_blocks - 1) * MOE_BLOCK])
    zero_on = jnp.concatenate([counts > 0, tail_blk < n_blocks]).astype(jnp.int32)
    dest_blocks = _dest_blocks(dest.astype(jnp.int32), ROW_TILE)

    x_rows = _scatter(zero_row, zero_on, dest_blocks, h2, n_rows)
    y_rows = _experts(block_e, n_used, x_rows, w_e_gate.astype(BF16), w_e_up.astype(BF16),
                      w_e_down.astype(BF16))
    dest_flat = jnp.transpose(dest.astype(jnp.int32)).reshape(2 * tokens)
    y_pairs = _sc_gather_rows(y_rows, dest_flat)
    out = _combine(y_pairs, x1, route, norm_final_g.reshape(1, D_MODEL))
    return out.reshape(batch, seq, D_MODEL)
```

```python
import jax
import jax.numpy as jnp
from jax import lax
from jax.experimental import pallas as pl
from jax.experimental.pallas import tpu as pltpu
from jax.experimental.pallas import tpu_sc as plsc

D_MODEL = 1024
HEAD_DIM = 64
N_HEADS = 16
N_GROUPS = 4
EXPERTS_PER_GROUP = 8
N_EXPERTS = N_GROUPS * EXPERTS_PER_GROUP
D_EXPERT = 512
MOE_BLOCK = 256
CONV_WIDTH = 3
EPS = 1e-6

LANES = 128
HEADS_PER_BLOCK = LANES // HEAD_DIM
NEG_BIG = -1e30
LOG2E = 1.4426950408889634

IN_PROJ_COLS = 256
IN_PROJ_ROWS = 512
ATT_BLOCK = 512
MIX_ROWS = 512
SC_WINDOW = 32
VMEM_LIMIT = 56 * 1024 * 1024

F32 = jnp.float32
BF16 = jnp.bfloat16


def _dot(a, b):
    return jnp.dot(a, b, preferred_element_type=F32)


def _in_proj_kernel(x_ref, g_ref, w_ref, wf_ref, bf_ref, cw_ref, cb_ref,
                    yc_ref, q_ref, k_ref, v_ref, sgc_ref, sga_ref, cq_ref, ck_ref, h_scr):
    j = pl.program_id(1)
    seq = x_ref.shape[0]
    n_chunks = seq // IN_PROJ_ROWS

    @pl.when(j == 0)
    def _():
        for r in range(n_chunks):
            rows = slice(r * IN_PROJ_ROWS, (r + 1) * IN_PROJ_ROWS)
            xs = x_ref[rows, :]
            ms = jnp.mean(xs * xs, axis=-1, keepdims=True)
            h_scr[rows, :] = (xs * lax.rsqrt(ms + EPS) * g_ref[...]).astype(BF16)
        f = _dot(h_scr[...], wf_ref[...]) + bf_ref[...]
        c = jnp.minimum(f, 0.0) - jnp.log(1.0 + jnp.exp(-jnp.abs(f)))
        row = lax.broadcasted_iota(jnp.int32, c.shape, 0)
        d = 1
        while d < seq:
            c = c + jnp.where(row >= d, pltpu.roll(c, d, axis=0), 0.0)
            d *= 2
        c = c * LOG2E
        hi = c.astype(BF16).astype(F32)
        mid = (c - hi).astype(BF16).astype(F32)
        lo = (c - hi - mid).astype(BF16).astype(F32)
        grp = lax.broadcasted_iota(jnp.int32, c.shape, 1) // N_HEADS
        terms = jnp.where(grp % 3 == 0, hi, jnp.where(grp % 3 == 1, mid, lo))
        one = jnp.float32(1.0)
        cq_ref[0] = jnp.where(grp < 3, terms, jnp.where(grp < 6, one, 0.0)).astype(BF16)
        ck_ref[0] = jnp.where(grp < 3, one, jnp.where(grp < 6, -terms, 0.0)).astype(BF16)

    cw0 = cw_ref[0:1, :]
    cw1 = cw_ref[1:2, :]
    cw2 = cw_ref[2:3, :]
    cb = cb_ref[...]
    rowc = lax.broadcasted_iota(jnp.int32, (IN_PROJ_ROWS, IN_PROJ_COLS), 0)
    zprev = None
    for r in range(n_chunks):
        rows = slice(r * IN_PROJ_ROWS, (r + 1) * IN_PROJ_ROWS)
        hs = h_scr[rows, :]
        cb_gate = _dot(hs, w_ref[0])
        z = _dot(hs, w_ref[1]) * _dot(hs, w_ref[2])
        z1 = pltpu.roll(z, 1, axis=0)
        z2 = pltpu.roll(z, 2, axis=0)
        if zprev is None:
            p1 = jnp.zeros_like(z)
            p2 = p1
        else:
            p1 = pltpu.roll(zprev, 1, axis=0)
            p2 = pltpu.roll(zprev, 2, axis=0)
        z1 = jnp.where(rowc < 1, p1, z1)
        z2 = jnp.where(rowc < 2, p2, z2)
        acc = cb + cw0 * z2 + cw1 * z1 + cw2 * z
        yc_ref[rows, :] = (cb_gate * acc).astype(BF16)
        zprev = z
        q_ref[rows, :] = (_dot(hs, w_ref[3]) * (LOG2E * HEAD_DIM ** -0.5)).astype(BF16)
        k_ref[rows, :] = _dot(hs, w_ref[4]).astype(BF16)
        v_ref[rows, :] = _dot(hs, w_ref[5]).astype(BF16)
        sgc_ref[rows, :] = jax.nn.sigmoid(_dot(hs, w_ref[6])).astype(BF16)
        sga_ref[rows, :] = jax.nn.sigmoid(_dot(hs, w_ref[7])).astype(BF16)


def _in_proj(x2d, g, w_stack, wf, bfv, conv_w, conv_b, batch, seq):
    tokens = batch * seq
    tn = IN_PROJ_COLS
    nj = D_MODEL // tn
    col_out = pl.BlockSpec((seq, tn), lambda b, j: (b, j))
    out_bf16 = jax.ShapeDtypeStruct((tokens, D_MODEL), BF16)
    return pl.pallas_call(
        _in_proj_kernel,
        grid=(batch, nj),
        in_specs=[
            pl.BlockSpec((seq, D_MODEL), lambda b, j: (b, 0)),
            pl.BlockSpec((1, D_MODEL), lambda b, j: (0, 0)),
            pl.BlockSpec((8, D_MODEL, tn), lambda b, j: (0, 0, j)),
            pl.BlockSpec((D_MODEL, LANES), lambda b, j: (0, 0)),
            pl.BlockSpec((1, LANES), lambda b, j: (0, 0)),
            pl.BlockSpec((CONV_WIDTH, tn), lambda b, j: (0, j)),
            pl.BlockSpec((1, tn), lambda b, j: (0, j)),
        ],
        out_specs=[col_out] * 6 + [pl.BlockSpec((1, seq, LANES), lambda b, j: (b, 0, 0))] * 2,
        out_shape=[out_bf16] * 6 + [jax.ShapeDtypeStruct((batch, seq, LANES), BF16)] * 2,
        scratch_shapes=[pltpu.VMEM((seq, D_MODEL), BF16)],
        compiler_params=pltpu.CompilerParams(
            dimension_semantics=("arbitrary", "arbitrary"), vmem_limit_bytes=VMEM_LIMIT),
        name="in_proj",
    )(x2d, g, w_stack, wf, bfv, conv_w, conv_b)


def _attention_kernel(q_ref, k_ref, v_ref, cq_ref, ck_ref, o_ref, qa_scr, ka_scr, va_scr):
    hp = pl.program_id(1)
    seq = q_ref.shape[0]
    blk = ATT_BLOCK
    lane = lax.broadcasted_iota(jnp.int32, (1, LANES), 1)
    qpos = lax.broadcasted_iota(jnp.int32, (blk, blk), 0)
    kpos = lax.broadcasted_iota(jnp.int32, (blk, blk), 1)
    causal = kpos <= qpos
    zero = jnp.zeros((), BF16)

    own, base = [], []
    for hh in range(HEADS_PER_BLOCK):
        own.append((lane // HEAD_DIM) == hh)
        base.append(((hh + 1) % HEADS_PER_BLOCK) * HEAD_DIM)
        gate = (lane % N_HEADS) == hp * HEADS_PER_BLOCK + hh
        qa_scr[hh, :, 0:LANES] = jnp.where(own[hh], q_ref[...], zero)
        qa_scr[hh, :, LANES:2 * LANES] = jnp.where(gate, cq_ref[0], zero)
        ka_scr[hh, :, 0:LANES] = jnp.where(own[hh], k_ref[...], zero)
        ka_scr[hh, :, LANES:2 * LANES] = jnp.where(gate, ck_ref[0], zero)
        va_scr[hh] = jnp.where(own[hh], v_ref[...],
                               jnp.where(lane == base[hh], 1.0, 0.0).astype(BF16))

    for rg in range(seq // blk):
        rows = slice(rg * blk, (rg + 1) * blk)
        out = None
        for hh in range(HEADS_PER_BLOCK):
            qa = qa_scr[hh, rows, :]
            m = acc = None
            for c in range(rg + 1):
                keys = slice(c * blk, (c + 1) * blk)
                s = lax.dot_general(qa, ka_scr[hh, keys, :], (((1,), (1,)), ((), ())),
                                    preferred_element_type=F32)
                if c == rg:
                    s = jnp.where(causal, s, NEG_BIG)
                mx = jnp.max(s, axis=1, keepdims=True)
                if m is None:
                    m = mx
                    acc = _dot(jnp.exp2(s - m).astype(BF16), va_scr[hh, keys, :])
                else:
                    m_new = jnp.maximum(m, mx)
                    acc = jnp.exp2(m - m_new) * acc + _dot(jnp.exp2(s - m_new).astype(BF16),
                                                           va_scr[hh, keys, :])
                    m = m_new
            res = acc / acc[:, base[hh]:base[hh] + 1]
            out = res if out is None else jnp.where(own[hh], res, out)
        o_ref[rows, :] = out.astype(o_ref.dtype)


def _attention(q, k, v, cq, ck, batch, seq):
    tokens = batch * seq
    n_hp = N_HEADS // HEADS_PER_BLOCK
    blk = pl.BlockSpec((seq, LANES), lambda b, hp: (b, hp))
    gate_blk = pl.BlockSpec((1, seq, LANES), lambda b, hp: (b, 0, 0))
    wide = pltpu.VMEM((HEADS_PER_BLOCK, seq, 2 * LANES), BF16)
    return pl.pallas_call(
        _attention_kernel,
        grid=(batch, n_hp),
        in_specs=[blk, blk, blk, gate_blk, gate_blk],
        out_specs=blk,
        out_shape=jax.ShapeDtypeStruct((tokens, D_MODEL), BF16),
        scratch_shapes=[wide, wide, pltpu.VMEM((HEADS_PER_BLOCK, seq, LANES), BF16)],
        compiler_params=pltpu.CompilerParams(
            dimension_semantics=("arbitrary", "arbitrary"), vmem_limit_bytes=VMEM_LIMIT),
        name="fox_attention",
    )(q, k, v, cq, ck)


def _mixer_out_kernel(yc_ref, o_ref, sgc_ref, sga_ref, x_ref, wco_ref, wao_ref, wo_ref,
                      g_ref, wr_ref, x1_ref, h2_ref, route_ref, counts_ref, carry_scr):
    i = pl.program_id(0)
    tm = x_ref.shape[0]

    @pl.when(i == 0)
    def _():
        carry_scr[...] = jnp.zeros_like(carry_scr)

    y_conv = _dot(yc_ref[...], wco_ref[...])
    y_att = _dot(o_ref[...], wao_ref[...])
    m = sgc_ref[...].astype(F32) * y_conv + sga_ref[...].astype(F32) * y_att
    x1 = x_ref[...] + _dot(m.astype(BF16), wo_ref[...])
    x1_ref[...] = x1
    ms = jnp.mean(x1 * x1, axis=-1, keepdims=True)
    h2 = x1 * lax.rsqrt(ms + EPS) * g_ref[...]
    h2_ref[...] = h2
    logits = _dot(h2.astype(BF16), wr_ref[...])

    lane = lax.broadcasted_iota(jnp.int32, (tm, LANES), 1)
    far = jnp.int32(4 * LANES)

    def first_lane_of_max(vals, vmax):
        return jnp.min(jnp.where(vals == vmax, lane, far), axis=1, keepdims=True)

    lg = jnp.where(lane < N_GROUPS, logits, NEG_BIG)
    gmax = jnp.max(lg, axis=1, keepdims=True)
    g_val = 1.0 / jnp.sum(jnp.exp(lg - gmax), axis=1, keepdims=True)
    g_idx = first_lane_of_max(lg, gmax)
    lo = N_GROUPS + EXPERTS_PER_GROUP * g_idx
    le = jnp.where((lane >= lo) & (lane < lo + EXPERTS_PER_GROUP), logits, NEG_BIG)
    e1max = jnp.max(le, axis=1, keepdims=True)
    e1 = first_lane_of_max(le, e1max)
    le2 = jnp.where(lane == e1, NEG_BIG, le)
    e2max = jnp.max(le2, axis=1, keepdims=True)
    e2 = first_lane_of_max(le2, e2max)
    ratio = jnp.exp(e2max - e1max)
    w1 = g_val / (1.0 + ratio)
    w2 = g_val * ratio / (1.0 + ratio)

    oh1 = lane == e1
    oh2 = lane == e2
    onehot = jnp.where(oh1 | oh2, 1.0, 0.0).astype(BF16)
    r_i = lax.broadcasted_iota(jnp.int32, (tm, tm), 0)
    c_i = lax.broadcasted_iota(jnp.int32, (tm, tm), 1)
    strict_lower = jnp.where(c_i < r_i, 1.0, 0.0).astype(BF16)
    before = _dot(strict_lower, onehot) + carry_scr[...]
    rank1 = jnp.sum(jnp.where(oh1, before, 0.0), axis=1, keepdims=True)
    rank2 = jnp.sum(jnp.where(oh2, before, 0.0), axis=1, keepdims=True)
    total = carry_scr[...] + jnp.sum(onehot.astype(F32), axis=0, keepdims=True)
    carry_scr[...] = total
    counts_ref[...] = total

    e1f = (e1 - N_GROUPS).astype(F32)
    e2f = (e2 - N_GROUPS).astype(F32)
    route = jnp.where(lane == 0, e1f, 0.0)
    route = jnp.where(lane == 1, e2f, route)
    route = jnp.where(lane == 2, w1, route)
    route = jnp.where(lane == 3, w2, route)
    route = jnp.where(lane == 4, rank1, route)
    route = jnp.where(lane == 5, rank2, route)
    route_ref[...] = route


def _mixer_out(yc, o, sgc, sga, x2d, wco, wao, wo, g, wr):
    tokens = x2d.shape[0]
    tm = MIX_ROWS
    row_blk = pl.BlockSpec((tm, D_MODEL), lambda i: (i, 0))
    w_blk = pl.BlockSpec((D_MODEL, D_MODEL), lambda i: (0, 0))
    return pl.pallas_call(
        _mixer_out_kernel,
        grid=(tokens // tm,),
        in_specs=[row_blk, row_blk, row_blk, row_blk, row_blk, w_blk, w_blk, w_blk,
                  pl.BlockSpec((1, D_MODEL), lambda i: (0, 0)),
                  pl.BlockSpec((D_MODEL, LANES), lambda i: (0, 0))],
        out_specs=[row_blk, row_blk,
                   pl.BlockSpec((tm, LANES), lambda i: (i, 0)),
                   pl.BlockSpec((1, LANES), lambda i: (0, 0))],
        out_shape=[jax.ShapeDtypeStruct((tokens, D_MODEL), F32),
                   jax.ShapeDtypeStruct((tokens, D_MODEL), F32),
                   jax.ShapeDtypeStruct((tokens, LANES), F32),
                   jax.ShapeDtypeStruct((1, LANES), F32)],
        scratch_shapes=[pltpu.VMEM((1, LANES), F32)],
        compiler_params=pltpu.CompilerParams(
            dimension_semantics=("arbitrary",), vmem_limit_bytes=VMEM_LIMIT),
        name="mixer_out",
    )(yc, o, sgc, sga, x2d, wco, wao, wo, g, wr)


def _sc_index_rows(indices):
    n = indices.shape[0]
    return jnp.pad(indices.reshape(n // SC_WINDOW, SC_WINDOW), ((0, 0), (0, LANES - SC_WINDOW)))


def _sc_mesh():
    return plsc.VectorSubcoreMesh(core_axis_name="c", subcore_axis_name="s")


def _sc_scatter_rows(data, indices, n_out):
    n = indices.shape[0]
    n_src, width = data.shape
    src_blocks = n_src // SC_WINDOW

    @pl.kernel(out_type=jax.ShapeDtypeStruct((n_out, width), data.dtype), mesh=_sc_mesh())
    def scatter_kernel(x_hbm, i_hbm, o_hbm):
        def body(x_vmem, i_vmem):
            pltpu.sync_copy(x_vmem, o_hbm.at[i_vmem.at[0, pl.ds(0, SC_WINDOW)]])

        pltpu.emit_pipeline(
            body,
            grid=(n // SC_WINDOW,),
            in_specs=[pl.BlockSpec((SC_WINDOW, width), lambda i: (i % src_blocks, 0)),
                      pl.BlockSpec((1, LANES), lambda i: (i, 0))],
            out_specs=[],
            core_axis_name=("c", "s"),
            dimension_semantics=(pltpu.PARALLEL,),
        )(x_hbm, i_hbm)

    return scatter_kernel(data, _sc_index_rows(indices))


def _sc_gather_rows(data, indices):
    n = indices.shape[0]
    width = data.shape[1]

    @pl.kernel(out_type=jax.ShapeDtypeStruct((n, width), data.dtype), mesh=_sc_mesh())
    def gather_kernel(x_hbm, i_hbm, o_hbm):
        def body(i_vmem, o_vmem):
            pltpu.sync_copy(x_hbm.at[i_vmem.at[0, pl.ds(0, SC_WINDOW)]], o_vmem)

        pltpu.emit_pipeline(
            body,
            grid=(n // SC_WINDOW,),
            in_specs=[pl.BlockSpec((1, LANES), lambda i: (i, 0))],
            out_specs=[pl.BlockSpec((SC_WINDOW, width), lambda i: (i, 0))],
            core_axis_name=("c", "s"),
            dimension_semantics=(pltpu.PARALLEL,),
        )(i_hbm, o_hbm)

    return gather_kernel(data, _sc_index_rows(indices))


def _expert_kernel(block_e_ref, n_valid_ref, x_ref, wg_ref, wu_ref, wd_ref, y_ref):
    i = pl.program_id(0)
    n_valid = n_valid_ref[i]

    @pl.when(n_valid > 0)
    def _():
        row = lax.broadcasted_iota(jnp.int32, x_ref.shape, 0)
        xb = jnp.where(row < n_valid, x_ref[...], 0.0).astype(BF16)
        a = _dot(xb, wg_ref[0])
        u = _dot(xb, wu_ref[0])
        hmid = (a * jax.nn.sigmoid(a) * u).astype(BF16)
        y_ref[...] = _dot(hmid, wd_ref[0])

    @pl.when(n_valid == 0)
    def _():
        y_ref[...] = jnp.zeros_like(y_ref)


def _experts(block_e, n_valid, x_rows, wg, wu, wd):
    n_rows = x_rows.shape[0]
    n_blocks = n_rows // MOE_BLOCK

    def w_map(i, be, nv):
        return (be[i], 0, 0)

    row_blk = pl.BlockSpec((MOE_BLOCK, D_MODEL), lambda i, be, nv: (i, 0))
    grid_spec = pltpu.PrefetchScalarGridSpec(
        num_scalar_prefetch=2,
        grid=(n_blocks,),
        in_specs=[row_blk,
                  pl.BlockSpec((1, D_MODEL, D_EXPERT), w_map),
                  pl.BlockSpec((1, D_MODEL, D_EXPERT), w_map),
                  pl.BlockSpec((1, D_EXPERT, D_MODEL), w_map)],
        out_specs=row_blk,
    )
    return pl.pallas_call(
        _expert_kernel,
        grid_spec=grid_spec,
        out_shape=jax.ShapeDtypeStruct((n_rows, D_MODEL), F32),
        compiler_params=pltpu.CompilerParams(
            dimension_semantics=("arbitrary",), vmem_limit_bytes=VMEM_LIMIT),
        name="moe_experts",
    )(block_e, n_valid, x_rows, wg, wu, wd)


def _combine_kernel(ya_ref, yb_ref, x1_ref, route_ref, g_ref, out_ref):
    route = route_ref[...]
    w1 = route[:, 2:3]
    w2 = route[:, 3:4]
    x2 = x1_ref[...] + (ya_ref[...] * w1 + yb_ref[...] * w2)
    ms = jnp.mean(x2 * x2, axis=-1, keepdims=True)
    out_ref[...] = x2 * lax.rsqrt(ms + EPS) * g_ref[...]


def _combine(y_pairs, x1, route, g):
    tokens = x1.shape[0]
    tc = MIX_ROWS
    n_tiles = tokens // tc
    row_blk = pl.BlockSpec((tc, D_MODEL), lambda i: (i, 0))
    return pl.pallas_call(
        _combine_kernel,
        grid=(n_tiles,),
        in_specs=[row_blk,
                  pl.BlockSpec((tc, D_MODEL), lambda i: (i + n_tiles, 0)),
                  row_blk,
                  pl.BlockSpec((tc, LANES), lambda i: (i, 0)),
                  pl.BlockSpec((1, D_MODEL), lambda i: (0, 0))],
        out_specs=row_blk,
        out_shape=jax.ShapeDtypeStruct((tokens, D_MODEL), F32),
        compiler_params=pltpu.CompilerParams(
            dimension_semantics=("arbitrary",), vmem_limit_bytes=VMEM_LIMIT),
        name="moe_combine",
    )(y_pairs, y_pairs, x1, route, g)


def kernel(x, norm_mix_g, w_in, conv_w, conv_b, b_forget, w_conv_out, w_att_out, w_out,
           norm_ffn_g, w_router_group, w_router_expert, w_e_gate, w_e_up, w_e_down,
           norm_final_g):
    batch, seq, d = x.shape
    assert d == D_MODEL and seq % max(ATT_BLOCK, IN_PROJ_ROWS) == 0
    tokens = batch * seq
    assert tokens % MIX_ROWS == 0 and tokens % SC_WINDOW == 0
    x2d = x.reshape(tokens, D_MODEL)

    wb = w_in.astype(BF16)
    n_main = 6 * D_MODEL
    gate0 = n_main + N_HEADS
    pieces = [wb[:, i * D_MODEL:(i + 1) * D_MODEL] for i in range(6)]
    pieces += [wb[:, gate0:gate0 + D_MODEL], wb[:, gate0 + D_MODEL:gate0 + 2 * D_MODEL]]
    w_stack = jnp.stack(pieces, axis=0)
    n_rep = 6
    wf = jnp.pad(jnp.tile(wb[:, n_main:gate0], (1, n_rep)), ((0, 0), (0, LANES - n_rep * N_HEADS)))
    bfv = jnp.pad(jnp.tile(b_forget.astype(F32), n_rep), (0, LANES - n_rep * N_HEADS)).reshape(1, LANES)
    wr = jnp.concatenate(
        [w_router_group, jnp.transpose(w_router_expert, (1, 0, 2)).reshape(D_MODEL, N_EXPERTS)], axis=1)
    wr = jnp.pad(wr, ((0, 0), (0, LANES - wr.shape[1]))).astype(BF16)

    yc, q, k, v, sgc, sga, cq, ck = _in_proj(
        x2d, norm_mix_g.reshape(1, D_MODEL), w_stack, wf, bfv, conv_w,
        conv_b.reshape(1, D_MODEL), batch, seq)
    o = _attention(q, k, v, cq, ck, batch, seq)

    x1, h2, route, counts = _mixer_out(
        yc, o, sgc, sga, x2d, w_conv_out.astype(BF16), w_att_out.astype(BF16),
        w_out.astype(BF16), norm_ffn_g.reshape(1, D_MODEL), wr)

    counts = counts[0, N_GROUPS:N_GROUPS + N_EXPERTS].astype(jnp.int32)
    padded = (counts + MOE_BLOCK - 1) // MOE_BLOCK * MOE_BLOCK
    end_padded = jnp.cumsum(padded)
    start_padded = end_padded - padded
    expert = route[:, 0:2].astype(jnp.int32)
    rank = route[:, 4:6].astype(jnp.int32)
    dest = start_padded[expert] + rank
    dest_flat = jnp.transpose(dest).reshape(2 * tokens)
    n_rows = tokens * 2 + N_EXPERTS * MOE_BLOCK
    block_start = jnp.arange(n_rows // MOE_BLOCK, dtype=jnp.int32) * MOE_BLOCK
    block_e = jnp.minimum(jnp.sum(end_padded[None, :] <= block_start[:, None], axis=1),
                          N_EXPERTS - 1).astype(jnp.int32)
    n_valid = jnp.clip(start_padded[block_e] + counts[block_e] - block_start, 0, MOE_BLOCK).astype(jnp.int32)

    x_rows = _sc_scatter_rows(h2, dest_flat, n_rows)
    y_rows = _experts(block_e, n_valid, x_rows, w_e_gate.astype(BF16), w_e_up.astype(BF16),
                      w_e_down.astype(BF16))
    y_pairs = _sc_gather_rows(y_rows, dest_flat)
    out = _combine(y_pairs, x1, route, norm_final_g.reshape(1, D_MODEL))
    return out.reshape(batch, seq, D_MODEL)
```

```python
import jax
import jax.numpy as jnp
from jax import lax
from jax.experimental import pallas as pl
from jax.experimental.pallas import tpu as pltpu
from jax.experimental.pallas import tpu_sc as plsc

D_MODEL = 1024
HEAD_DIM = 64
N_HEADS = 16
N_GROUPS = 4
EXPERTS_PER_GROUP = 8
N_EXPERTS = N_GROUPS * EXPERTS_PER_GROUP
D_EXPERT = 512
MOE_BLOCK = 256
CONV_WIDTH = 3
EPS = 1e-6

LANES = 128
HEADS_PER_BLOCK = LANES // HEAD_DIM
NEG_BIG = -1e30
LOG2E = 1.4426950408889634

IN_PROJ_COLS = 256
IN_PROJ_ROWS = 512
ATT_BLOCK = 512
MIX_ROWS = 512
MIX_SUB = 256
SC_WINDOW = 64
VMEM_LIMIT = 56 * 1024 * 1024

F32 = jnp.float32
BF16 = jnp.bfloat16


def _dot(a, b):
    return jnp.dot(a, b, preferred_element_type=F32)


def _pack_bf16_pairs(x):
    c = x.shape[1] // 2
    lo = lax.bitcast_convert_type(x[:, :c].astype(BF16).astype(F32), jnp.uint32) >> 16
    hi = lax.bitcast_convert_type(x[:, c:].astype(BF16).astype(F32), jnp.uint32) & jnp.uint32(0xFFFF0000)
    return lo | hi


def _unpack_bf16_pairs(w):
    lo = lax.bitcast_convert_type(w << 16, F32)
    hi = lax.bitcast_convert_type(w & jnp.uint32(0xFFFF0000), F32)
    return jnp.concatenate([lo, hi], axis=1)


def _in_proj_kernel(x_ref, g_ref, w_ref, wf_ref, bf_ref, cw_ref, cb_ref,
                    yc_ref, q_ref, k_ref, v_ref, sgc_ref, sga_ref, cq_ref, ck_ref, h_scr):
    j = pl.program_id(1)
    seq = x_ref.shape[0]
    n_chunks = seq // IN_PROJ_ROWS

    @pl.when(j == 0)
    def _():
        for r in range(n_chunks):
            rows = slice(r * IN_PROJ_ROWS, (r + 1) * IN_PROJ_ROWS)
            xs = x_ref[rows, :]
            ms = jnp.mean(xs * xs, axis=-1, keepdims=True)
            h_scr[rows, :] = (xs * lax.rsqrt(ms + EPS) * g_ref[...]).astype(BF16)
        f = _dot(h_scr[...], wf_ref[...]) + bf_ref[...]
        c = jnp.minimum(f, 0.0) - jnp.log(1.0 + jnp.exp(-jnp.abs(f)))
        row = lax.broadcasted_iota(jnp.int32, c.shape, 0)
        d = 1
        while d < seq:
            c = c + jnp.where(row >= d, pltpu.roll(c, d, axis=0), 0.0)
            d *= 2
        c = c * LOG2E
        hi = c.astype(BF16).astype(F32)
        mid = (c - hi).astype(BF16).astype(F32)
        lo = (c - hi - mid).astype(BF16).astype(F32)
        grp = lax.broadcasted_iota(jnp.int32, c.shape, 1) // N_HEADS
        terms = jnp.where(grp % 3 == 0, hi, jnp.where(grp % 3 == 1, mid, lo))
        one = jnp.float32(1.0)
        cq_ref[0] = jnp.where(grp < 3, terms, jnp.where(grp < 6, one, 0.0)).astype(BF16)
        ck_ref[0] = jnp.where(grp < 3, one, jnp.where(grp < 6, -terms, 0.0)).astype(BF16)

    cw0 = cw_ref[0:1, :]
    cw1 = cw_ref[1:2, :]
    cw2 = cw_ref[2:3, :]
    cb = cb_ref[...]
    rowc = lax.broadcasted_iota(jnp.int32, (IN_PROJ_ROWS, IN_PROJ_COLS), 0)
    zprev = None
    for r in range(n_chunks):
        rows = slice(r * IN_PROJ_ROWS, (r + 1) * IN_PROJ_ROWS)
        hs = h_scr[rows, :]
        cb_gate = _dot(hs, w_ref[0])
        z = _dot(hs, w_ref[1]) * _dot(hs, w_ref[2])
        z1 = pltpu.roll(z, 1, axis=0)
        z2 = pltpu.roll(z, 2, axis=0)
        if zprev is None:
            p1 = jnp.zeros_like(z)
            p2 = p1
        else:
            p1 = pltpu.roll(zprev, 1, axis=0)
            p2 = pltpu.roll(zprev, 2, axis=0)
        z1 = jnp.where(rowc < 1, p1, z1)
        z2 = jnp.where(rowc < 2, p2, z2)
        acc = cb + cw0 * z2 + cw1 * z1 + cw2 * z
        yc_ref[rows, :] = (cb_gate * acc).astype(BF16)
        zprev = z
        q_ref[rows, :] = (_dot(hs, w_ref[3]) * (LOG2E * HEAD_DIM ** -0.5)).astype(BF16)
        k_ref[rows, :] = _dot(hs, w_ref[4]).astype(BF16)
        v_ref[rows, :] = _dot(hs, w_ref[5]).astype(BF16)
        sgc_ref[rows, :] = jax.nn.sigmoid(_dot(hs, w_ref[6])).astype(BF16)
        sga_ref[rows, :] = jax.nn.sigmoid(_dot(hs, w_ref[7])).astype(BF16)


def _in_proj(x2d, g, w_stack, wf, bfv, conv_w, conv_b, batch, seq):
    tokens = batch * seq
    tn = IN_PROJ_COLS
    nj = D_MODEL // tn
    col_out = pl.BlockSpec((seq, tn), lambda b, j: (b, j))
    out_bf16 = jax.ShapeDtypeStruct((tokens, D_MODEL), BF16)
    return pl.pallas_call(
        _in_proj_kernel,
        grid=(batch, nj),
        in_specs=[
            pl.BlockSpec((seq, D_MODEL), lambda b, j: (b, 0)),
            pl.BlockSpec((1, D_MODEL), lambda b, j: (0, 0)),
            pl.BlockSpec((8, D_MODEL, tn), lambda b, j: (0, 0, j)),
            pl.BlockSpec((D_MODEL, LANES), lambda b, j: (0, 0)),
            pl.BlockSpec((1, LANES), lambda b, j: (0, 0)),
            pl.BlockSpec((CONV_WIDTH, tn), lambda b, j: (0, j)),
            pl.BlockSpec((1, tn), lambda b, j: (0, j)),
        ],
        out_specs=[col_out] * 6 + [pl.BlockSpec((1, seq, LANES), lambda b, j: (b, 0, 0))] * 2,
        out_shape=[out_bf16] * 6 + [jax.ShapeDtypeStruct((batch, seq, LANES), BF16)] * 2,
        scratch_shapes=[pltpu.VMEM((seq, D_MODEL), BF16)],
        compiler_params=pltpu.CompilerParams(
            dimension_semantics=("arbitrary", "arbitrary"), vmem_limit_bytes=VMEM_LIMIT),
        name="in_proj",
    )(x2d, g, w_stack, wf, bfv, conv_w, conv_b)


def _attention_kernel(q_ref, k_ref, v_ref, cq_ref, ck_ref, o_ref, qa_scr, ka_scr, va_scr):
    hp = pl.program_id(1)
    seq = q_ref.shape[0]
    blk = ATT_BLOCK
    lane = lax.broadcasted_iota(jnp.int32, (1, LANES), 1)
    qpos = lax.broadcasted_iota(jnp.int32, (blk, blk), 0)
    kpos = lax.broadcasted_iota(jnp.int32, (blk, blk), 1)
    causal = kpos <= qpos
    zero = jnp.zeros((), BF16)

    own, base = [], []
    for hh in range(HEADS_PER_BLOCK):
        own.append((lane // HEAD_DIM) == hh)
        base.append(((hh + 1) % HEADS_PER_BLOCK) * HEAD_DIM)
        gate = (lane % N_HEADS) == hp * HEADS_PER_BLOCK + hh
        qa_scr[hh, :, 0:LANES] = jnp.where(own[hh], q_ref[...], zero)
        qa_scr[hh, :, LANES:2 * LANES] = jnp.where(gate, cq_ref[0], zero)
        ka_scr[hh, :, 0:LANES] = jnp.where(own[hh], k_ref[...], zero)
        ka_scr[hh, :, LANES:2 * LANES] = jnp.where(gate, ck_ref[0], zero)
        va_scr[hh] = jnp.where(own[hh], v_ref[...],
                               jnp.where(lane == base[hh], 1.0, 0.0).astype(BF16))

    for rg in range(seq // blk):
        rows = slice(rg * blk, (rg + 1) * blk)
        out = None
        for hh in range(HEADS_PER_BLOCK):
            qa = qa_scr[hh, rows, :]
            m = acc = None
            for c in range(rg + 1):
                keys = slice(c * blk, (c + 1) * blk)
                s = lax.dot_general(qa, ka_scr[hh, keys, :], (((1,), (1,)), ((), ())),
                                    preferred_element_type=F32)
                if c == rg:
                    s = jnp.where(causal, s, NEG_BIG)
                mx = jnp.max(s, axis=1, keepdims=True)
                if m is None:
                    m = mx
                    acc = _dot(jnp.exp2(s - m).astype(BF16), va_scr[hh, keys, :])
                else:
                    m_new = jnp.maximum(m, mx)
                    acc = jnp.exp2(m - m_new) * acc + _dot(jnp.exp2(s - m_new).astype(BF16),
                                                           va_scr[hh, keys, :])
                    m = m_new
            res = acc / acc[:, base[hh]:base[hh] + 1]
            out = res if out is None else jnp.where(own[hh], res, out)
        o_ref[rows, :] = out.astype(o_ref.dtype)


def _attention(q, k, v, cq, ck, batch, seq):
    tokens = batch * seq
    n_hp = N_HEADS // HEADS_PER_BLOCK
    blk = pl.BlockSpec((seq, LANES), lambda b, hp: (b, hp))
    gate_blk = pl.BlockSpec((1, seq, LANES), lambda b, hp: (b, 0, 0))
    wide = pltpu.VMEM((HEADS_PER_BLOCK, seq, 2 * LANES), BF16)
    return pl.pallas_call(
        _attention_kernel,
        grid=(batch, n_hp),
        in_specs=[blk, blk, blk, gate_blk, gate_blk],
        out_specs=blk,
        out_shape=jax.ShapeDtypeStruct((tokens, D_MODEL), BF16),
        scratch_shapes=[wide, wide, pltpu.VMEM((HEADS_PER_BLOCK, seq, LANES), BF16)],
        compiler_params=pltpu.CompilerParams(
            dimension_semantics=("arbitrary", "arbitrary"), vmem_limit_bytes=VMEM_LIMIT),
        name="fox_attention",
    )(q, k, v, cq, ck)


def _mixer_out_kernel(yc_ref, o_ref, sgc_ref, sga_ref, x_ref, wco_ref, wao_ref, wo_ref,
                      g_ref, wr_ref, x1_ref, h2_ref, route_ref, counts_ref, carry_scr):
    i = pl.program_id(0)
    tm = x_ref.shape[0]
    sub = MIX_SUB

    @pl.when(i == 0)
    def _():
        carry_scr[...] = jnp.zeros_like(carry_scr)

    lane = lax.broadcasted_iota(jnp.int32, (sub, LANES), 1).astype(F32)
    far = jnp.float32(4 * LANES)
    r_i = lax.broadcasted_iota(jnp.int32, (sub, sub), 0)
    c_i = lax.broadcasted_iota(jnp.int32, (sub, sub), 1)
    strict_lower = jnp.where(c_i < r_i, 1.0, 0.0).astype(BF16)

    def first_lane_of_max(vals, vmax):
        return jnp.min(jnp.where(vals == vmax, lane, far), axis=1, keepdims=True)

    carry = carry_scr[...]
    for r in range(tm // sub):
        rows = slice(r * sub, (r + 1) * sub)
        y_conv = _dot(yc_ref[rows, :], wco_ref[...])
        y_att = _dot(o_ref[rows, :], wao_ref[...])
        m = sgc_ref[rows, :].astype(F32) * y_conv + sga_ref[rows, :].astype(F32) * y_att
        x1 = x_ref[rows, :] + _dot(m.astype(BF16), wo_ref[...])
        x1_ref[rows, :] = x1
        ms = jnp.mean(x1 * x1, axis=-1, keepdims=True)
        h2 = x1 * lax.rsqrt(ms + EPS) * g_ref[...]
        h2_ref[rows, :] = _pack_bf16_pairs(h2)
        logits = _dot(h2.astype(BF16), wr_ref[...])

        lg = jnp.where(lane < N_GROUPS, logits, NEG_BIG)
        gmax = jnp.max(lg, axis=1, keepdims=True)
        g_val = 1.0 / jnp.sum(jnp.exp(lg - gmax), axis=1, keepdims=True)
        g_idx = first_lane_of_max(lg, gmax)
        lo = N_GROUPS + EXPERTS_PER_GROUP * g_idx
        le = jnp.where((lane >= lo) & (lane < lo + EXPERTS_PER_GROUP), logits, NEG_BIG)
        e1max = jnp.max(le, axis=1, keepdims=True)
        e1 = first_lane_of_max(le, e1max)
        le2 = jnp.where(lane == e1, NEG_BIG, le)
        e2max = jnp.max(le2, axis=1, keepdims=True)
        e2 = first_lane_of_max(le2, e2max)
        ratio = jnp.exp(e2max - e1max)
        w1 = g_val / (1.0 + ratio)
        w2 = g_val * ratio / (1.0 + ratio)

        oh1 = lane == e1
        oh2 = lane == e2
        onehot = jnp.where(oh1 | oh2, 1.0, 0.0).astype(BF16)
        before = _dot(strict_lower, onehot) + carry
        rank1 = jnp.sum(jnp.where(oh1, before, 0.0), axis=1, keepdims=True)
        rank2 = jnp.sum(jnp.where(oh2, before, 0.0), axis=1, keepdims=True)
        carry = carry + jnp.sum(onehot.astype(F32), axis=0, keepdims=True)

        route = jnp.where(lane == 0, e1 - N_GROUPS, 0.0)
        route = jnp.where(lane == 1, e2 - N_GROUPS, route)
        route = jnp.where(lane == 2, w1, route)
        route = jnp.where(lane == 3, w2, route)
        route = jnp.where(lane == 4, rank1, route)
        route = jnp.where(lane == 5, rank2, route)
        route_ref[rows, :] = route

    carry_scr[...] = carry
    counts_ref[...] = carry


def _mixer_out(yc, o, sgc, sga, x2d, wco, wao, wo, g, wr):
    tokens = x2d.shape[0]
    tm = MIX_ROWS
    row_blk = pl.BlockSpec((tm, D_MODEL), lambda i: (i, 0))
    w_blk = pl.BlockSpec((D_MODEL, D_MODEL), lambda i: (0, 0))
    return pl.pallas_call(
        _mixer_out_kernel,
        grid=(tokens // tm,),
        in_specs=[row_blk, row_blk, row_blk, row_blk, row_blk, w_blk, w_blk, w_blk,
                  pl.BlockSpec((1, D_MODEL), lambda i: (0, 0)),
                  pl.BlockSpec((D_MODEL, LANES), lambda i: (0, 0))],
        out_specs=[row_blk,
                   pl.BlockSpec((tm, D_MODEL // 2), lambda i: (i, 0)),
                   pl.BlockSpec((tm, LANES), lambda i: (i, 0)),
                   pl.BlockSpec((1, LANES), lambda i: (0, 0))],
        out_shape=[jax.ShapeDtypeStruct((tokens, D_MODEL), F32),
                   jax.ShapeDtypeStruct((tokens, D_MODEL // 2), jnp.uint32),
                   jax.ShapeDtypeStruct((tokens, LANES), F32),
                   jax.ShapeDtypeStruct((1, LANES), F32)],
        scratch_shapes=[pltpu.VMEM((1, LANES), F32)],
        compiler_params=pltpu.CompilerParams(
            dimension_semantics=("arbitrary",), vmem_limit_bytes=VMEM_LIMIT),
        name="mixer_out",
    )(yc, o, sgc, sga, x2d, wco, wao, wo, g, wr)


def _sc_index_rows(indices):
    n = indices.shape[0]
    return jnp.pad(indices.reshape(n // SC_WINDOW, SC_WINDOW), ((0, 0), (0, LANES - SC_WINDOW)))


def _sc_mesh():
    return plsc.VectorSubcoreMesh(core_axis_name="c", subcore_axis_name="s")


def _sc_scatter_rows(data, indices, n_out):
    n = indices.shape[0]
    n_src, width = data.shape
    src_blocks = n_src // SC_WINDOW

    @pl.kernel(out_type=jax.ShapeDtypeStruct((n_out, width), data.dtype), mesh=_sc_mesh())
    def scatter_kernel(x_hbm, i_hbm, o_hbm):
        def body(x_vmem, i_vmem):
            pltpu.sync_copy(x_vmem, o_hbm.at[i_vmem.at[0, pl.ds(0, SC_WINDOW)]])

        pltpu.emit_pipeline(
            body,
            grid=(n // SC_WINDOW,),
            in_specs=[pl.BlockSpec((SC_WINDOW, width), lambda i: (i % src_blocks, 0)),
                      pl.BlockSpec((1, LANES), lambda i: (i, 0))],
            out_specs=[],
            core_axis_name=("c", "s"),
            dimension_semantics=(pltpu.PARALLEL,),
        )(x_hbm, i_hbm)

    return scatter_kernel(data, _sc_index_rows(indices))


def _sc_gather_rows(data, indices):
    n = indices.shape[0]
    width = data.shape[1]

    @pl.kernel(out_type=jax.ShapeDtypeStruct((n, width), data.dtype), mesh=_sc_mesh())
    def gather_kernel(x_hbm, i_hbm, o_hbm):
        def body(i_vmem, o_vmem):
            pltpu.sync_copy(x_hbm.at[i_vmem.at[0, pl.ds(0, SC_WINDOW)]], o_vmem)

        pltpu.emit_pipeline(
            body,
            grid=(n // SC_WINDOW,),
            in_specs=[pl.BlockSpec((1, LANES), lambda i: (i, 0))],
            out_specs=[pl.BlockSpec((SC_WINDOW, width), lambda i: (i, 0))],
            core_axis_name=("c", "s"),
            dimension_semantics=(pltpu.PARALLEL,),
        )(i_hbm, o_hbm)

    return gather_kernel(data, _sc_index_rows(indices))


def _expert_kernel(block_e_ref, n_valid_ref, x_ref, wg_ref, wu_ref, wd_ref, y_ref,
                   wg_scr, wu_scr, wd_scr):
    i = pl.program_id(0)
    n_valid = n_valid_ref[i]

    @pl.when((i == 0) | (block_e_ref[i] != block_e_ref[jnp.maximum(i - 1, 0)]))
    def _():
        wg_scr[...] = wg_ref[0].astype(BF16)
        wu_scr[...] = wu_ref[0].astype(BF16)
        wd_scr[...] = wd_ref[0].astype(BF16)

    @pl.when(n_valid > 0)
    def _():
        row = lax.broadcasted_iota(jnp.int32, x_ref.shape, 0)
        packed = jnp.where(row < n_valid, x_ref[...], jnp.uint32(0))
        xb = _unpack_bf16_pairs(packed).astype(BF16)
        a = _dot(xb, wg_scr[...])
        u = _dot(xb, wu_scr[...])
        hmid = (a * jax.nn.sigmoid(a) * u).astype(BF16)
        y_ref[...] = _pack_bf16_pairs(_dot(hmid, wd_scr[...]))

    @pl.when(n_valid == 0)
    def _():
        y_ref[...] = jnp.zeros_like(y_ref)


def _experts(block_e, n_valid, x_rows, wg, wu, wd):
    n_rows, half = x_rows.shape
    n_blocks = n_rows // MOE_BLOCK

    def w_map(i, be, nv):
        return (be[i], 0, 0)

    row_blk = pl.BlockSpec((MOE_BLOCK, half), lambda i, be, nv: (i, 0))
    grid_spec = pltpu.PrefetchScalarGridSpec(
        num_scalar_prefetch=2,
        grid=(n_blocks,),
        in_specs=[row_blk,
                  pl.BlockSpec((1, D_MODEL, D_EXPERT), w_map),
                  pl.BlockSpec((1, D_MODEL, D_EXPERT), w_map),
                  pl.BlockSpec((1, D_EXPERT, D_MODEL), w_map)],
        out_specs=row_blk,
        scratch_shapes=[pltpu.VMEM((D_MODEL, D_EXPERT), BF16), pltpu.VMEM((D_MODEL, D_EXPERT), BF16),
                        pltpu.VMEM((D_EXPERT, D_MODEL), BF16)],
    )
    return pl.pallas_call(
        _expert_kernel,
        grid_spec=grid_spec,
        out_shape=jax.ShapeDtypeStruct((n_rows, half), jnp.uint32),
        compiler_params=pltpu.CompilerParams(
            dimension_semantics=("arbitrary",), vmem_limit_bytes=VMEM_LIMIT),
        name="moe_experts",
    )(block_e, n_valid, x_rows, wg, wu, wd)


def _combine_kernel(ya_ref, yb_ref, x1_ref, route_ref, g_ref, out_ref):
    route = route_ref[...]
    w1 = route[:, 2:3]
    w2 = route[:, 3:4]
    x2 = x1_ref[...] + (_unpack_bf16_pairs(ya_ref[...]) * w1 + _unpack_bf16_pairs(yb_ref[...]) * w2)
    ms = jnp.mean(x2 * x2, axis=-1, keepdims=True)
    out_ref[...] = x2 * lax.rsqrt(ms + EPS) * g_ref[...]


def _combine(y_pairs, x1, route, g):
    tokens = x1.shape[0]
    tc = MIX_ROWS
    n_tiles = tokens // tc
    row_blk = pl.BlockSpec((tc, D_MODEL), lambda i: (i, 0))
    return pl.pallas_call(
        _combine_kernel,
        grid=(n_tiles,),
        in_specs=[pl.BlockSpec((tc, D_MODEL // 2), lambda i: (i, 0)),
                  pl.BlockSpec((tc, D_MODEL // 2), lambda i: (i + n_tiles, 0)),
                  row_blk,
                  pl.BlockSpec((tc, LANES), lambda i: (i, 0)),
                  pl.BlockSpec((1, D_MODEL), lambda i: (0, 0))],
        out_specs=row_blk,
        out_shape=jax.ShapeDtypeStruct((tokens, D_MODEL), F32),
        compiler_params=pltpu.CompilerParams(
            dimension_semantics=("arbitrary",), vmem_limit_bytes=VMEM_LIMIT),
        name="moe_combine",
    )(y_pairs, y_pairs, x1, route, g)


def kernel(x, norm_mix_g, w_in, conv_w, conv_b, b_forget, w_conv_out, w_att_out, w_out,
           norm_ffn_g, w_router_group, w_router_expert, w_e_gate, w_e_up, w_e_down,
           norm_final_g):
    batch, seq, d = x.shape
    assert d == D_MODEL and seq % max(ATT_BLOCK, IN_PROJ_ROWS) == 0
    tokens = batch * seq
    assert tokens % MIX_ROWS == 0 and tokens % SC_WINDOW == 0
    x2d = x.reshape(tokens, D_MODEL)

    wb = w_in.astype(BF16)
    n_main = 6 * D_MODEL
    gate0 = n_main + N_HEADS
    pieces = [wb[:, i * D_MODEL:(i + 1) * D_MODEL] for i in range(6)]
    pieces += [wb[:, gate0:gate0 + D_MODEL], wb[:, gate0 + D_MODEL:gate0 + 2 * D_MODEL]]
    w_stack = jnp.stack(pieces, axis=0)
    n_rep = 6
    wf = jnp.pad(jnp.tile(wb[:, n_main:gate0], (1, n_rep)), ((0, 0), (0, LANES - n_rep * N_HEADS)))
    bfv = jnp.pad(jnp.tile(b_forget.astype(F32), n_rep), (0, LANES - n_rep * N_HEADS)).reshape(1, LANES)
    wr = jnp.concatenate(
        [w_router_group, jnp.transpose(w_router_expert, (1, 0, 2)).reshape(D_MODEL, N_EXPERTS)], axis=1)
    wr = jnp.pad(wr, ((0, 0), (0, LANES - wr.shape[1]))).astype(BF16)

    yc, q, k, v, sgc, sga, cq, ck = _in_proj(
        x2d, norm_mix_g.reshape(1, D_MODEL), w_stack, wf, bfv, conv_w,
        conv_b.reshape(1, D_MODEL), batch, seq)
    o = _attention(q, k, v, cq, ck, batch, seq)

    x1, h2, route, counts = _mixer_out(
        yc, o, sgc, sga, x2d, w_conv_out.astype(BF16), w_att_out.astype(BF16),
        w_out.astype(BF16), norm_ffn_g.reshape(1, D_MODEL), wr)

    counts = counts[0, N_GROUPS:N_GROUPS + N_EXPERTS].astype(jnp.int32)
    padded = (counts + MOE_BLOCK - 1) // MOE_BLOCK * MOE_BLOCK
    end_padded = jnp.cumsum(padded)
    start_padded = end_padded - padded
    expert = route[:, 0:2].astype(jnp.int32)
    rank = route[:, 4:6].astype(jnp.int32)
    expert_ids = jnp.arange(N_EXPERTS, dtype=jnp.int32)
    dest = rank + jnp.sum(jnp.where(expert[..., None] == expert_ids, start_padded, 0), axis=-1)
    dest_flat = jnp.transpose(dest).reshape(2 * tokens)
    n_rows = tokens * 2 + N_EXPERTS * MOE_BLOCK
    block_start = jnp.arange(n_rows // MOE_BLOCK, dtype=jnp.int32) * MOE_BLOCK
    block_e = jnp.minimum(jnp.sum(end_padded[None, :] <= block_start[:, None], axis=1),
                          N_EXPERTS - 1).astype(jnp.int32)
    n_valid = jnp.clip(start_padded[block_e] + counts[block_e] - block_start, 0, MOE_BLOCK).astype(jnp.int32)

    x_rows = _sc_scatter_rows(h2, dest_flat, n_rows)
    y_rows = _experts(block_e, n_valid, x_rows, w_e_gate, w_e_up, w_e_down)
    y_pairs = _sc_gather_rows(y_rows, dest_flat)
    out = _combine(y_pairs, x1, route, norm_final_g.reshape(1, D_MODEL))
    return out.reshape(batch, seq, D_MODEL)
```

```python
import jax
import jax.numpy as jnp
from jax import lax
from jax.experimental import pallas as pl
from jax.experimental.pallas import tpu as pltpu
from jax.experimental.pallas import tpu_sc as plsc

D_MODEL = 1024
HEAD_DIM = 64
N_HEADS = 16
N_GROUPS = 4
EXPERTS_PER_GROUP = 8
N_EXPERTS = N_GROUPS * EXPERTS_PER_GROUP
D_EXPERT = 512
MOE_BLOCK = 512
CONV_WIDTH = 3
EPS = 1e-6

LANES = 128
HEADS_PER_BLOCK = LANES // HEAD_DIM
NEG_BIG = -1e30
LOG2E = 1.4426950408889634

IN_PROJ_COLS = 256
IN_PROJ_ROWS = 512
ATT_BLOCK = 512
MIX_ROWS = 512
MIX_SUB = 256
SC_WINDOW = 64
VMEM_LIMIT = 56 * 1024 * 1024

F32 = jnp.float32
BF16 = jnp.bfloat16


def _dot(a, b):
    return jnp.dot(a, b, preferred_element_type=F32)


def _pack_bf16_pairs(x):
    c = x.shape[1] // 2
    lo = lax.bitcast_convert_type(x[:, :c].astype(BF16).astype(F32), jnp.uint32) >> 16
    hi = lax.bitcast_convert_type(x[:, c:].astype(BF16).astype(F32), jnp.uint32) & jnp.uint32(0xFFFF0000)
    return lo | hi


def _unpack_bf16_pairs(w):
    lo = lax.bitcast_convert_type(w << 16, F32)
    hi = lax.bitcast_convert_type(w & jnp.uint32(0xFFFF0000), F32)
    return jnp.concatenate([lo, hi], axis=1)


def _in_proj_kernel(x_ref, g_ref, w_ref, wf_ref, bf_ref, cw_ref, cb_ref,
                    yc_ref, q_ref, k_ref, v_ref, sgc_ref, sga_ref, cq_ref, ck_ref, h_scr):
    j = pl.program_id(1)
    seq = x_ref.shape[0]
    n_chunks = seq // IN_PROJ_ROWS

    @pl.when(j == 0)
    def _():
        for r in range(n_chunks):
            rows = slice(r * IN_PROJ_ROWS, (r + 1) * IN_PROJ_ROWS)
            xs = x_ref[rows, :]
            ms = jnp.mean(xs * xs, axis=-1, keepdims=True)
            h_scr[rows, :] = (xs * lax.rsqrt(ms + EPS) * g_ref[...]).astype(BF16)
        f = _dot(h_scr[...], wf_ref[...]) + bf_ref[...]
        c = jnp.minimum(f, 0.0) - jnp.log(1.0 + jnp.exp(-jnp.abs(f)))
        row = lax.broadcasted_iota(jnp.int32, c.shape, 0)
        d = 1
        while d < seq:
            c = c + jnp.where(row >= d, pltpu.roll(c, d, axis=0), 0.0)
            d *= 2
        c = c * LOG2E
        hi = c.astype(BF16).astype(F32)
        mid = (c - hi).astype(BF16).astype(F32)
        lo = (c - hi - mid).astype(BF16).astype(F32)
        grp = lax.broadcasted_iota(jnp.int32, c.shape, 1) // N_HEADS
        terms = jnp.where(grp % 3 == 0, hi, jnp.where(grp % 3 == 1, mid, lo))
        one = jnp.float32(1.0)
        cq_ref[0] = jnp.where(grp < 3, terms, jnp.where(grp < 6, one, 0.0)).astype(BF16)
        ck_ref[0] = jnp.where(grp < 3, one, jnp.where(grp < 6, -terms, 0.0)).astype(BF16)

    cw0 = cw_ref[0:1, :]
    cw1 = cw_ref[1:2, :]
    cw2 = cw_ref[2:3, :]
    cb = cb_ref[...]
    rowc = lax.broadcasted_iota(jnp.int32, (IN_PROJ_ROWS, IN_PROJ_COLS), 0)
    zprev = None
    for r in range(n_chunks):
        rows = slice(r * IN_PROJ_ROWS, (r + 1) * IN_PROJ_ROWS)
        hs = h_scr[rows, :]
        cb_gate = _dot(hs, w_ref[0])
        z = _dot(hs, w_ref[1]) * _dot(hs, w_ref[2])
        z1 = pltpu.roll(z, 1, axis=0)
        z2 = pltpu.roll(z, 2, axis=0)
        if zprev is None:
            p1 = jnp.zeros_like(z)
            p2 = p1
        else:
            p1 = pltpu.roll(zprev, 1, axis=0)
            p2 = pltpu.roll(zprev, 2, axis=0)
        z1 = jnp.where(rowc < 1, p1, z1)
        z2 = jnp.where(rowc < 2, p2, z2)
        acc = cb + cw0 * z2 + cw1 * z1 + cw2 * z
        yc_ref[rows, :] = (cb_gate * acc).astype(BF16)
        zprev = z
        q_ref[rows, :] = (_dot(hs, w_ref[3]) * (LOG2E * HEAD_DIM ** -0.5)).astype(BF16)
        k_ref[rows, :] = _dot(hs, w_ref[4]).astype(BF16)
        v_ref[rows, :] = _dot(hs, w_ref[5]).astype(BF16)
        sgc_ref[rows, :] = jax.nn.sigmoid(_dot(hs, w_ref[6])).astype(BF16)
        sga_ref[rows, :] = jax.nn.sigmoid(_dot(hs, w_ref[7])).astype(BF16)


def _in_proj(x2d, g, w_stack, wf, bfv, conv_w, conv_b, batch, seq):
    tokens = batch * seq
    tn = IN_PROJ_COLS
    nj = D_MODEL // tn
    col_out = pl.BlockSpec((seq, tn), lambda b, j: (b, j))
    out_bf16 = jax.ShapeDtypeStruct((tokens, D_MODEL), BF16)
    return pl.pallas_call(
        _in_proj_kernel,
        grid=(batch, nj),
        in_specs=[
            pl.BlockSpec((seq, D_MODEL), lambda b, j: (b, 0)),
            pl.BlockSpec((1, D_MODEL), lambda b, j: (0, 0)),
            pl.BlockSpec((8, D_MODEL, tn), lambda b, j: (0, 0, j)),
            pl.BlockSpec((D_MODEL, LANES), lambda b, j: (0, 0)),
            pl.BlockSpec((1, LANES), lambda b, j: (0, 0)),
            pl.BlockSpec((CONV_WIDTH, tn), lambda b, j: (0, j)),
            pl.BlockSpec((1, tn), lambda b, j: (0, j)),
        ],
        out_specs=[col_out] * 6 + [pl.BlockSpec((1, seq, LANES), lambda b, j: (b, 0, 0))] * 2,
        out_shape=[out_bf16] * 6 + [jax.ShapeDtypeStruct((batch, seq, LANES), BF16)] * 2,
        scratch_shapes=[pltpu.VMEM((seq, D_MODEL), BF16)],
        compiler_params=pltpu.CompilerParams(
            dimension_semantics=("arbitrary", "arbitrary"), vmem_limit_bytes=VMEM_LIMIT),
        name="in_proj",
    )(x2d, g, w_stack, wf, bfv, conv_w, conv_b)


def _attention_kernel(q_ref, k_ref, v_ref, cq_ref, ck_ref, o_ref, qa_scr, ka_scr, va_scr):
    hp = pl.program_id(1)
    seq = q_ref.shape[0]
    blk = ATT_BLOCK
    lane = lax.broadcasted_iota(jnp.int32, (1, LANES), 1)
    qpos = lax.broadcasted_iota(jnp.int32, (blk, blk), 0)
    kpos = lax.broadcasted_iota(jnp.int32, (blk, blk), 1)
    causal = kpos <= qpos
    zero = jnp.zeros((), BF16)

    own, base = [], []
    for hh in range(HEADS_PER_BLOCK):
        own.append((lane // HEAD_DIM) == hh)
        base.append(((hh + 1) % HEADS_PER_BLOCK) * HEAD_DIM)
        gate = (lane % N_HEADS) == hp * HEADS_PER_BLOCK + hh
        qa_scr[hh, :, 0:LANES] = jnp.where(own[hh], q_ref[...], zero)
        qa_scr[hh, :, LANES:2 * LANES] = jnp.where(gate, cq_ref[0], zero)
        ka_scr[hh, :, 0:LANES] = jnp.where(own[hh], k_ref[...], zero)
        ka_scr[hh, :, LANES:2 * LANES] = jnp.where(gate, ck_ref[0], zero)
        va_scr[hh] = jnp.where(own[hh], v_ref[...],
                               jnp.where(lane == base[hh], 1.0, 0.0).astype(BF16))

    for rg in range(seq // blk):
        rows = slice(rg * blk, (rg + 1) * blk)
        out = None
        for hh in range(HEADS_PER_BLOCK):
            qa = qa_scr[hh, rows, :]
            m = acc = None
            for c in range(rg + 1):
                keys = slice(c * blk, (c + 1) * blk)
                s = lax.dot_general(qa, ka_scr[hh, keys, :], (((1,), (1,)), ((), ())),
                                    preferred_element_type=F32)
                if c == rg:
                    s = jnp.where(causal, s, NEG_BIG)
                mx = jnp.max(s, axis=1, keepdims=True)
                if m is None:
                    m = mx
                    acc = _dot(jnp.exp2(s - m).astype(BF16), va_scr[hh, keys, :])
                else:
                    m_new = jnp.maximum(m, mx)
                    acc = jnp.exp2(m - m_new) * acc + _dot(jnp.exp2(s - m_new).astype(BF16),
                                                           va_scr[hh, keys, :])
                    m = m_new
            res = acc / acc[:, base[hh]:base[hh] + 1]
            out = res if out is None else jnp.where(own[hh], res, out)
        o_ref[rows, :] = out.astype(o_ref.dtype)


def _attention(q, k, v, cq, ck, batch, seq):
    tokens = batch * seq
    n_hp = N_HEADS // HEADS_PER_BLOCK
    blk = pl.BlockSpec((seq, LANES), lambda b, hp: (b, hp))
    gate_blk = pl.BlockSpec((1, seq, LANES), lambda b, hp: (b, 0, 0))
    wide = pltpu.VMEM((HEADS_PER_BLOCK, seq, 2 * LANES), BF16)
    return pl.pallas_call(
        _attention_kernel,
        grid=(batch, n_hp),
        in_specs=[blk, blk, blk, gate_blk, gate_blk],
        out_specs=blk,
        out_shape=jax.ShapeDtypeStruct((tokens, D_MODEL), BF16),
        scratch_shapes=[wide, wide, pltpu.VMEM((HEADS_PER_BLOCK, seq, LANES), BF16)],
        compiler_params=pltpu.CompilerParams(
            dimension_semantics=("arbitrary", "arbitrary"), vmem_limit_bytes=VMEM_LIMIT),
        name="fox_attention",
    )(q, k, v, cq, ck)


def _mixer_out_kernel(yc_ref, o_ref, sgc_ref, sga_ref, x_ref, wco_ref, wao_ref, wo_ref,
                      g_ref, wr_ref, x1_ref, h2_ref, route_ref, counts_ref, carry_scr):
    i = pl.program_id(0)
    tm = x_ref.shape[0]
    sub = MIX_SUB

    @pl.when(i == 0)
    def _():
        carry_scr[...] = jnp.zeros_like(carry_scr)

    lane = lax.broadcasted_iota(jnp.int32, (sub, LANES), 1).astype(F32)
    far = jnp.float32(4 * LANES)
    r_i = lax.broadcasted_iota(jnp.int32, (sub, sub), 0)
    c_i = lax.broadcasted_iota(jnp.int32, (sub, sub), 1)
    strict_lower = jnp.where(c_i < r_i, 1.0, 0.0).astype(BF16)

    def first_lane_of_max(vals, vmax):
        return jnp.min(jnp.where(vals == vmax, lane, far), axis=1, keepdims=True)

    carry = carry_scr[...]
    for r in range(tm // sub):
        rows = slice(r * sub, (r + 1) * sub)
        y_conv = _dot(yc_ref[rows, :], wco_ref[...])
        y_att = _dot(o_ref[rows, :], wao_ref[...])
        m = sgc_ref[rows, :].astype(F32) * y_conv + sga_ref[rows, :].astype(F32) * y_att
        x1 = x_ref[rows, :] + _dot(m.astype(BF16), wo_ref[...])
        x1_ref[rows, :] = x1
        ms = jnp.mean(x1 * x1, axis=-1, keepdims=True)
        h2 = x1 * lax.rsqrt(ms + EPS) * g_ref[...]
        h2_ref[rows, :] = _pack_bf16_pairs(h2)
        logits = _dot(h2.astype(BF16), wr_ref[...])

        lg = jnp.where(lane < N_GROUPS, logits, NEG_BIG)
        gmax = jnp.max(lg, axis=1, keepdims=True)
        g_val = 1.0 / jnp.sum(jnp.exp(lg - gmax), axis=1, keepdims=True)
        g_idx = first_lane_of_max(lg, gmax)
        lo = N_GROUPS + EXPERTS_PER_GROUP * g_idx
        le = jnp.where((lane >= lo) & (lane < lo + EXPERTS_PER_GROUP), logits, NEG_BIG)
        e1max = jnp.max(le, axis=1, keepdims=True)
        e1 = first_lane_of_max(le, e1max)
        le2 = jnp.where(lane == e1, NEG_BIG, le)
        e2max = jnp.max(le2, axis=1, keepdims=True)
        e2 = first_lane_of_max(le2, e2max)
        ratio = jnp.exp(e2max - e1max)
        w1 = g_val / (1.0 + ratio)
        w2 = g_val * ratio / (1.0 + ratio)

        oh1 = lane == e1
        oh2 = lane == e2
        onehot = jnp.where(oh1 | oh2, 1.0, 0.0).astype(BF16)
        before = _dot(strict_lower, onehot) + carry
        rank1 = jnp.sum(jnp.where(oh1, before, 0.0), axis=1, keepdims=True)
        rank2 = jnp.sum(jnp.where(oh2, before, 0.0), axis=1, keepdims=True)
        carry = carry + jnp.sum(onehot.astype(F32), axis=0, keepdims=True)

        route = jnp.where(lane == 0, e1 - N_GROUPS, 0.0)
        route = jnp.where(lane == 1, e2 - N_GROUPS, route)
        route = jnp.where(lane == 2, w1, route)
        route = jnp.where(lane == 3, w2, route)
        route = jnp.where(lane == 4, rank1, route)
        route = jnp.where(lane == 5, rank2, route)
        route_ref[rows, :] = route

    carry_scr[...] = carry
    counts_ref[...] = carry


def _mixer_out(yc, o, sgc, sga, x2d, wco, wao, wo, g, wr):
    tokens = x2d.shape[0]
    tm = MIX_ROWS
    row_blk = pl.BlockSpec((tm, D_MODEL), lambda i: (i, 0))
    w_blk = pl.BlockSpec((D_MODEL, D_MODEL), lambda i: (0, 0))
    return pl.pallas_call(
        _mixer_out_kernel,
        grid=(tokens // tm,),
        in_specs=[row_blk, row_blk, row_blk, row_blk, row_blk, w_blk, w_blk, w_blk,
                  pl.BlockSpec((1, D_MODEL), lambda i: (0, 0)),
                  pl.BlockSpec((D_MODEL, LANES), lambda i: (0, 0))],
        out_specs=[row_blk,
                   pl.BlockSpec((tm, D_MODEL // 2), lambda i: (i, 0)),
                   pl.BlockSpec((tm, LANES), lambda i: (i, 0)),
                   pl.BlockSpec((1, LANES), lambda i: (0, 0))],
        out_shape=[jax.ShapeDtypeStruct((tokens, D_MODEL), F32),
                   jax.ShapeDtypeStruct((tokens, D_MODEL // 2), jnp.uint32),
                   jax.ShapeDtypeStruct((tokens, LANES), F32),
                   jax.ShapeDtypeStruct((1, LANES), F32)],
        scratch_shapes=[pltpu.VMEM((1, LANES), F32)],
        compiler_params=pltpu.CompilerParams(
            dimension_semantics=("arbitrary",), vmem_limit_bytes=VMEM_LIMIT),
        name="mixer_out",
    )(yc, o, sgc, sga, x2d, wco, wao, wo, g, wr)


def _sc_index_rows(indices):
    n = indices.shape[0]
    return jnp.pad(indices.reshape(n // SC_WINDOW, SC_WINDOW), ((0, 0), (0, LANES - SC_WINDOW)))


def _sc_mesh():
    return plsc.VectorSubcoreMesh(core_axis_name="c", subcore_axis_name="s")


def _sc_scatter_rows(data, indices, n_out):
    n = indices.shape[0]
    n_src, width = data.shape
    src_blocks = n_src // SC_WINDOW

    @pl.kernel(out_type=jax.ShapeDtypeStruct((n_out, width), data.dtype), mesh=_sc_mesh())
    def scatter_kernel(x_hbm, i_hbm, o_hbm):
        def body(x_vmem, i_vmem):
            pltpu.sync_copy(x_vmem, o_hbm.at[i_vmem.at[0, pl.ds(0, SC_WINDOW)]])

        pltpu.emit_pipeline(
            body,
            grid=(n // SC_WINDOW,),
            in_specs=[pl.BlockSpec((SC_WINDOW, width), lambda i: (i % src_blocks, 0)),
                      pl.BlockSpec((1, LANES), lambda i: (i, 0))],
            out_specs=[],
            core_axis_name=("c", "s"),
            dimension_semantics=(pltpu.PARALLEL,),
        )(x_hbm, i_hbm)

    return scatter_kernel(data, _sc_index_rows(indices))


def _sc_gather_rows(data, indices):
    n = indices.shape[0]
    width = data.shape[1]

    @pl.kernel(out_type=jax.ShapeDtypeStruct((n, width), data.dtype), mesh=_sc_mesh())
    def gather_kernel(x_hbm, i_hbm, o_hbm):
        def body(i_vmem, o_vmem):
            pltpu.sync_copy(x_hbm.at[i_vmem.at[0, pl.ds(0, SC_WINDOW)]], o_vmem)

        pltpu.emit_pipeline(
            body,
            grid=(n // SC_WINDOW,),
            in_specs=[pl.BlockSpec((1, LANES), lambda i: (i, 0))],
            out_specs=[pl.BlockSpec((SC_WINDOW, width), lambda i: (i, 0))],
            core_axis_name=("c", "s"),
            dimension_semantics=(pltpu.PARALLEL,),
        )(i_hbm, o_hbm)

    return gather_kernel(data, _sc_index_rows(indices))


def _expert_kernel(block_e_ref, n_valid_ref, x_ref, wg_ref, wu_ref, wd_ref, y_ref,
                   wg_scr, wu_scr, wd_scr):
    i = pl.program_id(0)
    n_valid = n_valid_ref[i]

    @pl.when((i == 0) | (block_e_ref[i] != block_e_ref[jnp.maximum(i - 1, 0)]))
    def _():
        wg_scr[...] = wg_ref[0].astype(BF16)
        wu_scr[...] = wu_ref[0].astype(BF16)
        wd_scr[...] = wd_ref[0].astype(BF16)

    @pl.when(n_valid > 0)
    def _():
        row = lax.broadcasted_iota(jnp.int32, x_ref.shape, 0)
        packed = jnp.where(row < n_valid, x_ref[...], jnp.uint32(0))
        xb = _unpack_bf16_pairs(packed).astype(BF16)
        a = _dot(xb, wg_scr[...])
        u = _dot(xb, wu_scr[...])
        hmid = (a * jax.nn.sigmoid(a) * u).astype(BF16)
        y_ref[...] = _pack_bf16_pairs(_dot(hmid, wd_scr[...]))

    @pl.when(n_valid == 0)
    def _():
        y_ref[...] = jnp.zeros_like(y_ref)


def _experts(block_e, n_valid, x_rows, wg, wu, wd):
    n_rows, half = x_rows.shape
    n_blocks = n_rows // MOE_BLOCK

    def w_map(i, be, nv):
        return (be[i], 0, 0)

    row_blk = pl.BlockSpec((MOE_BLOCK, half), lambda i, be, nv: (i, 0))
    grid_spec = pltpu.PrefetchScalarGridSpec(
        num_scalar_prefetch=2,
        grid=(n_blocks,),
        in_specs=[row_blk,
                  pl.BlockSpec((1, D_MODEL, D_EXPERT), w_map),
                  pl.BlockSpec((1, D_MODEL, D_EXPERT), w_map),
                  pl.BlockSpec((1, D_EXPERT, D_MODEL), w_map)],
        out_specs=row_blk,
        scratch_shapes=[pltpu.VMEM((D_MODEL, D_EXPERT), BF16), pltpu.VMEM((D_MODEL, D_EXPERT), BF16),
                        pltpu.VMEM((D_EXPERT, D_MODEL), BF16)],
    )
    return pl.pallas_call(
        _expert_kernel,
        grid_spec=grid_spec,
        out_shape=jax.ShapeDtypeStruct((n_rows, half), jnp.uint32),
        compiler_params=pltpu.CompilerParams(
            dimension_semantics=("arbitrary",), vmem_limit_bytes=VMEM_LIMIT),
        name="moe_experts",
    )(block_e, n_valid, x_rows, wg, wu, wd)


def _combine_kernel(ya_ref, yb_ref, x1_ref, route_ref, g_ref, out_ref):
    route = route_ref[...]
    w1 = route[:, 2:3]
    w2 = route[:, 3:4]
    x2 = x1_ref[...] + (_unpack_bf16_pairs(ya_ref[...]) * w1 + _unpack_bf16_pairs(yb_ref[...]) * w2)
    ms = jnp.mean(x2 * x2, axis=-1, keepdims=True)
    out_ref[...] = x2 * lax.rsqrt(ms + EPS) * g_ref[...]


def _combine(y_pairs, x1, route, g):
    tokens = x1.shape[0]
    tc = MIX_ROWS
    n_tiles = tokens // tc
    row_blk = pl.BlockSpec((tc, D_MODEL), lambda i: (i, 0))
    return pl.pallas_call(
        _combine_kernel,
        grid=(n_tiles,),
        in_specs=[pl.BlockSpec((tc, D_MODEL // 2), lambda i: (i, 0)),
                  pl.BlockSpec((tc, D_MODEL // 2), lambda i: (i + n_tiles, 0)),
                  row_blk,
                  pl.BlockSpec((tc, LANES), lambda i: (i, 0)),
                  pl.BlockSpec((1, D_MODEL), lambda i: (0, 0))],
        out_specs=row_blk,
        out_shape=jax.ShapeDtypeStruct((tokens, D_MODEL), F32),
        compiler_params=pltpu.CompilerParams(
            dimension_semantics=("arbitrary",), vmem_limit_bytes=VMEM_LIMIT),
        name="moe_combine",
    )(y_pairs, y_pairs, x1, route, g)


def kernel(x, norm_mix_g, w_in, conv_w, conv_b, b_forget, w_conv_out, w_att_out, w_out,
           norm_ffn_g, w_router_group, w_router_expert, w_e_gate, w_e_up, w_e_down,
           norm_final_g):
    batch, seq, d = x.shape
    assert d == D_MODEL and seq % max(ATT_BLOCK, IN_PROJ_ROWS) == 0
    tokens = batch * seq
    assert tokens % MIX_ROWS == 0 and tokens % SC_WINDOW == 0
    x2d = x.reshape(tokens, D_MODEL)

    wb = w_in.astype(BF16)
    n_main = 6 * D_MODEL
    gate0 = n_main + N_HEADS
    pieces = [wb[:, i * D_MODEL:(i + 1) * D_MODEL] for i in range(6)]
    pieces += [wb[:, gate0:gate0 + D_MODEL], wb[:, gate0 + D_MODEL:gate0 + 2 * D_MODEL]]
    w_stack = jnp.stack(pieces, axis=0)
    n_rep = 6
    wf = jnp.pad(jnp.tile(wb[:, n_main:gate0], (1, n_rep)), ((0, 0), (0, LANES - n_rep * N_HEADS)))
    bfv = jnp.pad(jnp.tile(b_forget.astype(F32), n_rep), (0, LANES - n_rep * N_HEADS)).reshape(1, LANES)
    wr = jnp.concatenate(
        [w_router_group, jnp.transpose(w_router_expert, (1, 0, 2)).reshape(D_MODEL, N_EXPERTS)], axis=1)
    wr = jnp.pad(wr, ((0, 0), (0, LANES - wr.shape[1]))).astype(BF16)

    yc, q, k, v, sgc, sga, cq, ck = _in_proj(
        x2d, norm_mix_g.reshape(1, D_MODEL), w_stack, wf, bfv, conv_w,
        conv_b.reshape(1, D_MODEL), batch, seq)
    o = _attention(q, k, v, cq, ck, batch, seq)

    x1, h2, route, counts = _mixer_out(
        yc, o, sgc, sga, x2d, w_conv_out.astype(BF16), w_att_out.astype(BF16),
        w_out.astype(BF16), norm_ffn_g.reshape(1, D_MODEL), wr)

    counts = counts[0, N_GROUPS:N_GROUPS + N_EXPERTS].astype(jnp.int32)
    padded = (counts + MOE_BLOCK - 1) // MOE_BLOCK * MOE_BLOCK
    end_padded = jnp.cumsum(padded)
    start_padded = end_padded - padded
    expert = route[:, 0:2].astype(jnp.int32)
    rank = route[:, 4:6].astype(jnp.int32)
    expert_ids = jnp.arange(N_EXPERTS, dtype=jnp.int32)
    dest = rank + jnp.sum(jnp.where(expert[..., None] == expert_ids, start_padded, 0), axis=-1)
    dest_flat = jnp.transpose(dest).reshape(2 * tokens)
    n_rows = tokens * 2 + N_EXPERTS * MOE_BLOCK
    block_start = jnp.arange(n_rows // MOE_BLOCK, dtype=jnp.int32) * MOE_BLOCK
    block_e = jnp.minimum(jnp.sum(end_padded[None, :] <= block_start[:, None], axis=1),
                          N_EXPERTS - 1).astype(jnp.int32)
    n_valid = jnp.clip(start_padded[block_e] + counts[block_e] - block_start, 0, MOE_BLOCK).astype(jnp.int32)

    x_rows = _sc_scatter_rows(h2, dest_flat, n_rows)
    y_rows = _experts(block_e, n_valid, x_rows, w_e_gate, w_e_up, w_e_down)
    y_pairs = _sc_gather_rows(y_rows, dest_flat)
    out = _combine(y_pairs, x1, route, norm_final_g.reshape(1, D_MODEL))
    return out.reshape(batch, seq, D_MODEL)
```

```python
import jax
import jax.numpy as jnp
from jax import lax
from jax.experimental import pallas as pl
from jax.experimental.pallas import tpu as pltpu
from jax.experimental.pallas import tpu_sc as plsc

D_MODEL = 1024
HEAD_DIM = 64
N_HEADS = 16
N_GROUPS = 4
EXPERTS_PER_GROUP = 8
N_EXPERTS = N_GROUPS * EXPERTS_PER_GROUP
D_EXPERT = 512
MOE_BLOCK = 512
CONV_WIDTH = 3
EPS = 1e-6

LANES = 128
HEADS_PER_BLOCK = LANES // HEAD_DIM
NEG_BIG = -1e30
LOG2E = 1.4426950408889634

IN_PROJ_COLS = 256
IN_PROJ_ROWS = 512
ATT_BLOCK = 256
ATT_KEYS = 256
MIX_ROWS = 512
MIX_SUB = 256
SC_WINDOW = 64
VMEM_LIMIT = 56 * 1024 * 1024

F32 = jnp.float32
BF16 = jnp.bfloat16


def _dot(a, b):
    return jnp.dot(a, b, preferred_element_type=F32)


def _pack_bf16_pairs(x):
    c = x.shape[1] // 2
    lo = lax.bitcast_convert_type(x[:, :c].astype(BF16).astype(F32), jnp.uint32) >> 16
    hi = lax.bitcast_convert_type(x[:, c:].astype(BF16).astype(F32), jnp.uint32) & jnp.uint32(0xFFFF0000)
    return lo | hi


def _unpack_bf16_pairs(w):
    lo = lax.bitcast_convert_type(w << 16, F32)
    hi = lax.bitcast_convert_type(w & jnp.uint32(0xFFFF0000), F32)
    return jnp.concatenate([lo, hi], axis=1)


def _in_proj_kernel(x_ref, g_ref, w_ref, wf_ref, bf_ref, cw_ref, cb_ref,
                    yc_ref, q_ref, k_ref, v_ref, sgc_ref, sga_ref, cq_ref, ck_ref, h_scr):
    j = pl.program_id(1)
    seq = x_ref.shape[0]
    n_chunks = seq // IN_PROJ_ROWS

    @pl.when(j == 0)
    def _():
        for r in range(n_chunks):
            rows = slice(r * IN_PROJ_ROWS, (r + 1) * IN_PROJ_ROWS)
            xs = x_ref[rows, :]
            ms = jnp.mean(xs * xs, axis=-1, keepdims=True)
            h_scr[rows, :] = (xs * lax.rsqrt(ms + EPS) * g_ref[...]).astype(BF16)
        f = _dot(h_scr[...], wf_ref[...]) + bf_ref[...]
        c = jnp.minimum(f, 0.0) - jnp.log(1.0 + jnp.exp(-jnp.abs(f)))
        row = lax.broadcasted_iota(jnp.int32, c.shape, 0)
        d = 1
        while d < seq:
            c = c + jnp.where(row >= d, pltpu.roll(c, d, axis=0), 0.0)
            d *= 2
        c = c * LOG2E
        hi = c.astype(BF16).astype(F32)
        mid = (c - hi).astype(BF16).astype(F32)
        lo = (c - hi - mid).astype(BF16).astype(F32)
        grp = lax.broadcasted_iota(jnp.int32, c.shape, 1) // N_HEADS
        terms = jnp.where(grp % 3 == 0, hi, jnp.where(grp % 3 == 1, mid, lo))
        one = jnp.float32(1.0)
        cq_ref[0] = jnp.where(grp < 3, terms, jnp.where(grp < 6, one, 0.0)).astype(BF16)
        ck_ref[0] = jnp.where(grp < 3, one, jnp.where(grp < 6, -terms, 0.0)).astype(BF16)

    cw0 = cw_ref[0:1, :]
    cw1 = cw_ref[1:2, :]
    cw2 = cw_ref[2:3, :]
    cb = cb_ref[...]
    rowc = lax.broadcasted_iota(jnp.int32, (IN_PROJ_ROWS, IN_PROJ_COLS), 0)
    zprev = None
    for r in range(n_chunks):
        rows = slice(r * IN_PROJ_ROWS, (r + 1) * IN_PROJ_ROWS)
        hs = h_scr[rows, :]
        cb_gate = _dot(hs, w_ref[0])
        z = _dot(hs, w_ref[1]) * _dot(hs, w_ref[2])
        z1 = pltpu.roll(z, 1, axis=0)
        z2 = pltpu.roll(z, 2, axis=0)
        if zprev is None:
            p1 = jnp.zeros_like(z)
            p2 = p1
        else:
            p1 = pltpu.roll(zprev, 1, axis=0)
            p2 = pltpu.roll(zprev, 2, axis=0)
        z1 = jnp.where(rowc < 1, p1, z1)
        z2 = jnp.where(rowc < 2, p2, z2)
        acc = cb + cw0 * z2 + cw1 * z1 + cw2 * z
        yc_ref[rows, :] = (cb_gate * acc).astype(BF16)
        zprev = z
        q_ref[rows, :] = (_dot(hs, w_ref[3]) * (LOG2E * HEAD_DIM ** -0.5)).astype(BF16)
        k_ref[rows, :] = _dot(hs, w_ref[4]).astype(BF16)
        v_ref[rows, :] = _dot(hs, w_ref[5]).astype(BF16)
        sgc_ref[rows, :] = jax.nn.sigmoid(_dot(hs, w_ref[6])).astype(BF16)
        sga_ref[rows, :] = jax.nn.sigmoid(_dot(hs, w_ref[7])).astype(BF16)


def _in_proj(x2d, g, w_stack, wf, bfv, conv_w, conv_b, batch, seq):
    tokens = batch * seq
    tn = IN_PROJ_COLS
    nj = D_MODEL // tn
    col_out = pl.BlockSpec((seq, tn), lambda b, j: (b, j))
    out_bf16 = jax.ShapeDtypeStruct((tokens, D_MODEL), BF16)
    return pl.pallas_call(
        _in_proj_kernel,
        grid=(batch, nj),
        in_specs=[
            pl.BlockSpec((seq, D_MODEL), lambda b, j: (b, 0)),
            pl.BlockSpec((1, D_MODEL), lambda b, j: (0, 0)),
            pl.BlockSpec((8, D_MODEL, tn), lambda b, j: (0, 0, j)),
            pl.BlockSpec((D_MODEL, LANES), lambda b, j: (0, 0)),
            pl.BlockSpec((1, LANES), lambda b, j: (0, 0)),
            pl.BlockSpec((CONV_WIDTH, tn), lambda b, j: (0, j)),
            pl.BlockSpec((1, tn), lambda b, j: (0, j)),
        ],
        out_specs=[col_out] * 6 + [pl.BlockSpec((1, seq, LANES), lambda b, j: (b, 0, 0))] * 2,
        out_shape=[out_bf16] * 6 + [jax.ShapeDtypeStruct((batch, seq, LANES), BF16)] * 2,
        scratch_shapes=[pltpu.VMEM((seq, D_MODEL), BF16)],
        compiler_params=pltpu.CompilerParams(
            dimension_semantics=("arbitrary", "arbitrary"), vmem_limit_bytes=VMEM_LIMIT),
        name="in_proj",
    )(x2d, g, w_stack, wf, bfv, conv_w, conv_b)


def _attention_kernel(q_ref, k_ref, v_ref, cq_ref, ck_ref, o_ref, qa_scr, ka_scr, va_scr):
    hp = pl.program_id(1)
    seq = q_ref.shape[0]
    blk = ATT_BLOCK
    lane = lax.broadcasted_iota(jnp.int32, (1, LANES), 1)
    zero = jnp.zeros((), BF16)

    own, base = [], []
    for hh in range(HEADS_PER_BLOCK):
        own.append((lane // HEAD_DIM) == hh)
        base.append(((hh + 1) % HEADS_PER_BLOCK) * HEAD_DIM)
        gate = (lane % N_HEADS) == hp * HEADS_PER_BLOCK + hh
        qa_scr[hh, :, 0:LANES] = jnp.where(own[hh], q_ref[...], zero)
        qa_scr[hh, :, LANES:2 * LANES] = jnp.where(gate, cq_ref[0], zero)
        ka_scr[hh, :, 0:LANES] = jnp.where(own[hh], k_ref[...], zero)
        ka_scr[hh, :, LANES:2 * LANES] = jnp.where(gate, ck_ref[0], zero)
        va_scr[hh] = jnp.where(own[hh], v_ref[...],
                               jnp.where(lane == base[hh], 1.0, 0.0).astype(BF16))

    for rg in range(seq // blk):
        rows = slice(rg * blk, (rg + 1) * blk)
        out = None
        for hh in range(HEADS_PER_BLOCK):
            qa = qa_scr[hh, rows, :]
            m = acc = None
            k_end = (rg + 1) * blk
            for k0 in range(0, k_end, ATT_KEYS):
                k1 = min(k0 + ATT_KEYS, k_end)
                keys = slice(k0, k1)
                s = lax.dot_general(qa, ka_scr[hh, keys, :], (((1,), (1,)), ((), ())),
                                    preferred_element_type=F32)
                if k1 > rg * blk:
                    qpos = rg * blk + lax.broadcasted_iota(jnp.int32, s.shape, 0)
                    kpos = k0 + lax.broadcasted_iota(jnp.int32, s.shape, 1)
                    s = jnp.where(kpos <= qpos, s, NEG_BIG)
                mx = jnp.max(s, axis=1, keepdims=True)
                if m is None:
                    m = mx
                    acc = _dot(jnp.exp2(s - m).astype(BF16), va_scr[hh, keys, :])
                else:
                    m_new = jnp.maximum(m, mx)
                    acc = jnp.exp2(m - m_new) * acc + _dot(jnp.exp2(s - m_new).astype(BF16),
                                                           va_scr[hh, keys, :])
                    m = m_new
            res = acc / acc[:, base[hh]:base[hh] + 1]
            out = res if out is None else jnp.where(own[hh], res, out)
        o_ref[rows, :] = out.astype(o_ref.dtype)


def _attention(q, k, v, cq, ck, batch, seq):
    tokens = batch * seq
    n_hp = N_HEADS // HEADS_PER_BLOCK
    blk = pl.BlockSpec((seq, LANES), lambda b, hp: (b, hp))
    gate_blk = pl.BlockSpec((1, seq, LANES), lambda b, hp: (b, 0, 0))
    wide = pltpu.VMEM((HEADS_PER_BLOCK, seq, 2 * LANES), BF16)
    return pl.pallas_call(
        _attention_kernel,
        grid=(batch, n_hp),
        in_specs=[blk, blk, blk, gate_blk, gate_blk],
        out_specs=blk,
        out_shape=jax.ShapeDtypeStruct((tokens, D_MODEL), BF16),
        scratch_shapes=[wide, wide, pltpu.VMEM((HEADS_PER_BLOCK, seq, LANES), BF16)],
        compiler_params=pltpu.CompilerParams(
            dimension_semantics=("arbitrary", "arbitrary"), vmem_limit_bytes=VMEM_LIMIT),
        name="fox_attention",
    )(q, k, v, cq, ck)


def _mixer_out_kernel(yc_ref, o_ref, sgc_ref, sga_ref, x_ref, wco_ref, wao_ref, wo_ref,
                      g_ref, wr_ref, x1_ref, h2_ref, route_ref, counts_ref, carry_scr):
    i = pl.program_id(0)
    tm = x_ref.shape[0]
    sub = MIX_SUB

    @pl.when(i == 0)
    def _():
        carry_scr[...] = jnp.zeros_like(carry_scr)

    lane = lax.broadcasted_iota(jnp.int32, (sub, LANES), 1).astype(F32)
    far = jnp.float32(4 * LANES)
    r_i = lax.broadcasted_iota(jnp.int32, (sub, sub), 0)
    c_i = lax.broadcasted_iota(jnp.int32, (sub, sub), 1)
    strict_lower = jnp.where(c_i < r_i, 1.0, 0.0).astype(BF16)

    def first_lane_of_max(vals, vmax):
        return jnp.min(jnp.where(vals == vmax, lane, far), axis=1, keepdims=True)

    carry = carry_scr[...]
    for r in range(tm // sub):
        rows = slice(r * sub, (r + 1) * sub)
        y_conv = _dot(yc_ref[rows, :], wco_ref[...])
        y_att = _dot(o_ref[rows, :], wao_ref[...])
        m = sgc_ref[rows, :].astype(F32) * y_conv + sga_ref[rows, :].astype(F32) * y_att
        x1 = x_ref[rows, :] + _dot(m.astype(BF16), wo_ref[...])
        x1_ref[rows, :] = x1
        ms = jnp.mean(x1 * x1, axis=-1, keepdims=True)
        h2 = x1 * lax.rsqrt(ms + EPS) * g_ref[...]
        h2_ref[rows, :] = _pack_bf16_pairs(h2)
        logits = _dot(h2.astype(BF16), wr_ref[...])

        lg = jnp.where(lane < N_GROUPS, logits, NEG_BIG)
        gmax = jnp.max(lg, axis=1, keepdims=True)
        g_val = 1.0 / jnp.sum(jnp.exp(lg - gmax), axis=1, keepdims=True)
        g_idx = first_lane_of_max(lg, gmax)
        lo = N_GROUPS + EXPERTS_PER_GROUP * g_idx
        le = jnp.where((lane >= lo) & (lane < lo + EXPERTS_PER_GROUP), logits, NEG_BIG)
        e1max = jnp.max(le, axis=1, keepdims=True)
        e1 = first_lane_of_max(le, e1max)
        le2 = jnp.where(lane == e1, NEG_BIG, le)
        e2max = jnp.max(le2, axis=1, keepdims=True)
        e2 = first_lane_of_max(le2, e2max)
        ratio = jnp.exp(e2max - e1max)
        w1 = g_val / (1.0 + ratio)
        w2 = g_val * ratio / (1.0 + ratio)

        oh1 = lane == e1
        oh2 = lane == e2
        onehot = jnp.where(oh1 | oh2, 1.0, 0.0).astype(BF16)
        before = _dot(strict_lower, onehot) + carry
        rank1 = jnp.sum(jnp.where(oh1, before, 0.0), axis=1, keepdims=True)
        rank2 = jnp.sum(jnp.where(oh2, before, 0.0), axis=1, keepdims=True)
        carry = carry + jnp.sum(onehot.astype(F32), axis=0, keepdims=True)

        route = jnp.where(lane == 0, e1 - N_GROUPS, 0.0)
        route = jnp.where(lane == 1, e2 - N_GROUPS, route)
        route = jnp.where(lane == 2, w1, route)
        route = jnp.where(lane == 3, w2, route)
        route = jnp.where(lane == 4, rank1, route)
        route = jnp.where(lane == 5, rank2, route)
        route_ref[rows, :] = route

    carry_scr[...] = carry
    counts_ref[...] = carry


def _mixer_out(yc, o, sgc, sga, x2d, wco, wao, wo, g, wr):
    tokens = x2d.shape[0]
    tm = MIX_ROWS
    row_blk = pl.BlockSpec((tm, D_MODEL), lambda i: (i, 0))
    w_blk = pl.BlockSpec((D_MODEL, D_MODEL), lambda i: (0, 0))
    return pl.pallas_call(
        _mixer_out_kernel,
        grid=(tokens // tm,),
        in_specs=[row_blk, row_blk, row_blk, row_blk, row_blk, w_blk, w_blk, w_blk,
                  pl.BlockSpec((1, D_MODEL), lambda i: (0, 0)),
                  pl.BlockSpec((D_MODEL, LANES), lambda i: (0, 0))],
        out_specs=[row_blk,
                   pl.BlockSpec((tm, D_MODEL // 2), lambda i: (i, 0)),
                   pl.BlockSpec((tm, LANES), lambda i: (i, 0)),
                   pl.BlockSpec((1, LANES), lambda i: (0, 0))],
        out_shape=[jax.ShapeDtypeStruct((tokens, D_MODEL), F32),
                   jax.ShapeDtypeStruct((tokens, D_MODEL // 2), jnp.uint32),
                   jax.ShapeDtypeStruct((tokens, LANES), F32),
                   jax.ShapeDtypeStruct((1, LANES), F32)],
        scratch_shapes=[pltpu.VMEM((1, LANES), F32)],
        compiler_params=pltpu.CompilerParams(
            dimension_semantics=("arbitrary",), vmem_limit_bytes=VMEM_LIMIT),
        name="mixer_out",
    )(yc, o, sgc, sga, x2d, wco, wao, wo, g, wr)


def _sc_index_rows(indices):
    n = indices.shape[0]
    return jnp.pad(indices.reshape(n // SC_WINDOW, SC_WINDOW), ((0, 0), (0, LANES - SC_WINDOW)))


def _sc_mesh():
    return plsc.VectorSubcoreMesh(core_axis_name="c", subcore_axis_name="s")


def _sc_scatter_rows(data, indices, n_out):
    n = indices.shape[0]
    n_src, width = data.shape
    src_blocks = n_src // SC_WINDOW

    @pl.kernel(out_type=jax.ShapeDtypeStruct((n_out, width), data.dtype), mesh=_sc_mesh())
    def scatter_kernel(x_hbm, i_hbm, o_hbm):
        def body(x_vmem, i_vmem):
            pltpu.sync_copy(x_vmem, o_hbm.at[i_vmem.at[0, pl.ds(0, SC_WINDOW)]])

        pltpu.emit_pipeline(
            body,
            grid=(n // SC_WINDOW,),
            in_specs=[pl.BlockSpec((SC_WINDOW, width), lambda i: (i % src_blocks, 0)),
                      pl.BlockSpec((1, LANES), lambda i: (i, 0))],
            out_specs=[],
            core_axis_name=("c", "s"),
            dimension_semantics=(pltpu.PARALLEL,),
        )(x_hbm, i_hbm)

    return scatter_kernel(data, _sc_index_rows(indices))


def _sc_gather_rows(data, indices):
    n = indices.shape[0]
    width = data.shape[1]

    @pl.kernel(out_type=jax.ShapeDtypeStruct((n, width), data.dtype), mesh=_sc_mesh())
    def gather_kernel(x_hbm, i_hbm, o_hbm):
        def body(i_vmem, o_vmem):
            pltpu.sync_copy(x_hbm.at[i_vmem.at[0, pl.ds(0, SC_WINDOW)]], o_vmem)

        pltpu.emit_pipeline(
            body,
            grid=(n // SC_WINDOW,),
            in_specs=[pl.BlockSpec((1, LANES), lambda i: (i, 0))],
            out_specs=[pl.BlockSpec((SC_WINDOW, width), lambda i: (i, 0))],
            core_axis_name=("c", "s"),
            dimension_semantics=(pltpu.PARALLEL,),
        )(i_hbm, o_hbm)

    return gather_kernel(data, _sc_index_rows(indices))


def _expert_kernel(block_e_ref, n_valid_ref, x_ref, wg_ref, wu_ref, wd_ref, y_ref,
                   wg_scr, wu_scr, wd_scr):
    i = pl.program_id(0)
    n_valid = n_valid_ref[i]

    @pl.when((i == 0) | (block_e_ref[i] != block_e_ref[jnp.maximum(i - 1, 0)]))
    def _():
        wg_scr[...] = wg_ref[0].astype(BF16)
        wu_scr[...] = wu_ref[0].astype(BF16)
        wd_scr[...] = wd_ref[0].astype(BF16)

    @pl.when(n_valid > 0)
    def _():
        row = lax.broadcasted_iota(jnp.int32, x_ref.shape, 0)
        packed = jnp.where(row < n_valid, x_ref[...], jnp.uint32(0))
        xb = _unpack_bf16_pairs(packed).astype(BF16)
        a = _dot(xb, wg_scr[...])
        u = _dot(xb, wu_scr[...])
        hmid = (a * jax.nn.sigmoid(a) * u).astype(BF16)
        y_ref[...] = _pack_bf16_pairs(_dot(hmid, wd_scr[...]))

    @pl.when(n_valid == 0)
    def _():
        y_ref[...] = jnp.zeros_like(y_ref)


def _experts(block_e, n_valid, x_rows, wg, wu, wd):
    n_rows, half = x_rows.shape
    n_blocks = n_rows // MOE_BLOCK

    def w_map(i, be, nv):
        return (be[i], 0, 0)

    row_blk = pl.BlockSpec((MOE_BLOCK, half), lambda i, be, nv: (i, 0))
    grid_spec = pltpu.PrefetchScalarGridSpec(
        num_scalar_prefetch=2,
        grid=(n_blocks,),
        in_specs=[row_blk,
                  pl.BlockSpec((1, D_MODEL, D_EXPERT), w_map),
                  pl.BlockSpec((1, D_MODEL, D_EXPERT), w_map),
                  pl.BlockSpec((1, D_EXPERT, D_MODEL), w_map)],
        out_specs=row_blk,
        scratch_shapes=[pltpu.VMEM((D_MODEL, D_EXPERT), BF16), pltpu.VMEM((D_MODEL, D_EXPERT), BF16),
                        pltpu.VMEM((D_EXPERT, D_MODEL), BF16)],
    )
    return pl.pallas_call(
        _expert_kernel,
        grid_spec=grid_spec,
        out_shape=jax.ShapeDtypeStruct((n_rows, half), jnp.uint32),
        compiler_params=pltpu.CompilerParams(
            dimension_semantics=("arbitrary",), vmem_limit_bytes=VMEM_LIMIT),
        name="moe_experts",
    )(block_e, n_valid, x_rows, wg, wu, wd)


def _combine_kernel(ya_ref, yb_ref, x1_ref, route_ref, g_ref, out_ref):
    route = route_ref[...]
    w1 = route[:, 2:3]
    w2 = route[:, 3:4]
    x2 = x1_ref[...] + (_unpack_bf16_pairs(ya_ref[...]) * w1 + _unpack_bf16_pairs(yb_ref[...]) * w2)
    ms = jnp.mean(x2 * x2, axis=-1, keepdims=True)
    out_ref[...] = x2 * lax.rsqrt(ms + EPS) * g_ref[...]


def _combine(y_pairs, x1, route, g):
    tokens = x1.shape[0]
    tc = MIX_ROWS
    n_tiles = tokens // tc
    row_blk = pl.BlockSpec((tc, D_MODEL), lambda i: (i, 0))
    return pl.pallas_call(
        _combine_kernel,
        grid=(n_tiles,),
        in_specs=[pl.BlockSpec((tc, D_MODEL // 2), lambda i: (i, 0)),
                  pl.BlockSpec((tc, D_MODEL // 2), lambda i: (i + n_tiles, 0)),
                  row_blk,
                  pl.BlockSpec((tc, LANES), lambda i: (i, 0)),
                  pl.BlockSpec((1, D_MODEL), lambda i: (0, 0))],
        out_specs=row_blk,
        out_shape=jax.ShapeDtypeStruct((tokens, D_MODEL), F32),
        compiler_params=pltpu.CompilerParams(
            dimension_semantics=("arbitrary",), vmem_limit_bytes=VMEM_LIMIT),
        name="moe_combine",
    )(y_pairs, y_pairs, x1, route, g)


def kernel(x, norm_mix_g, w_in, conv_w, conv_b, b_forget, w_conv_out, w_att_out, w_out,
           norm_ffn_g, w_router_group, w_router_expert, w_e_gate, w_e_up, w_e_down,
           norm_final_g):
    batch, seq, d = x.shape
    assert d == D_MODEL and seq % max(ATT_BLOCK, IN_PROJ_ROWS) == 0
    tokens = batch * seq
    assert tokens % MIX_ROWS == 0 and tokens % SC_WINDOW == 0
    x2d = x.reshape(tokens, D_MODEL)

    wb = w_in.astype(BF16)
    n_main = 6 * D_MODEL
    gate0 = n_main + N_HEADS
    pieces = [wb[:, i * D_MODEL:(i + 1) * D_MODEL] for i in range(6)]
    pieces += [wb[:, gate0:gate0 + D_MODEL], wb[:, gate0 + D_MODEL:gate0 + 2 * D_MODEL]]
    w_stack = jnp.stack(pieces, axis=0)
    n_rep = 6
    wf = jnp.pad(jnp.tile(wb[:, n_main:gate0], (1, n_rep)), ((0, 0), (0, LANES - n_rep * N_HEADS)))
    bfv = jnp.pad(jnp.tile(b_forget.astype(F32), n_rep), (0, LANES - n_rep * N_HEADS)).reshape(1, LANES)
    wr = jnp.concatenate(
        [w_router_group, jnp.transpose(w_router_expert, (1, 0, 2)).reshape(D_MODEL, N_EXPERTS)], axis=1)
    wr = jnp.pad(wr, ((0, 0), (0, LANES - wr.shape[1]))).astype(BF16)

    yc, q, k, v, sgc, sga, cq, ck = _in_proj(
        x2d, norm_mix_g.reshape(1, D_MODEL), w_stack, wf, bfv, conv_w,
        conv_b.reshape(1, D_MODEL), batch, seq)
    o = _attention(q, k, v, cq, ck, batch, seq)

    x1, h2, route, counts = _mixer_out(
        yc, o, sgc, sga, x2d, w_conv_out.astype(BF16), w_att_out.astype(BF16),
        w_out.astype(BF16), norm_ffn_g.reshape(1, D_MODEL), wr)

    counts = counts[0, N_GROUPS:N_GROUPS + N_EXPERTS].astype(jnp.int32)
    padded = (counts + MOE_BLOCK - 1) // MOE_BLOCK * MOE_BLOCK
    end_padded = jnp.cumsum(padded)
    start_padded = end_padded - padded
    expert = route[:, 0:2].astype(jnp.int32)
    rank = route[:, 4:6].astype(jnp.int32)
    expert_ids = jnp.arange(N_EXPERTS, dtype=jnp.int32)
    dest = rank + jnp.sum(jnp.where(expert[..., None] == expert_ids, start_padded, 0), axis=-1)
    dest_flat = jnp.transpose(dest).reshape(2 * tokens)
    n_rows = tokens * 2 + N_EXPERTS * MOE_BLOCK
    block_start = jnp.arange(n_rows // MOE_BLOCK, dtype=jnp.int32) * MOE_BLOCK
    block_e = jnp.minimum(jnp.sum(end_padded[None, :] <= block_start[:, None], axis=1),
                          N_EXPERTS - 1).astype(jnp.int32)
    n_valid = jnp.clip(start_padded[block_e] + counts[block_e] - block_start, 0, MOE_BLOCK).astype(jnp.int32)

    x_rows = _sc_scatter_rows(h2, dest_flat, n_rows)
    y_rows = _experts(block_e, n_valid, x_rows, w_e_gate, w_e_up, w_e_down)
    y_pairs = _sc_gather_rows(y_rows, dest_flat)
    out = _combine(y_pairs, x1, route, norm_final_g.reshape(1, D_MODEL))
    return out.reshape(batch, seq, D_MODEL)
```

```python
import jax
import jax.numpy as jnp
from jax import lax
from jax.experimental import pallas as pl
from jax.experimental.pallas import tpu as pltpu
from jax.experimental.pallas import tpu_sc as plsc

D_MODEL = 1024
HEAD_DIM = 64
N_HEADS = 16
N_GROUPS = 4
EXPERTS_PER_GROUP = 8
N_EXPERTS = N_GROUPS * EXPERTS_PER_GROUP
D_EXPERT = 512
MOE_BLOCK = 512
CONV_WIDTH = 3
EPS = 1e-6

LANES = 128
HEADS_PER_BLOCK = LANES // HEAD_DIM
NEG_BIG = -1e30
LOG2E = 1.4426950408889634

IN_PROJ_COLS = 256
IN_PROJ_ROWS = 512
ATT_BLOCK = 256
ATT_KEYS = 256
MIX_ROWS = 512
ROUTE_ROWS = 8
SC_WINDOW = 64
VMEM_LIMIT = 56 * 1024 * 1024

F32 = jnp.float32
BF16 = jnp.bfloat16


def _dot(a, b):
    return jnp.dot(a, b, preferred_element_type=F32)


def _pack_bf16_pairs(x):
    c = x.shape[1] // 2
    lo = lax.bitcast_convert_type(x[:, :c].astype(BF16).astype(F32), jnp.uint32) >> 16
    hi = lax.bitcast_convert_type(x[:, c:].astype(BF16).astype(F32), jnp.uint32) & jnp.uint32(0xFFFF0000)
    return lo | hi


def _unpack_bf16_pairs(w):
    lo = lax.bitcast_convert_type(w << 16, F32)
    hi = lax.bitcast_convert_type(w & jnp.uint32(0xFFFF0000), F32)
    return jnp.concatenate([lo, hi], axis=1)


def _in_proj_kernel(x_ref, g_ref, w_ref, wf_ref, bf_ref, cw_ref, cb_ref,
                    yc_ref, q_ref, k_ref, v_ref, sgc_ref, sga_ref, cq_ref, ck_ref, h_scr):
    j = pl.program_id(1)
    seq = x_ref.shape[0]
    n_chunks = seq // IN_PROJ_ROWS

    @pl.when(j == 0)
    def _():
        for r in range(n_chunks):
            rows = slice(r * IN_PROJ_ROWS, (r + 1) * IN_PROJ_ROWS)
            xs = x_ref[rows, :]
            ms = jnp.mean(xs * xs, axis=-1, keepdims=True)
            h_scr[rows, :] = (xs * lax.rsqrt(ms + EPS) * g_ref[...]).astype(BF16)
        f = _dot(h_scr[...], wf_ref[...]) + bf_ref[...]
        c = jnp.minimum(f, 0.0) - jnp.log(1.0 + jnp.exp(-jnp.abs(f)))
        row = lax.broadcasted_iota(jnp.int32, c.shape, 0)
        d = 1
        while d < seq:
            c = c + jnp.where(row >= d, pltpu.roll(c, d, axis=0), 0.0)
            d *= 2
        c = c * LOG2E
        hi = c.astype(BF16).astype(F32)
        mid = (c - hi).astype(BF16).astype(F32)
        lo = (c - hi - mid).astype(BF16).astype(F32)
        grp = lax.broadcasted_iota(jnp.int32, c.shape, 1) // N_HEADS
        terms = jnp.where(grp % 3 == 0, hi, jnp.where(grp % 3 == 1, mid, lo))
        one = jnp.float32(1.0)
        cq_ref[0] = jnp.where(grp < 3, terms, jnp.where(grp < 6, one, 0.0)).astype(BF16)
        ck_ref[0] = jnp.where(grp < 3, one, jnp.where(grp < 6, -terms, 0.0)).astype(BF16)

    cw0 = cw_ref[0:1, :]
    cw1 = cw_ref[1:2, :]
    cw2 = cw_ref[2:3, :]
    cb = cb_ref[...]
    rowc = lax.broadcasted_iota(jnp.int32, (IN_PROJ_ROWS, IN_PROJ_COLS), 0)
    zprev = None
    for r in range(n_chunks):
        rows = slice(r * IN_PROJ_ROWS, (r + 1) * IN_PROJ_ROWS)
        hs = h_scr[rows, :]
        cb_gate = _dot(hs, w_ref[0])
        z = _dot(hs, w_ref[1]) * _dot(hs, w_ref[2])
        z1 = pltpu.roll(z, 1, axis=0)
        z2 = pltpu.roll(z, 2, axis=0)
        if zprev is None:
            p1 = jnp.zeros_like(z)
            p2 = p1
        else:
            p1 = pltpu.roll(zprev, 1, axis=0)
            p2 = pltpu.roll(zprev, 2, axis=0)
        z1 = jnp.where(rowc < 1, p1, z1)
        z2 = jnp.where(rowc < 2, p2, z2)
        acc = cb + cw0 * z2 + cw1 * z1 + cw2 * z
        yc_ref[rows, :] = (cb_gate * acc).astype(BF16)
        zprev = z
        q_ref[rows, :] = (_dot(hs, w_ref[3]) * (LOG2E * HEAD_DIM ** -0.5)).astype(BF16)
        k_ref[rows, :] = _dot(hs, w_ref[4]).astype(BF16)
        v_ref[rows, :] = _dot(hs, w_ref[5]).astype(BF16)
        sgc_ref[rows, :] = jax.nn.sigmoid(_dot(hs, w_ref[6])).astype(BF16)
        sga_ref[rows, :] = jax.nn.sigmoid(_dot(hs, w_ref[7])).astype(BF16)


def _in_proj(x2d, g, w_stack, wf, bfv, conv_w, conv_b, batch, seq):
    tokens = batch * seq
    tn = IN_PROJ_COLS
    nj = D_MODEL // tn
    col_out = pl.BlockSpec((seq, tn), lambda b, j: (b, j))
    out_bf16 = jax.ShapeDtypeStruct((tokens, D_MODEL), BF16)
    return pl.pallas_call(
        _in_proj_kernel,
        grid=(batch, nj),
        in_specs=[
            pl.BlockSpec((seq, D_MODEL), lambda b, j: (b, 0)),
            pl.BlockSpec((1, D_MODEL), lambda b, j: (0, 0)),
            pl.BlockSpec((8, D_MODEL, tn), lambda b, j: (0, 0, j)),
            pl.BlockSpec((D_MODEL, LANES), lambda b, j: (0, 0)),
            pl.BlockSpec((1, LANES), lambda b, j: (0, 0)),
            pl.BlockSpec((CONV_WIDTH, tn), lambda b, j: (0, j)),
            pl.BlockSpec((1, tn), lambda b, j: (0, j)),
        ],
        out_specs=[col_out] * 6 + [pl.BlockSpec((1, seq, LANES), lambda b, j: (b, 0, 0))] * 2,
        out_shape=[out_bf16] * 6 + [jax.ShapeDtypeStruct((batch, seq, LANES), BF16)] * 2,
        scratch_shapes=[pltpu.VMEM((seq, D_MODEL), BF16)],
        compiler_params=pltpu.CompilerParams(
            dimension_semantics=("arbitrary", "arbitrary"), vmem_limit_bytes=VMEM_LIMIT),
        name="in_proj",
    )(x2d, g, w_stack, wf, bfv, conv_w, conv_b)


def _attention_kernel(q_ref, k_ref, v_ref, cq_ref, ck_ref, o_ref, qa_scr, ka_scr, va_scr):
    hp = pl.program_id(1)
    seq = q_ref.shape[0]
    blk = ATT_BLOCK
    lane = lax.broadcasted_iota(jnp.int32, (1, LANES), 1)
    zero = jnp.zeros((), BF16)

    own, base = [], []
    for hh in range(HEADS_PER_BLOCK):
        own.append((lane // HEAD_DIM) == hh)
        base.append(((hh + 1) % HEADS_PER_BLOCK) * HEAD_DIM)
        gate = (lane % N_HEADS) == hp * HEADS_PER_BLOCK + hh
        qa_scr[hh, :, 0:LANES] = jnp.where(own[hh], q_ref[...], zero)
        qa_scr[hh, :, LANES:2 * LANES] = jnp.where(gate, cq_ref[0], zero)
        ka_scr[hh, :, 0:LANES] = jnp.where(own[hh], k_ref[...], zero)
        ka_scr[hh, :, LANES:2 * LANES] = jnp.where(gate, ck_ref[0], zero)
        va_scr[hh] = jnp.where(own[hh], v_ref[...],
                               jnp.where(lane == base[hh], 1.0, 0.0).astype(BF16))

    for rg in range(seq // blk):
        rows = slice(rg * blk, (rg + 1) * blk)
        out = None
        for hh in range(HEADS_PER_BLOCK):
            qa = qa_scr[hh, rows, :]
            m = acc = None
            k_end = (rg + 1) * blk
            for k0 in range(0, k_end, ATT_KEYS):
                k1 = min(k0 + ATT_KEYS, k_end)
                keys = slice(k0, k1)
                s = lax.dot_general(qa, ka_scr[hh, keys, :], (((1,), (1,)), ((), ())),
                                    preferred_element_type=F32)
                if k1 > rg * blk:
                    qpos = rg * blk + lax.broadcasted_iota(jnp.int32, s.shape, 0)
                    kpos = k0 + lax.broadcasted_iota(jnp.int32, s.shape, 1)
                    s = jnp.where(kpos <= qpos, s, NEG_BIG)
                mx = jnp.max(s, axis=1, keepdims=True)
                if m is None:
                    m = mx
                    acc = _dot(jnp.exp2(s - m).astype(BF16), va_scr[hh, keys, :])
                else:
                    m_new = jnp.maximum(m, mx)
                    acc = jnp.exp2(m - m_new) * acc + _dot(jnp.exp2(s - m_new).astype(BF16),
                                                           va_scr[hh, keys, :])
                    m = m_new
            res = acc / acc[:, base[hh]:base[hh] + 1]
            out = res if out is None else jnp.where(own[hh], res, out)
        o_ref[rows, :] = out.astype(o_ref.dtype)


def _attention(q, k, v, cq, ck, batch, seq):
    tokens = batch * seq
    n_hp = N_HEADS // HEADS_PER_BLOCK
    blk = pl.BlockSpec((seq, LANES), lambda b, hp: (b, hp))
    gate_blk = pl.BlockSpec((1, seq, LANES), lambda b, hp: (b, 0, 0))
    wide = pltpu.VMEM((HEADS_PER_BLOCK, seq, 2 * LANES), BF16)
    return pl.pallas_call(
        _attention_kernel,
        grid=(batch, n_hp),
        in_specs=[blk, blk, blk, gate_blk, gate_blk],
        out_specs=blk,
        out_shape=jax.ShapeDtypeStruct((tokens, D_MODEL), BF16),
        scratch_shapes=[wide, wide, pltpu.VMEM((HEADS_PER_BLOCK, seq, LANES), BF16)],
        compiler_params=pltpu.CompilerParams(
            dimension_semantics=("arbitrary", "arbitrary"), vmem_limit_bytes=VMEM_LIMIT),
        name="fox_attention",
    )(q, k, v, cq, ck)


def _mixer_out_kernel(yc_ref, o_ref, sgc_ref, sga_ref, x_ref, wco_ref, wao_ref, wo_ref,
                      g_ref, wr_ref, x1_ref, h2_ref, route_ref, route_t_ref, counts_ref,
                      carry_scr, logit_scr):
    i = pl.program_id(0)
    tm = x_ref.shape[0]

    @pl.when(i == 0)
    def _():
        carry_scr[...] = jnp.zeros_like(carry_scr)
        logit_scr[...] = jnp.zeros_like(logit_scr)

    logits = logit_scr[...]
    lane = lax.broadcasted_iota(jnp.int32, (tm, LANES), 1).astype(F32)
    far = jnp.float32(4 * LANES)

    def first_lane_of_max(vals, vmax):
        return jnp.min(jnp.where(vals == vmax, lane, far), axis=1, keepdims=True)

    lg = jnp.where(lane < N_GROUPS, logits, NEG_BIG)
    gmax = jnp.max(lg, axis=1, keepdims=True)
    g_val = 1.0 / jnp.sum(jnp.exp(lg - gmax), axis=1, keepdims=True)
    g_idx = first_lane_of_max(lg, gmax)
    lo = N_GROUPS + EXPERTS_PER_GROUP * g_idx
    le = jnp.where((lane >= lo) & (lane < lo + EXPERTS_PER_GROUP), logits, NEG_BIG)
    e1max = jnp.max(le, axis=1, keepdims=True)
    e1 = first_lane_of_max(le, e1max)
    le2 = jnp.where(lane == e1, NEG_BIG, le)
    e2max = jnp.max(le2, axis=1, keepdims=True)
    e2 = first_lane_of_max(le2, e2max)
    ratio = jnp.exp(e2max - e1max)
    w1 = g_val / (1.0 + ratio)
    w2 = g_val * ratio / (1.0 + ratio)

    oh1 = lane == e1
    oh2 = lane == e2
    onehot = jnp.where(oh1 | oh2, 1.0, 0.0).astype(BF16)
    r_i = lax.broadcasted_iota(jnp.int32, (tm, tm), 0)
    c_i = lax.broadcasted_iota(jnp.int32, (tm, tm), 1)
    strict_lower = jnp.where(c_i < r_i, 1.0, 0.0).astype(BF16)
    carry = carry_scr[...]
    before = _dot(strict_lower, onehot) + carry
    rank1 = jnp.sum(jnp.where(oh1, before, 0.0), axis=1, keepdims=True)
    rank2 = jnp.sum(jnp.where(oh2, before, 0.0), axis=1, keepdims=True)
    is_tile = jnp.where(i > 0, 1.0, 0.0)
    carry = carry + is_tile * jnp.sum(onehot.astype(F32), axis=0, keepdims=True)
    carry_scr[...] = carry
    counts_ref[...] = carry

    route = jnp.where(lane == 0, e1 - N_GROUPS, 0.0)
    route = jnp.where(lane == 1, e2 - N_GROUPS, route)
    route = jnp.where(lane == 2, w1, route)
    route = jnp.where(lane == 3, w2, route)
    route = jnp.where(lane == 4, rank1, route)
    route = jnp.where(lane == 5, rank2, route)
    route_ref[...] = route
    route_t_ref[...] = route.T[0:ROUTE_ROWS, :]

    y_conv = _dot(yc_ref[...], wco_ref[...])
    y_att = _dot(o_ref[...], wao_ref[...])
    m = sgc_ref[...].astype(F32) * y_conv + sga_ref[...].astype(F32) * y_att
    x1 = x_ref[...] + _dot(m.astype(BF16), wo_ref[...])
    x1_ref[...] = x1
    ms = jnp.mean(x1 * x1, axis=-1, keepdims=True)
    h2 = x1 * lax.rsqrt(ms + EPS) * g_ref[...]
    h2_ref[...] = _pack_bf16_pairs(h2)
    logit_scr[...] = _dot(h2.astype(BF16), wr_ref[...])


def _mixer_out(yc, o, sgc, sga, x2d, wco, wao, wo, g, wr):
    tokens = x2d.shape[0]
    tm = MIX_ROWS
    n_tiles = tokens // tm

    def cur(i):
        return (jnp.minimum(i, n_tiles - 1), 0)

    def prev(i):
        return (jnp.maximum(i - 1, 0), 0)

    row_blk = pl.BlockSpec((tm, D_MODEL), cur)
    w_blk = pl.BlockSpec((D_MODEL, D_MODEL), lambda i: (0, 0))
    return pl.pallas_call(
        _mixer_out_kernel,
        grid=(n_tiles + 1,),
        in_specs=[row_blk, row_blk, row_blk, row_blk, row_blk, w_blk, w_blk, w_blk,
                  pl.BlockSpec((1, D_MODEL), lambda i: (0, 0)),
                  pl.BlockSpec((D_MODEL, LANES), lambda i: (0, 0))],
        out_specs=[row_blk,
                   pl.BlockSpec((tm, D_MODEL // 2), cur),
                   pl.BlockSpec((tm, LANES), prev),
                   pl.BlockSpec((ROUTE_ROWS, tm), lambda i: (0, jnp.maximum(i - 1, 0))),
                   pl.BlockSpec((1, LANES), lambda i: (0, 0))],
        out_shape=[jax.ShapeDtypeStruct((tokens, D_MODEL), F32),
                   jax.ShapeDtypeStruct((tokens, D_MODEL // 2), jnp.uint32),
                   jax.ShapeDtypeStruct((tokens, LANES), F32),
                   jax.ShapeDtypeStruct((ROUTE_ROWS, tokens), F32),
                   jax.ShapeDtypeStruct((1, LANES), F32)],
        scratch_shapes=[pltpu.VMEM((1, LANES), F32), pltpu.VMEM((tm, LANES), F32)],
        compiler_params=pltpu.CompilerParams(
            dimension_semantics=("arbitrary",), vmem_limit_bytes=VMEM_LIMIT),
        name="mixer_out",
    )(yc, o, sgc, sga, x2d, wco, wao, wo, g, wr)


def _sc_index_rows(indices):
    n = indices.shape[0]
    return jnp.pad(indices.reshape(n // SC_WINDOW, SC_WINDOW), ((0, 0), (0, LANES - SC_WINDOW)))


def _sc_mesh():
    return plsc.VectorSubcoreMesh(core_axis_name="c", subcore_axis_name="s")


def _sc_scatter_rows(data, indices, n_out):
    n = indices.shape[0]
    n_src, width = data.shape
    src_blocks = n_src // SC_WINDOW

    @pl.kernel(out_type=jax.ShapeDtypeStruct((n_out, width), data.dtype), mesh=_sc_mesh())
    def scatter_kernel(x_hbm, i_hbm, o_hbm):
        def body(x_vmem, i_vmem):
            pltpu.sync_copy(x_vmem, o_hbm.at[i_vmem.at[0, pl.ds(0, SC_WINDOW)]])

        pltpu.emit_pipeline(
            body,
            grid=(n // SC_WINDOW,),
            in_specs=[pl.BlockSpec((SC_WINDOW, width), lambda i: (i % src_blocks, 0)),
                      pl.BlockSpec((1, LANES), lambda i: (i, 0))],
            out_specs=[],
            core_axis_name=("c", "s"),
            dimension_semantics=(pltpu.PARALLEL,),
        )(x_hbm, i_hbm)

    return scatter_kernel(data, _sc_index_rows(indices))


def _sc_gather_rows(data, indices):
    n = indices.shape[0]
    width = data.shape[1]

    @pl.kernel(out_type=jax.ShapeDtypeStruct((n, width), data.dtype), mesh=_sc_mesh())
    def gather_kernel(x_hbm, i_hbm, o_hbm):
        def body(i_vmem, o_vmem):
            pltpu.sync_copy(x_hbm.at[i_vmem.at[0, pl.ds(0, SC_WINDOW)]], o_vmem)

        pltpu.emit_pipeline(
            body,
            grid=(n // SC_WINDOW,),
            in_specs=[pl.BlockSpec((1, LANES), lambda i: (i, 0))],
            out_specs=[pl.BlockSpec((SC_WINDOW, width), lambda i: (i, 0))],
            core_axis_name=("c", "s"),
            dimension_semantics=(pltpu.PARALLEL,),
        )(i_hbm, o_hbm)

    return gather_kernel(data, _sc_index_rows(indices))


def _expert_kernel(block_e_ref, n_valid_ref, x_ref, wg_ref, wu_ref, wd_ref, y_ref,
                   wg_scr, wu_scr, wd_scr):
    i = pl.program_id(0)
    n_valid = n_valid_ref[i]

    @pl.when((i == 0) | (block_e_ref[i] != block_e_ref[jnp.maximum(i - 1, 0)]))
    def _():
        wg_scr[...] = wg_ref[0].astype(BF16)
        wu_scr[...] = wu_ref[0].astype(BF16)
        wd_scr[...] = wd_ref[0].astype(BF16)

    @pl.when(n_valid > 0)
    def _():
        row = lax.broadcasted_iota(jnp.int32, x_ref.shape, 0)
        packed = jnp.where(row < n_valid, x_ref[...], jnp.uint32(0))
        xb = _unpack_bf16_pairs(packed).astype(BF16)
        a = _dot(xb, wg_scr[...])
        u = _dot(xb, wu_scr[...])
        hmid = (a * jax.nn.sigmoid(a) * u).astype(BF16)
        y_ref[...] = _pack_bf16_pairs(_dot(hmid, wd_scr[...]))

    @pl.when(n_valid == 0)
    def _():
        y_ref[...] = jnp.zeros_like(y_ref)


def _experts(block_e, n_valid, x_rows, wg, wu, wd):
    n_rows, half = x_rows.shape
    n_blocks = n_rows // MOE_BLOCK

    def w_map(i, be, nv):
        return (be[i], 0, 0)

    row_blk = pl.BlockSpec((MOE_BLOCK, half), lambda i, be, nv: (i, 0))
    grid_spec = pltpu.PrefetchScalarGridSpec(
        num_scalar_prefetch=2,
        grid=(n_blocks,),
        in_specs=[row_blk,
                  pl.BlockSpec((1, D_MODEL, D_EXPERT), w_map),
                  pl.BlockSpec((1, D_MODEL, D_EXPERT), w_map),
                  pl.BlockSpec((1, D_EXPERT, D_MODEL), w_map)],
        out_specs=row_blk,
        scratch_shapes=[pltpu.VMEM((D_MODEL, D_EXPERT), BF16), pltpu.VMEM((D_MODEL, D_EXPERT), BF16),
                        pltpu.VMEM((D_EXPERT, D_MODEL), BF16)],
    )
    return pl.pallas_call(
        _expert_kernel,
        grid_spec=grid_spec,
        out_shape=jax.ShapeDtypeStruct((n_rows, half), jnp.uint32),
        compiler_params=pltpu.CompilerParams(
            dimension_semantics=("arbitrary",), vmem_limit_bytes=VMEM_LIMIT),
        name="moe_experts",
    )(block_e, n_valid, x_rows, wg, wu, wd)


def _combine_kernel(ya_ref, yb_ref, x1_ref, route_ref, g_ref, out_ref):
    route = route_ref[...]
    w1 = route[:, 2:3]
    w2 = route[:, 3:4]
    x2 = x1_ref[...] + (_unpack_bf16_pairs(ya_ref[...]) * w1 + _unpack_bf16_pairs(yb_ref[...]) * w2)
    ms = jnp.mean(x2 * x2, axis=-1, keepdims=True)
    out_ref[...] = x2 * lax.rsqrt(ms + EPS) * g_ref[...]


def _combine(y_pairs, x1, route, g):
    tokens = x1.shape[0]
    tc = MIX_ROWS
    n_tiles = tokens // tc
    row_blk = pl.BlockSpec((tc, D_MODEL), lambda i: (i, 0))
    return pl.pallas_call(
        _combine_kernel,
        grid=(n_tiles,),
        in_specs=[pl.BlockSpec((tc, D_MODEL // 2), lambda i: (i, 0)),
                  pl.BlockSpec((tc, D_MODEL // 2), lambda i: (i + n_tiles, 0)),
                  row_blk,
                  pl.BlockSpec((tc, LANES), lambda i: (i, 0)),
                  pl.BlockSpec((1, D_MODEL), lambda i: (0, 0))],
        out_specs=row_blk,
        out_shape=jax.ShapeDtypeStruct((tokens, D_MODEL), F32),
        compiler_params=pltpu.CompilerParams(
            dimension_semantics=("arbitrary",), vmem_limit_bytes=VMEM_LIMIT),
        name="moe_combine",
    )(y_pairs, y_pairs, x1, route, g)


def kernel(x, norm_mix_g, w_in, conv_w, conv_b, b_forget, w_conv_out, w_att_out, w_out,
           norm_ffn_g, w_router_group, w_router_expert, w_e_gate, w_e_up, w_e_down,
           norm_final_g):
    batch, seq, d = x.shape
    assert d == D_MODEL and seq % max(ATT_BLOCK, IN_PROJ_ROWS) == 0
    tokens = batch * seq
    assert tokens % MIX_ROWS == 0 and tokens % SC_WINDOW == 0
    x2d = x.reshape(tokens, D_MODEL)

    wb = w_in.astype(BF16)
    n_main = 6 * D_MODEL
    gate0 = n_main + N_HEADS
    pieces = [wb[:, i * D_MODEL:(i + 1) * D_MODEL] for i in range(6)]
    pieces += [wb[:, gate0:gate0 + D_MODEL], wb[:, gate0 + D_MODEL:gate0 + 2 * D_MODEL]]
    w_stack = jnp.stack(pieces, axis=0)
    n_rep = 6
    wf = jnp.pad(jnp.tile(wb[:, n_main:gate0], (1, n_rep)), ((0, 0), (0, LANES - n_rep * N_HEADS)))
    bfv = jnp.pad(jnp.tile(b_forget.astype(F32), n_rep), (0, LANES - n_rep * N_HEADS)).reshape(1, LANES)
    wr = jnp.concatenate(
        [w_router_group, jnp.transpose(w_router_expert, (1, 0, 2)).reshape(D_MODEL, N_EXPERTS)], axis=1)
    wr = jnp.pad(wr, ((0, 0), (0, LANES - wr.shape[1]))).astype(BF16)

    yc, q, k, v, sgc, sga, cq, ck = _in_proj(
        x2d, norm_mix_g.reshape(1, D_MODEL), w_stack, wf, bfv, conv_w,
        conv_b.reshape(1, D_MODEL), batch, seq)
    o = _attention(q, k, v, cq, ck, batch, seq)

    x1, h2, route, route_t, counts = _mixer_out(
        yc, o, sgc, sga, x2d, w_conv_out.astype(BF16), w_att_out.astype(BF16),
        w_out.astype(BF16), norm_ffn_g.reshape(1, D_MODEL), wr)

    counts = counts[0, N_GROUPS:N_GROUPS + N_EXPERTS].astype(jnp.int32)
    padded = (counts + MOE_BLOCK - 1) // MOE_BLOCK * MOE_BLOCK
    end_padded = jnp.cumsum(padded)
    start_padded = end_padded - padded
    expert = route_t[0:2].astype(jnp.int32)
    rank = route_t[4:6].astype(jnp.int32)
    expert_ids = jnp.arange(N_EXPERTS, dtype=jnp.int32)[:, None, None]
    dest = rank + jnp.sum(jnp.where(expert[None] == expert_ids, start_padded[:, None, None], 0), axis=0)
    dest_flat = dest.reshape(2 * tokens)
    n_rows = tokens * 2 + N_EXPERTS * MOE_BLOCK
    block_start = jnp.arange(n_rows // MOE_BLOCK, dtype=jnp.int32) * MOE_BLOCK
    block_e = jnp.minimum(jnp.sum(end_padded[None, :] <= block_start[:, None], axis=1),
                          N_EXPERTS - 1).astype(jnp.int32)
    used_lo = jnp.maximum(start_padded[None, :], block_start[:, None])
    used_hi = jnp.minimum((start_padded + counts)[None, :], block_start[:, None] + MOE_BLOCK)
    n_valid = jnp.sum(jnp.maximum(used_hi - used_lo, 0), axis=1).astype(jnp.int32)

    x_rows = _sc_scatter_rows(h2, dest_flat, n_rows)
    y_rows = _experts(block_e, n_valid, x_rows, w_e_gate, w_e_up, w_e_down)
    y_pairs = _sc_gather_rows(y_rows, dest_flat)
    out = _combine(y_pairs, x1, route, norm_final_g.reshape(1, D_MODEL))
    return out.reshape(batch, seq, D_MODEL)
```

```python
import functools

import jax
import jax.numpy as jnp
from jax import lax
from jax.experimental import pallas as pl
from jax.experimental.pallas import tpu as pltpu
from jax.experimental.pallas import tpu_sc as plsc

D_MODEL = 1024
HEAD_DIM = 64
N_HEADS = 16
N_GROUPS = 4
EXPERTS_PER_GROUP = 8
N_EXPERTS = N_GROUPS * EXPERTS_PER_GROUP
D_EXPERT = 512
MOE_BLOCK = 512
MOE_SUB = 256
MOE_SPLITS = 2
CONV_WIDTH = 3
EPS = 1e-6

LANES = 128
HEADS_PER_BLOCK = LANES // HEAD_DIM
NEG_BIG = -1e30
LOG2E = 1.4426950408889634

IN_PROJ_COLS = 256
IN_PROJ_ROWS = 512
ATT_BLOCK = 256
ATT_KEYS = 256
MIX_ROWS = 512
ROUTE_ROWS = 8
SC_WINDOW = 64
VMEM_LIMIT = 56 * 1024 * 1024

F32 = jnp.float32
BF16 = jnp.bfloat16


def _dot(a, b):
    return jnp.dot(a, b, preferred_element_type=F32)


def _pack_bf16_pairs(x):
    c = x.shape[1] // 2
    lo = lax.bitcast_convert_type(x[:, :c].astype(BF16).astype(F32), jnp.uint32) >> 16
    hi = lax.bitcast_convert_type(x[:, c:].astype(BF16).astype(F32), jnp.uint32) & jnp.uint32(0xFFFF0000)
    return lo | hi


def _unpack_bf16_pairs(w):
    lo = lax.bitcast_convert_type(w << 16, F32)
    hi = lax.bitcast_convert_type(w & jnp.uint32(0xFFFF0000), F32)
    return jnp.concatenate([lo, hi], axis=1)


def _in_proj_kernel(x_ref, g_ref, w_ref, wf_ref, bf_ref, cw_ref, cb_ref,
                    yc_ref, q_ref, k_ref, v_ref, sgc_ref, sga_ref, cq_ref, ck_ref, h_scr):
    j = pl.program_id(1)
    seq = x_ref.shape[0]
    n_chunks = seq // IN_PROJ_ROWS

    @pl.when(j == 0)
    def _():
        for r in range(n_chunks):
            rows = slice(r * IN_PROJ_ROWS, (r + 1) * IN_PROJ_ROWS)
            xs = x_ref[rows, :]
            ms = jnp.mean(xs * xs, axis=-1, keepdims=True)
            h_scr[rows, :] = (xs * lax.rsqrt(ms + EPS) * g_ref[...]).astype(BF16)
        f = _dot(h_scr[...], wf_ref[...]) + bf_ref[...]
        c = jnp.minimum(f, 0.0) - jnp.log(1.0 + jnp.exp(-jnp.abs(f)))
        row = lax.broadcasted_iota(jnp.int32, c.shape, 0)
        d = 1
        while d < seq:
            c = c + jnp.where(row >= d, pltpu.roll(c, d, axis=0), 0.0)
            d *= 2
        c = c * LOG2E
        hi = c.astype(BF16).astype(F32)
        mid = (c - hi).astype(BF16).astype(F32)
        lo = (c - hi - mid).astype(BF16).astype(F32)
        grp = lax.broadcasted_iota(jnp.int32, c.shape, 1) // N_HEADS
        terms = jnp.where(grp % 3 == 0, hi, jnp.where(grp % 3 == 1, mid, lo))
        one = jnp.float32(1.0)
        cq_ref[0] = jnp.where(grp < 3, terms, jnp.where(grp < 6, one, 0.0)).astype(BF16)
        ck_ref[0] = jnp.where(grp < 3, one, jnp.where(grp < 6, -terms, 0.0)).astype(BF16)

    cw0 = cw_ref[0:1, :]
    cw1 = cw_ref[1:2, :]
    cw2 = cw_ref[2:3, :]
    cb = cb_ref[...]
    rowc = lax.broadcasted_iota(jnp.int32, (IN_PROJ_ROWS, IN_PROJ_COLS), 0)
    zprev = None
    for r in range(n_chunks):
        rows = slice(r * IN_PROJ_ROWS, (r + 1) * IN_PROJ_ROWS)
        hs = h_scr[rows, :]
        cb_gate = _dot(hs, w_ref[0])
        z = _dot(hs, w_ref[1]) * _dot(hs, w_ref[2])
        z1 = pltpu.roll(z, 1, axis=0)
        z2 = pltpu.roll(z, 2, axis=0)
        if zprev is None:
            p1 = jnp.zeros_like(z)
            p2 = p1
        else:
            p1 = pltpu.roll(zprev, 1, axis=0)
            p2 = pltpu.roll(zprev, 2, axis=0)
        z1 = jnp.where(rowc < 1, p1, z1)
        z2 = jnp.where(rowc < 2, p2, z2)
        acc = cb + cw0 * z2 + cw1 * z1 + cw2 * z
        yc_ref[rows, :] = (cb_gate * acc).astype(BF16)
        zprev = z
        q_ref[rows, :] = (_dot(hs, w_ref[3]) * (LOG2E * HEAD_DIM ** -0.5)).astype(BF16)
        k_ref[rows, :] = _dot(hs, w_ref[4]).astype(BF16)
        v_ref[rows, :] = _dot(hs, w_ref[5]).astype(BF16)
        sgc_ref[rows, :] = jax.nn.sigmoid(_dot(hs, w_ref[6])).astype(BF16)
        sga_ref[rows, :] = jax.nn.sigmoid(_dot(hs, w_ref[7])).astype(BF16)


def _in_proj(x2d, g, w_stack, wf, bfv, conv_w, conv_b, batch, seq):
    tokens = batch * seq
    tn = IN_PROJ_COLS
    nj = D_MODEL // tn
    col_out = pl.BlockSpec((seq, tn), lambda b, j: (b, j))
    out_bf16 = jax.ShapeDtypeStruct((tokens, D_MODEL), BF16)
    return pl.pallas_call(
        _in_proj_kernel,
        grid=(batch, nj),
        in_specs=[
            pl.BlockSpec((seq, D_MODEL), lambda b, j: (b, 0)),
            pl.BlockSpec((1, D_MODEL), lambda b, j: (0, 0)),
            pl.BlockSpec((8, D_MODEL, tn), lambda b, j: (0, 0, j)),
            pl.BlockSpec((D_MODEL, LANES), lambda b, j: (0, 0)),
            pl.BlockSpec((1, LANES), lambda b, j: (0, 0)),
            pl.BlockSpec((CONV_WIDTH, tn), lambda b, j: (0, j)),
            pl.BlockSpec((1, tn), lambda b, j: (0, j)),
        ],
        out_specs=[col_out] * 6 + [pl.BlockSpec((1, seq, LANES), lambda b, j: (b, 0, 0))] * 2,
        out_shape=[out_bf16] * 6 + [jax.ShapeDtypeStruct((batch, seq, LANES), BF16)] * 2,
        scratch_shapes=[pltpu.VMEM((seq, D_MODEL), BF16)],
        compiler_params=pltpu.CompilerParams(
            dimension_semantics=("arbitrary", "arbitrary"), vmem_limit_bytes=VMEM_LIMIT),
        name="in_proj",
    )(x2d, g, w_stack, wf, bfv, conv_w, conv_b)


def _attention_kernel(q_ref, k_ref, v_ref, cq_ref, ck_ref, o_ref, qa_scr, ka_scr, va_scr):
    hp = pl.program_id(1)
    seq = q_ref.shape[0]
    blk = ATT_BLOCK
    lane = lax.broadcasted_iota(jnp.int32, (1, LANES), 1)
    zero = jnp.zeros((), BF16)

    own, base = [], []
    for hh in range(HEADS_PER_BLOCK):
        own.append((lane // HEAD_DIM) == hh)
        base.append(((hh + 1) % HEADS_PER_BLOCK) * HEAD_DIM)
        gate = (lane % N_HEADS) == hp * HEADS_PER_BLOCK + hh
        qa_scr[hh, :, 0:LANES] = jnp.where(own[hh], q_ref[...], zero)
        qa_scr[hh, :, LANES:2 * LANES] = jnp.where(gate, cq_ref[0], zero)
        ka_scr[hh, :, 0:LANES] = jnp.where(own[hh], k_ref[...], zero)
        ka_scr[hh, :, LANES:2 * LANES] = jnp.where(gate, ck_ref[0], zero)
        va_scr[hh] = jnp.where(own[hh], v_ref[...],
                               jnp.where(lane == base[hh], 1.0, 0.0).astype(BF16))

    for rg in range(seq // blk):
        rows = slice(rg * blk, (rg + 1) * blk)
        out = None
        for hh in range(HEADS_PER_BLOCK):
            qa = qa_scr[hh, rows, :]
            m = acc = None
            k_end = (rg + 1) * blk
            for k0 in range(0, k_end, ATT_KEYS):
                k1 = min(k0 + ATT_KEYS, k_end)
                keys = slice(k0, k1)
                s = lax.dot_general(qa, ka_scr[hh, keys, :], (((1,), (1,)), ((), ())),
                                    preferred_element_type=F32)
                if k1 > rg * blk:
                    qpos = rg * blk + lax.broadcasted_iota(jnp.int32, s.shape, 0)
                    kpos = k0 + lax.broadcasted_iota(jnp.int32, s.shape, 1)
                    s = jnp.where(kpos <= qpos, s, NEG_BIG)
                mx = jnp.max(s, axis=1, keepdims=True)
                if m is None:
                    m = mx
                    acc = _dot(jnp.exp2(s - m).astype(BF16), va_scr[hh, keys, :])
                else:
                    m_new = jnp.maximum(m, mx)
                    acc = jnp.exp2(m - m_new) * acc + _dot(jnp.exp2(s - m_new).astype(BF16),
                                                           va_scr[hh, keys, :])
                    m = m_new
            res = acc / acc[:, base[hh]:base[hh] + 1]
            out = res if out is None else jnp.where(own[hh], res, out)
        o_ref[rows, :] = out.astype(o_ref.dtype)


def _attention(q, k, v, cq, ck, batch, seq):
    tokens = batch * seq
    n_hp = N_HEADS // HEADS_PER_BLOCK
    blk = pl.BlockSpec((seq, LANES), lambda b, hp: (b, hp))
    gate_blk = pl.BlockSpec((1, seq, LANES), lambda b, hp: (b, 0, 0))
    wide = pltpu.VMEM((HEADS_PER_BLOCK, seq, 2 * LANES), BF16)
    return pl.pallas_call(
        _attention_kernel,
        grid=(batch, n_hp),
        in_specs=[blk, blk, blk, gate_blk, gate_blk],
        out_specs=blk,
        out_shape=jax.ShapeDtypeStruct((tokens, D_MODEL), BF16),
        scratch_shapes=[wide, wide, pltpu.VMEM((HEADS_PER_BLOCK, seq, LANES), BF16)],
        compiler_params=pltpu.CompilerParams(
            dimension_semantics=("arbitrary", "arbitrary"), vmem_limit_bytes=VMEM_LIMIT),
        name="fox_attention",
    )(q, k, v, cq, ck)


def _mixer_out_kernel(tiles_per_split, yc_ref, o_ref, sgc_ref, sga_ref, x_ref, wco_ref, wao_ref, wo_ref,
                      g_ref, wr_ref, x1_ref, h2_ref, route_ref, route_t_ref, counts_ref,
                      carry_scr, logit_scr):
    i = pl.program_id(0)
    tm = x_ref.shape[0]

    @pl.when(i == 0)
    def _():
        carry_scr[...] = jnp.zeros_like(carry_scr)
        logit_scr[...] = jnp.zeros_like(logit_scr)

    logits = logit_scr[...]
    lane = lax.broadcasted_iota(jnp.int32, (tm, LANES), 1).astype(F32)
    far = jnp.float32(4 * LANES)

    def first_lane_of_max(vals, vmax):
        return jnp.min(jnp.where(vals == vmax, lane, far), axis=1, keepdims=True)

    lg = jnp.where(lane < N_GROUPS, logits, NEG_BIG)
    gmax = jnp.max(lg, axis=1, keepdims=True)
    g_val = 1.0 / jnp.sum(jnp.exp(lg - gmax), axis=1, keepdims=True)
    g_idx = first_lane_of_max(lg, gmax)
    lo = N_GROUPS + EXPERTS_PER_GROUP * g_idx
    le = jnp.where((lane >= lo) & (lane < lo + EXPERTS_PER_GROUP), logits, NEG_BIG)
    e1max = jnp.max(le, axis=1, keepdims=True)
    e1 = first_lane_of_max(le, e1max)
    le2 = jnp.where(lane == e1, NEG_BIG, le)
    e2max = jnp.max(le2, axis=1, keepdims=True)
    e2 = first_lane_of_max(le2, e2max)
    ratio = jnp.exp(e2max - e1max)
    w1 = g_val / (1.0 + ratio)
    w2 = g_val * ratio / (1.0 + ratio)

    oh1 = lane == e1
    oh2 = lane == e2
    onehot = jnp.where(oh1 | oh2, 1.0, 0.0).astype(BF16)
    r_i = lax.broadcasted_iota(jnp.int32, (tm, tm), 0)
    c_i = lax.broadcasted_iota(jnp.int32, (tm, tm), 1)
    strict_lower = jnp.where(c_i < r_i, 1.0, 0.0).astype(BF16)
    keep = jnp.where((i - 1) % tiles_per_split == 0, 0.0, 1.0)
    carry = keep * carry_scr[...]
    before = _dot(strict_lower, onehot) + carry
    rank1 = jnp.sum(jnp.where(oh1, before, 0.0), axis=1, keepdims=True)
    rank2 = jnp.sum(jnp.where(oh2, before, 0.0), axis=1, keepdims=True)
    is_tile = jnp.where(i > 0, 1.0, 0.0)
    carry = carry + is_tile * jnp.sum(onehot.astype(F32), axis=0, keepdims=True)
    carry_scr[...] = carry
    counts_ref[0] = carry

    route = jnp.where(lane == 0, e1 - N_GROUPS, 0.0)
    route = jnp.where(lane == 1, e2 - N_GROUPS, route)
    route = jnp.where(lane == 2, w1, route)
    route = jnp.where(lane == 3, w2, route)
    route = jnp.where(lane == 4, rank1, route)
    route = jnp.where(lane == 5, rank2, route)
    route_ref[...] = route
    route_t_ref[...] = route.T[0:ROUTE_ROWS, :]

    y_conv = _dot(yc_ref[...], wco_ref[...])
    y_att = _dot(o_ref[...], wao_ref[...])
    m = sgc_ref[...].astype(F32) * y_conv + sga_ref[...].astype(F32) * y_att
    x1 = x_ref[...] + _dot(m.astype(BF16), wo_ref[...])
    x1_ref[...] = x1
    ms = jnp.mean(x1 * x1, axis=-1, keepdims=True)
    h2 = x1 * lax.rsqrt(ms + EPS) * g_ref[...]
    h2_ref[...] = _pack_bf16_pairs(h2)
    logit_scr[...] = _dot(h2.astype(BF16), wr_ref[...])


def _mixer_out(yc, o, sgc, sga, x2d, wco, wao, wo, g, wr):
    tokens = x2d.shape[0]
    tm = MIX_ROWS
    n_tiles = tokens // tm

    def cur(i):
        return (jnp.minimum(i, n_tiles - 1), 0)

    def prev(i):
        return (jnp.maximum(i - 1, 0), 0)

    row_blk = pl.BlockSpec((tm, D_MODEL), cur)
    w_blk = pl.BlockSpec((D_MODEL, D_MODEL), lambda i: (0, 0))
    return pl.pallas_call(
        functools.partial(_mixer_out_kernel, n_tiles // MOE_SPLITS),
        grid=(n_tiles + 1,),
        in_specs=[row_blk, row_blk, row_blk, row_blk, row_blk, w_blk, w_blk, w_blk,
                  pl.BlockSpec((1, D_MODEL), lambda i: (0, 0)),
                  pl.BlockSpec((D_MODEL, LANES), lambda i: (0, 0))],
        out_specs=[row_blk,
                   pl.BlockSpec((tm, D_MODEL // 2), cur),
                   pl.BlockSpec((tm, LANES), prev),
                   pl.BlockSpec((ROUTE_ROWS, tm), lambda i: (0, jnp.maximum(i - 1, 0))),
                   pl.BlockSpec((1, 1, LANES),
                                lambda i: (jnp.maximum(i - 1, 0) // (n_tiles // MOE_SPLITS), 0, 0))],
        out_shape=[jax.ShapeDtypeStruct((tokens, D_MODEL), F32),
                   jax.ShapeDtypeStruct((tokens, D_MODEL // 2), jnp.uint32),
                   jax.ShapeDtypeStruct((tokens, LANES), F32),
                   jax.ShapeDtypeStruct((ROUTE_ROWS, tokens), F32),
                   jax.ShapeDtypeStruct((MOE_SPLITS, 1, LANES), F32)],
        scratch_shapes=[pltpu.VMEM((1, LANES), F32), pltpu.VMEM((tm, LANES), F32)],
        compiler_params=pltpu.CompilerParams(
            dimension_semantics=("arbitrary",), vmem_limit_bytes=VMEM_LIMIT),
        name="mixer_out",
    )(yc, o, sgc, sga, x2d, wco, wao, wo, g, wr)


def _sc_index_rows(indices):
    n = indices.shape[0]
    return jnp.pad(indices.reshape(n // SC_WINDOW, SC_WINDOW), ((0, 0), (0, LANES - SC_WINDOW)))


def _sc_mesh():
    return plsc.VectorSubcoreMesh(core_axis_name="c", subcore_axis_name="s")


def _sc_scatter_rows(data, row0, n_src, indices, n_out):
    n = indices.shape[0]
    width = data.shape[1]
    src_blocks = n_src // SC_WINDOW
    block0 = row0 // SC_WINDOW

    @pl.kernel(out_type=jax.ShapeDtypeStruct((n_out, width), data.dtype), mesh=_sc_mesh())
    def scatter_kernel(x_hbm, i_hbm, o_hbm):
        def body(x_vmem, i_vmem):
            pltpu.sync_copy(x_vmem, o_hbm.at[i_vmem.at[0, pl.ds(0, SC_WINDOW)]])

        pltpu.emit_pipeline(
            body,
            grid=(n // SC_WINDOW,),
            in_specs=[pl.BlockSpec((SC_WINDOW, width), lambda i: (block0 + i % src_blocks, 0)),
                      pl.BlockSpec((1, LANES), lambda i: (i, 0))],
            out_specs=[],
            core_axis_name=("c", "s"),
            dimension_semantics=(pltpu.PARALLEL,),
        )(x_hbm, i_hbm)

    return scatter_kernel(data, _sc_index_rows(indices))


def _sc_gather_rows(data, indices):
    n = indices.shape[0]
    width = data.shape[1]

    @pl.kernel(out_type=jax.ShapeDtypeStruct((n, width), data.dtype), mesh=_sc_mesh())
    def gather_kernel(x_hbm, i_hbm, o_hbm):
        def body(i_vmem, o_vmem):
            pltpu.sync_copy(x_hbm.at[i_vmem.at[0, pl.ds(0, SC_WINDOW)]], o_vmem)

        pltpu.emit_pipeline(
            body,
            grid=(n // SC_WINDOW,),
            in_specs=[pl.BlockSpec((1, LANES), lambda i: (i, 0))],
            out_specs=[pl.BlockSpec((SC_WINDOW, width), lambda i: (i, 0))],
            core_axis_name=("c", "s"),
            dimension_semantics=(pltpu.PARALLEL,),
        )(i_hbm, o_hbm)

    return gather_kernel(data, _sc_index_rows(indices))


def _expert_kernel(block_e_ref, n_valid_ref, x_ref, wg_ref, wu_ref, wd_ref, y_ref,
                   wg_scr, wu_scr, wd_scr):
    i = pl.program_id(0)
    n_valid = n_valid_ref[i]

    @pl.when((i == 0) | (block_e_ref[i] != block_e_ref[jnp.maximum(i - 1, 0)]))
    def _():
        wg_scr[...] = wg_ref[0].astype(BF16)
        wu_scr[...] = wu_ref[0].astype(BF16)
        wd_scr[...] = wd_ref[0].astype(BF16)

    @pl.when(n_valid > 0)
    def _():
        for r in range(MOE_BLOCK // MOE_SUB):
            rows = slice(r * MOE_SUB, (r + 1) * MOE_SUB)
            row = r * MOE_SUB + lax.broadcasted_iota(jnp.int32, (MOE_SUB, x_ref.shape[1]), 0)
            packed = jnp.where(row < n_valid, x_ref[rows, :], jnp.uint32(0))
            xb = _unpack_bf16_pairs(packed).astype(BF16)
            a = _dot(xb, wg_scr[...])
            u = _dot(xb, wu_scr[...])
            hmid = (a * jax.nn.sigmoid(a) * u).astype(BF16)
            y_ref[rows, :] = _pack_bf16_pairs(_dot(hmid, wd_scr[...]))

    @pl.when(n_valid == 0)
    def _():
        y_ref[...] = jnp.zeros_like(y_ref)


def _experts(block_e, n_valid, x_rows, wg, wu, wd):
    n_rows, half = x_rows.shape
    n_blocks = n_rows // MOE_BLOCK

    def w_map(i, be, nv):
        return (be[i], 0, 0)

    row_blk = pl.BlockSpec((MOE_BLOCK, half), lambda i, be, nv: (i, 0))
    grid_spec = pltpu.PrefetchScalarGridSpec(
        num_scalar_prefetch=2,
        grid=(n_blocks,),
        in_specs=[row_blk,
                  pl.BlockSpec((1, D_MODEL, D_EXPERT), w_map),
                  pl.BlockSpec((1, D_MODEL, D_EXPERT), w_map),
                  pl.BlockSpec((1, D_EXPERT, D_MODEL), w_map)],
        out_specs=row_blk,
        scratch_shapes=[pltpu.VMEM((D_MODEL, D_EXPERT), BF16), pltpu.VMEM((D_MODEL, D_EXPERT), BF16),
                        pltpu.VMEM((D_EXPERT, D_MODEL), BF16)],
    )
    return pl.pallas_call(
        _expert_kernel,
        grid_spec=grid_spec,
        out_shape=jax.ShapeDtypeStruct((n_rows, half), jnp.uint32),
        compiler_params=pltpu.CompilerParams(
            dimension_semantics=("arbitrary",), vmem_limit_bytes=VMEM_LIMIT),
        name="moe_experts",
    )(block_e, n_valid, x_rows, wg, wu, wd)


def _combine_kernel(ya_ref, yb_ref, x1_ref, route_ref, g_ref, *rest):
    out_ref = rest[-1]
    route = route_ref[...]
    w1 = route[:, 2:3]
    w2 = route[:, 3:4]
    x2 = x1_ref[...] + (_unpack_bf16_pairs(ya_ref[...]) * w1 + _unpack_bf16_pairs(yb_ref[...]) * w2)
    ms = jnp.mean(x2 * x2, axis=-1, keepdims=True)
    out_ref[...] = x2 * lax.rsqrt(ms + EPS) * g_ref[...]


def _combine(y_pairs, x1, route, g, row0, prev_out):
    tokens = x1.shape[0]
    tc = MIX_ROWS
    n_tiles = y_pairs.shape[0] // 2 // tc
    tile0 = row0 // tc
    row_blk = pl.BlockSpec((tc, D_MODEL), lambda i: (tile0 + i, 0))
    in_specs = [pl.BlockSpec((tc, D_MODEL // 2), lambda i: (i, 0)),
                pl.BlockSpec((tc, D_MODEL // 2), lambda i: (i + n_tiles, 0)),
                row_blk,
                pl.BlockSpec((tc, LANES), lambda i: (tile0 + i, 0)),
                pl.BlockSpec((1, D_MODEL), lambda i: (0, 0))]
    args = [y_pairs, y_pairs, x1, route, g]
    aliases = {}
    if prev_out is not None:
        in_specs.append(pl.BlockSpec(memory_space=pl.ANY))
        args.append(prev_out)
        aliases = {len(args) - 1: 0}
    return pl.pallas_call(
        _combine_kernel,
        grid=(n_tiles,),
        in_specs=in_specs,
        out_specs=row_blk,
        out_shape=jax.ShapeDtypeStruct((tokens, D_MODEL), F32),
        input_output_aliases=aliases,
        compiler_params=pltpu.CompilerParams(
            dimension_semantics=("arbitrary",), vmem_limit_bytes=VMEM_LIMIT),
        name="moe_combine",
    )(*args)


def kernel(x, norm_mix_g, w_in, conv_w, conv_b, b_forget, w_conv_out, w_att_out, w_out,
           norm_ffn_g, w_router_group, w_router_expert, w_e_gate, w_e_up, w_e_down,
           norm_final_g):
    batch, seq, d = x.shape
    assert d == D_MODEL and seq % max(ATT_BLOCK, IN_PROJ_ROWS) == 0
    tokens = batch * seq
    assert tokens % (MOE_SPLITS * MIX_ROWS) == 0 and MIX_ROWS % SC_WINDOW == 0
    x2d = x.reshape(tokens, D_MODEL)

    wb = w_in.astype(BF16)
    n_main = 6 * D_MODEL
    gate0 = n_main + N_HEADS
    pieces = [wb[:, i * D_MODEL:(i + 1) * D_MODEL] for i in range(6)]
    pieces += [wb[:, gate0:gate0 + D_MODEL], wb[:, gate0 + D_MODEL:gate0 + 2 * D_MODEL]]
    w_stack = jnp.stack(pieces, axis=0)
    n_rep = 6
    wf = jnp.pad(jnp.tile(wb[:, n_main:gate0], (1, n_rep)), ((0, 0), (0, LANES - n_rep * N_HEADS)))
    bfv = jnp.pad(jnp.tile(b_forget.astype(F32), n_rep), (0, LANES - n_rep * N_HEADS)).reshape(1, LANES)
    wr = jnp.concatenate(
        [w_router_group, jnp.transpose(w_router_expert, (1, 0, 2)).reshape(D_MODEL, N_EXPERTS)], axis=1)
    wr = jnp.pad(wr, ((0, 0), (0, LANES - wr.shape[1]))).astype(BF16)

    yc, q, k, v, sgc, sga, cq, ck = _in_proj(
        x2d, norm_mix_g.reshape(1, D_MODEL), w_stack, wf, bfv, conv_w,
        conv_b.reshape(1, D_MODEL), batch, seq)
    o = _attention(q, k, v, cq, ck, batch, seq)

    x1, h2, route, route_t, counts = _mixer_out(
        yc, o, sgc, sga, x2d, w_conv_out.astype(BF16), w_att_out.astype(BF16),
        w_out.astype(BF16), norm_ffn_g.reshape(1, D_MODEL), wr)

    split = tokens // MOE_SPLITS
    n_rows = split * 2 + N_EXPERTS * MOE_BLOCK
    block_start = jnp.arange(n_rows // MOE_BLOCK, dtype=jnp.int32) * MOE_BLOCK
    expert_ids = jnp.arange(N_EXPERTS, dtype=jnp.int32)[:, None, None]
    g_final = norm_final_g.reshape(1, D_MODEL)
    out = None
    for sp in range(MOE_SPLITS):
        tok0 = sp * split
        cnt = counts[sp, 0, N_GROUPS:N_GROUPS + N_EXPERTS].astype(jnp.int32)
        padded = (cnt + MOE_BLOCK - 1) // MOE_BLOCK * MOE_BLOCK
        end_padded = jnp.cumsum(padded)
        start_padded = end_padded - padded
        expert = route_t[0:2, tok0:tok0 + split].astype(jnp.int32)
        rank = route_t[4:6, tok0:tok0 + split].astype(jnp.int32)
        dest = rank + jnp.sum(jnp.where(expert[None] == expert_ids, start_padded[:, None, None], 0), axis=0)
        dest_flat = dest.reshape(2 * split)
        block_e = jnp.minimum(jnp.sum(end_padded[None, :] <= block_start[:, None], axis=1),
                              N_EXPERTS - 1).astype(jnp.int32)
        used_lo = jnp.maximum(start_padded[None, :], block_start[:, None])
        used_hi = jnp.minimum((start_padded + cnt)[None, :], block_start[:, None] + MOE_BLOCK)
        n_valid = jnp.sum(jnp.maximum(used_hi - used_lo, 0), axis=1).astype(jnp.int32)

        x_rows = _sc_scatter_rows(h2, tok0, split, dest_flat, n_rows)
        y_rows = _experts(block_e, n_valid, x_rows, w_e_gate, w_e_up, w_e_down)
        y_pairs = _sc_gather_rows(y_rows, dest_flat)
        out = _combine(y_pairs, x1, route, g_final, tok0, out)
    return out.reshape(batch, seq, D_MODEL)
```

```python
import functools

import jax
import jax.numpy as jnp
from jax import lax
from jax.experimental import pallas as pl
from jax.experimental.pallas import tpu as pltpu
from jax.experimental.pallas import tpu_sc as plsc

D_MODEL = 1024
HEAD_DIM = 64
N_HEADS = 16
N_GROUPS = 4
EXPERTS_PER_GROUP = 8
N_EXPERTS = N_GROUPS * EXPERTS_PER_GROUP
D_EXPERT = 512
MOE_BLOCK = 512
MOE_SUB = 256
MOE_SPLITS = 1
CONV_WIDTH = 3
EPS = 1e-6

LANES = 128
HEADS_PER_BLOCK = LANES // HEAD_DIM
NEG_BIG = -1e30
LOG2E = 1.4426950408889634

IN_PROJ_COLS = 256
IN_PROJ_ROWS = 512
ATT_BLOCK = 256
ATT_KEYS = 256
MIX_ROWS = 512
ROUTE_ROWS = 8
SC_WINDOW = 64
VMEM_LIMIT = 56 * 1024 * 1024

F32 = jnp.float32
BF16 = jnp.bfloat16


def _dot(a, b):
    return jnp.dot(a, b, preferred_element_type=F32)


def _pack_bf16_pairs(x):
    c = x.shape[1] // 2
    lo = lax.bitcast_convert_type(x[:, :c].astype(BF16).astype(F32), jnp.uint32) >> 16
    hi = lax.bitcast_convert_type(x[:, c:].astype(BF16).astype(F32), jnp.uint32) & jnp.uint32(0xFFFF0000)
    return lo | hi


def _unpack_bf16_pairs(w):
    lo = lax.bitcast_convert_type(w << 16, F32)
    hi = lax.bitcast_convert_type(w & jnp.uint32(0xFFFF0000), F32)
    return jnp.concatenate([lo, hi], axis=1)


def _in_proj_kernel(x_ref, g_ref, w_ref, wf_ref, bf_ref, cw_ref, cb_ref,
                    yc_ref, q_ref, k_ref, v_ref, sgc_ref, sga_ref, cq_ref, ck_ref, h_scr):
    j = pl.program_id(1)
    seq = x_ref.shape[0]
    n_chunks = seq // IN_PROJ_ROWS

    @pl.when(j == 0)
    def _():
        for r in range(n_chunks):
            rows = slice(r * IN_PROJ_ROWS, (r + 1) * IN_PROJ_ROWS)
            xs = x_ref[rows, :]
            ms = jnp.mean(xs * xs, axis=-1, keepdims=True)
            h_scr[rows, :] = (xs * lax.rsqrt(ms + EPS) * g_ref[...]).astype(BF16)
        f = _dot(h_scr[...], wf_ref[...]) + bf_ref[...]
        c = jnp.minimum(f, 0.0) - jnp.log(1.0 + jnp.exp(-jnp.abs(f)))
        row = lax.broadcasted_iota(jnp.int32, c.shape, 0)
        d = 1
        while d < seq:
            c = c + jnp.where(row >= d, pltpu.roll(c, d, axis=0), 0.0)
            d *= 2
        c = c * LOG2E
        hi = c.astype(BF16).astype(F32)
        mid = (c - hi).astype(BF16).astype(F32)
        lo = (c - hi - mid).astype(BF16).astype(F32)
        grp = lax.broadcasted_iota(jnp.int32, c.shape, 1) // N_HEADS
        terms = jnp.where(grp % 3 == 0, hi, jnp.where(grp % 3 == 1, mid, lo))
        one = jnp.float32(1.0)
        cq_ref[0] = jnp.where(grp < 3, terms, jnp.where(grp < 6, one, 0.0)).astype(BF16)
        ck_ref[0] = jnp.where(grp < 3, one, jnp.where(grp < 6, -terms, 0.0)).astype(BF16)

    cw0 = cw_ref[0:1, :]
    cw1 = cw_ref[1:2, :]
    cw2 = cw_ref[2:3, :]
    cb = cb_ref[...]
    rowc = lax.broadcasted_iota(jnp.int32, (IN_PROJ_ROWS, IN_PROJ_COLS), 0)
    zprev = None
    for r in range(n_chunks):
        rows = slice(r * IN_PROJ_ROWS, (r + 1) * IN_PROJ_ROWS)
        hs = h_scr[rows, :]
        cb_gate = _dot(hs, w_ref[0])
        z = _dot(hs, w_ref[1]) * _dot(hs, w_ref[2])
        z1 = pltpu.roll(z, 1, axis=0)
        z2 = pltpu.roll(z, 2, axis=0)
        if zprev is None:
            p1 = jnp.zeros_like(z)
            p2 = p1
        else:
            p1 = pltpu.roll(zprev, 1, axis=0)
            p2 = pltpu.roll(zprev, 2, axis=0)
        z1 = jnp.where(rowc < 1, p1, z1)
        z2 = jnp.where(rowc < 2, p2, z2)
        acc = cb + cw0 * z2 + cw1 * z1 + cw2 * z
        yc_ref[rows, :] = (cb_gate * acc).astype(BF16)
        zprev = z
        q_ref[rows, :] = (_dot(hs, w_ref[3]) * (LOG2E * HEAD_DIM ** -0.5)).astype(BF16)
        k_ref[rows, :] = _dot(hs, w_ref[4]).astype(BF16)
        v_ref[rows, :] = _dot(hs, w_ref[5]).astype(BF16)
        sgc_ref[rows, :] = jax.nn.sigmoid(_dot(hs, w_ref[6])).astype(BF16)
        sga_ref[rows, :] = jax.nn.sigmoid(_dot(hs, w_ref[7])).astype(BF16)


def _in_proj(x2d, g, w_stack, wf, bfv, conv_w, conv_b, batch, seq):
    tokens = batch * seq
    tn = IN_PROJ_COLS
    nj = D_MODEL // tn
    col_out = pl.BlockSpec((seq, tn), lambda b, j: (b, j))
    out_bf16 = jax.ShapeDtypeStruct((tokens, D_MODEL), BF16)
    return pl.pallas_call(
        _in_proj_kernel,
        grid=(batch, nj),
        in_specs=[
            pl.BlockSpec((seq, D_MODEL), lambda b, j: (b, 0)),
            pl.BlockSpec((1, D_MODEL), lambda b, j: (0, 0)),
            pl.BlockSpec((8, D_MODEL, tn), lambda b, j: (0, 0, j)),
            pl.BlockSpec((D_MODEL, LANES), lambda b, j: (0, 0)),
            pl.BlockSpec((1, LANES), lambda b, j: (0, 0)),
            pl.BlockSpec((CONV_WIDTH, tn), lambda b, j: (0, j)),
            pl.BlockSpec((1, tn), lambda b, j: (0, j)),
        ],
        out_specs=[col_out] * 6 + [pl.BlockSpec((1, seq, LANES), lambda b, j: (b, 0, 0))] * 2,
        out_shape=[out_bf16] * 6 + [jax.ShapeDtypeStruct((batch, seq, LANES), BF16)] * 2,
        scratch_shapes=[pltpu.VMEM((seq, D_MODEL), BF16)],
        compiler_params=pltpu.CompilerParams(
            dimension_semantics=("arbitrary", "arbitrary"), vmem_limit_bytes=VMEM_LIMIT),
        name="in_proj",
    )(x2d, g, w_stack, wf, bfv, conv_w, conv_b)


def _attention_kernel(q_ref, k_ref, v_ref, cq_ref, ck_ref, o_ref, qa_scr, ka_scr, va_scr):
    hp = pl.program_id(1)
    seq = q_ref.shape[0]
    blk = ATT_BLOCK
    lane = lax.broadcasted_iota(jnp.int32, (1, LANES), 1)
    zero = jnp.zeros((), BF16)

    own, base = [], []
    for hh in range(HEADS_PER_BLOCK):
        own.append((lane // HEAD_DIM) == hh)
        base.append(((hh + 1) % HEADS_PER_BLOCK) * HEAD_DIM)
        gate = (lane % N_HEADS) == hp * HEADS_PER_BLOCK + hh
        qa_scr[hh, :, 0:LANES] = jnp.where(own[hh], q_ref[...], zero)
        qa_scr[hh, :, LANES:2 * LANES] = jnp.where(gate, cq_ref[0], zero)
        ka_scr[hh, :, 0:LANES] = jnp.where(own[hh], k_ref[...], zero)
        ka_scr[hh, :, LANES:2 * LANES] = jnp.where(gate, ck_ref[0], zero)
        va_scr[hh] = jnp.where(own[hh], v_ref[...],
                               jnp.where(lane == base[hh], 1.0, 0.0).astype(BF16))

    for rg in range(seq // blk):
        rows = slice(rg * blk, (rg + 1) * blk)
        out = None
        for hh in range(HEADS_PER_BLOCK):
            qa = qa_scr[hh, rows, :]
            m = acc = None
            k_end = (rg + 1) * blk
            for k0 in range(0, k_end, ATT_KEYS):
                k1 = min(k0 + ATT_KEYS, k_end)
                keys = slice(k0, k1)
                s = lax.dot_general(qa, ka_scr[hh, keys, :], (((1,), (1,)), ((), ())),
                                    preferred_element_type=F32)
                if k1 > rg * blk:
                    qpos = rg * blk + lax.broadcasted_iota(jnp.int32, s.shape, 0)
                    kpos = k0 + lax.broadcasted_iota(jnp.int32, s.shape, 1)
                    s = jnp.where(kpos <= qpos, s, NEG_BIG)
                mx = jnp.max(s, axis=1, keepdims=True)
                if m is None:
                    m = mx
                    acc = _dot(jnp.exp2(s - m).astype(BF16), va_scr[hh, keys, :])
                else:
                    m_new = jnp.maximum(m, mx)
                    acc = jnp.exp2(m - m_new) * acc + _dot(jnp.exp2(s - m_new).astype(BF16),
                                                           va_scr[hh, keys, :])
                    m = m_new
            res = acc / acc[:, base[hh]:base[hh] + 1]
            out = res if out is None else jnp.where(own[hh], res, out)
        o_ref[rows, :] = out.astype(o_ref.dtype)


def _attention(q, k, v, cq, ck, batch, seq):
    tokens = batch * seq
    n_hp = N_HEADS // HEADS_PER_BLOCK
    blk = pl.BlockSpec((seq, LANES), lambda b, hp: (b, hp))
    gate_blk = pl.BlockSpec((1, seq, LANES), lambda b, hp: (b, 0, 0))
    wide = pltpu.VMEM((HEADS_PER_BLOCK, seq, 2 * LANES), BF16)
    return pl.pallas_call(
        _attention_kernel,
        grid=(batch, n_hp),
        in_specs=[blk, blk, blk, gate_blk, gate_blk],
        out_specs=blk,
        out_shape=jax.ShapeDtypeStruct((tokens, D_MODEL), BF16),
        scratch_shapes=[wide, wide, pltpu.VMEM((HEADS_PER_BLOCK, seq, LANES), BF16)],
        compiler_params=pltpu.CompilerParams(
            dimension_semantics=("arbitrary", "arbitrary"), vmem_limit_bytes=VMEM_LIMIT),
        name="fox_attention",
    )(q, k, v, cq, ck)


def _mixer_out_kernel(tiles_per_split, yc_ref, o_ref, sgc_ref, sga_ref, x_ref, wco_ref, wao_ref, wo_ref,
                      g_ref, wr_ref, x1_ref, h2_ref, route_ref, route_t_ref, counts_ref,
                      carry_scr, logit_scr):
    i = pl.program_id(0)
    tm = x_ref.shape[0]

    @pl.when(i == 0)
    def _():
        carry_scr[...] = jnp.zeros_like(carry_scr)
        logit_scr[...] = jnp.zeros_like(logit_scr)

    logits = logit_scr[...]
    lane = lax.broadcasted_iota(jnp.int32, (tm, LANES), 1).astype(F32)
    far = jnp.float32(4 * LANES)

    def first_lane_of_max(vals, vmax):
        return jnp.min(jnp.where(vals == vmax, lane, far), axis=1, keepdims=True)

    lg = jnp.where(lane < N_GROUPS, logits, NEG_BIG)
    gmax = jnp.max(lg, axis=1, keepdims=True)
    g_val = 1.0 / jnp.sum(jnp.exp(lg - gmax), axis=1, keepdims=True)
    g_idx = first_lane_of_max(lg, gmax)
    lo = N_GROUPS + EXPERTS_PER_GROUP * g_idx
    le = jnp.where((lane >= lo) & (lane < lo + EXPERTS_PER_GROUP), logits, NEG_BIG)
    e1max = jnp.max(le, axis=1, keepdims=True)
    e1 = first_lane_of_max(le, e1max)
    le2 = jnp.where(lane == e1, NEG_BIG, le)
    e2max = jnp.max(le2, axis=1, keepdims=True)
    e2 = first_lane_of_max(le2, e2max)
    ratio = jnp.exp(e2max - e1max)
    w1 = g_val / (1.0 + ratio)
    w2 = g_val * ratio / (1.0 + ratio)

    oh1 = lane == e1
    oh2 = lane == e2
    onehot = jnp.where(oh1 | oh2, 1.0, 0.0).astype(BF16)
    r_i = lax.broadcasted_iota(jnp.int32, (tm, tm), 0)
    c_i = lax.broadcasted_iota(jnp.int32, (tm, tm), 1)
    strict_lower = jnp.where(c_i < r_i, 1.0, 0.0).astype(BF16)
    keep = jnp.where((i - 1) % tiles_per_split == 0, 0.0, 1.0)
    carry = keep * carry_scr[...]
    before = _dot(strict_lower, onehot) + carry
    rank1 = jnp.sum(jnp.where(oh1, before, 0.0), axis=1, keepdims=True)
    rank2 = jnp.sum(jnp.where(oh2, before, 0.0), axis=1, keepdims=True)
    is_tile = jnp.where(i > 0, 1.0, 0.0)
    carry = carry + is_tile * jnp.sum(onehot.astype(F32), axis=0, keepdims=True)
    carry_scr[...] = carry
    counts_ref[0] = carry

    route = jnp.where(lane == 0, e1 - N_GROUPS, 0.0)
    route = jnp.where(lane == 1, e2 - N_GROUPS, route)
    route = jnp.where(lane == 2, w1, route)
    route = jnp.where(lane == 3, w2, route)
    route = jnp.where(lane == 4, rank1, route)
    route = jnp.where(lane == 5, rank2, route)
    route_ref[...] = route
    route_t_ref[...] = route.T[0:ROUTE_ROWS, :]

    y_conv = _dot(yc_ref[...], wco_ref[...])
    y_att = _dot(o_ref[...], wao_ref[...])
    m = sgc_ref[...].astype(F32) * y_conv + sga_ref[...].astype(F32) * y_att
    x1 = x_ref[...] + _dot(m.astype(BF16), wo_ref[...])
    x1_ref[...] = x1
    ms = jnp.mean(x1 * x1, axis=-1, keepdims=True)
    h2 = x1 * lax.rsqrt(ms + EPS) * g_ref[...]
    h2_ref[...] = _pack_bf16_pairs(h2)
    logit_scr[...] = _dot(h2.astype(BF16), wr_ref[...])


def _mixer_out(yc, o, sgc, sga, x2d, wco, wao, wo, g, wr):
    tokens = x2d.shape[0]
    tm = MIX_ROWS
    n_tiles = tokens // tm

    def cur(i):
        return (jnp.minimum(i, n_tiles - 1), 0)

    def prev(i):
        return (jnp.maximum(i - 1, 0), 0)

    row_blk = pl.BlockSpec((tm, D_MODEL), cur)
    w_blk = pl.BlockSpec((D_MODEL, D_MODEL), lambda i: (0, 0))
    return pl.pallas_call(
        functools.partial(_mixer_out_kernel, n_tiles // MOE_SPLITS),
        grid=(n_tiles + 1,),
        in_specs=[row_blk, row_blk, row_blk, row_blk, row_blk, w_blk, w_blk, w_blk,
                  pl.BlockSpec((1, D_MODEL), lambda i: (0, 0)),
                  pl.BlockSpec((D_MODEL, LANES), lambda i: (0, 0))],
        out_specs=[row_blk,
                   pl.BlockSpec((tm, D_MODEL // 2), cur),
                   pl.BlockSpec((tm, LANES), prev),
                   pl.BlockSpec((ROUTE_ROWS, tm), lambda i: (0, jnp.maximum(i - 1, 0))),
                   pl.BlockSpec((1, 1, LANES),
                                lambda i: (jnp.maximum(i - 1, 0) // (n_tiles // MOE_SPLITS), 0, 0))],
        out_shape=[jax.ShapeDtypeStruct((tokens, D_MODEL), F32),
                   jax.ShapeDtypeStruct((tokens, D_MODEL // 2), jnp.uint32),
                   jax.ShapeDtypeStruct((tokens, LANES), F32),
                   jax.ShapeDtypeStruct((ROUTE_ROWS, tokens), F32),
                   jax.ShapeDtypeStruct((MOE_SPLITS, 1, LANES), F32)],
        scratch_shapes=[pltpu.VMEM((1, LANES), F32), pltpu.VMEM((tm, LANES), F32)],
        compiler_params=pltpu.CompilerParams(
            dimension_semantics=("arbitrary",), vmem_limit_bytes=VMEM_LIMIT),
        name="mixer_out",
    )(yc, o, sgc, sga, x2d, wco, wao, wo, g, wr)


def _sc_index_rows(indices):
    n = indices.shape[0]
    return jnp.pad(indices.reshape(n // SC_WINDOW, SC_WINDOW), ((0, 0), (0, LANES - SC_WINDOW)))


def _sc_mesh():
    return plsc.VectorSubcoreMesh(core_axis_name="c", subcore_axis_name="s")


def _sc_scatter_rows(data, row0, n_src, indices, n_out):
    n = indices.shape[0]
    width = data.shape[1]
    src_blocks = n_src // SC_WINDOW
    block0 = row0 // SC_WINDOW

    @pl.kernel(out_type=jax.ShapeDtypeStruct((n_out, width), data.dtype), mesh=_sc_mesh())
    def scatter_kernel(x_hbm, i_hbm, o_hbm):
        def body(x_vmem, i_vmem):
            pltpu.sync_copy(x_vmem, o_hbm.at[i_vmem.at[0, pl.ds(0, SC_WINDOW)]])

        pltpu.emit_pipeline(
            body,
            grid=(n // SC_WINDOW,),
            in_specs=[pl.BlockSpec((SC_WINDOW, width), lambda i: (block0 + i % src_blocks, 0)),
                      pl.BlockSpec((1, LANES), lambda i: (i, 0))],
            out_specs=[],
            core_axis_name=("c", "s"),
            dimension_semantics=(pltpu.PARALLEL,),
        )(x_hbm, i_hbm)

    return scatter_kernel(data, _sc_index_rows(indices))


def _sc_gather_rows(data, indices):
    n = indices.shape[0]
    width = data.shape[1]

    @pl.kernel(out_type=jax.ShapeDtypeStruct((n, width), data.dtype), mesh=_sc_mesh())
    def gather_kernel(x_hbm, i_hbm, o_hbm):
        def body(i_vmem, o_vmem):
            pltpu.sync_copy(x_hbm.at[i_vmem.at[0, pl.ds(0, SC_WINDOW)]], o_vmem)

        pltpu.emit_pipeline(
            body,
            grid=(n // SC_WINDOW,),
            in_specs=[pl.BlockSpec((1, LANES), lambda i: (i, 0))],
            out_specs=[pl.BlockSpec((SC_WINDOW, width), lambda i: (i, 0))],
            core_axis_name=("c", "s"),
            dimension_semantics=(pltpu.PARALLEL,),
        )(i_hbm, o_hbm)

    return gather_kernel(data, _sc_index_rows(indices))


def _expert_kernel(block_e_ref, n_valid_ref, x_ref, wg_ref, wu_ref, wd_ref, y_ref,
                   wg_scr, wu_scr, wd_scr):
    i = pl.program_id(0)
    n_valid = n_valid_ref[i]

    @pl.when((i == 0) | (block_e_ref[i] != block_e_ref[jnp.maximum(i - 1, 0)]))
    def _():
        wg_scr[...] = wg_ref[0].astype(BF16)
        wu_scr[...] = wu_ref[0].astype(BF16)
        wd_scr[...] = wd_ref[0].astype(BF16)

    @pl.when(n_valid > 0)
    def _():
        for r in range(MOE_BLOCK // MOE_SUB):
            rows = slice(r * MOE_SUB, (r + 1) * MOE_SUB)
            row = r * MOE_SUB + lax.broadcasted_iota(jnp.int32, (MOE_SUB, x_ref.shape[1]), 0)
            packed = jnp.where(row < n_valid, x_ref[rows, :], jnp.uint32(0))
            xb = _unpack_bf16_pairs(packed).astype(BF16)
            a = _dot(xb, wg_scr[...])
            u = _dot(xb, wu_scr[...])
            hmid = (a * jax.nn.sigmoid(a) * u).astype(BF16)
            y_ref[rows, :] = _pack_bf16_pairs(_dot(hmid, wd_scr[...]))

    @pl.when(n_valid == 0)
    def _():
        y_ref[...] = jnp.zeros_like(y_ref)


def _experts(block_e, n_valid, x_rows, wg, wu, wd):
    n_rows, half = x_rows.shape
    n_blocks = n_rows // MOE_BLOCK

    def w_map(i, be, nv):
        return (be[i], 0, 0)

    row_blk = pl.BlockSpec((MOE_BLOCK, half), lambda i, be, nv: (i, 0))
    grid_spec = pltpu.PrefetchScalarGridSpec(
        num_scalar_prefetch=2,
        grid=(n_blocks,),
        in_specs=[row_blk,
                  pl.BlockSpec((1, D_MODEL, D_EXPERT), w_map),
                  pl.BlockSpec((1, D_MODEL, D_EXPERT), w_map),
                  pl.BlockSpec((1, D_EXPERT, D_MODEL), w_map)],
        out_specs=row_blk,
        scratch_shapes=[pltpu.VMEM((D_MODEL, D_EXPERT), BF16), pltpu.VMEM((D_MODEL, D_EXPERT), BF16),
                        pltpu.VMEM((D_EXPERT, D_MODEL), BF16)],
    )
    return pl.pallas_call(
        _expert_kernel,
        grid_spec=grid_spec,
        out_shape=jax.ShapeDtypeStruct((n_rows, half), jnp.uint32),
        compiler_params=pltpu.CompilerParams(
            dimension_semantics=("arbitrary",), vmem_limit_bytes=VMEM_LIMIT),
        name="moe_experts",
    )(block_e, n_valid, x_rows, wg, wu, wd)


def _combine_kernel(ya_ref, yb_ref, x1_ref, route_ref, g_ref, *rest):
    out_ref = rest[-1]
    route = route_ref[...]
    w1 = route[:, 2:3]
    w2 = route[:, 3:4]
    x2 = x1_ref[...] + (_unpack_bf16_pairs(ya_ref[...]) * w1 + _unpack_bf16_pairs(yb_ref[...]) * w2)
    ms = jnp.mean(x2 * x2, axis=-1, keepdims=True)
    out_ref[...] = x2 * lax.rsqrt(ms + EPS) * g_ref[...]


def _combine(y_pairs, x1, route, g, row0, prev_out):
    tokens = x1.shape[0]
    tc = MIX_ROWS
    n_tiles = y_pairs.shape[0] // 2 // tc
    tile0 = row0 // tc
    row_blk = pl.BlockSpec((tc, D_MODEL), lambda i: (tile0 + i, 0))
    in_specs = [pl.BlockSpec((tc, D_MODEL // 2), lambda i: (i, 0)),
                pl.BlockSpec((tc, D_MODEL // 2), lambda i: (i + n_tiles, 0)),
                row_blk,
                pl.BlockSpec((tc, LANES), lambda i: (tile0 + i, 0)),
                pl.BlockSpec((1, D_MODEL), lambda i: (0, 0))]
    args = [y_pairs, y_pairs, x1, route, g]
    aliases = {}
    if prev_out is not None:
        in_specs.append(pl.BlockSpec(memory_space=pl.ANY))
        args.append(prev_out)
        aliases = {len(args) - 1: 0}
    return pl.pallas_call(
        _combine_kernel,
        grid=(n_tiles,),
        in_specs=in_specs,
        out_specs=row_blk,
        out_shape=jax.ShapeDtypeStruct((tokens, D_MODEL), F32),
        input_output_aliases=aliases,
        compiler_params=pltpu.CompilerParams(
            dimension_semantics=("arbitrary",), vmem_limit_bytes=VMEM_LIMIT),
        name="moe_combine",
    )(*args)


def kernel(x, norm_mix_g, w_in, conv_w, conv_b, b_forget, w_conv_out, w_att_out, w_out,
           norm_ffn_g, w_router_group, w_router_expert, w_e_gate, w_e_up, w_e_down,
           norm_final_g):
    batch, seq, d = x.shape
    assert d == D_MODEL and seq % max(ATT_BLOCK, IN_PROJ_ROWS) == 0
    tokens = batch * seq
    assert tokens % (MOE_SPLITS * MIX_ROWS) == 0 and MIX_ROWS % SC_WINDOW == 0
    x2d = x.reshape(tokens, D_MODEL)

    wb = w_in.astype(BF16)
    n_main = 6 * D_MODEL
    gate0 = n_main + N_HEADS
    pieces = [wb[:, i * D_MODEL:(i + 1) * D_MODEL] for i in range(6)]
    pieces += [wb[:, gate0:gate0 + D_MODEL], wb[:, gate0 + D_MODEL:gate0 + 2 * D_MODEL]]
    w_stack = jnp.stack(pieces, axis=0)
    n_rep = 6
    wf = jnp.pad(jnp.tile(wb[:, n_main:gate0], (1, n_rep)), ((0, 0), (0, LANES - n_rep * N_HEADS)))
    bfv = jnp.pad(jnp.tile(b_forget.astype(F32), n_rep), (0, LANES - n_rep * N_HEADS)).reshape(1, LANES)
    wr = jnp.concatenate(
        [w_router_group, jnp.transpose(w_router_expert, (1, 0, 2)).reshape(D_MODEL, N_EXPERTS)], axis=1)
    wr = jnp.pad(wr, ((0, 0), (0, LANES - wr.shape[1]))).astype(BF16)

    yc, q, k, v, sgc, sga, cq, ck = _in_proj(
        x2d, norm_mix_g.reshape(1, D_MODEL), w_stack, wf, bfv, conv_w,
        conv_b.reshape(1, D_MODEL), batch, seq)
    o = _attention(q, k, v, cq, ck, batch, seq)

    x1, h2, route, route_t, counts = _mixer_out(
        yc, o, sgc, sga, x2d, w_conv_out.astype(BF16), w_att_out.astype(BF16),
        w_out.astype(BF16), norm_ffn_g.reshape(1, D_MODEL), wr)

    split = tokens // MOE_SPLITS
    n_rows = split * 2 + N_EXPERTS * MOE_BLOCK
    block_start = jnp.arange(n_rows // MOE_BLOCK, dtype=jnp.int32) * MOE_BLOCK
    expert_ids = jnp.arange(N_EXPERTS, dtype=jnp.int32)[:, None, None]
    g_final = norm_final_g.reshape(1, D_MODEL)
    out = None
    for sp in range(MOE_SPLITS):
        tok0 = sp * split
        cnt = counts[sp, 0, N_GROUPS:N_GROUPS + N_EXPERTS].astype(jnp.int32)
        padded = (cnt + MOE_BLOCK - 1) // MOE_BLOCK * MOE_BLOCK
        end_padded = jnp.cumsum(padded)
        start_padded = end_padded - padded
        expert = route_t[0:2, tok0:tok0 + split].astype(jnp.int32)
        rank = route_t[4:6, tok0:tok0 + split].astype(jnp.int32)
        dest = rank + jnp.sum(jnp.where(expert[None] == expert_ids, start_padded[:, None, None], 0), axis=0)
        dest_flat = dest.reshape(2 * split)
        block_e = jnp.minimum(jnp.sum(end_padded[None, :] <= block_start[:, None], axis=1),
                              N_EXPERTS - 1).astype(jnp.int32)
        used_lo = jnp.maximum(start_padded[None, :], block_start[:, None])
        used_hi = jnp.minimum((start_padded + cnt)[None, :], block_start[:, None] + MOE_BLOCK)
        n_valid = jnp.sum(jnp.maximum(used_hi - used_lo, 0), axis=1).astype(jnp.int32)

        x_rows = _sc_scatter_rows(h2, tok0, split, dest_flat, n_rows)
        y_rows = _experts(block_e, n_valid, x_rows, w_e_gate, w_e_up, w_e_down)
        y_pairs = _sc_gather_rows(y_rows, dest_flat)
        out = _combine(y_pairs, x1, route, g_final, tok0, out)
    return out.reshape(batch, seq, D_MODEL)
```

```python
import functools

import jax
import jax.numpy as jnp
from jax import lax
from jax.experimental import pallas as pl
from jax.experimental.pallas import tpu as pltpu
from jax.experimental.pallas import tpu_sc as plsc

D_MODEL = 1024
HEAD_DIM = 64
N_HEADS = 16
N_GROUPS = 4
EXPERTS_PER_GROUP = 8
N_EXPERTS = N_GROUPS * EXPERTS_PER_GROUP
D_EXPERT = 512
MOE_BLOCK = 512
MOE_SUB = 512
MOE_SPLITS = 1
CONV_WIDTH = 3
EPS = 1e-6

LANES = 128
HEADS_PER_BLOCK = LANES // HEAD_DIM
NEG_BIG = -1e30
LOG2E = 1.4426950408889634

IN_PROJ_COLS = 256
IN_PROJ_ROWS = 512
ATT_BLOCK = 256
ATT_KEYS = 256
ATT_PAIRS = 1
MIX_ROWS = 512
ROUTE_ROWS = 8
SC_WINDOW = 64
VMEM_LIMIT = 56 * 1024 * 1024

F32 = jnp.float32
BF16 = jnp.bfloat16


def _dot(a, b):
    return jnp.dot(a, b, preferred_element_type=F32)


def _pack_bf16_pairs(x):
    c = x.shape[1] // 2
    lo = lax.bitcast_convert_type(x[:, :c].astype(BF16).astype(F32), jnp.uint32) >> 16
    hi = lax.bitcast_convert_type(x[:, c:].astype(BF16).astype(F32), jnp.uint32) & jnp.uint32(0xFFFF0000)
    return lo | hi


def _unpack_bf16_pairs(w):
    lo = lax.bitcast_convert_type(w << 16, F32)
    hi = lax.bitcast_convert_type(w & jnp.uint32(0xFFFF0000), F32)
    return jnp.concatenate([lo, hi], axis=1)


def _in_proj_kernel(x_ref, g_ref, w_ref, wf_ref, bf_ref, cw_ref, cb_ref,
                    yc_ref, q_ref, k_ref, v_ref, sgc_ref, sga_ref, cq_ref, ck_ref, h_scr):
    j = pl.program_id(1)
    seq = x_ref.shape[0]
    n_chunks = seq // IN_PROJ_ROWS

    @pl.when(j == 0)
    def _():
        for r in range(n_chunks):
            rows = slice(r * IN_PROJ_ROWS, (r + 1) * IN_PROJ_ROWS)
            xs = x_ref[rows, :]
            ms = jnp.mean(xs * xs, axis=-1, keepdims=True)
            h_scr[rows, :] = (xs * lax.rsqrt(ms + EPS) * g_ref[...]).astype(BF16)
        f = _dot(h_scr[...], wf_ref[...]) + bf_ref[...]
        c = jnp.minimum(f, 0.0) - jnp.log(1.0 + jnp.exp(-jnp.abs(f)))
        row = lax.broadcasted_iota(jnp.int32, c.shape, 0)
        d = 1
        while d < seq:
            c = c + jnp.where(row >= d, pltpu.roll(c, d, axis=0), 0.0)
            d *= 2
        c = c * LOG2E
        hi = c.astype(BF16).astype(F32)
        mid = (c - hi).astype(BF16).astype(F32)
        lo = (c - hi - mid).astype(BF16).astype(F32)
        grp = lax.broadcasted_iota(jnp.int32, c.shape, 1) // N_HEADS
        terms = jnp.where(grp % 3 == 0, hi, jnp.where(grp % 3 == 1, mid, lo))
        one = jnp.float32(1.0)
        cq_ref[0] = jnp.where(grp < 3, terms, jnp.where(grp < 6, one, 0.0)).astype(BF16)
        ck_ref[0] = jnp.where(grp < 3, one, jnp.where(grp < 6, -terms, 0.0)).astype(BF16)

    cw0 = cw_ref[0:1, :]
    cw1 = cw_ref[1:2, :]
    cw2 = cw_ref[2:3, :]
    cb = cb_ref[...]
    rowc = lax.broadcasted_iota(jnp.int32, (IN_PROJ_ROWS, IN_PROJ_COLS), 0)
    zprev = None
    for r in range(n_chunks):
        rows = slice(r * IN_PROJ_ROWS, (r + 1) * IN_PROJ_ROWS)
        hs = h_scr[rows, :]
        cb_gate = _dot(hs, w_ref[0])
        z = _dot(hs, w_ref[1]) * _dot(hs, w_ref[2])
        z1 = pltpu.roll(z, 1, axis=0)
        z2 = pltpu.roll(z, 2, axis=0)
        if zprev is None:
            p1 = jnp.zeros_like(z)
            p2 = p1
        else:
            p1 = pltpu.roll(zprev, 1, axis=0)
            p2 = pltpu.roll(zprev, 2, axis=0)
        z1 = jnp.where(rowc < 1, p1, z1)
        z2 = jnp.where(rowc < 2, p2, z2)
        acc = cb + cw0 * z2 + cw1 * z1 + cw2 * z
        yc_ref[rows, :] = (cb_gate * acc).astype(BF16)
        zprev = z
        q_ref[rows, :] = (_dot(hs, w_ref[3]) * (LOG2E * HEAD_DIM ** -0.5)).astype(BF16)
        k_ref[rows, :] = _dot(hs, w_ref[4]).astype(BF16)
        v_ref[rows, :] = _dot(hs, w_ref[5]).astype(BF16)
        sgc_ref[rows, :] = jax.nn.sigmoid(_dot(hs, w_ref[6])).astype(BF16)
        sga_ref[rows, :] = jax.nn.sigmoid(_dot(hs, w_ref[7])).astype(BF16)


def _in_proj(x2d, g, w_stack, wf, bfv, conv_w, conv_b, batch, seq):
    tokens = batch * seq
    tn = IN_PROJ_COLS
    nj = D_MODEL // tn
    col_out = pl.BlockSpec((seq, tn), lambda b, j: (b, j))
    out_bf16 = jax.ShapeDtypeStruct((tokens, D_MODEL), BF16)
    return pl.pallas_call(
        _in_proj_kernel,
        grid=(batch, nj),
        in_specs=[
            pl.BlockSpec((seq, D_MODEL), lambda b, j: (b, 0)),
            pl.BlockSpec((1, D_MODEL), lambda b, j: (0, 0)),
            pl.BlockSpec((8, D_MODEL, tn), lambda b, j: (0, 0, j)),
            pl.BlockSpec((D_MODEL, LANES), lambda b, j: (0, 0)),
            pl.BlockSpec((1, LANES), lambda b, j: (0, 0)),
            pl.BlockSpec((CONV_WIDTH, tn), lambda b, j: (0, j)),
            pl.BlockSpec((1, tn), lambda b, j: (0, j)),
        ],
        out_specs=[col_out] * 6 + [pl.BlockSpec((1, seq, LANES), lambda b, j: (b, 0, 0))] * 2,
        out_shape=[out_bf16] * 6 + [jax.ShapeDtypeStruct((batch, seq, LANES), BF16)] * 2,
        scratch_shapes=[pltpu.VMEM((seq, D_MODEL), BF16)],
        compiler_params=pltpu.CompilerParams(
            dimension_semantics=("arbitrary", "arbitrary"), vmem_limit_bytes=VMEM_LIMIT),
        name="in_proj",
    )(x2d, g, w_stack, wf, bfv, conv_w, conv_b)


def _attention_kernel(q_ref, k_ref, v_ref, cq_ref, ck_ref, o_ref, qa_scr, ka_scr, va_scr):
    seq = q_ref.shape[0]
    blk = ATT_BLOCK
    n_heads = ATT_PAIRS * HEADS_PER_BLOCK
    lane = lax.broadcasted_iota(jnp.int32, (1, LANES), 1)
    zero = jnp.zeros((), BF16)

    own, base, gate, ones_col, lanes = [], [], [], [], []
    for h in range(n_heads):
        pair, hh = divmod(h, HEADS_PER_BLOCK)
        head = (pl.program_id(1) * ATT_PAIRS + pair) * HEADS_PER_BLOCK + hh
        own.append((lane // HEAD_DIM) == hh)
        base.append(((hh + 1) % HEADS_PER_BLOCK) * HEAD_DIM)
        gate.append((lane % N_HEADS) == head)
        ones_col.append(jnp.where(lane == base[h], 1.0, 0.0).astype(BF16))
        lanes.append(slice(pair * LANES, (pair + 1) * LANES))
    for r in range(seq // blk):
        rows = slice(r * blk, (r + 1) * blk)
        for h in range(n_heads):
            qa_scr[h, rows, 0:LANES] = jnp.where(own[h], q_ref[rows, lanes[h]], zero)
            qa_scr[h, rows, LANES:2 * LANES] = jnp.where(gate[h], cq_ref[0, rows, :], zero)
            ka_scr[h, rows, 0:LANES] = jnp.where(own[h], k_ref[rows, lanes[h]], zero)
            ka_scr[h, rows, LANES:2 * LANES] = jnp.where(gate[h], ck_ref[0, rows, :], zero)
            va_scr[h, rows, :] = jnp.where(own[h], v_ref[rows, lanes[h]], ones_col[h])

    for rg in range(seq // blk):
        rows = slice(rg * blk, (rg + 1) * blk)
        out = None
        for h in range(n_heads):
            qa = qa_scr[h, rows, :]
            m = acc = None
            k_end = (rg + 1) * blk
            for k0 in range(0, k_end, ATT_KEYS):
                k1 = min(k0 + ATT_KEYS, k_end)
                keys = slice(k0, k1)
                s = lax.dot_general(qa, ka_scr[h, keys, :], (((1,), (1,)), ((), ())),
                                    preferred_element_type=F32)
                if k1 > rg * blk:
                    qpos = rg * blk + lax.broadcasted_iota(jnp.int32, s.shape, 0)
                    kpos = k0 + lax.broadcasted_iota(jnp.int32, s.shape, 1)
                    s = jnp.where(kpos <= qpos, s, NEG_BIG)
                mx = jnp.max(s, axis=1, keepdims=True)
                if m is None:
                    m = mx
                    acc = _dot(jnp.exp2(s - m).astype(BF16), va_scr[h, keys, :])
                else:
                    m_new = jnp.maximum(m, mx)
                    acc = jnp.exp2(m - m_new) * acc + _dot(jnp.exp2(s - m_new).astype(BF16),
                                                           va_scr[h, keys, :])
                    m = m_new
            res = acc / acc[:, base[h]:base[h] + 1]
            out = res if h % HEADS_PER_BLOCK == 0 else jnp.where(own[h], res, out)
            if h % HEADS_PER_BLOCK == HEADS_PER_BLOCK - 1:
                o_ref[rows, lanes[h]] = out.astype(o_ref.dtype)


def _attention(q, k, v, cq, ck, batch, seq):
    tokens = batch * seq
    n_groups = N_HEADS // (HEADS_PER_BLOCK * ATT_PAIRS)
    n_heads = ATT_PAIRS * HEADS_PER_BLOCK
    blk = pl.BlockSpec((seq, ATT_PAIRS * LANES), lambda b, g: (b, g))
    gate_blk = pl.BlockSpec((1, seq, LANES), lambda b, g: (b, 0, 0))
    wide = pltpu.VMEM((n_heads, seq, 2 * LANES), BF16)
    return pl.pallas_call(
        _attention_kernel,
        grid=(batch, n_groups),
        in_specs=[blk, blk, blk, gate_blk, gate_blk],
        out_specs=blk,
        out_shape=jax.ShapeDtypeStruct((tokens, D_MODEL), BF16),
        scratch_shapes=[wide, wide, pltpu.VMEM((n_heads, seq, LANES), BF16)],
        compiler_params=pltpu.CompilerParams(
            dimension_semantics=("arbitrary", "arbitrary"), vmem_limit_bytes=VMEM_LIMIT),
        name="fox_attention",
    )(q, k, v, cq, ck)


def _mixer_out_kernel(tiles_per_split, yc_ref, o_ref, sgc_ref, sga_ref, x_ref, wco_ref, wao_ref, wo_ref,
                      g_ref, wr_ref, x1_ref, h2_ref, route_ref, route_t_ref, counts_ref,
                      carry_scr, logit_scr):
    i = pl.program_id(0)
    tm = x_ref.shape[0]

    @pl.when(i == 0)
    def _():
        carry_scr[...] = jnp.zeros_like(carry_scr)
        logit_scr[...] = jnp.zeros_like(logit_scr)

    logits = logit_scr[...]
    lane = lax.broadcasted_iota(jnp.int32, (tm, LANES), 1).astype(F32)
    far = jnp.float32(4 * LANES)

    def first_lane_of_max(vals, vmax):
        return jnp.min(jnp.where(vals == vmax, lane, far), axis=1, keepdims=True)

    lg = jnp.where(lane < N_GROUPS, logits, NEG_BIG)
    gmax = jnp.max(lg, axis=1, keepdims=True)
    g_val = 1.0 / jnp.sum(jnp.exp(lg - gmax), axis=1, keepdims=True)
    g_idx = first_lane_of_max(lg, gmax)
    lo = N_GROUPS + EXPERTS_PER_GROUP * g_idx
    le = jnp.where((lane >= lo) & (lane < lo + EXPERTS_PER_GROUP), logits, NEG_BIG)
    e1max = jnp.max(le, axis=1, keepdims=True)
    e1 = first_lane_of_max(le, e1max)
    le2 = jnp.where(lane == e1, NEG_BIG, le)
    e2max = jnp.max(le2, axis=1, keepdims=True)
    e2 = first_lane_of_max(le2, e2max)
    ratio = jnp.exp(e2max - e1max)
    w1 = g_val / (1.0 + ratio)
    w2 = g_val * ratio / (1.0 + ratio)

    oh1 = lane == e1
    oh2 = lane == e2
    onehot = jnp.where(oh1 | oh2, 1.0, 0.0).astype(BF16)
    r_i = lax.broadcasted_iota(jnp.int32, (tm, tm), 0)
    c_i = lax.broadcasted_iota(jnp.int32, (tm, tm), 1)
    strict_lower = jnp.where(c_i < r_i, 1.0, 0.0).astype(BF16)
    keep = jnp.where((i - 1) % tiles_per_split == 0, 0.0, 1.0)
    carry = keep * carry_scr[...]
    before = _dot(strict_lower, onehot) + carry
    rank1 = jnp.sum(jnp.where(oh1, before, 0.0), axis=1, keepdims=True)
    rank2 = jnp.sum(jnp.where(oh2, before, 0.0), axis=1, keepdims=True)
    is_tile = jnp.where(i > 0, 1.0, 0.0)
    carry = carry + is_tile * jnp.sum(onehot.astype(F32), axis=0, keepdims=True)
    carry_scr[...] = carry
    counts_ref[0] = carry

    route = jnp.where(lane == 0, e1 - N_GROUPS, 0.0)
    route = jnp.where(lane == 1, e2 - N_GROUPS, route)
    route = jnp.where(lane == 2, w1, route)
    route = jnp.where(lane == 3, w2, route)
    route = jnp.where(lane == 4, rank1, route)
    route = jnp.where(lane == 5, rank2, route)
    route_ref[...] = route
    route_t_ref[...] = route.T[0:ROUTE_ROWS, :]

    y_conv = _dot(yc_ref[...], wco_ref[...])
    y_att = _dot(o_ref[...], wao_ref[...])
    m = sgc_ref[...].astype(F32) * y_conv + sga_ref[...].astype(F32) * y_att
    x1 = x_ref[...] + _dot(m.astype(BF16), wo_ref[...])
    x1_ref[...] = x1
    ms = jnp.mean(x1 * x1, axis=-1, keepdims=True)
    h2 = x1 * lax.rsqrt(ms + EPS) * g_ref[...]
    h2_ref[...] = _pack_bf16_pairs(h2)
    logit_scr[...] = _dot(h2.astype(BF16), wr_ref[...])


def _mixer_out(yc, o, sgc, sga, x2d, wco, wao, wo, g, wr):
    tokens = x2d.shape[0]
    tm = MIX_ROWS
    n_tiles = tokens // tm

    def cur(i):
        return (jnp.minimum(i, n_tiles - 1), 0)

    def prev(i):
        return (jnp.maximum(i - 1, 0), 0)

    row_blk = pl.BlockSpec((tm, D_MODEL), cur)
    w_blk = pl.BlockSpec((D_MODEL, D_MODEL), lambda i: (0, 0))
    return pl.pallas_call(
        functools.partial(_mixer_out_kernel, n_tiles // MOE_SPLITS),
        grid=(n_tiles + 1,),
        in_specs=[row_blk, row_blk, row_blk, row_blk, row_blk, w_blk, w_blk, w_blk,
                  pl.BlockSpec((1, D_MODEL), lambda i: (0, 0)),
                  pl.BlockSpec((D_MODEL, LANES), lambda i: (0, 0))],
        out_specs=[row_blk,
                   pl.BlockSpec((tm, D_MODEL // 2), cur),
                   pl.BlockSpec((tm, LANES), prev),
                   pl.BlockSpec((ROUTE_ROWS, tm), lambda i: (0, jnp.maximum(i - 1, 0))),
                   pl.BlockSpec((1, 1, LANES),
                                lambda i: (jnp.maximum(i - 1, 0) // (n_tiles // MOE_SPLITS), 0, 0))],
        out_shape=[jax.ShapeDtypeStruct((tokens, D_MODEL), F32),
                   jax.ShapeDtypeStruct((tokens, D_MODEL // 2), jnp.uint32),
                   jax.ShapeDtypeStruct((tokens, LANES), F32),
                   jax.ShapeDtypeStruct((ROUTE_ROWS, tokens), F32),
                   jax.ShapeDtypeStruct((MOE_SPLITS, 1, LANES), F32)],
        scratch_shapes=[pltpu.VMEM((1, LANES), F32), pltpu.VMEM((tm, LANES), F32)],
        compiler_params=pltpu.CompilerParams(
            dimension_semantics=("arbitrary",), vmem_limit_bytes=VMEM_LIMIT),
        name="mixer_out",
    )(yc, o, sgc, sga, x2d, wco, wao, wo, g, wr)


def _sc_index_rows(indices):
    n = indices.shape[0]
    return jnp.pad(indices.reshape(n // SC_WINDOW, SC_WINDOW), ((0, 0), (0, LANES - SC_WINDOW)))


def _sc_mesh():
    return plsc.VectorSubcoreMesh(core_axis_name="c", subcore_axis_name="s")


def _sc_scatter_rows(data, row0, n_src, indices, n_out):
    n = indices.shape[0]
    width = data.shape[1]
    src_blocks = n_src // SC_WINDOW
    block0 = row0 // SC_WINDOW

    @pl.kernel(out_type=jax.ShapeDtypeStruct((n_out, width), data.dtype), mesh=_sc_mesh())
    def scatter_kernel(x_hbm, i_hbm, o_hbm):
        def body(x_vmem, i_vmem):
            pltpu.sync_copy(x_vmem, o_hbm.at[i_vmem.at[0, pl.ds(0, SC_WINDOW)]])

        pltpu.emit_pipeline(
            body,
            grid=(n // SC_WINDOW,),
            in_specs=[pl.BlockSpec((SC_WINDOW, width), lambda i: (block0 + i % src_blocks, 0)),
                      pl.BlockSpec((1, LANES), lambda i: (i, 0))],
            out_specs=[],
            core_axis_name=("c", "s"),
            dimension_semantics=(pltpu.PARALLEL,),
        )(x_hbm, i_hbm)

    return scatter_kernel(data, _sc_index_rows(indices))


def _sc_gather_rows(data, indices):
    n = indices.shape[0]
    width = data.shape[1]

    @pl.kernel(out_type=jax.ShapeDtypeStruct((n, width), data.dtype), mesh=_sc_mesh())
    def gather_kernel(x_hbm, i_hbm, o_hbm):
        def body(i_vmem, o_vmem):
            pltpu.sync_copy(x_hbm.at[i_vmem.at[0, pl.ds(0, SC_WINDOW)]], o_vmem)

        pltpu.emit_pipeline(
            body,
            grid=(n // SC_WINDOW,),
            in_specs=[pl.BlockSpec((1, LANES), lambda i: (i, 0))],
            out_specs=[pl.BlockSpec((SC_WINDOW, width), lambda i: (i, 0))],
            core_axis_name=("c", "s"),
            dimension_semantics=(pltpu.PARALLEL,),
        )(i_hbm, o_hbm)

    return gather_kernel(data, _sc_index_rows(indices))


def _expert_kernel(block_e_ref, n_valid_ref, x_ref, wg_ref, wu_ref, wd_ref, y_ref,
                   wg_scr, wu_scr, wd_scr):
    i = pl.program_id(0)
    n_valid = n_valid_ref[i]

    @pl.when((i == 0) | (block_e_ref[i] != block_e_ref[jnp.maximum(i - 1, 0)]))
    def _():
        wg_scr[...] = wg_ref[0].astype(BF16)
        wu_scr[...] = wu_ref[0].astype(BF16)
        wd_scr[...] = wd_ref[0].astype(BF16)

    @pl.when(n_valid > 0)
    def _():
        for r in range(MOE_BLOCK // MOE_SUB):
            rows = slice(r * MOE_SUB, (r + 1) * MOE_SUB)
            row = r * MOE_SUB + lax.broadcasted_iota(jnp.int32, (MOE_SUB, x_ref.shape[1]), 0)
            packed = jnp.where(row < n_valid, x_ref[rows, :], jnp.uint32(0))
            xb = _unpack_bf16_pairs(packed).astype(BF16)
            a = _dot(xb, wg_scr[...])
            u = _dot(xb, wu_scr[...])
            hmid = (a * jax.nn.sigmoid(a) * u).astype(BF16)
            y_ref[rows, :] = _pack_bf16_pairs(_dot(hmid, wd_scr[...]))

    @pl.when(n_valid == 0)
    def _():
        y_ref[...] = jnp.zeros_like(y_ref)


def _experts(block_e, n_valid, x_rows, wg, wu, wd):
    n_rows, half = x_rows.shape
    n_blocks = n_rows // MOE_BLOCK

    def w_map(i, be, nv):
        return (be[i], 0, 0)

    row_blk = pl.BlockSpec((MOE_BLOCK, half), lambda i, be, nv: (i, 0))
    grid_spec = pltpu.PrefetchScalarGridSpec(
        num_scalar_prefetch=2,
        grid=(n_blocks,),
        in_specs=[row_blk,
                  pl.BlockSpec((1, D_MODEL, D_EXPERT), w_map),
                  pl.BlockSpec((1, D_MODEL, D_EXPERT), w_map),
                  pl.BlockSpec((1, D_EXPERT, D_MODEL), w_map)],
        out_specs=row_blk,
        scratch_shapes=[pltpu.VMEM((D_MODEL, D_EXPERT), BF16), pltpu.VMEM((D_MODEL, D_EXPERT), BF16),
                        pltpu.VMEM((D_EXPERT, D_MODEL), BF16)],
    )
    return pl.pallas_call(
        _expert_kernel,
        grid_spec=grid_spec,
        out_shape=jax.ShapeDtypeStruct((n_rows, half), jnp.uint32),
        compiler_params=pltpu.CompilerParams(
            dimension_semantics=("arbitrary",), vmem_limit_bytes=VMEM_LIMIT),
        name="moe_experts",
    )(block_e, n_valid, x_rows, wg, wu, wd)


def _combine_kernel(ya_ref, yb_ref, x1_ref, route_ref, g_ref, *rest):
    out_ref = rest[-1]
    route = route_ref[...]
    w1 = route[:, 2:3]
    w2 = route[:, 3:4]
    x2 = x1_ref[...] + (_unpack_bf16_pairs(ya_ref[...]) * w1 + _unpack_bf16_pairs(yb_ref[...]) * w2)
    ms = jnp.mean(x2 * x2, axis=-1, keepdims=True)
    out_ref[...] = x2 * lax.rsqrt(ms + EPS) * g_ref[...]


def _combine(y_pairs, x1, route, g, row0, prev_out):
    tokens = x1.shape[0]
    tc = MIX_ROWS
    n_tiles = y_pairs.shape[0] // 2 // tc
    tile0 = row0 // tc
    row_blk = pl.BlockSpec((tc, D_MODEL), lambda i: (tile0 + i, 0))
    in_specs = [pl.BlockSpec((tc, D_MODEL // 2), lambda i: (i, 0)),
                pl.BlockSpec((tc, D_MODEL // 2), lambda i: (i + n_tiles, 0)),
                row_blk,
                pl.BlockSpec((tc, LANES), lambda i: (tile0 + i, 0)),
                pl.BlockSpec((1, D_MODEL), lambda i: (0, 0))]
    args = [y_pairs, y_pairs, x1, route, g]
    aliases = {}
    if prev_out is not None:
        in_specs.append(pl.BlockSpec(memory_space=pl.ANY))
        args.append(prev_out)
        aliases = {len(args) - 1: 0}
    return pl.pallas_call(
        _combine_kernel,
        grid=(n_tiles,),
        in_specs=in_specs,
        out_specs=row_blk,
        out_shape=jax.ShapeDtypeStruct((tokens, D_MODEL), F32),
        input_output_aliases=aliases,
        compiler_params=pltpu.CompilerParams(
            dimension_semantics=("arbitrary",), vmem_limit_bytes=VMEM_LIMIT),
        name="moe_combine",
    )(*args)


def kernel(x, norm_mix_g, w_in, conv_w, conv_b, b_forget, w_conv_out, w_att_out, w_out,
           norm_ffn_g, w_router_group, w_router_expert, w_e_gate, w_e_up, w_e_down,
           norm_final_g):
    batch, seq, d = x.shape
    assert d == D_MODEL and seq % max(ATT_BLOCK, IN_PROJ_ROWS) == 0
    tokens = batch * seq
    assert tokens % (MOE_SPLITS * MIX_ROWS) == 0 and MIX_ROWS % SC_WINDOW == 0
    x2d = x.reshape(tokens, D_MODEL)

    wb = w_in.astype(BF16)
    n_main = 6 * D_MODEL
    gate0 = n_main + N_HEADS
    pieces = [wb[:, i * D_MODEL:(i + 1) * D_MODEL] for i in range(6)]
    pieces += [wb[:, gate0:gate0 + D_MODEL], wb[:, gate0 + D_MODEL:gate0 + 2 * D_MODEL]]
    w_stack = jnp.stack(pieces, axis=0)
    n_rep = 6
    wf = jnp.pad(jnp.tile(wb[:, n_main:gate0], (1, n_rep)), ((0, 0), (0, LANES - n_rep * N_HEADS)))
    bfv = jnp.pad(jnp.tile(b_forget.astype(F32), n_rep), (0, LANES - n_rep * N_HEADS)).reshape(1, LANES)
    wr = jnp.concatenate(
        [w_router_group, jnp.transpose(w_router_expert, (1, 0, 2)).reshape(D_MODEL, N_EXPERTS)], axis=1)
    wr = jnp.pad(wr, ((0, 0), (0, LANES - wr.shape[1]))).astype(BF16)

    yc, q, k, v, sgc, sga, cq, ck = _in_proj(
        x2d, norm_mix_g.reshape(1, D_MODEL), w_stack, wf, bfv, conv_w,
        conv_b.reshape(1, D_MODEL), batch, seq)
    o = _attention(q, k, v, cq, ck, batch, seq)

    x1, h2, route, route_t, counts = _mixer_out(
        yc, o, sgc, sga, x2d, w_conv_out.astype(BF16), w_att_out.astype(BF16),
        w_out.astype(BF16), norm_ffn_g.reshape(1, D_MODEL), wr)

    split = tokens // MOE_SPLITS
    n_rows = split * 2 + N_EXPERTS * MOE_BLOCK
    block_start = jnp.arange(n_rows // MOE_BLOCK, dtype=jnp.int32) * MOE_BLOCK
    expert_ids = jnp.arange(N_EXPERTS, dtype=jnp.int32)[:, None, None]
    g_final = norm_final_g.reshape(1, D_MODEL)
    out = None
    for sp in range(MOE_SPLITS):
        tok0 = sp * split
        cnt = counts[sp, 0, N_GROUPS:N_GROUPS + N_EXPERTS].astype(jnp.int32)
        padded = (cnt + MOE_BLOCK - 1) // MOE_BLOCK * MOE_BLOCK
        end_padded = jnp.cumsum(padded)
        start_padded = end_padded - padded
        expert = route_t[0:2, tok0:tok0 + split].astype(jnp.int32)
        rank = route_t[4:6, tok0:tok0 + split].astype(jnp.int32)
        dest = rank + jnp.sum(jnp.where(expert[None] == expert_ids, start_padded[:, None, None], 0), axis=0)
        dest_flat = dest.reshape(2 * split)
        block_e = jnp.minimum(jnp.sum(end_padded[None, :] <= block_start[:, None], axis=1),
                              N_EXPERTS - 1).astype(jnp.int32)
        used_lo = jnp.maximum(start_padded[None, :], block_start[:, None])
        used_hi = jnp.minimum((start_padded + cnt)[None, :], block_start[:, None] + MOE_BLOCK)
        n_valid = jnp.sum(jnp.maximum(used_hi - used_lo, 0), axis=1).astype(jnp.int32)

        x_rows = _sc_scatter_rows(h2, tok0, split, dest_flat, n_rows)
        y_rows = _experts(block_e, n_valid, x_rows, w_e_gate, w_e_up, w_e_down)
        y_pairs = _sc_gather_rows(y_rows, dest_flat)
        out = _combine(y_pairs, x1, route, g_final, tok0, out)
    return out.reshape(batch, seq, D_MODEL)
```

```python
import jax
import jax.numpy as jnp
from jax import lax
from jax.experimental import pallas as pl
from jax.experimental.pallas import tpu as pltpu
from jax.experimental.pallas import tpu_sc as plsc

D_MODEL = 1024
HEAD_DIM = 64
N_HEADS = 16
N_GROUPS = 4
EXPERTS_PER_GROUP = 8
N_EXPERTS = N_GROUPS * EXPERTS_PER_GROUP
D_EXPERT = 512
MOE_BLOCK = 512
CONV_WIDTH = 3
EPS = 1e-6

LANES = 128
HEADS_PER_BLOCK = LANES // HEAD_DIM
NEG_BIG = -1e30
LOG2E = 1.4426950408889634

IN_PROJ_COLS = 256
IN_PROJ_ROWS = 256
ATT_BLOCK = 256
ATT_KEYS = 256
MIX_ROWS = 512
ROUTE_ROWS = 8
SC_WINDOW = 64
VMEM_LIMIT = 56 * 1024 * 1024

F32 = jnp.float32
BF16 = jnp.bfloat16


def _dot(a, b):
    return jnp.dot(a, b, preferred_element_type=F32)


def _pack_bf16_pairs(x):
    c = x.shape[1] // 2
    lo = lax.bitcast_convert_type(x[:, :c].astype(BF16).astype(F32), jnp.uint32) >> 16
    hi = lax.bitcast_convert_type(x[:, c:].astype(BF16).astype(F32), jnp.uint32) & jnp.uint32(0xFFFF0000)
    return lo | hi


def _unpack_bf16_pairs(w):
    lo = lax.bitcast_convert_type(w << 16, F32)
    hi = lax.bitcast_convert_type(w & jnp.uint32(0xFFFF0000), F32)
    return jnp.concatenate([lo, hi], axis=1)


def _in_proj_kernel(x_ref, g_ref, w_ref, wf_ref, bf_ref, cw_ref, cb_ref,
                    yc_ref, q_ref, k_ref, v_ref, sgc_ref, sga_ref, cq_ref, ck_ref, h_scr):
    j = pl.program_id(1)
    seq = x_ref.shape[0]
    n_chunks = seq // IN_PROJ_ROWS

    @pl.when(j == 0)
    def _():
        for r in range(n_chunks):
            rows = slice(r * IN_PROJ_ROWS, (r + 1) * IN_PROJ_ROWS)
            xs = x_ref[rows, :]
            ms = jnp.mean(xs * xs, axis=-1, keepdims=True)
            h_scr[rows, :] = (xs * lax.rsqrt(ms + EPS) * g_ref[...]).astype(BF16)
        f = _dot(h_scr[...], wf_ref[...]) + bf_ref[...]
        c = jnp.minimum(f, 0.0) - jnp.log(1.0 + jnp.exp(-jnp.abs(f)))
        row = lax.broadcasted_iota(jnp.int32, c.shape, 0)
        d = 1
        while d < seq:
            c = c + jnp.where(row >= d, pltpu.roll(c, d, axis=0), 0.0)
            d *= 2
        c = c * LOG2E
        hi = c.astype(BF16).astype(F32)
        mid = (c - hi).astype(BF16).astype(F32)
        lo = (c - hi - mid).astype(BF16).astype(F32)
        grp = lax.broadcasted_iota(jnp.int32, c.shape, 1) // N_HEADS
        terms = jnp.where(grp % 3 == 0, hi, jnp.where(grp % 3 == 1, mid, lo))
        one = jnp.float32(1.0)
        cq_ref[0] = jnp.where(grp < 3, terms, jnp.where(grp < 6, one, 0.0)).astype(BF16)
        ck_ref[0] = jnp.where(grp < 3, one, jnp.where(grp < 6, -terms, 0.0)).astype(BF16)

    cw0 = cw_ref[0:1, :]
    cw1 = cw_ref[1:2, :]
    cw2 = cw_ref[2:3, :]
    cb = cb_ref[...]
    rowc = lax.broadcasted_iota(jnp.int32, (IN_PROJ_ROWS, IN_PROJ_COLS), 0)
    zprev = None
    for r in range(n_chunks):
        rows = slice(r * IN_PROJ_ROWS, (r + 1) * IN_PROJ_ROWS)
        hs = h_scr[rows, :]
        cb_gate = _dot(hs, w_ref[0])
        z = _dot(hs, w_ref[1]) * _dot(hs, w_ref[2])
        z1 = pltpu.roll(z, 1, axis=0)
        z2 = pltpu.roll(z, 2, axis=0)
        if zprev is None:
            p1 = jnp.zeros_like(z)
            p2 = p1
        else:
            p1 = pltpu.roll(zprev, 1, axis=0)
            p2 = pltpu.roll(zprev, 2, axis=0)
        z1 = jnp.where(rowc < 1, p1, z1)
        z2 = jnp.where(rowc < 2, p2, z2)
        acc = cb + cw0 * z2 + cw1 * z1 + cw2 * z
        yc_ref[rows, :] = (cb_gate * acc).astype(BF16)
        zprev = z
        q_ref[rows, :] = (_dot(hs, w_ref[3]) * (LOG2E * HEAD_DIM ** -0.5)).astype(BF16)
        k_ref[rows, :] = _dot(hs, w_ref[4]).astype(BF16)
        v_ref[rows, :] = _dot(hs, w_ref[5]).astype(BF16)
        sgc_ref[rows, :] = jax.nn.sigmoid(_dot(hs, w_ref[6])).astype(BF16)
        sga_ref[rows, :] = jax.nn.sigmoid(_dot(hs, w_ref[7])).astype(BF16)


def _in_proj(x2d, g, w_stack, wf, bfv, conv_w, conv_b, batch, seq):
    tokens = batch * seq
    tn = IN_PROJ_COLS
    nj = D_MODEL // tn
    col_out = pl.BlockSpec((seq, tn), lambda b, j: (b, j))
    out_bf16 = jax.ShapeDtypeStruct((tokens, D_MODEL), BF16)
    return pl.pallas_call(
        _in_proj_kernel,
        grid=(batch, nj),
        in_specs=[
            pl.BlockSpec((seq, D_MODEL), lambda b, j: (b, 0)),
            pl.BlockSpec((1, D_MODEL), lambda b, j: (0, 0)),
            pl.BlockSpec((8, D_MODEL, tn), lambda b, j: (0, 0, j)),
            pl.BlockSpec((D_MODEL, LANES), lambda b, j: (0, 0)),
            pl.BlockSpec((1, LANES), lambda b, j: (0, 0)),
            pl.BlockSpec((CONV_WIDTH, tn), lambda b, j: (0, j)),
            pl.BlockSpec((1, tn), lambda b, j: (0, j)),
        ],
        out_specs=[col_out] * 6 + [pl.BlockSpec((1, seq, LANES), lambda b, j: (b, 0, 0))] * 2,
        out_shape=[out_bf16] * 6 + [jax.ShapeDtypeStruct((batch, seq, LANES), BF16)] * 2,
        scratch_shapes=[pltpu.VMEM((seq, D_MODEL), BF16)],
        compiler_params=pltpu.CompilerParams(
            dimension_semantics=("arbitrary", "arbitrary"), vmem_limit_bytes=VMEM_LIMIT),
        name="in_proj",
    )(x2d, g, w_stack, wf, bfv, conv_w, conv_b)


def _attention_kernel(q_ref, k_ref, v_ref, cq_ref, ck_ref, o_ref, qa_scr, ka_scr, va_scr):
    hp = pl.program_id(1)
    seq = q_ref.shape[0]
    blk = ATT_BLOCK
    lane = lax.broadcasted_iota(jnp.int32, (1, LANES), 1)
    zero = jnp.zeros((), BF16)

    own, base = [], []
    for hh in range(HEADS_PER_BLOCK):
        own.append((lane // HEAD_DIM) == hh)
        base.append(((hh + 1) % HEADS_PER_BLOCK) * HEAD_DIM)
        gate = (lane % N_HEADS) == hp * HEADS_PER_BLOCK + hh
        qa_scr[hh, :, 0:LANES] = jnp.where(own[hh], q_ref[...], zero)
        qa_scr[hh, :, LANES:2 * LANES] = jnp.where(gate, cq_ref[0], zero)
        ka_scr[hh, :, 0:LANES] = jnp.where(own[hh], k_ref[...], zero)
        ka_scr[hh, :, LANES:2 * LANES] = jnp.where(gate, ck_ref[0], zero)
        va_scr[hh] = jnp.where(own[hh], v_ref[...],
                               jnp.where(lane == base[hh], 1.0, 0.0).astype(BF16))

    for rg in range(seq // blk):
        rows = slice(rg * blk, (rg + 1) * blk)
        out = None
        for hh in range(HEADS_PER_BLOCK):
            qa = qa_scr[hh, rows, :]
            m = acc = None
            k_end = (rg + 1) * blk
            for k0 in range(0, k_end, ATT_KEYS):
                k1 = min(k0 + ATT_KEYS, k_end)
                keys = slice(k0, k1)
                s = lax.dot_general(qa, ka_scr[hh, keys, :], (((1,), (1,)), ((), ())),
                                    preferred_element_type=F32)
                if k1 > rg * blk:
                    qpos = rg * blk + lax.broadcasted_iota(jnp.int32, s.shape, 0)
                    kpos = k0 + lax.broadcasted_iota(jnp.int32, s.shape, 1)
                    s = jnp.where(kpos <= qpos, s, NEG_BIG)
                mx = jnp.max(s, axis=1, keepdims=True)
                if m is None:
                    m = mx
                    acc = _dot(jnp.exp2(s - m).astype(BF16), va_scr[hh, keys, :])
                else:
                    m_new = jnp.maximum(m, mx)
                    acc = jnp.exp2(m - m_new) * acc + _dot(jnp.exp2(s - m_new).astype(BF16),
                                                           va_scr[hh, keys, :])
                    m = m_new
            res = acc / acc[:, base[hh]:base[hh] + 1]
            out = res if out is None else jnp.where(own[hh], res, out)
        o_ref[rows, :] = out.astype(o_ref.dtype)


def _attention(q, k, v, cq, ck, batch, seq):
    tokens = batch * seq
    n_hp = N_HEADS // HEADS_PER_BLOCK
    blk = pl.BlockSpec((seq, LANES), lambda b, hp: (b, hp))
    gate_blk = pl.BlockSpec((1, seq, LANES), lambda b, hp: (b, 0, 0))
    wide = pltpu.VMEM((HEADS_PER_BLOCK, seq, 2 * LANES), BF16)
    return pl.pallas_call(
        _attention_kernel,
        grid=(batch, n_hp),
        in_specs=[blk, blk, blk, gate_blk, gate_blk],
        out_specs=blk,
        out_shape=jax.ShapeDtypeStruct((tokens, D_MODEL), BF16),
        scratch_shapes=[wide, wide, pltpu.VMEM((HEADS_PER_BLOCK, seq, LANES), BF16)],
        compiler_params=pltpu.CompilerParams(
            dimension_semantics=("arbitrary", "arbitrary"), vmem_limit_bytes=VMEM_LIMIT),
        name="fox_attention",
    )(q, k, v, cq, ck)


def _mixer_out_kernel(yc_ref, o_ref, sgc_ref, sga_ref, x_ref, wco_ref, wao_ref, wo_ref,
                      g_ref, wr_ref, x1_ref, h2_ref, route_ref, route_t_ref, counts_ref,
                      carry_scr, logit_scr):
    i = pl.program_id(0)
    tm = x_ref.shape[0]

    @pl.when(i == 0)
    def _():
        carry_scr[...] = jnp.zeros_like(carry_scr)
        logit_scr[...] = jnp.zeros_like(logit_scr)

    logits = logit_scr[...]
    lane = lax.broadcasted_iota(jnp.int32, (tm, LANES), 1).astype(F32)
    far = jnp.float32(4 * LANES)

    def first_lane_of_max(vals, vmax):
        return jnp.min(jnp.where(vals == vmax, lane, far), axis=1, keepdims=True)

    lg = jnp.where(lane < N_GROUPS, logits, NEG_BIG)
    gmax = jnp.max(lg, axis=1, keepdims=True)
    g_val = 1.0 / jnp.sum(jnp.exp(lg - gmax), axis=1, keepdims=True)
    g_idx = first_lane_of_max(lg, gmax)
    lo = N_GROUPS + EXPERTS_PER_GROUP * g_idx
    le = jnp.where((lane >= lo) & (lane < lo + EXPERTS_PER_GROUP), logits, NEG_BIG)
    e1max = jnp.max(le, axis=1, keepdims=True)
    e1 = first_lane_of_max(le, e1max)
    le2 = jnp.where(lane == e1, NEG_BIG, le)
    e2max = jnp.max(le2, axis=1, keepdims=True)
    e2 = first_lane_of_max(le2, e2max)
    ratio = jnp.exp(e2max - e1max)
    w1 = g_val / (1.0 + ratio)
    w2 = g_val * ratio / (1.0 + ratio)

    oh1 = lane == e1
    oh2 = lane == e2
    onehot = jnp.where(oh1 | oh2, 1.0, 0.0).astype(BF16)
    r_i = lax.broadcasted_iota(jnp.int32, (tm, tm), 0)
    c_i = lax.broadcasted_iota(jnp.int32, (tm, tm), 1)
    strict_lower = jnp.where(c_i < r_i, 1.0, 0.0).astype(BF16)
    carry = carry_scr[...]
    before = _dot(strict_lower, onehot) + carry
    rank1 = jnp.sum(jnp.where(oh1, before, 0.0), axis=1, keepdims=True)
    rank2 = jnp.sum(jnp.where(oh2, before, 0.0), axis=1, keepdims=True)
    is_tile = jnp.where(i > 0, 1.0, 0.0)
    carry = carry + is_tile * jnp.sum(onehot.astype(F32), axis=0, keepdims=True)
    carry_scr[...] = carry
    counts_ref[...] = carry

    route = jnp.where(lane == 0, e1 - N_GROUPS, 0.0)
    route = jnp.where(lane == 1, e2 - N_GROUPS, route)
    route = jnp.where(lane == 2, w1, route)
    route = jnp.where(lane == 3, w2, route)
    route = jnp.where(lane == 4, rank1, route)
    route = jnp.where(lane == 5, rank2, route)
    route_ref[...] = route
    route_t_ref[...] = route.T[0:ROUTE_ROWS, :]

    y_conv = _dot(yc_ref[...], wco_ref[...])
    y_att = _dot(o_ref[...], wao_ref[...])
    m = sgc_ref[...].astype(F32) * y_conv + sga_ref[...].astype(F32) * y_att
    x1 = x_ref[...] + _dot(m.astype(BF16), wo_ref[...])
    x1_ref[...] = x1
    ms = jnp.mean(x1 * x1, axis=-1, keepdims=True)
    h2 = x1 * lax.rsqrt(ms + EPS) * g_ref[...]
    h2_ref[...] = _pack_bf16_pairs(h2)
    logit_scr[...] = _dot(h2.astype(BF16), wr_ref[...])


def _mixer_out(yc, o, sgc, sga, x2d, wco, wao, wo, g, wr):
    tokens = x2d.shape[0]
    tm = MIX_ROWS
    n_tiles = tokens // tm

    def cur(i):
        return (jnp.minimum(i, n_tiles - 1), 0)

    def prev(i):
        return (jnp.maximum(i - 1, 0), 0)

    row_blk = pl.BlockSpec((tm, D_MODEL), cur)
    w_blk = pl.BlockSpec((D_MODEL, D_MODEL), lambda i: (0, 0))
    return pl.pallas_call(
        _mixer_out_kernel,
        grid=(n_tiles + 1,),
        in_specs=[row_blk, row_blk, row_blk, row_blk, row_blk, w_blk, w_blk, w_blk,
                  pl.BlockSpec((1, D_MODEL), lambda i: (0, 0)),
                  pl.BlockSpec((D_MODEL, LANES), lambda i: (0, 0))],
        out_specs=[row_blk,
                   pl.BlockSpec((tm, D_MODEL // 2), cur),
                   pl.BlockSpec((tm, LANES), prev),
                   pl.BlockSpec((ROUTE_ROWS, tm), lambda i: (0, jnp.maximum(i - 1, 0))),
                   pl.BlockSpec((1, LANES), lambda i: (0, 0))],
        out_shape=[jax.ShapeDtypeStruct((tokens, D_MODEL), F32),
                   jax.ShapeDtypeStruct((tokens, D_MODEL // 2), jnp.uint32),
                   jax.ShapeDtypeStruct((tokens, LANES), F32),
                   jax.ShapeDtypeStruct((ROUTE_ROWS, tokens), F32),
                   jax.ShapeDtypeStruct((1, LANES), F32)],
        scratch_shapes=[pltpu.VMEM((1, LANES), F32), pltpu.VMEM((tm, LANES), F32)],
        compiler_params=pltpu.CompilerParams(
            dimension_semantics=("arbitrary",), vmem_limit_bytes=VMEM_LIMIT),
        name="mixer_out",
    )(yc, o, sgc, sga, x2d, wco, wao, wo, g, wr)


def _sc_index_rows(indices):
    n = indices.shape[0]
    return jnp.pad(indices.reshape(n // SC_WINDOW, SC_WINDOW), ((0, 0), (0, LANES - SC_WINDOW)))


def _sc_mesh():
    return plsc.VectorSubcoreMesh(core_axis_name="c", subcore_axis_name="s")


def _sc_scatter_rows(data, indices, n_out):
    n = indices.shape[0]
    n_src, width = data.shape
    src_blocks = n_src // SC_WINDOW

    @pl.kernel(out_type=jax.ShapeDtypeStruct((n_out, width), data.dtype), mesh=_sc_mesh())
    def scatter_kernel(x_hbm, i_hbm, o_hbm):
        def body(x_vmem, i_vmem):
            pltpu.sync_copy(x_vmem, o_hbm.at[i_vmem.at[0, pl.ds(0, SC_WINDOW)]])

        pltpu.emit_pipeline(
            body,
            grid=(n // SC_WINDOW,),
            in_specs=[pl.BlockSpec((SC_WINDOW, width), lambda i: (i % src_blocks, 0)),
                      pl.BlockSpec((1, LANES), lambda i: (i, 0))],
            out_specs=[],
            core_axis_name=("c", "s"),
            dimension_semantics=(pltpu.PARALLEL,),
        )(x_hbm, i_hbm)

    return scatter_kernel(data, _sc_index_rows(indices))


def _sc_gather_rows(data, indices):
    n = indices.shape[0]
    width = data.shape[1]

    @pl.kernel(out_type=jax.ShapeDtypeStruct((n, width), data.dtype), mesh=_sc_mesh())
    def gather_kernel(x_hbm, i_hbm, o_hbm):
        def body(i_vmem, o_vmem):
            pltpu.sync_copy(x_hbm.at[i_vmem.at[0, pl.ds(0, SC_WINDOW)]], o_vmem)

        pltpu.emit_pipeline(
            body,
            grid=(n // SC_WINDOW,),
            in_specs=[pl.BlockSpec((1, LANES), lambda i: (i, 0))],
            out_specs=[pl.BlockSpec((SC_WINDOW, width), lambda i: (i, 0))],
            core_axis_name=("c", "s"),
            dimension_semantics=(pltpu.PARALLEL,),
        )(i_hbm, o_hbm)

    return gather_kernel(data, _sc_index_rows(indices))


def _expert_kernel(block_e_ref, n_valid_ref, x_ref, wg_ref, wu_ref, wd_ref, y_ref,
                   wg_scr, wu_scr, wd_scr):
    i = pl.program_id(0)
    n_valid = n_valid_ref[i]

    @pl.when((i == 0) | (block_e_ref[i] != block_e_ref[jnp.maximum(i - 1, 0)]))
    def _():
        wg_scr[...] = wg_ref[0].astype(BF16)
        wu_scr[...] = wu_ref[0].astype(BF16)
        wd_scr[...] = wd_ref[0].astype(BF16)

    @pl.when(n_valid > 0)
    def _():
        row = lax.broadcasted_iota(jnp.int32, x_ref.shape, 0)
        packed = jnp.where(row < n_valid, x_ref[...], jnp.uint32(0))
        xb = _unpack_bf16_pairs(packed).astype(BF16)
        a = _dot(xb, wg_scr[...])
        u = _dot(xb, wu_scr[...])
        hmid = (a * jax.nn.sigmoid(a) * u).astype(BF16)
        y_ref[...] = _pack_bf16_pairs(_dot(hmid, wd_scr[...]))

    @pl.when(n_valid == 0)
    def _():
        y_ref[...] = jnp.zeros_like(y_ref)


def _experts(block_e, n_valid, x_rows, wg, wu, wd):
    n_rows, half = x_rows.shape
    n_blocks = n_rows // MOE_BLOCK

    def w_map(i, be, nv):
        return (be[i], 0, 0)

    row_blk = pl.BlockSpec((MOE_BLOCK, half), lambda i, be, nv: (i, 0))
    grid_spec = pltpu.PrefetchScalarGridSpec(
        num_scalar_prefetch=2,
        grid=(n_blocks,),
        in_specs=[row_blk,
                  pl.BlockSpec((1, D_MODEL, D_EXPERT), w_map),
                  pl.BlockSpec((1, D_MODEL, D_EXPERT), w_map),
                  pl.BlockSpec((1, D_EXPERT, D_MODEL), w_map)],
        out_specs=row_blk,
        scratch_shapes=[pltpu.VMEM((D_MODEL, D_EXPERT), BF16), pltpu.VMEM((D_MODEL, D_EXPERT), BF16),
                        pltpu.VMEM((D_EXPERT, D_MODEL), BF16)],
    )
    return pl.pallas_call(
        _expert_kernel,
        grid_spec=grid_spec,
        out_shape=jax.ShapeDtypeStruct((n_rows, half), jnp.uint32),
        compiler_params=pltpu.CompilerParams(
            dimension_semantics=("arbitrary",), vmem_limit_bytes=VMEM_LIMIT),
        name="moe_experts",
    )(block_e, n_valid, x_rows, wg, wu, wd)


def _combine_kernel(ya_ref, yb_ref, x1_ref, route_ref, g_ref, out_ref):
    route = route_ref[...]
    w1 = route[:, 2:3]
    w2 = route[:, 3:4]
    x2 = x1_ref[...] + (_unpack_bf16_pairs(ya_ref[...]) * w1 + _unpack_bf16_pairs(yb_ref[...]) * w2)
    ms = jnp.mean(x2 * x2, axis=-1, keepdims=True)
    out_ref[...] = x2 * lax.rsqrt(ms + EPS) * g_ref[...]


def _combine(y_pairs, x1, route, g):
    tokens = x1.shape[0]
    tc = MIX_ROWS
    n_tiles = tokens // tc
    row_blk = pl.BlockSpec((tc, D_MODEL), lambda i: (i, 0))
    return pl.pallas_call(
        _combine_kernel,
        grid=(n_tiles,),
        in_specs=[pl.BlockSpec((tc, D_MODEL // 2), lambda i: (i, 0)),
                  pl.BlockSpec((tc, D_MODEL // 2), lambda i: (i + n_tiles, 0)),
                  row_blk,
                  pl.BlockSpec((tc, LANES), lambda i: (i, 0)),
                  pl.BlockSpec((1, D_MODEL), lambda i: (0, 0))],
        out_specs=row_blk,
        out_shape=jax.ShapeDtypeStruct((tokens, D_MODEL), F32),
        compiler_params=pltpu.CompilerParams(
            dimension_semantics=("arbitrary",), vmem_limit_bytes=VMEM_LIMIT),
        name="moe_combine",
    )(y_pairs, y_pairs, x1, route, g)


def kernel(x, norm_mix_g, w_in, conv_w, conv_b, b_forget, w_conv_out, w_att_out, w_out,
           norm_ffn_g, w_router_group, w_router_expert, w_e_gate, w_e_up, w_e_down,
           norm_final_g):
    batch, seq, d = x.shape
    assert d == D_MODEL and seq % max(ATT_BLOCK, IN_PROJ_ROWS) == 0
    tokens = batch * seq
    assert tokens % MIX_ROWS == 0 and tokens % SC_WINDOW == 0
    x2d = x.reshape(tokens, D_MODEL)

    wb = w_in.astype(BF16)
    n_main = 6 * D_MODEL
    gate0 = n_main + N_HEADS
    pieces = [wb[:, i * D_MODEL:(i + 1) * D_MODEL] for i in range(6)]
    pieces += [wb[:, gate0:gate0 + D_MODEL], wb[:, gate0 + D_MODEL:gate0 + 2 * D_MODEL]]
    w_stack = jnp.stack(pieces, axis=0)
    n_rep = 6
    wf = jnp.pad(jnp.tile(wb[:, n_main:gate0], (1, n_rep)), ((0, 0), (0, LANES - n_rep * N_HEADS)))
    bfv = jnp.pad(jnp.tile(b_forget.astype(F32), n_rep), (0, LANES - n_rep * N_HEADS)).reshape(1, LANES)
    wr = jnp.concatenate(
        [w_router_group, jnp.transpose(w_router_expert, (1, 0, 2)).reshape(D_MODEL, N_EXPERTS)], axis=1)
    wr = jnp.pad(wr, ((0, 0), (0, LANES - wr.shape[1]))).astype(BF16)

    yc, q, k, v, sgc, sga, cq, ck = _in_proj(
        x2d, norm_mix_g.reshape(1, D_MODEL), w_stack, wf, bfv, conv_w,
        conv_b.reshape(1, D_MODEL), batch, seq)
    o = _attention(q, k, v, cq, ck, batch, seq)

    x1, h2, route, route_t, counts = _mixer_out(
        yc, o, sgc, sga, x2d, w_conv_out.astype(BF16), w_att_out.astype(BF16),
        w_out.astype(BF16), norm_ffn_g.reshape(1, D_MODEL), wr)

    counts = counts[0, N_GROUPS:N_GROUPS + N_EXPERTS].astype(jnp.int32)
    padded = (counts + MOE_BLOCK - 1) // MOE_BLOCK * MOE_BLOCK
    end_padded = jnp.cumsum(padded)
    start_padded = end_padded - padded
    expert = route_t[0:2].astype(jnp.int32)
    rank = route_t[4:6].astype(jnp.int32)
    expert_ids = jnp.arange(N_EXPERTS, dtype=jnp.int32)[:, None, None]
    dest = rank + jnp.sum(jnp.where(expert[None] == expert_ids, start_padded[:, None, None], 0), axis=0)
    dest_flat = dest.reshape(2 * tokens)
    n_rows = tokens * 2 + N_EXPERTS * MOE_BLOCK
    block_start = jnp.arange(n_rows // MOE_BLOCK, dtype=jnp.int32) * MOE_BLOCK
    block_e = jnp.minimum(jnp.sum(end_padded[None, :] <= block_start[:, None], axis=1),
                          N_EXPERTS - 1).astype(jnp.int32)
    used_lo = jnp.maximum(start_padded[None, :], block_start[:, None])
    used_hi = jnp.minimum((start_padded + counts)[None, :], block_start[:, None] + MOE_BLOCK)
    n_valid = jnp.sum(jnp.maximum(used_hi - used_lo, 0), axis=1).astype(jnp.int32)

    x_rows = _sc_scatter_rows(h2, dest_flat, n_rows)
    y_rows = _experts(block_e, n_valid, x_rows, w_e_gate, w_e_up, w_e_down)
    y_pairs = _sc_gather_rows(y_rows, dest_flat)
    out = _combine(y_pairs, x1, route, norm_final_g.reshape(1, D_MODEL))
    return out.reshape(batch, seq, D_MODEL)
```

```python
import jax
import jax.numpy as jnp
from jax import lax
from jax.experimental import pallas as pl
from jax.experimental.pallas import tpu as pltpu
from jax.experimental.pallas import tpu_sc as plsc

D_MODEL = 1024
HEAD_DIM = 64
N_HEADS = 16
N_GROUPS = 4
EXPERTS_PER_GROUP = 8
N_EXPERTS = N_GROUPS * EXPERTS_PER_GROUP
D_EXPERT = 512
MOE_BLOCK = 512
CONV_WIDTH = 3
EPS = 1e-6

LANES = 128
HEADS_PER_BLOCK = LANES // HEAD_DIM
NEG_BIG = -1e30
LOG2E = 1.4426950408889634

IN_PROJ_COLS = 256
IN_PROJ_ROWS = 256
ATT_BLOCK = 256
ATT_KEYS = 256
MIX_ROWS = 512
MIX_PART = 128
ROUTE_ROWS = 8
SC_WINDOW = 64
VMEM_LIMIT = 56 * 1024 * 1024

F32 = jnp.float32
BF16 = jnp.bfloat16


def _dot(a, b):
    return jnp.dot(a, b, preferred_element_type=F32)


def _pack_bf16_pairs(x):
    c = x.shape[1] // 2
    lo = lax.bitcast_convert_type(x[:, :c].astype(BF16).astype(F32), jnp.uint32) >> 16
    hi = lax.bitcast_convert_type(x[:, c:].astype(BF16).astype(F32), jnp.uint32) & jnp.uint32(0xFFFF0000)
    return lo | hi


def _unpack_bf16_pairs(w):
    lo = lax.bitcast_convert_type(w << 16, F32)
    hi = lax.bitcast_convert_type(w & jnp.uint32(0xFFFF0000), F32)
    return jnp.concatenate([lo, hi], axis=1)


def _in_proj_kernel(x_ref, g_ref, w_ref, wf_ref, bf_ref, cw_ref, cb_ref,
                    yc_ref, q_ref, k_ref, v_ref, sgc_ref, sga_ref, cq_ref, ck_ref, h_scr):
    j = pl.program_id(1)
    seq = x_ref.shape[0]
    n_chunks = seq // IN_PROJ_ROWS

    @pl.when(j == 0)
    def _():
        for r in range(n_chunks):
            rows = slice(r * IN_PROJ_ROWS, (r + 1) * IN_PROJ_ROWS)
            xs = x_ref[rows, :]
            ms = jnp.mean(xs * xs, axis=-1, keepdims=True)
            h_scr[rows, :] = (xs * lax.rsqrt(ms + EPS) * g_ref[...]).astype(BF16)
        f = _dot(h_scr[...], wf_ref[...]) + bf_ref[...]
        c = jnp.minimum(f, 0.0) - jnp.log(1.0 + jnp.exp(-jnp.abs(f)))
        row = lax.broadcasted_iota(jnp.int32, c.shape, 0)
        d = 1
        while d < seq:
            c = c + jnp.where(row >= d, pltpu.roll(c, d, axis=0), 0.0)
            d *= 2
        c = c * LOG2E
        hi = c.astype(BF16).astype(F32)
        mid = (c - hi).astype(BF16).astype(F32)
        lo = (c - hi - mid).astype(BF16).astype(F32)
        grp = lax.broadcasted_iota(jnp.int32, c.shape, 1) // N_HEADS
        terms = jnp.where(grp % 3 == 0, hi, jnp.where(grp % 3 == 1, mid, lo))
        one = jnp.float32(1.0)
        cq_ref[0] = jnp.where(grp < 3, terms, jnp.where(grp < 6, one, 0.0)).astype(BF16)
        ck_ref[0] = jnp.where(grp < 3, one, jnp.where(grp < 6, -terms, 0.0)).astype(BF16)

    cw0 = cw_ref[0:1, :]
    cw1 = cw_ref[1:2, :]
    cw2 = cw_ref[2:3, :]
    cb = cb_ref[...]
    rowc = lax.broadcasted_iota(jnp.int32, (IN_PROJ_ROWS, IN_PROJ_COLS), 0)
    zprev = None
    for r in range(n_chunks):
        rows = slice(r * IN_PROJ_ROWS, (r + 1) * IN_PROJ_ROWS)
        hs = h_scr[rows, :]
        cb_gate = _dot(hs, w_ref[0])
        z = _dot(hs, w_ref[1]) * _dot(hs, w_ref[2])
        z1 = pltpu.roll(z, 1, axis=0)
        z2 = pltpu.roll(z, 2, axis=0)
        if zprev is None:
            p1 = jnp.zeros_like(z)
            p2 = p1
        else:
            p1 = pltpu.roll(zprev, 1, axis=0)
            p2 = pltpu.roll(zprev, 2, axis=0)
        z1 = jnp.where(rowc < 1, p1, z1)
        z2 = jnp.where(rowc < 2, p2, z2)
        acc = cb + cw0 * z2 + cw1 * z1 + cw2 * z
        yc_ref[rows, :] = (cb_gate * acc).astype(BF16)
        zprev = z
        q_ref[rows, :] = (_dot(hs, w_ref[3]) * (LOG2E * HEAD_DIM ** -0.5)).astype(BF16)
        k_ref[rows, :] = _dot(hs, w_ref[4]).astype(BF16)
        v_ref[rows, :] = _dot(hs, w_ref[5]).astype(BF16)
        sgc_ref[rows, :] = jax.nn.sigmoid(_dot(hs, w_ref[6])).astype(BF16)
        sga_ref[rows, :] = jax.nn.sigmoid(_dot(hs, w_ref[7])).astype(BF16)


def _in_proj(x2d, g, w_stack, wf, bfv, conv_w, conv_b, batch, seq):
    tokens = batch * seq
    tn = IN_PROJ_COLS
    nj = D_MODEL // tn
    col_out = pl.BlockSpec((seq, tn), lambda b, j: (b, j))
    out_bf16 = jax.ShapeDtypeStruct((tokens, D_MODEL), BF16)
    return pl.pallas_call(
        _in_proj_kernel,
        grid=(batch, nj),
        in_specs=[
            pl.BlockSpec((seq, D_MODEL), lambda b, j: (b, 0)),
            pl.BlockSpec((1, D_MODEL), lambda b, j: (0, 0)),
            pl.BlockSpec((8, D_MODEL, tn), lambda b, j: (0, 0, j)),
            pl.BlockSpec((D_MODEL, LANES), lambda b, j: (0, 0)),
            pl.BlockSpec((1, LANES), lambda b, j: (0, 0)),
            pl.BlockSpec((CONV_WIDTH, tn), lambda b, j: (0, j)),
            pl.BlockSpec((1, tn), lambda b, j: (0, j)),
        ],
        out_specs=[col_out] * 6 + [pl.BlockSpec((1, seq, LANES), lambda b, j: (b, 0, 0))] * 2,
        out_shape=[out_bf16] * 6 + [jax.ShapeDtypeStruct((batch, seq, LANES), BF16)] * 2,
        scratch_shapes=[pltpu.VMEM((seq, D_MODEL), BF16)],
        compiler_params=pltpu.CompilerParams(
            dimension_semantics=("arbitrary", "arbitrary"), vmem_limit_bytes=VMEM_LIMIT),
        name="in_proj",
    )(x2d, g, w_stack, wf, bfv, conv_w, conv_b)


def _attention_kernel(q_ref, k_ref, v_ref, cq_ref, ck_ref, o_ref, qa_scr, ka_scr, va_scr):
    hp = pl.program_id(1)
    seq = q_ref.shape[0]
    blk = ATT_BLOCK
    lane = lax.broadcasted_iota(jnp.int32, (1, LANES), 1)
    zero = jnp.zeros((), BF16)

    own, base = [], []
    for hh in range(HEADS_PER_BLOCK):
        own.append((lane // HEAD_DIM) == hh)
        base.append(((hh + 1) % HEADS_PER_BLOCK) * HEAD_DIM)
        gate = (lane % N_HEADS) == hp * HEADS_PER_BLOCK + hh
        qa_scr[hh, :, 0:LANES] = jnp.where(own[hh], q_ref[...], zero)
        qa_scr[hh, :, LANES:2 * LANES] = jnp.where(gate, cq_ref[0], zero)
        ka_scr[hh, :, 0:LANES] = jnp.where(own[hh], k_ref[...], zero)
        ka_scr[hh, :, LANES:2 * LANES] = jnp.where(gate, ck_ref[0], zero)
        va_scr[hh] = jnp.where(own[hh], v_ref[...],
                               jnp.where(lane == base[hh], 1.0, 0.0).astype(BF16))

    for rg in range(seq // blk):
        rows = slice(rg * blk, (rg + 1) * blk)
        out = None
        for hh in range(HEADS_PER_BLOCK):
            qa = qa_scr[hh, rows, :]
            m = acc = None
            k_end = (rg + 1) * blk
            for k0 in range(0, k_end, ATT_KEYS):
                k1 = min(k0 + ATT_KEYS, k_end)
                keys = slice(k0, k1)
                s = lax.dot_general(qa, ka_scr[hh, keys, :], (((1,), (1,)), ((), ())),
                                    preferred_element_type=F32)
                if k1 > rg * blk:
                    qpos = rg * blk + lax.broadcasted_iota(jnp.int32, s.shape, 0)
                    kpos = k0 + lax.broadcasted_iota(jnp.int32, s.shape, 1)
                    s = jnp.where(kpos <= qpos, s, NEG_BIG)
                mx = jnp.max(s, axis=1, keepdims=True)
                if m is None:
                    m = mx
                    acc = _dot(jnp.exp2(s - m).astype(BF16), va_scr[hh, keys, :])
                else:
                    m_new = jnp.maximum(m, mx)
                    acc = jnp.exp2(m - m_new) * acc + _dot(jnp.exp2(s - m_new).astype(BF16),
                                                           va_scr[hh, keys, :])
                    m = m_new
            res = acc / acc[:, base[hh]:base[hh] + 1]
            out = res if out is None else jnp.where(own[hh], res, out)
        o_ref[rows, :] = out.astype(o_ref.dtype)


def _attention(q, k, v, cq, ck, batch, seq):
    tokens = batch * seq
    n_hp = N_HEADS // HEADS_PER_BLOCK
    blk = pl.BlockSpec((seq, LANES), lambda b, hp: (b, hp))
    gate_blk = pl.BlockSpec((1, seq, LANES), lambda b, hp: (b, 0, 0))
    wide = pltpu.VMEM((HEADS_PER_BLOCK, seq, 2 * LANES), BF16)
    return pl.pallas_call(
        _attention_kernel,
        grid=(batch, n_hp),
        in_specs=[blk, blk, blk, gate_blk, gate_blk],
        out_specs=blk,
        out_shape=jax.ShapeDtypeStruct((tokens, D_MODEL), BF16),
        scratch_shapes=[wide, wide, pltpu.VMEM((HEADS_PER_BLOCK, seq, LANES), BF16)],
        compiler_params=pltpu.CompilerParams(
            dimension_semantics=("arbitrary", "arbitrary"), vmem_limit_bytes=VMEM_LIMIT),
        name="fox_attention",
    )(q, k, v, cq, ck)


def _mixer_out_kernel(yc_ref, o_ref, sgc_ref, sga_ref, x_ref, wco_ref, wao_ref, wo_ref,
                      g_ref, wr_ref, x1_ref, h2_ref, route_ref, route_t_ref, counts_ref,
                      carry_scr, logit_scr):
    i = pl.program_id(0)
    tm = x_ref.shape[0]

    @pl.when(i == 0)
    def _():
        carry_scr[...] = jnp.zeros_like(carry_scr)
        logit_scr[...] = jnp.zeros_like(logit_scr)

    logits = logit_scr[...]
    lane = lax.broadcasted_iota(jnp.int32, (tm, LANES), 1).astype(F32)
    far = jnp.float32(4 * LANES)

    def first_lane_of_max(vals, vmax):
        return jnp.min(jnp.where(vals == vmax, lane, far), axis=1, keepdims=True)

    lg = jnp.where(lane < N_GROUPS, logits, NEG_BIG)
    gmax = jnp.max(lg, axis=1, keepdims=True)
    g_val = 1.0 / jnp.sum(jnp.exp(lg - gmax), axis=1, keepdims=True)
    g_idx = first_lane_of_max(lg, gmax)
    lo = N_GROUPS + EXPERTS_PER_GROUP * g_idx
    le = jnp.where((lane >= lo) & (lane < lo + EXPERTS_PER_GROUP), logits, NEG_BIG)
    e1max = jnp.max(le, axis=1, keepdims=True)
    e1 = first_lane_of_max(le, e1max)
    le2 = jnp.where(lane == e1, NEG_BIG, le)
    e2max = jnp.max(le2, axis=1, keepdims=True)
    e2 = first_lane_of_max(le2, e2max)
    ratio = jnp.exp(e2max - e1max)
    w1 = g_val / (1.0 + ratio)
    w2 = g_val * ratio / (1.0 + ratio)

    oh1 = lane == e1
    oh2 = lane == e2
    onehot = jnp.where(oh1 | oh2, 1.0, 0.0).astype(BF16)
    r_i = lax.broadcasted_iota(jnp.int32, (tm, tm), 0)
    c_i = lax.broadcasted_iota(jnp.int32, (tm, tm), 1)
    strict_lower = jnp.where(c_i < r_i, 1.0, 0.0).astype(BF16)
    carry = carry_scr[...]
    before = _dot(strict_lower, onehot) + carry
    rank1 = jnp.sum(jnp.where(oh1, before, 0.0), axis=1, keepdims=True)
    rank2 = jnp.sum(jnp.where(oh2, before, 0.0), axis=1, keepdims=True)
    is_tile = jnp.where(i > 0, 1.0, 0.0)
    carry = carry + is_tile * jnp.sum(onehot.astype(F32), axis=0, keepdims=True)
    carry_scr[...] = carry
    counts_ref[...] = carry

    route = jnp.where(lane == 0, e1 - N_GROUPS, 0.0)
    route = jnp.where(lane == 1, e2 - N_GROUPS, route)
    route = jnp.where(lane == 2, w1, route)
    route = jnp.where(lane == 3, w2, route)
    route = jnp.where(lane == 4, rank1, route)
    route = jnp.where(lane == 5, rank2, route)
    route_ref[...] = route
    route_t_ref[...] = route.T[0:ROUTE_ROWS, :]

    parts = [slice(c * MIX_PART, (c + 1) * MIX_PART) for c in range(tm // MIX_PART)]
    y_conv = [_dot(yc_ref[p, :], wco_ref[...]) for p in parts]
    y_att = [_dot(o_ref[p, :], wao_ref[...]) for p in parts]
    m = [(sgc_ref[p, :].astype(F32) * yc + sga_ref[p, :].astype(F32) * ya).astype(BF16)
         for p, yc, ya in zip(parts, y_conv, y_att)]
    x1 = [x_ref[p, :] + _dot(mm, wo_ref[...]) for p, mm in zip(parts, m)]
    for p, xx in zip(parts, x1):
        x1_ref[p, :] = xx
    h2 = [xx * lax.rsqrt(jnp.mean(xx * xx, axis=-1, keepdims=True) + EPS) * g_ref[...] for xx in x1]
    for p, hh in zip(parts, h2):
        h2_ref[p, :] = _pack_bf16_pairs(hh)
        logit_scr[p, :] = _dot(hh.astype(BF16), wr_ref[...])


def _mixer_out(yc, o, sgc, sga, x2d, wco, wao, wo, g, wr):
    tokens = x2d.shape[0]
    tm = MIX_ROWS
    n_tiles = tokens // tm

    def cur(i):
        return (jnp.minimum(i, n_tiles - 1), 0)

    def prev(i):
        return (jnp.maximum(i - 1, 0), 0)

    row_blk = pl.BlockSpec((tm, D_MODEL), cur)
    w_blk = pl.BlockSpec((D_MODEL, D_MODEL), lambda i: (0, 0))
    return pl.pallas_call(
        _mixer_out_kernel,
        grid=(n_tiles + 1,),
        in_specs=[row_blk, row_blk, row_blk, row_blk, row_blk, w_blk, w_blk, w_blk,
                  pl.BlockSpec((1, D_MODEL), lambda i: (0, 0)),
                  pl.BlockSpec((D_MODEL, LANES), lambda i: (0, 0))],
        out_specs=[row_blk,
                   pl.BlockSpec((tm, D_MODEL // 2), cur),
                   pl.BlockSpec((tm, LANES), prev),
                   pl.BlockSpec((ROUTE_ROWS, tm), lambda i: (0, jnp.maximum(i - 1, 0))),
                   pl.BlockSpec((1, LANES), lambda i: (0, 0))],
        out_shape=[jax.ShapeDtypeStruct((tokens, D_MODEL), F32),
                   jax.ShapeDtypeStruct((tokens, D_MODEL // 2), jnp.uint32),
                   jax.ShapeDtypeStruct((tokens, LANES), F32),
                   jax.ShapeDtypeStruct((ROUTE_ROWS, tokens), F32),
                   jax.ShapeDtypeStruct((1, LANES), F32)],
        scratch_shapes=[pltpu.VMEM((1, LANES), F32), pltpu.VMEM((tm, LANES), F32)],
        compiler_params=pltpu.CompilerParams(
            dimension_semantics=("arbitrary",), vmem_limit_bytes=VMEM_LIMIT),
        name="mixer_out",
    )(yc, o, sgc, sga, x2d, wco, wao, wo, g, wr)


def _sc_index_rows(indices):
    n = indices.shape[0]
    return jnp.pad(indices.reshape(n // SC_WINDOW, SC_WINDOW), ((0, 0), (0, LANES - SC_WINDOW)))


def _sc_mesh():
    return plsc.VectorSubcoreMesh(core_axis_name="c", subcore_axis_name="s")


def _sc_scatter_rows(data, indices, n_out):
    n = indices.shape[0]
    n_src, width = data.shape
    src_blocks = n_src // SC_WINDOW

    @pl.kernel(out_type=jax.ShapeDtypeStruct((n_out, width), data.dtype), mesh=_sc_mesh())
    def scatter_kernel(x_hbm, i_hbm, o_hbm):
        def body(x_vmem, i_vmem):
            pltpu.sync_copy(x_vmem, o_hbm.at[i_vmem.at[0, pl.ds(0, SC_WINDOW)]])

        pltpu.emit_pipeline(
            body,
            grid=(n // SC_WINDOW,),
            in_specs=[pl.BlockSpec((SC_WINDOW, width), lambda i: (i % src_blocks, 0)),
                      pl.BlockSpec((1, LANES), lambda i: (i, 0))],
            out_specs=[],
            core_axis_name=("c", "s"),
            dimension_semantics=(pltpu.PARALLEL,),
        )(x_hbm, i_hbm)

    return scatter_kernel(data, _sc_index_rows(indices))


def _sc_gather_rows(data, indices):
    n = indices.shape[0]
    width = data.shape[1]

    @pl.kernel(out_type=jax.ShapeDtypeStruct((n, width), data.dtype), mesh=_sc_mesh())
    def gather_kernel(x_hbm, i_hbm, o_hbm):
        def body(i_vmem, o_vmem):
            pltpu.sync_copy(x_hbm.at[i_vmem.at[0, pl.ds(0, SC_WINDOW)]], o_vmem)

        pltpu.emit_pipeline(
            body,
            grid=(n // SC_WINDOW,),
            in_specs=[pl.BlockSpec((1, LANES), lambda i: (i, 0))],
            out_specs=[pl.BlockSpec((SC_WINDOW, width), lambda i: (i, 0))],
            core_axis_name=("c", "s"),
            dimension_semantics=(pltpu.PARALLEL,),
        )(i_hbm, o_hbm)

    return gather_kernel(data, _sc_index_rows(indices))


def _expert_kernel(block_e_ref, n_valid_ref, x_ref, wg_ref, wu_ref, wd_ref, y_ref,
                   wg_scr, wu_scr, wd_scr):
    i = pl.program_id(0)
    n_valid = n_valid_ref[i]

    @pl.when((i == 0) | (block_e_ref[i] != block_e_ref[jnp.maximum(i - 1, 0)]))
    def _():
        wg_scr[...] = wg_ref[0].astype(BF16)
        wu_scr[...] = wu_ref[0].astype(BF16)
        wd_scr[...] = wd_ref[0].astype(BF16)

    @pl.when(n_valid > 0)
    def _():
        row = lax.broadcasted_iota(jnp.int32, x_ref.shape, 0)
        packed = jnp.where(row < n_valid, x_ref[...], jnp.uint32(0))
        xb = _unpack_bf16_pairs(packed).astype(BF16)
        a = _dot(xb, wg_scr[...])
        u = _dot(xb, wu_scr[...])
        hmid = (a * jax.nn.sigmoid(a) * u).astype(BF16)
        y_ref[...] = _pack_bf16_pairs(_dot(hmid, wd_scr[...]))

    @pl.when(n_valid == 0)
    def _():
        y_ref[...] = jnp.zeros_like(y_ref)


def _experts(block_e, n_valid, x_rows, wg, wu, wd):
    n_rows, half = x_rows.shape
    n_blocks = n_rows // MOE_BLOCK

    def w_map(i, be, nv):
        return (be[i], 0, 0)

    row_blk = pl.BlockSpec((MOE_BLOCK, half), lambda i, be, nv: (i, 0))
    grid_spec = pltpu.PrefetchScalarGridSpec(
        num_scalar_prefetch=2,
        grid=(n_blocks,),
        in_specs=[row_blk,
                  pl.BlockSpec((1, D_MODEL, D_EXPERT), w_map),
                  pl.BlockSpec((1, D_MODEL, D_EXPERT), w_map),
                  pl.BlockSpec((1, D_EXPERT, D_MODEL), w_map)],
        out_specs=row_blk,
        scratch_shapes=[pltpu.VMEM((D_MODEL, D_EXPERT), BF16), pltpu.VMEM((D_MODEL, D_EXPERT), BF16),
                        pltpu.VMEM((D_EXPERT, D_MODEL), BF16)],
    )
    return pl.pallas_call(
        _expert_kernel,
        grid_spec=grid_spec,
        out_shape=jax.ShapeDtypeStruct((n_rows, half), jnp.uint32),
        compiler_params=pltpu.CompilerParams(
            dimension_semantics=("arbitrary",), vmem_limit_bytes=VMEM_LIMIT),
        name="moe_experts",
    )(block_e, n_valid, x_rows, wg, wu, wd)


def _combine_kernel(ya_ref, yb_ref, x1_ref, route_ref, g_ref, out_ref):
    route = route_ref[...]
    w1 = route[:, 2:3]
    w2 = route[:, 3:4]
    x2 = x1_ref[...] + (_unpack_bf16_pairs(ya_ref[...]) * w1 + _unpack_bf16_pairs(yb_ref[...]) * w2)
    ms = jnp.mean(x2 * x2, axis=-1, keepdims=True)
    out_ref[...] = x2 * lax.rsqrt(ms + EPS) * g_ref[...]


def _combine(y_pairs, x1, route, g):
    tokens = x1.shape[0]
    tc = MIX_ROWS
    n_tiles = tokens // tc
    row_blk = pl.BlockSpec((tc, D_MODEL), lambda i: (i, 0))
    return pl.pallas_call(
        _combine_kernel,
        grid=(n_tiles,),
        in_specs=[pl.BlockSpec((tc, D_MODEL // 2), lambda i: (i, 0)),
                  pl.BlockSpec((tc, D_MODEL // 2), lambda i: (i + n_tiles, 0)),
                  row_blk,
                  pl.BlockSpec((tc, LANES), lambda i: (i, 0)),
                  pl.BlockSpec((1, D_MODEL), lambda i: (0, 0))],
        out_specs=row_blk,
        out_shape=jax.ShapeDtypeStruct((tokens, D_MODEL), F32),
        compiler_params=pltpu.CompilerParams(
            dimension_semantics=("arbitrary",), vmem_limit_bytes=VMEM_LIMIT),
        name="moe_combine",
    )(y_pairs, y_pairs, x1, route, g)


def kernel(x, norm_mix_g, w_in, conv_w, conv_b, b_forget, w_conv_out, w_att_out, w_out,
           norm_ffn_g, w_router_group, w_router_expert, w_e_gate, w_e_up, w_e_down,
           norm_final_g):
    batch, seq, d = x.shape
    assert d == D_MODEL and seq % max(ATT_BLOCK, IN_PROJ_ROWS) == 0
    tokens = batch * seq
    assert tokens % MIX_ROWS == 0 and tokens % SC_WINDOW == 0
    x2d = x.reshape(tokens, D_MODEL)

    wb = w_in.astype(BF16)
    n_main = 6 * D_MODEL
    gate0 = n_main + N_HEADS
    pieces = [wb[:, i * D_MODEL:(i + 1) * D_MODEL] for i in range(6)]
    pieces += [wb[:, gate0:gate0 + D_MODEL], wb[:, gate0 + D_MODEL:gate0 + 2 * D_MODEL]]
    w_stack = jnp.stack(pieces, axis=0)
    n_rep = 6
    wf = jnp.pad(jnp.tile(wb[:, n_main:gate0], (1, n_rep)), ((0, 0), (0, LANES - n_rep * N_HEADS)))
    bfv = jnp.pad(jnp.tile(b_forget.astype(F32), n_rep), (0, LANES - n_rep * N_HEADS)).reshape(1, LANES)
    wr = jnp.concatenate(
        [w_router_group, jnp.transpose(w_router_expert, (1, 0, 2)).reshape(D_MODEL, N_EXPERTS)], axis=1)
    wr = jnp.pad(wr, ((0, 0), (0, LANES - wr.shape[1]))).astype(BF16)

    yc, q, k, v, sgc, sga, cq, ck = _in_proj(
        x2d, norm_mix_g.reshape(1, D_MODEL), w_stack, wf, bfv, conv_w,
        conv_b.reshape(1, D_MODEL), batch, seq)
    o = _attention(q, k, v, cq, ck, batch, seq)

    x1, h2, route, route_t, counts = _mixer_out(
        yc, o, sgc, sga, x2d, w_conv_out.astype(BF16), w_att_out.astype(BF16),
        w_out.astype(BF16), norm_ffn_g.reshape(1, D_MODEL), wr)

    counts = counts[0, N_GROUPS:N_GROUPS + N_EXPERTS].astype(jnp.int32)
    padded = (counts + MOE_BLOCK - 1) // MOE_BLOCK * MOE_BLOCK
    end_padded = jnp.cumsum(padded)
    start_padded = end_padded - padded
    expert = route_t[0:2].astype(jnp.int32)
    rank = route_t[4:6].astype(jnp.int32)
    expert_ids = jnp.arange(N_EXPERTS, dtype=jnp.int32)[:, None, None]
    dest = rank + jnp.sum(jnp.where(expert[None] == expert_ids, start_padded[:, None, None], 0), axis=0)
    dest_flat = dest.reshape(2 * tokens)
    n_rows = tokens * 2 + N_EXPERTS * MOE_BLOCK
    block_start = jnp.arange(n_rows // MOE_BLOCK, dtype=jnp.int32) * MOE_BLOCK
    block_e = jnp.minimum(jnp.sum(end_padded[None, :] <= block_start[:, None], axis=1),
                          N_EXPERTS - 1).astype(jnp.int32)
    used_lo = jnp.maximum(start_padded[None, :], block_start[:, None])
    used_hi = jnp.minimum((start_padded + counts)[None, :], block_start[:, None] + MOE_BLOCK)
    n_valid = jnp.sum(jnp.maximum(used_hi - used_lo, 0), axis=1).astype(jnp.int32)

    x_rows = _sc_scatter_rows(h2, dest_flat, n_rows)
    y_rows = _experts(block_e, n_valid, x_rows, w_e_gate, w_e_up, w_e_down)
    y_pairs = _sc_gather_rows(y_rows, dest_flat)
    out = _combine(y_pairs, x1, route, norm_final_g.reshape(1, D_MODEL))
    return out.reshape(batch, seq, D_MODEL)
```

```python
import jax
import jax.numpy as jnp
from jax import lax
from jax.experimental import pallas as pl
from jax.experimental.pallas import tpu as pltpu
from jax.experimental.pallas import tpu_sc as plsc

D_MODEL = 1024
HEAD_DIM = 64
N_HEADS = 16
N_GROUPS = 4
EXPERTS_PER_GROUP = 8
N_EXPERTS = N_GROUPS * EXPERTS_PER_GROUP
D_EXPERT = 512
MOE_BLOCK = 512
CONV_WIDTH = 3
EPS = 1e-6

LANES = 128
HEADS_PER_BLOCK = LANES // HEAD_DIM
NEG_BIG = -1e30
LOG2E = 1.4426950408889634

IN_PROJ_COLS = 256
IN_PROJ_ROWS = 256
ATT_BLOCK = 256
ATT_KEYS = 256
MIX_ROWS = 512
MIX_PART = 128
ROUTE_ROWS = 8
SC_WINDOW = 64
VMEM_LIMIT = 56 * 1024 * 1024

F32 = jnp.float32
BF16 = jnp.bfloat16


def _dot(a, b):
    return jnp.dot(a, b, preferred_element_type=F32)


def _pack_bf16_pairs(x):
    c = x.shape[1] // 2
    lo = lax.bitcast_convert_type(x[:, :c].astype(BF16).astype(F32), jnp.uint32) >> 16
    hi = lax.bitcast_convert_type(x[:, c:].astype(BF16).astype(F32), jnp.uint32) & jnp.uint32(0xFFFF0000)
    return lo | hi


def _unpack_bf16_pairs(w):
    lo = lax.bitcast_convert_type(w << 16, F32)
    hi = lax.bitcast_convert_type(w & jnp.uint32(0xFFFF0000), F32)
    return jnp.concatenate([lo, hi], axis=1)


def _in_proj_kernel(x_ref, g_ref, w_ref, wf_ref, bf_ref, cw_ref, cb_ref,
                    yc_ref, q_ref, k_ref, v_ref, sgc_ref, sga_ref, cq_ref, ck_ref, h_scr):
    j = pl.program_id(1)
    seq = x_ref.shape[0]
    n_chunks = seq // IN_PROJ_ROWS

    @pl.when(j == 0)
    def _():
        for r in range(n_chunks):
            rows = slice(r * IN_PROJ_ROWS, (r + 1) * IN_PROJ_ROWS)
            xs = x_ref[rows, :]
            ms = jnp.mean(xs * xs, axis=-1, keepdims=True)
            h_scr[rows, :] = (xs * lax.rsqrt(ms + EPS) * g_ref[...]).astype(BF16)
        f = _dot(h_scr[...], wf_ref[...]) + bf_ref[...]
        c = jnp.minimum(f, 0.0) - jnp.log(1.0 + jnp.exp(-jnp.abs(f)))
        row = lax.broadcasted_iota(jnp.int32, c.shape, 0)
        d = 1
        while d < seq:
            c = c + jnp.where(row >= d, pltpu.roll(c, d, axis=0), 0.0)
            d *= 2
        c = c * LOG2E
        hi = c.astype(BF16).astype(F32)
        mid = (c - hi).astype(BF16).astype(F32)
        lo = (c - hi - mid).astype(BF16).astype(F32)
        grp = lax.broadcasted_iota(jnp.int32, c.shape, 1) // N_HEADS
        terms = jnp.where(grp % 3 == 0, hi, jnp.where(grp % 3 == 1, mid, lo))
        one = jnp.float32(1.0)
        cq_ref[0] = jnp.where(grp < 3, terms, jnp.where(grp < 6, one, 0.0)).astype(BF16)
        ck_ref[0] = jnp.where(grp < 3, one, jnp.where(grp < 6, -terms, 0.0)).astype(BF16)

    cw0 = cw_ref[0:1, :]
    cw1 = cw_ref[1:2, :]
    cw2 = cw_ref[2:3, :]
    cb = cb_ref[...]
    rowc = lax.broadcasted_iota(jnp.int32, (IN_PROJ_ROWS, IN_PROJ_COLS), 0)
    zprev = None
    for r in range(n_chunks):
        rows = slice(r * IN_PROJ_ROWS, (r + 1) * IN_PROJ_ROWS)
        hs = h_scr[rows, :]
        cb_gate = _dot(hs, w_ref[0])
        z = _dot(hs, w_ref[1]) * _dot(hs, w_ref[2])
        z1 = pltpu.roll(z, 1, axis=0)
        z2 = pltpu.roll(z, 2, axis=0)
        if zprev is None:
            p1 = jnp.zeros_like(z)
            p2 = p1
        else:
            p1 = pltpu.roll(zprev, 1, axis=0)
            p2 = pltpu.roll(zprev, 2, axis=0)
        z1 = jnp.where(rowc < 1, p1, z1)
        z2 = jnp.where(rowc < 2, p2, z2)
        acc = cb + cw0 * z2 + cw1 * z1 + cw2 * z
        yc_ref[rows, :] = (cb_gate * acc).astype(BF16)
        zprev = z
        q_ref[rows, :] = (_dot(hs, w_ref[3]) * (LOG2E * HEAD_DIM ** -0.5)).astype(BF16)
        k_ref[rows, :] = _dot(hs, w_ref[4]).astype(BF16)
        v_ref[rows, :] = _dot(hs, w_ref[5]).astype(BF16)
        sgc_ref[rows, :] = jax.nn.sigmoid(_dot(hs, w_ref[6])).astype(BF16)
        sga_ref[rows, :] = jax.nn.sigmoid(_dot(hs, w_ref[7])).astype(BF16)


def _in_proj(x2d, g, w_stack, wf, bfv, conv_w, conv_b, batch, seq):
    tokens = batch * seq
    tn = IN_PROJ_COLS
    nj = D_MODEL // tn
    col_out = pl.BlockSpec((seq, tn), lambda b, j: (b, j))
    out_bf16 = jax.ShapeDtypeStruct((tokens, D_MODEL), BF16)
    return pl.pallas_call(
        _in_proj_kernel,
        grid=(batch, nj),
        in_specs=[
            pl.BlockSpec((seq, D_MODEL), lambda b, j: (b, 0)),
            pl.BlockSpec((1, D_MODEL), lambda b, j: (0, 0)),
            pl.BlockSpec((8, D_MODEL, tn), lambda b, j: (0, 0, j)),
            pl.BlockSpec((D_MODEL, LANES), lambda b, j: (0, 0)),
            pl.BlockSpec((1, LANES), lambda b, j: (0, 0)),
            pl.BlockSpec((CONV_WIDTH, tn), lambda b, j: (0, j)),
            pl.BlockSpec((1, tn), lambda b, j: (0, j)),
        ],
        out_specs=[col_out] * 6 + [pl.BlockSpec((1, seq, LANES), lambda b, j: (b, 0, 0))] * 2,
        out_shape=[out_bf16] * 6 + [jax.ShapeDtypeStruct((batch, seq, LANES), BF16)] * 2,
        scratch_shapes=[pltpu.VMEM((seq, D_MODEL), BF16)],
        compiler_params=pltpu.CompilerParams(
            dimension_semantics=("arbitrary", "arbitrary"), vmem_limit_bytes=VMEM_LIMIT),
        name="in_proj",
    )(x2d, g, w_stack, wf, bfv, conv_w, conv_b)


def _attention_kernel(q_ref, k_ref, v_ref, cq_ref, ck_ref, o_ref, qa_scr, ka_scr, va_scr):
    hp = pl.program_id(1)
    seq = q_ref.shape[0]
    blk = ATT_BLOCK
    lane = lax.broadcasted_iota(jnp.int32, (1, LANES), 1)
    zero = jnp.zeros((), BF16)

    own, base = [], []
    for hh in range(HEADS_PER_BLOCK):
        own.append((lane // HEAD_DIM) == hh)
        base.append(((hh + 1) % HEADS_PER_BLOCK) * HEAD_DIM)
        gate = (lane % N_HEADS) == hp * HEADS_PER_BLOCK + hh
        qa_scr[hh, :, 0:LANES] = jnp.where(own[hh], q_ref[...], zero)
        qa_scr[hh, :, LANES:2 * LANES] = jnp.where(gate, cq_ref[0], zero)
        ka_scr[hh, :, 0:LANES] = jnp.where(own[hh], k_ref[...], zero)
        ka_scr[hh, :, LANES:2 * LANES] = jnp.where(gate, ck_ref[0], zero)
        va_scr[hh] = jnp.where(own[hh], v_ref[...],
                               jnp.where(lane == base[hh], 1.0, 0.0).astype(BF16))

    for rg in range(seq // blk):
        rows = slice(rg * blk, (rg + 1) * blk)
        out = None
        for hh in range(HEADS_PER_BLOCK):
            qa = qa_scr[hh, rows, :]
            m = acc = None
            k_end = (rg + 1) * blk
            for k0 in range(0, k_end, ATT_KEYS):
                k1 = min(k0 + ATT_KEYS, k_end)
                keys = slice(k0, k1)
                s = lax.dot_general(qa, ka_scr[hh, keys, :], (((1,), (1,)), ((), ())),
                                    preferred_element_type=F32)
                if k1 > rg * blk:
                    qpos = rg * blk + lax.broadcasted_iota(jnp.int32, s.shape, 0)
                    kpos = k0 + lax.broadcasted_iota(jnp.int32, s.shape, 1)
                    s = jnp.where(kpos <= qpos, s, NEG_BIG)
                mx = jnp.max(s, axis=1, keepdims=True)
                if m is None:
                    m = mx
                    acc = _dot(jnp.exp2(s - m).astype(BF16), va_scr[hh, keys, :])
                else:
                    m_new = jnp.maximum(m, mx)
                    acc = jnp.exp2(m - m_new) * acc + _dot(jnp.exp2(s - m_new).astype(BF16),
                                                           va_scr[hh, keys, :])
                    m = m_new
            res = acc / acc[:, base[hh]:base[hh] + 1]
            out = res if out is None else jnp.where(own[hh], res, out)
        o_ref[rows, :] = out.astype(o_ref.dtype)


def _attention(q, k, v, cq, ck, batch, seq):
    tokens = batch * seq
    n_hp = N_HEADS // HEADS_PER_BLOCK
    blk = pl.BlockSpec((seq, LANES), lambda b, hp: (b, hp))
    gate_blk = pl.BlockSpec((1, seq, LANES), lambda b, hp: (b, 0, 0))
    wide = pltpu.VMEM((HEADS_PER_BLOCK, seq, 2 * LANES), BF16)
    return pl.pallas_call(
        _attention_kernel,
        grid=(batch, n_hp),
        in_specs=[blk, blk, blk, gate_blk, gate_blk],
        out_specs=blk,
        out_shape=jax.ShapeDtypeStruct((tokens, D_MODEL), BF16),
        scratch_shapes=[wide, wide, pltpu.VMEM((HEADS_PER_BLOCK, seq, LANES), BF16)],
        compiler_params=pltpu.CompilerParams(
            dimension_semantics=("arbitrary", "arbitrary"), vmem_limit_bytes=VMEM_LIMIT),
        name="fox_attention",
    )(q, k, v, cq, ck)


def _mixer_out_kernel(yc_ref, o_ref, sgc_ref, sga_ref, x_ref, wco_ref, wao_ref, wo_ref,
                      g_ref, wr_ref, x1_ref, h2_ref, route_ref, route_t_ref, counts_ref,
                      carry_scr, logit_scr):
    i = pl.program_id(0)
    tm = x_ref.shape[0]

    @pl.when(i == 0)
    def _():
        carry_scr[...] = jnp.zeros_like(carry_scr)
        logit_scr[...] = jnp.zeros_like(logit_scr)

    logits = logit_scr[...]

    def route_previous_tile():
        lane = lax.broadcasted_iota(jnp.int32, (tm, LANES), 1).astype(F32)
        far = jnp.float32(4 * LANES)

        def first_lane_of_max(vals, vmax):
            return jnp.min(jnp.where(vals == vmax, lane, far), axis=1, keepdims=True)

        lg = jnp.where(lane < N_GROUPS, logits, NEG_BIG)
        gmax = jnp.max(lg, axis=1, keepdims=True)
        g_val = 1.0 / jnp.sum(jnp.exp(lg - gmax), axis=1, keepdims=True)
        g_idx = first_lane_of_max(lg, gmax)
        lo = N_GROUPS + EXPERTS_PER_GROUP * g_idx
        le = jnp.where((lane >= lo) & (lane < lo + EXPERTS_PER_GROUP), logits, NEG_BIG)
        e1max = jnp.max(le, axis=1, keepdims=True)
        e1 = first_lane_of_max(le, e1max)
        le2 = jnp.where(lane == e1, NEG_BIG, le)
        e2max = jnp.max(le2, axis=1, keepdims=True)
        e2 = first_lane_of_max(le2, e2max)
        ratio = jnp.exp(e2max - e1max)
        w1 = g_val / (1.0 + ratio)
        w2 = g_val * ratio / (1.0 + ratio)

        oh1 = lane == e1
        oh2 = lane == e2
        onehot = jnp.where(oh1 | oh2, 1.0, 0.0).astype(BF16)
        r_i = lax.broadcasted_iota(jnp.int32, (tm, tm), 0)
        c_i = lax.broadcasted_iota(jnp.int32, (tm, tm), 1)
        strict_lower = jnp.where(c_i < r_i, 1.0, 0.0).astype(BF16)
        carry = carry_scr[...]
        before = _dot(strict_lower, onehot) + carry
        rank1 = jnp.sum(jnp.where(oh1, before, 0.0), axis=1, keepdims=True)
        rank2 = jnp.sum(jnp.where(oh2, before, 0.0), axis=1, keepdims=True)
        is_tile = jnp.where(i > 0, 1.0, 0.0)
        carry = carry + is_tile * jnp.sum(onehot.astype(F32), axis=0, keepdims=True)
        carry_scr[...] = carry
        counts_ref[...] = carry

        route = jnp.where(lane == 0, e1 - N_GROUPS, 0.0)
        route = jnp.where(lane == 1, e2 - N_GROUPS, route)
        route = jnp.where(lane == 2, w1, route)
        route = jnp.where(lane == 3, w2, route)
        route = jnp.where(lane == 4, rank1, route)
        route = jnp.where(lane == 5, rank2, route)
        route_ref[...] = route
        route_t_ref[...] = route.T[0:ROUTE_ROWS, :]

    parts = [slice(c * MIX_PART, (c + 1) * MIX_PART) for c in range(tm // MIX_PART)]
    y_conv = [_dot(yc_ref[p, :], wco_ref[...]) for p in parts]
    y_att = [_dot(o_ref[p, :], wao_ref[...]) for p in parts]
    m = [(sgc_ref[p, :].astype(F32) * yc + sga_ref[p, :].astype(F32) * ya).astype(BF16)
         for p, yc, ya in zip(parts, y_conv, y_att)]
    x1 = [x_ref[p, :] + _dot(mm, wo_ref[...]) for p, mm in zip(parts, m)]
    for p, xx in zip(parts, x1):
        x1_ref[p, :] = xx
    route_previous_tile()
    h2 = [xx * lax.rsqrt(jnp.mean(xx * xx, axis=-1, keepdims=True) + EPS) * g_ref[...] for xx in x1]
    for p, hh in zip(parts, h2):
        h2_ref[p, :] = _pack_bf16_pairs(hh)
        logit_scr[p, :] = _dot(hh.astype(BF16), wr_ref[...])


def _mixer_out(yc, o, sgc, sga, x2d, wco, wao, wo, g, wr):
    tokens = x2d.shape[0]
    tm = MIX_ROWS
    n_tiles = tokens // tm

    def cur(i):
        return (jnp.minimum(i, n_tiles - 1), 0)

    def prev(i):
        return (jnp.maximum(i - 1, 0), 0)

    row_blk = pl.BlockSpec((tm, D_MODEL), cur)
    w_blk = pl.BlockSpec((D_MODEL, D_MODEL), lambda i: (0, 0))
    return pl.pallas_call(
        _mixer_out_kernel,
        grid=(n_tiles + 1,),
        in_specs=[row_blk, row_blk, row_blk, row_blk, row_blk, w_blk, w_blk, w_blk,
                  pl.BlockSpec((1, D_MODEL), lambda i: (0, 0)),
                  pl.BlockSpec((D_MODEL, LANES), lambda i: (0, 0))],
        out_specs=[row_blk,
                   pl.BlockSpec((tm, D_MODEL // 2), cur),
                   pl.BlockSpec((tm, LANES), prev),
                   pl.BlockSpec((ROUTE_ROWS, tm), lambda i: (0, jnp.maximum(i - 1, 0))),
                   pl.BlockSpec((1, LANES), lambda i: (0, 0))],
        out_shape=[jax.ShapeDtypeStruct((tokens, D_MODEL), F32),
                   jax.ShapeDtypeStruct((tokens, D_MODEL // 2), jnp.uint32),
                   jax.ShapeDtypeStruct((tokens, LANES), F32),
                   jax.ShapeDtypeStruct((ROUTE_ROWS, tokens), F32),
                   jax.ShapeDtypeStruct((1, LANES), F32)],
        scratch_shapes=[pltpu.VMEM((1, LANES), F32), pltpu.VMEM((tm, LANES), F32)],
        compiler_params=pltpu.CompilerParams(
            dimension_semantics=("arbitrary",), vmem_limit_bytes=VMEM_LIMIT),
        name="mixer_out",
    )(yc, o, sgc, sga, x2d, wco, wao, wo, g, wr)


def _sc_index_rows(indices):
    n = indices.shape[0]
    return jnp.pad(indices.reshape(n // SC_WINDOW, SC_WINDOW), ((0, 0), (0, LANES - SC_WINDOW)))


def _sc_mesh():
    return plsc.VectorSubcoreMesh(core_axis_name="c", subcore_axis_name="s")


def _sc_scatter_rows(data, indices, n_out):
    n = indices.shape[0]
    n_src, width = data.shape
    src_blocks = n_src // SC_WINDOW

    @pl.kernel(out_type=jax.ShapeDtypeStruct((n_out, width), data.dtype), mesh=_sc_mesh())
    def scatter_kernel(x_hbm, i_hbm, o_hbm):
        def body(x_vmem, i_vmem):
            pltpu.sync_copy(x_vmem, o_hbm.at[i_vmem.at[0, pl.ds(0, SC_WINDOW)]])

        pltpu.emit_pipeline(
            body,
            grid=(n // SC_WINDOW,),
            in_specs=[pl.BlockSpec((SC_WINDOW, width), lambda i: (i % src_blocks, 0)),
                      pl.BlockSpec((1, LANES), lambda i: (i, 0))],
            out_specs=[],
            core_axis_name=("c", "s"),
            dimension_semantics=(pltpu.PARALLEL,),
        )(x_hbm, i_hbm)

    return scatter_kernel(data, _sc_index_rows(indices))


def _sc_gather_rows(data, indices):
    n = indices.shape[0]
    width = data.shape[1]

    @pl.kernel(out_type=jax.ShapeDtypeStruct((n, width), data.dtype), mesh=_sc_mesh())
    def gather_kernel(x_hbm, i_hbm, o_hbm):
        def body(i_vmem, o_vmem):
            pltpu.sync_copy(x_hbm.at[i_vmem.at[0, pl.ds(0, SC_WINDOW)]], o_vmem)

        pltpu.emit_pipeline(
            body,
            grid=(n // SC_WINDOW,),
            in_specs=[pl.BlockSpec((1, LANES), lambda i: (i, 0))],
            out_specs=[pl.BlockSpec((SC_WINDOW, width), lambda i: (i, 0))],
            core_axis_name=("c", "s"),
            dimension_semantics=(pltpu.PARALLEL,),
        )(i_hbm, o_hbm)

    return gather_kernel(data, _sc_index_rows(indices))


def _expert_kernel(block_e_ref, n_valid_ref, x_ref, wg_ref, wu_ref, wd_ref, y_ref,
                   wg_scr, wu_scr, wd_scr):
    i = pl.program_id(0)
    n_valid = n_valid_ref[i]

    @pl.when((i == 0) | (block_e_ref[i] != block_e_ref[jnp.maximum(i - 1, 0)]))
    def _():
        wg_scr[...] = wg_ref[0].astype(BF16)
        wu_scr[...] = wu_ref[0].astype(BF16)
        wd_scr[...] = wd_ref[0].astype(BF16)

    @pl.when(n_valid > 0)
    def _():
        row = lax.broadcasted_iota(jnp.int32, x_ref.shape, 0)
        packed = jnp.where(row < n_valid, x_ref[...], jnp.uint32(0))
        xb = _unpack_bf16_pairs(packed).astype(BF16)
        a = _dot(xb, wg_scr[...])
        u = _dot(xb, wu_scr[...])
        hmid = (a * jax.nn.sigmoid(a) * u).astype(BF16)
        y_ref[...] = _pack_bf16_pairs(_dot(hmid, wd_scr[...]))

    @pl.when(n_valid == 0)
    def _():
        y_ref[...] = jnp.zeros_like(y_ref)


def _experts(block_e, n_valid, x_rows, wg, wu, wd):
    n_rows, half = x_rows.shape
    n_blocks = n_rows // MOE_BLOCK

    def w_map(i, be, nv):
        return (be[i], 0, 0)

    row_blk = pl.BlockSpec((MOE_BLOCK, half), lambda i, be, nv: (i, 0))
    grid_spec = pltpu.PrefetchScalarGridSpec(
        num_scalar_prefetch=2,
        grid=(n_blocks,),
        in_specs=[row_blk,
                  pl.BlockSpec((1, D_MODEL, D_EXPERT), w_map),
                  pl.BlockSpec((1, D_MODEL, D_EXPERT), w_map),
                  pl.BlockSpec((1, D_EXPERT, D_MODEL), w_map)],
        out_specs=row_blk,
        scratch_shapes=[pltpu.VMEM((D_MODEL, D_EXPERT), BF16), pltpu.VMEM((D_MODEL, D_EXPERT), BF16),
                        pltpu.VMEM((D_EXPERT, D_MODEL), BF16)],
    )
    return pl.pallas_call(
        _expert_kernel,
        grid_spec=grid_spec,
        out_shape=jax.ShapeDtypeStruct((n_rows, half), jnp.uint32),
        compiler_params=pltpu.CompilerParams(
            dimension_semantics=("arbitrary",), vmem_limit_bytes=VMEM_LIMIT),
        name="moe_experts",
    )(block_e, n_valid, x_rows, wg, wu, wd)


def _combine_kernel(ya_ref, yb_ref, x1_ref, route_ref, g_ref, out_ref):
    route = route_ref[...]
    w1 = route[:, 2:3]
    w2 = route[:, 3:4]
    x2 = x1_ref[...] + (_unpack_bf16_pairs(ya_ref[...]) * w1 + _unpack_bf16_pairs(yb_ref[...]) * w2)
    ms = jnp.mean(x2 * x2, axis=-1, keepdims=True)
    out_ref[...] = x2 * lax.rsqrt(ms + EPS) * g_ref[...]


def _combine(y_pairs, x1, route, g):
    tokens = x1.shape[0]
    tc = MIX_ROWS
    n_tiles = tokens // tc
    row_blk = pl.BlockSpec((tc, D_MODEL), lambda i: (i, 0))
    return pl.pallas_call(
        _combine_kernel,
        grid=(n_tiles,),
        in_specs=[pl.BlockSpec((tc, D_MODEL // 2), lambda i: (i, 0)),
                  pl.BlockSpec((tc, D_MODEL // 2), lambda i: (i + n_tiles, 0)),
                  row_blk,
                  pl.BlockSpec((tc, LANES), lambda i: (i, 0)),
                  pl.BlockSpec((1, D_MODEL), lambda i: (0, 0))],
        out_specs=row_blk,
        out_shape=jax.ShapeDtypeStruct((tokens, D_MODEL), F32),
        compiler_params=pltpu.CompilerParams(
            dimension_semantics=("arbitrary",), vmem_limit_bytes=VMEM_LIMIT),
        name="moe_combine",
    )(y_pairs, y_pairs, x1, route, g)


def kernel(x, norm_mix_g, w_in, conv_w, conv_b, b_forget, w_conv_out, w_att_out, w_out,
           norm_ffn_g, w_router_group, w_router_expert, w_e_gate, w_e_up, w_e_down,
           norm_final_g):
    batch, seq, d = x.shape
    assert d == D_MODEL and seq % max(ATT_BLOCK, IN_PROJ_ROWS) == 0
    tokens = batch * seq
    assert tokens % MIX_ROWS == 0 and tokens % SC_WINDOW == 0
    x2d = x.reshape(tokens, D_MODEL)

    wb = w_in.astype(BF16)
    n_main = 6 * D_MODEL
    gate0 = n_main + N_HEADS
    pieces = [wb[:, i * D_MODEL:(i + 1) * D_MODEL] for i in range(6)]
    pieces += [wb[:, gate0:gate0 + D_MODEL], wb[:, gate0 + D_MODEL:gate0 + 2 * D_MODEL]]
    w_stack = jnp.stack(pieces, axis=0)
    n_rep = 6
    wf = jnp.pad(jnp.tile(wb[:, n_main:gate0], (1, n_rep)), ((0, 0), (0, LANES - n_rep * N_HEADS)))
    bfv = jnp.pad(jnp.tile(b_forget.astype(F32), n_rep), (0, LANES - n_rep * N_HEADS)).reshape(1, LANES)
    wr = jnp.concatenate(
        [w_router_group, jnp.transpose(w_router_expert, (1, 0, 2)).reshape(D_MODEL, N_EXPERTS)], axis=1)
    wr = jnp.pad(wr, ((0, 0), (0, LANES - wr.shape[1]))).astype(BF16)

    yc, q, k, v, sgc, sga, cq, ck = _in_proj(
        x2d, norm_mix_g.reshape(1, D_MODEL), w_stack, wf, bfv, conv_w,
        conv_b.reshape(1, D_MODEL), batch, seq)
    o = _attention(q, k, v, cq, ck, batch, seq)

    x1, h2, route, route_t, counts = _mixer_out(
        yc, o, sgc, sga, x2d, w_conv_out.astype(BF16), w_att_out.astype(BF16),
        w_out.astype(BF16), norm_ffn_g.reshape(1, D_MODEL), wr)

    counts = counts[0, N_GROUPS:N_GROUPS + N_EXPERTS].astype(jnp.int32)
    padded = (counts + MOE_BLOCK - 1) // MOE_BLOCK * MOE_BLOCK
    end_padded = jnp.cumsum(padded)
    start_padded = end_padded - padded
    expert = route_t[0:2].astype(jnp.int32)
    rank = route_t[4:6].astype(jnp.int32)
    expert_ids = jnp.arange(N_EXPERTS, dtype=jnp.int32)[:, None, None]
    dest = rank + jnp.sum(jnp.where(expert[None] == expert_ids, start_padded[:, None, None], 0), axis=0)
    dest_flat = dest.reshape(2 * tokens)
    n_rows = tokens * 2 + N_EXPERTS * MOE_BLOCK
    block_start = jnp.arange(n_rows // MOE_BLOCK, dtype=jnp.int32) * MOE_BLOCK
    block_e = jnp.minimum(jnp.sum(end_padded[None, :] <= block_start[:, None], axis=1),
                          N_EXPERTS - 1).astype(jnp.int32)
    used_lo = jnp.maximum(start_padded[None, :], block_start[:, None])
    used_hi = jnp.minimum((start_padded + counts)[None, :], block_start[:, None] + MOE_BLOCK)
    n_valid = jnp.sum(jnp.maximum(used_hi - used_lo, 0), axis=1).astype(jnp.int32)

    x_rows = _sc_scatter_rows(h2, dest_flat, n_rows)
    y_rows = _experts(block_e, n_valid, x_rows, w_e_gate, w_e_up, w_e_down)
    y_pairs = _sc_gather_rows(y_rows, dest_flat)
    out = _combine(y_pairs, x1, route, norm_final_g.reshape(1, D_MODEL))
    return out.reshape(batch, seq, D_MODEL)
```

```python
import jax
import jax.numpy as jnp
from jax import lax
from jax.experimental import pallas as pl
from jax.experimental.pallas import tpu as pltpu
from jax.experimental.pallas import tpu_sc as plsc

D_MODEL = 1024
HEAD_DIM = 64
N_HEADS = 16
N_GROUPS = 4
EXPERTS_PER_GROUP = 8
N_EXPERTS = N_GROUPS * EXPERTS_PER_GROUP
D_EXPERT = 512
MOE_BLOCK = 512
CONV_WIDTH = 3
EPS = 1e-6

LANES = 128
HEADS_PER_BLOCK = LANES // HEAD_DIM
NEG_BIG = -1e30
LOG2E = 1.4426950408889634

IN_PROJ_COLS = 256
IN_PROJ_ROWS = 256
ATT_BLOCK = 256
ATT_KEYS = 256
MIX_ROWS = 512
MIX_PART = 128
ROUTE_ROWS = 8
SC_WINDOW = 32
VMEM_LIMIT = 56 * 1024 * 1024

F32 = jnp.float32
BF16 = jnp.bfloat16


def _dot(a, b):
    return jnp.dot(a, b, preferred_element_type=F32)


def _pack_bf16_pairs(x):
    c = x.shape[1] // 2
    lo = lax.bitcast_convert_type(x[:, :c].astype(BF16).astype(F32), jnp.uint32) >> 16
    hi = lax.bitcast_convert_type(x[:, c:].astype(BF16).astype(F32), jnp.uint32) & jnp.uint32(0xFFFF0000)
    return lo | hi


def _unpack_bf16_pairs(w):
    lo = lax.bitcast_convert_type(w << 16, F32)
    hi = lax.bitcast_convert_type(w & jnp.uint32(0xFFFF0000), F32)
    return jnp.concatenate([lo, hi], axis=1)


def _in_proj_kernel(x_ref, g_ref, w_ref, wf_ref, bf_ref, cw_ref, cb_ref,
                    yc_ref, q_ref, k_ref, v_ref, sgc_ref, sga_ref, cq_ref, ck_ref, h_scr):
    j = pl.program_id(1)
    seq = x_ref.shape[0]
    n_chunks = seq // IN_PROJ_ROWS

    @pl.when(j == 0)
    def _():
        for r in range(n_chunks):
            rows = slice(r * IN_PROJ_ROWS, (r + 1) * IN_PROJ_ROWS)
            xs = x_ref[rows, :]
            ms = jnp.mean(xs * xs, axis=-1, keepdims=True)
            h_scr[rows, :] = (xs * lax.rsqrt(ms + EPS) * g_ref[...]).astype(BF16)
        f = _dot(h_scr[...], wf_ref[...]) + bf_ref[...]
        c = jnp.minimum(f, 0.0) - jnp.log(1.0 + jnp.exp(-jnp.abs(f)))
        row = lax.broadcasted_iota(jnp.int32, c.shape, 0)
        d = 1
        while d < seq:
            c = c + jnp.where(row >= d, pltpu.roll(c, d, axis=0), 0.0)
            d *= 2
        c = c * LOG2E
        hi = c.astype(BF16).astype(F32)
        mid = (c - hi).astype(BF16).astype(F32)
        lo = (c - hi - mid).astype(BF16).astype(F32)
        grp = lax.broadcasted_iota(jnp.int32, c.shape, 1) // N_HEADS
        terms = jnp.where(grp % 3 == 0, hi, jnp.where(grp % 3 == 1, mid, lo))
        one = jnp.float32(1.0)
        cq_ref[0] = jnp.where(grp < 3, terms, jnp.where(grp < 6, one, 0.0)).astype(BF16)
        ck_ref[0] = jnp.where(grp < 3, one, jnp.where(grp < 6, -terms, 0.0)).astype(BF16)

    cw0 = cw_ref[0:1, :]
    cw1 = cw_ref[1:2, :]
    cw2 = cw_ref[2:3, :]
    cb = cb_ref[...]
    rowc = lax.broadcasted_iota(jnp.int32, (IN_PROJ_ROWS, IN_PROJ_COLS), 0)
    zprev = None
    for r in range(n_chunks):
        rows = slice(r * IN_PROJ_ROWS, (r + 1) * IN_PROJ_ROWS)
        hs = h_scr[rows, :]
        cb_gate = _dot(hs, w_ref[0])
        z = _dot(hs, w_ref[1]) * _dot(hs, w_ref[2])
        z1 = pltpu.roll(z, 1, axis=0)
        z2 = pltpu.roll(z, 2, axis=0)
        if zprev is None:
            p1 = jnp.zeros_like(z)
            p2 = p1
        else:
            p1 = pltpu.roll(zprev, 1, axis=0)
            p2 = pltpu.roll(zprev, 2, axis=0)
        z1 = jnp.where(rowc < 1, p1, z1)
        z2 = jnp.where(rowc < 2, p2, z2)
        acc = cb + cw0 * z2 + cw1 * z1 + cw2 * z
        yc_ref[rows, :] = (cb_gate * acc).astype(BF16)
        zprev = z
        q_ref[rows, :] = (_dot(hs, w_ref[3]) * (LOG2E * HEAD_DIM ** -0.5)).astype(BF16)
        k_ref[rows, :] = _dot(hs, w_ref[4]).astype(BF16)
        v_ref[rows, :] = _dot(hs, w_ref[5]).astype(BF16)
        sgc_ref[rows, :] = jax.nn.sigmoid(_dot(hs, w_ref[6])).astype(BF16)
        sga_ref[rows, :] = jax.nn.sigmoid(_dot(hs, w_ref[7])).astype(BF16)


def _in_proj(x2d, g, w_stack, wf, bfv, conv_w, conv_b, batch, seq):
    tokens = batch * seq
    tn = IN_PROJ_COLS
    nj = D_MODEL // tn
    col_out = pl.BlockSpec((seq, tn), lambda b, j: (b, j))
    out_bf16 = jax.ShapeDtypeStruct((tokens, D_MODEL), BF16)
    return pl.pallas_call(
        _in_proj_kernel,
        grid=(batch, nj),
        in_specs=[
            pl.BlockSpec((seq, D_MODEL), lambda b, j: (b, 0)),
            pl.BlockSpec((1, D_MODEL), lambda b, j: (0, 0)),
            pl.BlockSpec((8, D_MODEL, tn), lambda b, j: (0, 0, j)),
            pl.BlockSpec((D_MODEL, LANES), lambda b, j: (0, 0)),
            pl.BlockSpec((1, LANES), lambda b, j: (0, 0)),
            pl.BlockSpec((CONV_WIDTH, tn), lambda b, j: (0, j)),
            pl.BlockSpec((1, tn), lambda b, j: (0, j)),
        ],
        out_specs=[col_out] * 6 + [pl.BlockSpec((1, seq, LANES), lambda b, j: (b, 0, 0))] * 2,
        out_shape=[out_bf16] * 6 + [jax.ShapeDtypeStruct((batch, seq, LANES), BF16)] * 2,
        scratch_shapes=[pltpu.VMEM((seq, D_MODEL), BF16)],
        compiler_params=pltpu.CompilerParams(
            dimension_semantics=("arbitrary", "arbitrary"), vmem_limit_bytes=VMEM_LIMIT),
        name="in_proj",
    )(x2d, g, w_stack, wf, bfv, conv_w, conv_b)


def _attention_kernel(q_ref, k_ref, v_ref, cq_ref, ck_ref, o_ref, qa_scr, ka_scr, va_scr):
    hp = pl.program_id(1)
    seq = q_ref.shape[0]
    blk = ATT_BLOCK
    lane = lax.broadcasted_iota(jnp.int32, (1, LANES), 1)
    zero = jnp.zeros((), BF16)

    own, base = [], []
    for hh in range(HEADS_PER_BLOCK):
        own.append((lane // HEAD_DIM) == hh)
        base.append(((hh + 1) % HEADS_PER_BLOCK) * HEAD_DIM)
        gate = (lane % N_HEADS) == hp * HEADS_PER_BLOCK + hh
        qa_scr[hh, :, 0:LANES] = jnp.where(own[hh], q_ref[...], zero)
        qa_scr[hh, :, LANES:2 * LANES] = jnp.where(gate, cq_ref[0], zero)
        ka_scr[hh, :, 0:LANES] = jnp.where(own[hh], k_ref[...], zero)
        ka_scr[hh, :, LANES:2 * LANES] = jnp.where(gate, ck_ref[0], zero)
        va_scr[hh] = jnp.where(own[hh], v_ref[...],
                               jnp.where(lane == base[hh], 1.0, 0.0).astype(BF16))

    for rg in range(seq // blk):
        rows = slice(rg * blk, (rg + 1) * blk)
        out = None
        for hh in range(HEADS_PER_BLOCK):
            qa = qa_scr[hh, rows, :]
            m = acc = None
            k_end = (rg + 1) * blk
            for k0 in range(0, k_end, ATT_KEYS):
                k1 = min(k0 + ATT_KEYS, k_end)
                keys = slice(k0, k1)
                s = lax.dot_general(qa, ka_scr[hh, keys, :], (((1,), (1,)), ((), ())),
                                    preferred_element_type=F32)
                if k1 > rg * blk:
                    qpos = rg * blk + lax.broadcasted_iota(jnp.int32, s.shape, 0)
                    kpos = k0 + lax.broadcasted_iota(jnp.int32, s.shape, 1)
                    s = jnp.where(kpos <= qpos, s, NEG_BIG)
                mx = jnp.max(s, axis=1, keepdims=True)
                if m is None:
                    m = mx
                    acc = _dot(jnp.exp2(s - m).astype(BF16), va_scr[hh, keys, :])
                else:
                    m_new = jnp.maximum(m, mx)
                    acc = jnp.exp2(m - m_new) * acc + _dot(jnp.exp2(s - m_new).astype(BF16),
                                                           va_scr[hh, keys, :])
                    m = m_new
            res = acc / acc[:, base[hh]:base[hh] + 1]
            out = res if out is None else jnp.where(own[hh], res, out)
        o_ref[rows, :] = out.astype(o_ref.dtype)


def _attention(q, k, v, cq, ck, batch, seq):
    tokens = batch * seq
    n_hp = N_HEADS // HEADS_PER_BLOCK
    blk = pl.BlockSpec((seq, LANES), lambda b, hp: (b, hp))
    gate_blk = pl.BlockSpec((1, seq, LANES), lambda b, hp: (b, 0, 0))
    wide = pltpu.VMEM((HEADS_PER_BLOCK, seq, 2 * LANES), BF16)
    return pl.pallas_call(
        _attention_kernel,
        grid=(batch, n_hp),
        in_specs=[blk, blk, blk, gate_blk, gate_blk],
        out_specs=blk,
        out_shape=jax.ShapeDtypeStruct((tokens, D_MODEL), BF16),
        scratch_shapes=[wide, wide, pltpu.VMEM((HEADS_PER_BLOCK, seq, LANES), BF16)],
        compiler_params=pltpu.CompilerParams(
            dimension_semantics=("arbitrary", "arbitrary"), vmem_limit_bytes=VMEM_LIMIT),
        name="fox_attention",
    )(q, k, v, cq, ck)


def _mixer_out_kernel(yc_ref, o_ref, sgc_ref, sga_ref, x_ref, wco_ref, wao_ref, wo_ref,
                      g_ref, wr_ref, x1_ref, h2_ref, route_ref, route_t_ref, counts_ref,
                      carry_scr, logit_scr):
    i = pl.program_id(0)
    tm = x_ref.shape[0]

    @pl.when(i == 0)
    def _():
        carry_scr[...] = jnp.zeros_like(carry_scr)
        logit_scr[...] = jnp.zeros_like(logit_scr)

    logits = logit_scr[...]

    def route_previous_tile():
        lane = lax.broadcasted_iota(jnp.int32, (tm, LANES), 1).astype(F32)
        far = jnp.float32(4 * LANES)

        def first_lane_of_max(vals, vmax):
            return jnp.min(jnp.where(vals == vmax, lane, far), axis=1, keepdims=True)

        lg = jnp.where(lane < N_GROUPS, logits, NEG_BIG)
        gmax = jnp.max(lg, axis=1, keepdims=True)
        g_val = 1.0 / jnp.sum(jnp.exp(lg - gmax), axis=1, keepdims=True)
        g_idx = first_lane_of_max(lg, gmax)
        lo = N_GROUPS + EXPERTS_PER_GROUP * g_idx
        le = jnp.where((lane >= lo) & (lane < lo + EXPERTS_PER_GROUP), logits, NEG_BIG)
        e1max = jnp.max(le, axis=1, keepdims=True)
        e1 = first_lane_of_max(le, e1max)
        le2 = jnp.where(lane == e1, NEG_BIG, le)
        e2max = jnp.max(le2, axis=1, keepdims=True)
        e2 = first_lane_of_max(le2, e2max)
        ratio = jnp.exp(e2max - e1max)
        w1 = g_val / (1.0 + ratio)
        w2 = g_val * ratio / (1.0 + ratio)

        oh1 = lane == e1
        oh2 = lane == e2
        onehot = jnp.where(oh1 | oh2, 1.0, 0.0).astype(BF16)
        r_i = lax.broadcasted_iota(jnp.int32, (tm, tm), 0)
        c_i = lax.broadcasted_iota(jnp.int32, (tm, tm), 1)
        strict_lower = jnp.where(c_i < r_i, 1.0, 0.0).astype(BF16)
        carry = carry_scr[...]
        before = _dot(strict_lower, onehot) + carry
        rank1 = jnp.sum(jnp.where(oh1, before, 0.0), axis=1, keepdims=True)
        rank2 = jnp.sum(jnp.where(oh2, before, 0.0), axis=1, keepdims=True)
        is_tile = jnp.where(i > 0, 1.0, 0.0)
        carry = carry + is_tile * jnp.sum(onehot.astype(F32), axis=0, keepdims=True)
        carry_scr[...] = carry
        counts_ref[...] = carry

        route = jnp.where(lane == 0, e1 - N_GROUPS, 0.0)
        route = jnp.where(lane == 1, e2 - N_GROUPS, route)
        route = jnp.where(lane == 2, w1, route)
        route = jnp.where(lane == 3, w2, route)
        route = jnp.where(lane == 4, rank1, route)
        route = jnp.where(lane == 5, rank2, route)
        route_ref[...] = route
        route_t_ref[...] = route.T[0:ROUTE_ROWS, :]

    parts = [slice(c * MIX_PART, (c + 1) * MIX_PART) for c in range(tm // MIX_PART)]
    y_conv = [_dot(yc_ref[p, :], wco_ref[...]) for p in parts]
    y_att = [_dot(o_ref[p, :], wao_ref[...]) for p in parts]
    m = [(sgc_ref[p, :].astype(F32) * yc + sga_ref[p, :].astype(F32) * ya).astype(BF16)
         for p, yc, ya in zip(parts, y_conv, y_att)]
    x1 = [x_ref[p, :] + _dot(mm, wo_ref[...]) for p, mm in zip(parts, m)]
    for p, xx in zip(parts, x1):
        x1_ref[p, :] = xx
    route_previous_tile()
    h2 = [xx * lax.rsqrt(jnp.mean(xx * xx, axis=-1, keepdims=True) + EPS) * g_ref[...] for xx in x1]
    for p, hh in zip(parts, h2):
        h2_ref[p, :] = _pack_bf16_pairs(hh)
        logit_scr[p, :] = _dot(hh.astype(BF16), wr_ref[...])


def _mixer_out(yc, o, sgc, sga, x2d, wco, wao, wo, g, wr):
    tokens = x2d.shape[0]
    tm = MIX_ROWS
    n_tiles = tokens // tm

    def cur(i):
        return (jnp.minimum(i, n_tiles - 1), 0)

    def prev(i):
        return (jnp.maximum(i - 1, 0), 0)

    row_blk = pl.BlockSpec((tm, D_MODEL), cur)
    w_blk = pl.BlockSpec((D_MODEL, D_MODEL), lambda i: (0, 0))
    return pl.pallas_call(
        _mixer_out_kernel,
        grid=(n_tiles + 1,),
        in_specs=[row_blk, row_blk, row_blk, row_blk, row_blk, w_blk, w_blk, w_blk,
                  pl.BlockSpec((1, D_MODEL), lambda i: (0, 0)),
                  pl.BlockSpec((D_MODEL, LANES), lambda i: (0, 0))],
        out_specs=[row_blk,
                   pl.BlockSpec((tm, D_MODEL // 2), cur),
                   pl.BlockSpec((tm, LANES), prev),
                   pl.BlockSpec((ROUTE_ROWS, tm), lambda i: (0, jnp.maximum(i - 1, 0))),
                   pl.BlockSpec((1, LANES), lambda i: (0, 0))],
        out_shape=[jax.ShapeDtypeStruct((tokens, D_MODEL), F32),
                   jax.ShapeDtypeStruct((tokens, D_MODEL // 2), jnp.uint32),
                   jax.ShapeDtypeStruct((tokens, LANES), F32),
                   jax.ShapeDtypeStruct((ROUTE_ROWS, tokens), F32),
                   jax.ShapeDtypeStruct((1, LANES), F32)],
        scratch_shapes=[pltpu.VMEM((1, LANES), F32), pltpu.VMEM((tm, LANES), F32)],
        compiler_params=pltpu.CompilerParams(
            dimension_semantics=("arbitrary",), vmem_limit_bytes=VMEM_LIMIT),
        name="mixer_out",
    )(yc, o, sgc, sga, x2d, wco, wao, wo, g, wr)


def _sc_index_rows(indices):
    n = indices.shape[0]
    return jnp.pad(indices.reshape(n // SC_WINDOW, SC_WINDOW), ((0, 0), (0, LANES - SC_WINDOW)))


def _sc_mesh():
    return plsc.VectorSubcoreMesh(core_axis_name="c", subcore_axis_name="s")


def _sc_scatter_rows(data, indices, n_out):
    n = indices.shape[0]
    n_src, width = data.shape
    src_blocks = n_src // SC_WINDOW

    @pl.kernel(out_type=jax.ShapeDtypeStruct((n_out, width), data.dtype), mesh=_sc_mesh())
    def scatter_kernel(x_hbm, i_hbm, o_hbm):
        def body(x_vmem, i_vmem):
            pltpu.sync_copy(x_vmem, o_hbm.at[i_vmem.at[0, pl.ds(0, SC_WINDOW)]])

        pltpu.emit_pipeline(
            body,
            grid=(n // SC_WINDOW,),
            in_specs=[pl.BlockSpec((SC_WINDOW, width), lambda i: (i % src_blocks, 0)),
                      pl.BlockSpec((1, LANES), lambda i: (i, 0))],
            out_specs=[],
            core_axis_name=("c", "s"),
            dimension_semantics=(pltpu.PARALLEL,),
        )(x_hbm, i_hbm)

    return scatter_kernel(data, _sc_index_rows(indices))


def _sc_gather_rows(data, indices):
    n = indices.shape[0]
    width = data.shape[1]

    @pl.kernel(out_type=jax.ShapeDtypeStruct((n, width), data.dtype), mesh=_sc_mesh())
    def gather_kernel(x_hbm, i_hbm, o_hbm):
        def body(i_vmem, o_vmem):
            pltpu.sync_copy(x_hbm.at[i_vmem.at[0, pl.ds(0, SC_WINDOW)]], o_vmem)

        pltpu.emit_pipeline(
            body,
            grid=(n // SC_WINDOW,),
            in_specs=[pl.BlockSpec((1, LANES), lambda i: (i, 0))],
            out_specs=[pl.BlockSpec((SC_WINDOW, width), lambda i: (i, 0))],
            core_axis_name=("c", "s"),
            dimension_semantics=(pltpu.PARALLEL,),
        )(i_hbm, o_hbm)

    return gather_kernel(data, _sc_index_rows(indices))


def _expert_kernel(block_e_ref, n_valid_ref, x_ref, wg_ref, wu_ref, wd_ref, y_ref,
                   wg_scr, wu_scr, wd_scr):
    i = pl.program_id(0)
    n_valid = n_valid_ref[i]

    @pl.when((i == 0) | (block_e_ref[i] != block_e_ref[jnp.maximum(i - 1, 0)]))
    def _():
        wg_scr[...] = wg_ref[0].astype(BF16)
        wu_scr[...] = wu_ref[0].astype(BF16)
        wd_scr[...] = wd_ref[0].astype(BF16)

    @pl.when(n_valid > 0)
    def _():
        row = lax.broadcasted_iota(jnp.int32, x_ref.shape, 0)
        packed = jnp.where(row < n_valid, x_ref[...], jnp.uint32(0))
        xb = _unpack_bf16_pairs(packed).astype(BF16)
        a = _dot(xb, wg_scr[...])
        u = _dot(xb, wu_scr[...])
        hmid = (a * jax.nn.sigmoid(a) * u).astype(BF16)
        y_ref[...] = _pack_bf16_pairs(_dot(hmid, wd_scr[...]))

    @pl.when(n_valid == 0)
    def _():
        y_ref[...] = jnp.zeros_like(y_ref)


def _experts(block_e, n_valid, x_rows, wg, wu, wd):
    n_rows, half = x_rows.shape
    n_blocks = n_rows // MOE_BLOCK

    def w_map(i, be, nv):
        return (be[i], 0, 0)

    row_blk = pl.BlockSpec((MOE_BLOCK, half), lambda i, be, nv: (i, 0))
    grid_spec = pltpu.PrefetchScalarGridSpec(
        num_scalar_prefetch=2,
        grid=(n_blocks,),
        in_specs=[row_blk,
                  pl.BlockSpec((1, D_MODEL, D_EXPERT), w_map),
                  pl.BlockSpec((1, D_MODEL, D_EXPERT), w_map),
                  pl.BlockSpec((1, D_EXPERT, D_MODEL), w_map)],
        out_specs=row_blk,
        scratch_shapes=[pltpu.VMEM((D_MODEL, D_EXPERT), BF16), pltpu.VMEM((D_MODEL, D_EXPERT), BF16),
                        pltpu.VMEM((D_EXPERT, D_MODEL), BF16)],
    )
    return pl.pallas_call(
        _expert_kernel,
        grid_spec=grid_spec,
        out_shape=jax.ShapeDtypeStruct((n_rows, half), jnp.uint32),
        compiler_params=pltpu.CompilerParams(
            dimension_semantics=("arbitrary",), vmem_limit_bytes=VMEM_LIMIT),
        name="moe_experts",
    )(block_e, n_valid, x_rows, wg, wu, wd)


def _combine_kernel(ya_ref, yb_ref, x1_ref, route_ref, g_ref, out_ref):
    route = route_ref[...]
    w1 = route[:, 2:3]
    w2 = route[:, 3:4]
    x2 = x1_ref[...] + (_unpack_bf16_pairs(ya_ref[...]) * w1 + _unpack_bf16_pairs(yb_ref[...]) * w2)
    ms = jnp.mean(x2 * x2, axis=-1, keepdims=True)
    out_ref[...] = x2 * lax.rsqrt(ms + EPS) * g_ref[...]


def _combine(y_pairs, x1, route, g):
    tokens = x1.shape[0]
    tc = MIX_ROWS
    n_tiles = tokens // tc
    row_blk = pl.BlockSpec((tc, D_MODEL), lambda i: (i, 0))
    return pl.pallas_call(
        _combine_kernel,
        grid=(n_tiles,),
        in_specs=[pl.BlockSpec((tc, D_MODEL // 2), lambda i: (i, 0)),
                  pl.BlockSpec((tc, D_MODEL // 2), lambda i: (i + n_tiles, 0)),
                  row_blk,
                  pl.BlockSpec((tc, LANES), lambda i: (i, 0)),
                  pl.BlockSpec((1, D_MODEL), lambda i: (0, 0))],
        out_specs=row_blk,
        out_shape=jax.ShapeDtypeStruct((tokens, D_MODEL), F32),
        compiler_params=pltpu.CompilerParams(
            dimension_semantics=("arbitrary",), vmem_limit_bytes=VMEM_LIMIT),
        name="moe_combine",
    )(y_pairs, y_pairs, x1, route, g)


def kernel(x, norm_mix_g, w_in, conv_w, conv_b, b_forget, w_conv_out, w_att_out, w_out,
           norm_ffn_g, w_router_group, w_router_expert, w_e_gate, w_e_up, w_e_down,
           norm_final_g):
    batch, seq, d = x.shape
    assert d == D_MODEL and seq % max(ATT_BLOCK, IN_PROJ_ROWS) == 0
    tokens = batch * seq
    assert tokens % MIX_ROWS == 0 and tokens % SC_WINDOW == 0
    x2d = x.reshape(tokens, D_MODEL)

    wb = w_in.astype(BF16)
    n_main = 6 * D_MODEL
    gate0 = n_main + N_HEADS
    pieces = [wb[:, i * D_MODEL:(i + 1) * D_MODEL] for i in range(6)]
    pieces += [wb[:, gate0:gate0 + D_MODEL], wb[:, gate0 + D_MODEL:gate0 + 2 * D_MODEL]]
    w_stack = jnp.stack(pieces, axis=0)
    n_rep = 6
    wf = jnp.pad(jnp.tile(wb[:, n_main:gate0], (1, n_rep)), ((0, 0), (0, LANES - n_rep * N_HEADS)))
    bfv = jnp.pad(jnp.tile(b_forget.astype(F32), n_rep), (0, LANES - n_rep * N_HEADS)).reshape(1, LANES)
    wr = jnp.concatenate(
        [w_router_group, jnp.transpose(w_router_expert, (1, 0, 2)).reshape(D_MODEL, N_EXPERTS)], axis=1)
    wr = jnp.pad(wr, ((0, 0), (0, LANES - wr.shape[1]))).astype(BF16)

    yc, q, k, v, sgc, sga, cq, ck = _in_proj(
        x2d, norm_mix_g.reshape(1, D_MODEL), w_stack, wf, bfv, conv_w,
        conv_b.reshape(1, D_MODEL), batch, seq)
    o = _attention(q, k, v, cq, ck, batch, seq)

    x1, h2, route, route_t, counts = _mixer_out(
        yc, o, sgc, sga, x2d, w_conv_out.astype(BF16), w_att_out.astype(BF16),
        w_out.astype(BF16), norm_ffn_g.reshape(1, D_MODEL), wr)

    counts = counts[0, N_GROUPS:N_GROUPS + N_EXPERTS].astype(jnp.int32)
    padded = (counts + MOE_BLOCK - 1) // MOE_BLOCK * MOE_BLOCK
    end_padded = jnp.cumsum(padded)
    start_padded = end_padded - padded
    expert = route_t[0:2].astype(jnp.int32)
    rank = route_t[4:6].astype(jnp.int32)
    expert_ids = jnp.arange(N_EXPERTS, dtype=jnp.int32)[:, None, None]
    dest = rank + jnp.sum(jnp.where(expert[None] == expert_ids, start_padded[:, None, None], 0), axis=0)
    dest_flat = dest.reshape(2 * tokens)
    n_rows = tokens * 2 + N_EXPERTS * MOE_BLOCK
    block_start = jnp.arange(n_rows // MOE_BLOCK, dtype=jnp.int32) * MOE_BLOCK
    block_e = jnp.minimum(jnp.sum(end_padded[None, :] <= block_start[:, None], axis=1),
                          N_EXPERTS - 1).astype(jnp.int32)
    used_lo = jnp.maximum(start_padded[None, :], block_start[:, None])
    used_hi = jnp.minimum((start_padded + counts)[None, :], block_start[:, None] + MOE_BLOCK)
    n_valid = jnp.sum(jnp.maximum(used_hi - used_lo, 0), axis=1).astype(jnp.int32)

    x_rows = _sc_scatter_rows(h2, dest_flat, n_rows)
    y_rows = _experts(block_e, n_valid, x_rows, w_e_gate, w_e_up, w_e_down)
    y_pairs = _sc_gather_rows(y_rows, dest_flat)
    out = _combine(y_pairs, x1, route, norm_final_g.reshape(1, D_MODEL))
    return out.reshape(batch, seq, D_MODEL)
```

```python
import jax
import jax.numpy as jnp
from jax import lax
from jax.experimental import pallas as pl
from jax.experimental.pallas import tpu as pltpu
from jax.experimental.pallas import tpu_sc as plsc

D_MODEL = 1024
HEAD_DIM = 64
N_HEADS = 16
N_GROUPS = 4
EXPERTS_PER_GROUP = 8
N_EXPERTS = N_GROUPS * EXPERTS_PER_GROUP
D_EXPERT = 512
MOE_BLOCK = 512
CONV_WIDTH = 3
EPS = 1e-6

LANES = 128
HEADS_PER_BLOCK = LANES // HEAD_DIM
NEG_BIG = -1e30
LOG2E = 1.4426950408889634

IN_PROJ_COLS = 256
IN_PROJ_ROWS = 256
ATT_BLOCK = 256
ATT_KEYS = 256
MIX_ROWS = 512
MIX_PART = 128
ROUTE_ROWS = 8
SC_WINDOW = 64
VMEM_LIMIT = 56 * 1024 * 1024

F32 = jnp.float32
BF16 = jnp.bfloat16


def _dot(a, b):
    return jnp.dot(a, b, preferred_element_type=F32)


def _pack_bf16_pairs(x):
    c = x.shape[1] // 2
    lo = lax.bitcast_convert_type(x[:, :c].astype(BF16).astype(F32), jnp.uint32) >> 16
    hi = lax.bitcast_convert_type(x[:, c:].astype(BF16).astype(F32), jnp.uint32) & jnp.uint32(0xFFFF0000)
    return lo | hi


def _unpack_bf16_pairs(w):
    lo = lax.bitcast_convert_type(w << 16, F32)
    hi = lax.bitcast_convert_type(w & jnp.uint32(0xFFFF0000), F32)
    return jnp.concatenate([lo, hi], axis=1)


def _in_proj_kernel(x_ref, g_ref, w_ref, wf_ref, bf_ref, cw_ref, cb_ref,
                    yc_ref, q_ref, k_ref, v_ref, sgc_ref, sga_ref, cq_ref, ck_ref, h_scr):
    j = pl.program_id(1)
    seq = x_ref.shape[0]
    n_chunks = seq // IN_PROJ_ROWS

    @pl.when(j == 0)
    def _():
        for r in range(n_chunks):
            rows = slice(r * IN_PROJ_ROWS, (r + 1) * IN_PROJ_ROWS)
            xs = x_ref[rows, :]
            ms = jnp.mean(xs * xs, axis=-1, keepdims=True)
            h_scr[rows, :] = (xs * lax.rsqrt(ms + EPS) * g_ref[...]).astype(BF16)
        f = _dot(h_scr[...], wf_ref[...]) + bf_ref[...]
        c = jnp.minimum(f, 0.0) - jnp.log(1.0 + jnp.exp(-jnp.abs(f)))
        row = lax.broadcasted_iota(jnp.int32, c.shape, 0)
        d = 1
        while d < seq:
            c = c + jnp.where(row >= d, pltpu.roll(c, d, axis=0), 0.0)
            d *= 2
        c = c * LOG2E
        hi = c.astype(BF16).astype(F32)
        mid = (c - hi).astype(BF16).astype(F32)
        lo = (c - hi - mid).astype(BF16).astype(F32)
        grp = lax.broadcasted_iota(jnp.int32, c.shape, 1) // N_HEADS
        terms = jnp.where(grp % 3 == 0, hi, jnp.where(grp % 3 == 1, mid, lo))
        one = jnp.float32(1.0)
        cq_ref[0] = jnp.where(grp < 3, terms, jnp.where(grp < 6, one, 0.0)).astype(BF16)
        ck_ref[0] = jnp.where(grp < 3, one, jnp.where(grp < 6, -terms, 0.0)).astype(BF16)

    cw0 = cw_ref[0:1, :]
    cw1 = cw_ref[1:2, :]
    cw2 = cw_ref[2:3, :]
    cb = cb_ref[...]
    rowc = lax.broadcasted_iota(jnp.int32, (IN_PROJ_ROWS, IN_PROJ_COLS), 0)
    zprev = None
    for r in range(n_chunks):
        rows = slice(r * IN_PROJ_ROWS, (r + 1) * IN_PROJ_ROWS)
        hs = h_scr[rows, :]
        cb_gate = _dot(hs, w_ref[0])
        z = _dot(hs, w_ref[1]) * _dot(hs, w_ref[2])
        z1 = pltpu.roll(z, 1, axis=0)
        z2 = pltpu.roll(z, 2, axis=0)
        if zprev is None:
            p1 = jnp.zeros_like(z)
            p2 = p1
        else:
            p1 = pltpu.roll(zprev, 1, axis=0)
            p2 = pltpu.roll(zprev, 2, axis=0)
        z1 = jnp.where(rowc < 1, p1, z1)
        z2 = jnp.where(rowc < 2, p2, z2)
        acc = cb + cw0 * z2 + cw1 * z1 + cw2 * z
        yc_ref[rows, :] = (cb_gate * acc).astype(BF16)
        zprev = z
        q_ref[rows, :] = (_dot(hs, w_ref[3]) * (LOG2E * HEAD_DIM ** -0.5)).astype(BF16)
        k_ref[rows, :] = _dot(hs, w_ref[4]).astype(BF16)
        v_ref[rows, :] = _dot(hs, w_ref[5]).astype(BF16)
        sgc_ref[rows, :] = jax.nn.sigmoid(_dot(hs, w_ref[6])).astype(BF16)
        sga_ref[rows, :] = jax.nn.sigmoid(_dot(hs, w_ref[7])).astype(BF16)


def _in_proj(x2d, g, w_stack, wf, bfv, conv_w, conv_b, batch, seq):
    tokens = batch * seq
    tn = IN_PROJ_COLS
    nj = D_MODEL // tn
    col_out = pl.BlockSpec((seq, tn), lambda b, j: (b, j))
    out_bf16 = jax.ShapeDtypeStruct((tokens, D_MODEL), BF16)
    return pl.pallas_call(
        _in_proj_kernel,
        grid=(batch, nj),
        in_specs=[
            pl.BlockSpec((seq, D_MODEL), lambda b, j: (b, 0)),
            pl.BlockSpec((1, D_MODEL), lambda b, j: (0, 0)),
            pl.BlockSpec((8, D_MODEL, tn), lambda b, j: (0, 0, j)),
            pl.BlockSpec((D_MODEL, LANES), lambda b, j: (0, 0)),
            pl.BlockSpec((1, LANES), lambda b, j: (0, 0)),
            pl.BlockSpec((CONV_WIDTH, tn), lambda b, j: (0, j)),
            pl.BlockSpec((1, tn), lambda b, j: (0, j)),
        ],
        out_specs=[col_out] * 6 + [pl.BlockSpec((1, seq, LANES), lambda b, j: (b, 0, 0))] * 2,
        out_shape=[out_bf16] * 6 + [jax.ShapeDtypeStruct((batch, seq, LANES), BF16)] * 2,
        scratch_shapes=[pltpu.VMEM((seq, D_MODEL), BF16)],
        compiler_params=pltpu.CompilerParams(
            dimension_semantics=("arbitrary", "arbitrary"), vmem_limit_bytes=VMEM_LIMIT),
        name="in_proj",
    )(x2d, g, w_stack, wf, bfv, conv_w, conv_b)


def _attention_kernel(q_ref, k_ref, v_ref, cq_ref, ck_ref, o_ref, qa_scr, ka_scr, va_scr):
    hp = pl.program_id(1)
    seq = q_ref.shape[0]
    blk = ATT_BLOCK
    lane = lax.broadcasted_iota(jnp.int32, (1, LANES), 1)
    zero = jnp.zeros((), BF16)

    own, base = [], []
    for hh in range(HEADS_PER_BLOCK):
        own.append((lane // HEAD_DIM) == hh)
        base.append(((hh + 1) % HEADS_PER_BLOCK) * HEAD_DIM)
        gate = (lane % N_HEADS) == hp * HEADS_PER_BLOCK + hh
        qa_scr[hh, :, 0:LANES] = jnp.where(own[hh], q_ref[...], zero)
        qa_scr[hh, :, LANES:2 * LANES] = jnp.where(gate, cq_ref[0], zero)
        ka_scr[hh, :, 0:LANES] = jnp.where(own[hh], k_ref[...], zero)
        ka_scr[hh, :, LANES:2 * LANES] = jnp.where(gate, ck_ref[0], zero)
        va_scr[hh] = jnp.where(own[hh], v_ref[...],
                               jnp.where(lane == base[hh], 1.0, 0.0).astype(BF16))

    for rg in range(seq // blk):
        rows = slice(rg * blk, (rg + 1) * blk)
        out = None
        for hh in range(HEADS_PER_BLOCK):
            qa = qa_scr[hh, rows, :]
            m = acc = None
            k_end = (rg + 1) * blk
            for k0 in range(0, k_end, ATT_KEYS):
                k1 = min(k0 + ATT_KEYS, k_end)
                keys = slice(k0, k1)
                s = lax.dot_general(qa, ka_scr[hh, keys, :], (((1,), (1,)), ((), ())),
                                    preferred_element_type=F32)
                if k1 > rg * blk:
                    qpos = rg * blk + lax.broadcasted_iota(jnp.int32, s.shape, 0)
                    kpos = k0 + lax.broadcasted_iota(jnp.int32, s.shape, 1)
                    s = jnp.where(kpos <= qpos, s, NEG_BIG)
                mx = jnp.max(s, axis=1, keepdims=True)
                if m is None:
                    m = mx
                    acc = _dot(jnp.exp2(s - m).astype(BF16), va_scr[hh, keys, :])
                else:
                    m_new = jnp.maximum(m, mx)
                    acc = jnp.exp2(m - m_new) * acc + _dot(jnp.exp2(s - m_new).astype(BF16),
                                                           va_scr[hh, keys, :])
                    m = m_new
            res = acc / acc[:, base[hh]:base[hh] + 1]
            out = res if out is None else jnp.where(own[hh], res, out)
        o_ref[rows, :] = out.astype(o_ref.dtype)


def _attention(q, k, v, cq, ck, batch, seq):
    tokens = batch * seq
    n_hp = N_HEADS // HEADS_PER_BLOCK
    blk = pl.BlockSpec((seq, LANES), lambda b, hp: (b, hp))
    gate_blk = pl.BlockSpec((1, seq, LANES), lambda b, hp: (b, 0, 0))
    wide = pltpu.VMEM((HEADS_PER_BLOCK, seq, 2 * LANES), BF16)
    return pl.pallas_call(
        _attention_kernel,
        grid=(batch, n_hp),
        in_specs=[blk, blk, blk, gate_blk, gate_blk],
        out_specs=blk,
        out_shape=jax.ShapeDtypeStruct((tokens, D_MODEL), BF16),
        scratch_shapes=[wide, wide, pltpu.VMEM((HEADS_PER_BLOCK, seq, LANES), BF16)],
        compiler_params=pltpu.CompilerParams(
            dimension_semantics=("arbitrary", "arbitrary"), vmem_limit_bytes=VMEM_LIMIT),
        name="fox_attention",
    )(q, k, v, cq, ck)


def _mixer_out_kernel(yc_ref, o_ref, sgc_ref, sga_ref, x_ref, wco_ref, wao_ref, wo_ref,
                      g_ref, wr_ref, x1_ref, h2_ref, route_ref, route_t_ref, counts_ref,
                      carry_scr, logit_scr):
    i = pl.program_id(0)
    tm = x_ref.shape[0]

    @pl.when(i == 0)
    def _():
        carry_scr[...] = jnp.zeros_like(carry_scr)
        logit_scr[...] = jnp.zeros_like(logit_scr)

    logits = logit_scr[...]

    def route_previous_tile():
        lane = lax.broadcasted_iota(jnp.int32, (tm, LANES), 1).astype(F32)
        far = jnp.float32(4 * LANES)

        def first_lane_of_max(vals, vmax):
            return jnp.min(jnp.where(vals == vmax, lane, far), axis=1, keepdims=True)

        lg = jnp.where(lane < N_GROUPS, logits, NEG_BIG)
        gmax = jnp.max(lg, axis=1, keepdims=True)
        g_val = 1.0 / jnp.sum(jnp.exp(lg - gmax), axis=1, keepdims=True)
        g_idx = first_lane_of_max(lg, gmax)
        lo = N_GROUPS + EXPERTS_PER_GROUP * g_idx
        le = jnp.where((lane >= lo) & (lane < lo + EXPERTS_PER_GROUP), logits, NEG_BIG)
        e1max = jnp.max(le, axis=1, keepdims=True)
        e1 = first_lane_of_max(le, e1max)
        le2 = jnp.where(lane == e1, NEG_BIG, le)
        e2max = jnp.max(le2, axis=1, keepdims=True)
        e2 = first_lane_of_max(le2, e2max)
        ratio = jnp.exp(e2max - e1max)
        w1 = g_val / (1.0 + ratio)
        w2 = g_val * ratio / (1.0 + ratio)

        oh1 = lane == e1
        oh2 = lane == e2
        onehot = jnp.where(oh1 | oh2, 1.0, 0.0).astype(BF16)
        r_i = lax.broadcasted_iota(jnp.int32, (tm, tm), 0)
        c_i = lax.broadcasted_iota(jnp.int32, (tm, tm), 1)
        strict_lower = jnp.where(c_i < r_i, 1.0, 0.0).astype(BF16)
        carry = carry_scr[...]
        before = _dot(strict_lower, onehot) + carry
        rank1 = jnp.sum(jnp.where(oh1, before, 0.0), axis=1, keepdims=True)
        rank2 = jnp.sum(jnp.where(oh2, before, 0.0), axis=1, keepdims=True)
        is_tile = jnp.where(i > 0, 1.0, 0.0)
        carry = carry + is_tile * jnp.sum(onehot.astype(F32), axis=0, keepdims=True)
        carry_scr[...] = carry
        counts_ref[...] = carry

        route = jnp.where(lane == 0, e1 - N_GROUPS, 0.0)
        route = jnp.where(lane == 1, e2 - N_GROUPS, route)
        route = jnp.where(lane == 2, w1, route)
        route = jnp.where(lane == 3, w2, route)
        route = jnp.where(lane == 4, rank1, route)
        route = jnp.where(lane == 5, rank2, route)
        route_ref[...] = route
        route_t_ref[...] = route.T[0:ROUTE_ROWS, :]

    parts = [slice(c * MIX_PART, (c + 1) * MIX_PART) for c in range(tm // MIX_PART)]
    y_conv = [_dot(yc_ref[p, :], wco_ref[...]) for p in parts]
    y_att = [_dot(o_ref[p, :], wao_ref[...]) for p in parts]
    m = [(sgc_ref[p, :].astype(F32) * yc + sga_ref[p, :].astype(F32) * ya).astype(BF16)
         for p, yc, ya in zip(parts, y_conv, y_att)]
    x1 = [x_ref[p, :] + _dot(mm, wo_ref[...]) for p, mm in zip(parts, m)]
    for p, xx in zip(parts, x1):
        x1_ref[p, :] = xx
    route_previous_tile()
    h2 = [xx * lax.rsqrt(jnp.mean(xx * xx, axis=-1, keepdims=True) + EPS) * g_ref[...] for xx in x1]
    for p, hh in zip(parts, h2):
        h2_ref[p, :] = _pack_bf16_pairs(hh)
        logit_scr[p, :] = _dot(hh.astype(BF16), wr_ref[...])


def _mixer_out(yc, o, sgc, sga, x2d, wco, wao, wo, g, wr):
    tokens = x2d.shape[0]
    tm = MIX_ROWS
    n_tiles = tokens // tm

    def cur(i):
        return (jnp.minimum(i, n_tiles - 1), 0)

    def prev(i):
        return (jnp.maximum(i - 1, 0), 0)

    row_blk = pl.BlockSpec((tm, D_MODEL), cur)
    w_blk = pl.BlockSpec((D_MODEL, D_MODEL), lambda i: (0, 0))
    return pl.pallas_call(
        _mixer_out_kernel,
        grid=(n_tiles + 1,),
        in_specs=[row_blk, row_blk, row_blk, row_blk, row_blk, w_blk, w_blk, w_blk,
                  pl.BlockSpec((1, D_MODEL), lambda i: (0, 0)),
                  pl.BlockSpec((D_MODEL, LANES), lambda i: (0, 0))],
        out_specs=[row_blk,
                   pl.BlockSpec((tm, D_MODEL // 2), cur),
                   pl.BlockSpec((tm, LANES), prev),
                   pl.BlockSpec((ROUTE_ROWS, tm), lambda i: (0, jnp.maximum(i - 1, 0))),
                   pl.BlockSpec((1, LANES), lambda i: (0, 0))],
        out_shape=[jax.ShapeDtypeStruct((tokens, D_MODEL), F32),
                   jax.ShapeDtypeStruct((tokens, D_MODEL // 2), jnp.uint32),
                   jax.ShapeDtypeStruct((tokens, LANES), F32),
                   jax.ShapeDtypeStruct((ROUTE_ROWS, tokens), F32),
                   jax.ShapeDtypeStruct((1, LANES), F32)],
        scratch_shapes=[pltpu.VMEM((1, LANES), F32), pltpu.VMEM((tm, LANES), F32)],
        compiler_params=pltpu.CompilerParams(
            dimension_semantics=("arbitrary",), vmem_limit_bytes=VMEM_LIMIT),
        name="mixer_out",
    )(yc, o, sgc, sga, x2d, wco, wao, wo, g, wr)


def _sc_index_rows(indices):
    n = indices.shape[0]
    return jnp.pad(indices.reshape(n // SC_WINDOW, SC_WINDOW), ((0, 0), (0, LANES - SC_WINDOW)))


def _sc_mesh():
    return plsc.VectorSubcoreMesh(core_axis_name="c", subcore_axis_name="s")


def _sc_scatter_rows(data, indices, n_out):
    n = indices.shape[0]
    n_src, width = data.shape
    src_blocks = n_src // SC_WINDOW

    @pl.kernel(out_type=jax.ShapeDtypeStruct((n_out, width), data.dtype), mesh=_sc_mesh())
    def scatter_kernel(x_hbm, i_hbm, o_hbm):
        def body(x_vmem, i_vmem):
            pltpu.sync_copy(x_vmem, o_hbm.at[i_vmem.at[0, pl.ds(0, SC_WINDOW)]])

        pltpu.emit_pipeline(
            body,
            grid=(n // SC_WINDOW,),
            in_specs=[pl.BlockSpec((SC_WINDOW, width), lambda i: (i % src_blocks, 0)),
                      pl.BlockSpec((1, LANES), lambda i: (i, 0))],
            out_specs=[],
            core_axis_name=("c", "s"),
            dimension_semantics=(pltpu.PARALLEL,),
        )(x_hbm, i_hbm)

    return scatter_kernel(data, _sc_index_rows(indices))


def _sc_gather_rows(data, indices):
    n = indices.shape[0]
    width = data.shape[1]

    @pl.kernel(out_type=jax.ShapeDtypeStruct((n, width), data.dtype), mesh=_sc_mesh())
    def gather_kernel(x_hbm, i_hbm, o_hbm):
        def body(i_vmem, o_vmem):
            pltpu.sync_copy(x_hbm.at[i_vmem.at[0, pl.ds(0, SC_WINDOW)]], o_vmem)

        pltpu.emit_pipeline(
            body,
            grid=(n // SC_WINDOW,),
            in_specs=[pl.BlockSpec((1, LANES), lambda i: (i, 0))],
            out_specs=[pl.BlockSpec((SC_WINDOW, width), lambda i: (i, 0))],
            core_axis_name=("c", "s"),
            dimension_semantics=(pltpu.PARALLEL,),
        )(i_hbm, o_hbm)

    return gather_kernel(data, _sc_index_rows(indices))


def _expert_kernel(block_e_ref, n_valid_ref, x_ref, wg_ref, wu_ref, wd_ref, y_ref,
                   wg_scr, wu_scr, wd_scr):
    i = pl.program_id(0)
    n_valid = n_valid_ref[i]

    @pl.when((i == 0) | (block_e_ref[i] != block_e_ref[jnp.maximum(i - 1, 0)]))
    def _():
        wg_scr[...] = wg_ref[0].astype(BF16)
        wu_scr[...] = wu_ref[0].astype(BF16)
        wd_scr[...] = wd_ref[0].astype(BF16)

    def swiglu(n_rows):
        row = lax.broadcasted_iota(jnp.int32, (n_rows, x_ref.shape[1]), 0)
        packed = jnp.where(row < n_valid, x_ref[0:n_rows, :], jnp.uint32(0))
        xb = _unpack_bf16_pairs(packed).astype(BF16)
        a = _dot(xb, wg_scr[...])
        u = _dot(xb, wu_scr[...])
        hmid = (a * jax.nn.sigmoid(a) * u).astype(BF16)
        y_ref[0:n_rows, :] = _pack_bf16_pairs(_dot(hmid, wd_scr[...]))

    half_rows = MOE_BLOCK // 2

    @pl.when(n_valid > half_rows)
    def _():
        swiglu(MOE_BLOCK)

    @pl.when((n_valid > 0) & (n_valid <= half_rows))
    def _():
        swiglu(half_rows)
        y_ref[half_rows:, :] = jnp.zeros((MOE_BLOCK - half_rows, y_ref.shape[1]), y_ref.dtype)

    @pl.when(n_valid == 0)
    def _():
        y_ref[...] = jnp.zeros_like(y_ref)


def _experts(block_e, n_valid, x_rows, wg, wu, wd):
    n_rows, half = x_rows.shape
    n_blocks = n_rows // MOE_BLOCK

    def w_map(i, be, nv):
        return (be[i], 0, 0)

    row_blk = pl.BlockSpec((MOE_BLOCK, half), lambda i, be, nv: (i, 0))
    grid_spec = pltpu.PrefetchScalarGridSpec(
        num_scalar_prefetch=2,
        grid=(n_blocks,),
        in_specs=[row_blk,
                  pl.BlockSpec((1, D_MODEL, D_EXPERT), w_map),
                  pl.BlockSpec((1, D_MODEL, D_EXPERT), w_map),
                  pl.BlockSpec((1, D_EXPERT, D_MODEL), w_map)],
        out_specs=row_blk,
        scratch_shapes=[pltpu.VMEM((D_MODEL, D_EXPERT), BF16), pltpu.VMEM((D_MODEL, D_EXPERT), BF16),
                        pltpu.VMEM((D_EXPERT, D_MODEL), BF16)],
    )
    return pl.pallas_call(
        _expert_kernel,
        grid_spec=grid_spec,
        out_shape=jax.ShapeDtypeStruct((n_rows, half), jnp.uint32),
        compiler_params=pltpu.CompilerParams(
            dimension_semantics=("arbitrary",), vmem_limit_bytes=VMEM_LIMIT),
        name="moe_experts",
    )(block_e, n_valid, x_rows, wg, wu, wd)


def _combine_kernel(ya_ref, yb_ref, x1_ref, route_ref, g_ref, out_ref):
    route = route_ref[...]
    w1 = route[:, 2:3]
    w2 = route[:, 3:4]
    x2 = x1_ref[...] + (_unpack_bf16_pairs(ya_ref[...]) * w1 + _unpack_bf16_pairs(yb_ref[...]) * w2)
    ms = jnp.mean(x2 * x2, axis=-1, keepdims=True)
    out_ref[...] = x2 * lax.rsqrt(ms + EPS) * g_ref[...]


def _combine(y_pairs, x1, route, g):
    tokens = x1.shape[0]
    tc = MIX_ROWS
    n_tiles = tokens // tc
    row_blk = pl.BlockSpec((tc, D_MODEL), lambda i: (i, 0))
    return pl.pallas_call(
        _combine_kernel,
        grid=(n_tiles,),
        in_specs=[pl.BlockSpec((tc, D_MODEL // 2), lambda i: (i, 0)),
                  pl.BlockSpec((tc, D_MODEL // 2), lambda i: (i + n_tiles, 0)),
                  row_blk,
                  pl.BlockSpec((tc, LANES), lambda i: (i, 0)),
                  pl.BlockSpec((1, D_MODEL), lambda i: (0, 0))],
        out_specs=row_blk,
        out_shape=jax.ShapeDtypeStruct((tokens, D_MODEL), F32),
        compiler_params=pltpu.CompilerParams(
            dimension_semantics=("arbitrary",), vmem_limit_bytes=VMEM_LIMIT),
        name="moe_combine",
    )(y_pairs, y_pairs, x1, route, g)


def kernel(x, norm_mix_g, w_in, conv_w, conv_b, b_forget, w_conv_out, w_att_out, w_out,
           norm_ffn_g, w_router_group, w_router_expert, w_e_gate, w_e_up, w_e_down,
           norm_final_g):
    batch, seq, d = x.shape
    assert d == D_MODEL and seq % max(ATT_BLOCK, IN_PROJ_ROWS) == 0
    tokens = batch * seq
    assert tokens % MIX_ROWS == 0 and tokens % SC_WINDOW == 0
    x2d = x.reshape(tokens, D_MODEL)

    wb = w_in.astype(BF16)
    n_main = 6 * D_MODEL
    gate0 = n_main + N_HEADS
    pieces = [wb[:, i * D_MODEL:(i + 1) * D_MODEL] for i in range(6)]
    pieces += [wb[:, gate0:gate0 + D_MODEL], wb[:, gate0 + D_MODEL:gate0 + 2 * D_MODEL]]
    w_stack = jnp.stack(pieces, axis=0)
    n_rep = 6
    wf = jnp.pad(jnp.tile(wb[:, n_main:gate0], (1, n_rep)), ((0, 0), (0, LANES - n_rep * N_HEADS)))
    bfv = jnp.pad(jnp.tile(b_forget.astype(F32), n_rep), (0, LANES - n_rep * N_HEADS)).reshape(1, LANES)
    wr = jnp.concatenate(
        [w_router_group, jnp.transpose(w_router_expert, (1, 0, 2)).reshape(D_MODEL, N_EXPERTS)], axis=1)
    wr = jnp.pad(wr, ((0, 0), (0, LANES - wr.shape[1]))).astype(BF16)

    yc, q, k, v, sgc, sga, cq, ck = _in_proj(
        x2d, norm_mix_g.reshape(1, D_MODEL), w_stack, wf, bfv, conv_w,
        conv_b.reshape(1, D_MODEL), batch, seq)
    o = _attention(q, k, v, cq, ck, batch, seq)

    x1, h2, route, route_t, counts = _mixer_out(
        yc, o, sgc, sga, x2d, w_conv_out.astype(BF16), w_att_out.astype(BF16),
        w_out.astype(BF16), norm_ffn_g.reshape(1, D_MODEL), wr)

    counts = counts[0, N_GROUPS:N_GROUPS + N_EXPERTS].astype(jnp.int32)
    padded = (counts + MOE_BLOCK - 1) // MOE_BLOCK * MOE_BLOCK
    end_padded = jnp.cumsum(padded)
    start_padded = end_padded - padded
    expert = route_t[0:2].astype(jnp.int32)
    rank = route_t[4:6].astype(jnp.int32)
    expert_ids = jnp.arange(N_EXPERTS, dtype=jnp.int32)[:, None, None]
    dest = rank + jnp.sum(jnp.where(expert[None] == expert_ids, start_padded[:, None, None], 0), axis=0)
    dest_flat = dest.reshape(2 * tokens)
    n_rows = tokens * 2 + N_EXPERTS * MOE_BLOCK
    block_start = jnp.arange(n_rows // MOE_BLOCK, dtype=jnp.int32) * MOE_BLOCK
    block_e = jnp.minimum(jnp.sum(end_padded[None, :] <= block_start[:, None], axis=1),
                          N_EXPERTS - 1).astype(jnp.int32)
    used_lo = jnp.maximum(start_padded[None, :], block_start[:, None])
    used_hi = jnp.minimum((start_padded + counts)[None, :], block_start[:, None] + MOE_BLOCK)
    n_valid = jnp.sum(jnp.maximum(used_hi - used_lo, 0), axis=1).astype(jnp.int32)

    x_rows = _sc_scatter_rows(h2, dest_flat, n_rows)
    y_rows = _experts(block_e, n_valid, x_rows, w_e_gate, w_e_up, w_e_down)
    y_pairs = _sc_gather_rows(y_rows, dest_flat)
    out = _combine(y_pairs, x1, route, norm_final_g.reshape(1, D_MODEL))
    return out.reshape(batch, seq, D_MODEL)
```

```python
import jax
import jax.numpy as jnp
from jax import lax
from jax.experimental import pallas as pl
from jax.experimental.pallas import tpu as pltpu
from jax.experimental.pallas import tpu_sc as plsc

D_MODEL = 1024
HEAD_DIM = 64
N_HEADS = 16
N_GROUPS = 4
EXPERTS_PER_GROUP = 8
N_EXPERTS = N_GROUPS * EXPERTS_PER_GROUP
D_EXPERT = 512
MOE_BLOCK = 512
CONV_WIDTH = 3
EPS = 1e-6

LANES = 128
HEADS_PER_BLOCK = LANES // HEAD_DIM
NEG_BIG = -1e30
LOG2E = 1.4426950408889634

IN_PROJ_COLS = 256
IN_PROJ_ROWS = 256
ATT_BLOCK = 256
ATT_KEYS = 256
MIX_ROWS = 512
MIX_PART = 128
ROUTE_ROWS = 8
SC_WINDOW = 64
VMEM_LIMIT = 56 * 1024 * 1024

F32 = jnp.float32
BF16 = jnp.bfloat16


def _dot(a, b):
    return jnp.dot(a, b, preferred_element_type=F32)


def _pack_bf16_pairs(x):
    c = x.shape[1] // 2
    lo = lax.bitcast_convert_type(x[:, :c].astype(BF16).astype(F32), jnp.uint32) >> 16
    hi = lax.bitcast_convert_type(x[:, c:].astype(BF16).astype(F32), jnp.uint32) & jnp.uint32(0xFFFF0000)
    return lo | hi


def _unpack_bf16_pairs(w):
    lo = lax.bitcast_convert_type(w << 16, F32)
    hi = lax.bitcast_convert_type(w & jnp.uint32(0xFFFF0000), F32)
    return jnp.concatenate([lo, hi], axis=1)


def _in_proj_kernel(x_ref, g_ref, w_ref, wf_ref, bf_ref, cw_ref, cb_ref,
                    yc_ref, q_ref, k_ref, v_ref, sgc_ref, sga_ref, cq_ref, ck_ref, h_scr):
    j = pl.program_id(1)
    seq = x_ref.shape[0]
    n_chunks = seq // IN_PROJ_ROWS

    @pl.when(j == 0)
    def _():
        for r in range(n_chunks):
            rows = slice(r * IN_PROJ_ROWS, (r + 1) * IN_PROJ_ROWS)
            xs = x_ref[rows, :]
            ms = jnp.mean(xs * xs, axis=-1, keepdims=True)
            h_scr[rows, :] = (xs * lax.rsqrt(ms + EPS) * g_ref[...]).astype(BF16)
        f = _dot(h_scr[...], wf_ref[...]) + bf_ref[...]
        c = jnp.minimum(f, 0.0) - jnp.log(1.0 + jnp.exp(-jnp.abs(f)))
        row = lax.broadcasted_iota(jnp.int32, c.shape, 0)
        d = 1
        while d < seq:
            c = c + jnp.where(row >= d, pltpu.roll(c, d, axis=0), 0.0)
            d *= 2
        c = c * LOG2E
        hi = c.astype(BF16).astype(F32)
        mid = (c - hi).astype(BF16).astype(F32)
        lo = (c - hi - mid).astype(BF16).astype(F32)
        grp = lax.broadcasted_iota(jnp.int32, c.shape, 1) // N_HEADS
        terms = jnp.where(grp % 3 == 0, hi, jnp.where(grp % 3 == 1, mid, lo))
        one = jnp.float32(1.0)
        cq_ref[0] = jnp.where(grp < 3, terms, jnp.where(grp < 6, one, 0.0)).astype(BF16)
        ck_ref[0] = jnp.where(grp < 3, one, jnp.where(grp < 6, -terms, 0.0)).astype(BF16)

    cw0 = cw_ref[0:1, :]
    cw1 = cw_ref[1:2, :]
    cw2 = cw_ref[2:3, :]
    cb = cb_ref[...]
    rowc = lax.broadcasted_iota(jnp.int32, (IN_PROJ_ROWS, IN_PROJ_COLS), 0)
    zprev = None
    for r in range(n_chunks):
        rows = slice(r * IN_PROJ_ROWS, (r + 1) * IN_PROJ_ROWS)
        hs = h_scr[rows, :]
        cb_gate = _dot(hs, w_ref[0])
        z = _dot(hs, w_ref[1]) * _dot(hs, w_ref[2])
        z1 = pltpu.roll(z, 1, axis=0)
        z2 = pltpu.roll(z, 2, axis=0)
        if zprev is None:
            p1 = jnp.zeros_like(z)
            p2 = p1
        else:
            p1 = pltpu.roll(zprev, 1, axis=0)
            p2 = pltpu.roll(zprev, 2, axis=0)
        z1 = jnp.where(rowc < 1, p1, z1)
        z2 = jnp.where(rowc < 2, p2, z2)
        acc = cb + cw0 * z2 + cw1 * z1 + cw2 * z
        yc_ref[rows, :] = (cb_gate * acc).astype(BF16)
        zprev = z
        q_ref[rows, :] = (_dot(hs, w_ref[3]) * (LOG2E * HEAD_DIM ** -0.5)).astype(BF16)
        k_ref[rows, :] = _dot(hs, w_ref[4]).astype(BF16)
        v_ref[rows, :] = _dot(hs, w_ref[5]).astype(BF16)
        sgc_ref[rows, :] = jax.nn.sigmoid(_dot(hs, w_ref[6])).astype(BF16)
        sga_ref[rows, :] = jax.nn.sigmoid(_dot(hs, w_ref[7])).astype(BF16)


def _in_proj(x2d, g, w_stack, wf, bfv, conv_w, conv_b, batch, seq):
    tokens = batch * seq
    tn = IN_PROJ_COLS
    nj = D_MODEL // tn
    col_out = pl.BlockSpec((seq, tn), lambda b, j: (b, j))
    out_bf16 = jax.ShapeDtypeStruct((tokens, D_MODEL), BF16)
    return pl.pallas_call(
        _in_proj_kernel,
        grid=(batch, nj),
        in_specs=[
            pl.BlockSpec((seq, D_MODEL), lambda b, j: (b, 0)),
            pl.BlockSpec((1, D_MODEL), lambda b, j: (0, 0)),
            pl.BlockSpec((8, D_MODEL, tn), lambda b, j: (0, 0, j)),
            pl.BlockSpec((D_MODEL, LANES), lambda b, j: (0, 0)),
            pl.BlockSpec((1, LANES), lambda b, j: (0, 0)),
            pl.BlockSpec((CONV_WIDTH, tn), lambda b, j: (0, j)),
            pl.BlockSpec((1, tn), lambda b, j: (0, j)),
        ],
        out_specs=[col_out] * 6 + [pl.BlockSpec((1, seq, LANES), lambda b, j: (b, 0, 0))] * 2,
        out_shape=[out_bf16] * 6 + [jax.ShapeDtypeStruct((batch, seq, LANES), BF16)] * 2,
        scratch_shapes=[pltpu.VMEM((seq, D_MODEL), BF16)],
        compiler_params=pltpu.CompilerParams(
            dimension_semantics=("arbitrary", "arbitrary"), vmem_limit_bytes=VMEM_LIMIT),
        name="in_proj",
    )(x2d, g, w_stack, wf, bfv, conv_w, conv_b)


def _attention_kernel(q_ref, k_ref, v_ref, cq_ref, ck_ref, o_ref, va_scr):
    hp = pl.program_id(1)
    seq = q_ref.shape[0]
    blk = ATT_BLOCK
    lane = lax.broadcasted_iota(jnp.int32, (1, LANES), 1)
    zero = jnp.zeros((), BF16)

    own, base, q_mask = [], [], []
    for hh in range(HEADS_PER_BLOCK):
        own.append((lane // HEAD_DIM) == hh)
        base.append(((hh + 1) % HEADS_PER_BLOCK) * HEAD_DIM)
        gate = (lane % N_HEADS) == hp * HEADS_PER_BLOCK + hh
        q_mask.append(jnp.concatenate([own[hh], gate], axis=1))
        va_scr[hh] = jnp.where(own[hh], v_ref[...],
                               jnp.where(lane == base[hh], 1.0, 0.0).astype(BF16))

    def pv(p, keys):
        return jnp.concatenate([_dot(p[hh * blk:(hh + 1) * blk, :], va_scr[hh, keys, :])
                                for hh in range(HEADS_PER_BLOCK)], axis=0)

    for rg in range(seq // blk):
        rows = slice(rg * blk, (rg + 1) * blk)
        q_wide = jnp.concatenate([q_ref[rows, :], cq_ref[0, rows, :]], axis=1)
        qa = jnp.concatenate([jnp.where(q_mask[hh], q_wide, zero) for hh in range(HEADS_PER_BLOCK)], axis=0)
        m = acc = None
        k_end = (rg + 1) * blk
        for k0 in range(0, k_end, ATT_KEYS):
            k1 = min(k0 + ATT_KEYS, k_end)
            keys = slice(k0, k1)
            k_wide = jnp.concatenate([k_ref[keys, :], ck_ref[0, keys, :]], axis=1)
            s = lax.dot_general(qa, k_wide, (((1,), (1,)), ((), ())),
                                preferred_element_type=F32)
            if k1 > rg * blk:
                qpos = rg * blk + lax.broadcasted_iota(jnp.int32, s.shape, 0) % blk
                kpos = k0 + lax.broadcasted_iota(jnp.int32, s.shape, 1)
                s = jnp.where(kpos <= qpos, s, NEG_BIG)
            mx = jnp.max(s, axis=1, keepdims=True)
            if m is None:
                m = mx
                acc = pv(jnp.exp2(s - m).astype(BF16), keys)
            else:
                m_new = jnp.maximum(m, mx)
                acc = jnp.exp2(m - m_new) * acc + pv(jnp.exp2(s - m_new).astype(BF16), keys)
                m = m_new
        out = None
        for hh in range(HEADS_PER_BLOCK):
            a_h = acc[hh * blk:(hh + 1) * blk, :]
            res = a_h / a_h[:, base[hh]:base[hh] + 1]
            out = res if out is None else jnp.where(own[hh], res, out)
        o_ref[rows, :] = out.astype(o_ref.dtype)


def _attention(q, k, v, cq, ck, batch, seq):
    tokens = batch * seq
    n_hp = N_HEADS // HEADS_PER_BLOCK
    blk = pl.BlockSpec((seq, LANES), lambda b, hp: (b, hp))
    gate_blk = pl.BlockSpec((1, seq, LANES), lambda b, hp: (b, 0, 0))
    return pl.pallas_call(
        _attention_kernel,
        grid=(batch, n_hp),
        in_specs=[blk, blk, blk, gate_blk, gate_blk],
        out_specs=blk,
        out_shape=jax.ShapeDtypeStruct((tokens, D_MODEL), BF16),
        scratch_shapes=[pltpu.VMEM((HEADS_PER_BLOCK, seq, LANES), BF16)],
        compiler_params=pltpu.CompilerParams(
            dimension_semantics=("arbitrary", "arbitrary"), vmem_limit_bytes=VMEM_LIMIT),
        name="fox_attention",
    )(q, k, v, cq, ck)


def _mixer_out_kernel(yc_ref, o_ref, sgc_ref, sga_ref, x_ref, wco_ref, wao_ref, wo_ref,
                      g_ref, wr_ref, x1_ref, h2_ref, route_ref, route_t_ref, counts_ref,
                      carry_scr, logit_scr):
    i = pl.program_id(0)
    tm = x_ref.shape[0]

    @pl.when(i == 0)
    def _():
        carry_scr[...] = jnp.zeros_like(carry_scr)
        logit_scr[...] = jnp.zeros_like(logit_scr)

    logits = logit_scr[...]

    def route_previous_tile():
        lane = lax.broadcasted_iota(jnp.int32, (tm, LANES), 1).astype(F32)
        far = jnp.float32(4 * LANES)

        def first_lane_of_max(vals, vmax):
            return jnp.min(jnp.where(vals == vmax, lane, far), axis=1, keepdims=True)

        lg = jnp.where(lane < N_GROUPS, logits, NEG_BIG)
        gmax = jnp.max(lg, axis=1, keepdims=True)
        g_val = 1.0 / jnp.sum(jnp.exp(lg - gmax), axis=1, keepdims=True)
        g_idx = first_lane_of_max(lg, gmax)
        lo = N_GROUPS + EXPERTS_PER_GROUP * g_idx
        le = jnp.where((lane >= lo) & (lane < lo + EXPERTS_PER_GROUP), logits, NEG_BIG)
        e1max = jnp.max(le, axis=1, keepdims=True)
        e1 = first_lane_of_max(le, e1max)
        le2 = jnp.where(lane == e1, NEG_BIG, le)
        e2max = jnp.max(le2, axis=1, keepdims=True)
        e2 = first_lane_of_max(le2, e2max)
        ratio = jnp.exp(e2max - e1max)
        w1 = g_val / (1.0 + ratio)
        w2 = g_val * ratio / (1.0 + ratio)

        oh1 = lane == e1
        oh2 = lane == e2
        onehot = jnp.where(oh1 | oh2, 1.0, 0.0).astype(BF16)
        r_i = lax.broadcasted_iota(jnp.int32, (tm, tm), 0)
        c_i = lax.broadcasted_iota(jnp.int32, (tm, tm), 1)
        strict_lower = jnp.where(c_i < r_i, 1.0, 0.0).astype(BF16)
        carry = carry_scr[...]
        before = _dot(strict_lower, onehot) + carry
        rank1 = jnp.sum(jnp.where(oh1, before, 0.0), axis=1, keepdims=True)
        rank2 = jnp.sum(jnp.where(oh2, before, 0.0), axis=1, keepdims=True)
        is_tile = jnp.where(i > 0, 1.0, 0.0)
        carry = carry + is_tile * jnp.sum(onehot.astype(F32), axis=0, keepdims=True)
        carry_scr[...] = carry
        counts_ref[...] = carry

        route = jnp.where(lane == 0, e1 - N_GROUPS, 0.0)
        route = jnp.where(lane == 1, e2 - N_GROUPS, route)
        route = jnp.where(lane == 2, w1, route)
        route = jnp.where(lane == 3, w2, route)
        route = jnp.where(lane == 4, rank1, route)
        route = jnp.where(lane == 5, rank2, route)
        route_ref[...] = route
        route_t_ref[...] = route.T[0:ROUTE_ROWS, :]

    parts = [slice(c * MIX_PART, (c + 1) * MIX_PART) for c in range(tm // MIX_PART)]
    y_conv = [_dot(yc_ref[p, :], wco_ref[...]) for p in parts]
    y_att = [_dot(o_ref[p, :], wao_ref[...]) for p in parts]
    m = [(sgc_ref[p, :].astype(F32) * yc + sga_ref[p, :].astype(F32) * ya).astype(BF16)
         for p, yc, ya in zip(parts, y_conv, y_att)]
    x1 = [x_ref[p, :] + _dot(mm, wo_ref[...]) for p, mm in zip(parts, m)]
    for p, xx in zip(parts, x1):
        x1_ref[p, :] = xx
    route_previous_tile()
    h2 = [xx * lax.rsqrt(jnp.mean(xx * xx, axis=-1, keepdims=True) + EPS) * g_ref[...] for xx in x1]
    for p, hh in zip(parts, h2):
        h2_ref[p, :] = _pack_bf16_pairs(hh)
        logit_scr[p, :] = _dot(hh.astype(BF16), wr_ref[...])


def _mixer_out(yc, o, sgc, sga, x2d, wco, wao, wo, g, wr):
    tokens = x2d.shape[0]
    tm = MIX_ROWS
    n_tiles = tokens // tm

    def cur(i):
        return (jnp.minimum(i, n_tiles - 1), 0)

    def prev(i):
        return (jnp.maximum(i - 1, 0), 0)

    row_blk = pl.BlockSpec((tm, D_MODEL), cur)
    w_blk = pl.BlockSpec((D_MODEL, D_MODEL), lambda i: (0, 0))
    return pl.pallas_call(
        _mixer_out_kernel,
        grid=(n_tiles + 1,),
        in_specs=[row_blk, row_blk, row_blk, row_blk, row_blk, w_blk, w_blk, w_blk,
                  pl.BlockSpec((1, D_MODEL), lambda i: (0, 0)),
                  pl.BlockSpec((D_MODEL, LANES), lambda i: (0, 0))],
        out_specs=[row_blk,
                   pl.BlockSpec((tm, D_MODEL // 2), cur),
                   pl.BlockSpec((tm, LANES), prev),
                   pl.BlockSpec((ROUTE_ROWS, tm), lambda i: (0, jnp.maximum(i - 1, 0))),
                   pl.BlockSpec((1, LANES), lambda i: (0, 0))],
        out_shape=[jax.ShapeDtypeStruct((tokens, D_MODEL), F32),
                   jax.ShapeDtypeStruct((tokens, D_MODEL // 2), jnp.uint32),
                   jax.ShapeDtypeStruct((tokens, LANES), F32),
                   jax.ShapeDtypeStruct((ROUTE_ROWS, tokens), F32),
                   jax.ShapeDtypeStruct((1, LANES), F32)],
        scratch_shapes=[pltpu.VMEM((1, LANES), F32), pltpu.VMEM((tm, LANES), F32)],
        compiler_params=pltpu.CompilerParams(
            dimension_semantics=("arbitrary",), vmem_limit_bytes=VMEM_LIMIT),
        name="mixer_out",
    )(yc, o, sgc, sga, x2d, wco, wao, wo, g, wr)


def _sc_index_rows(indices):
    n = indices.shape[0]
    return jnp.pad(indices.reshape(n // SC_WINDOW, SC_WINDOW), ((0, 0), (0, LANES - SC_WINDOW)))


def _sc_mesh():
    return plsc.VectorSubcoreMesh(core_axis_name="c", subcore_axis_name="s")


def _sc_scatter_rows(data, indices, n_out):
    n = indices.shape[0]
    n_src, width = data.shape
    src_blocks = n_src // SC_WINDOW

    @pl.kernel(out_type=jax.ShapeDtypeStruct((n_out, width), data.dtype), mesh=_sc_mesh())
    def scatter_kernel(x_hbm, i_hbm, o_hbm):
        def body(x_vmem, i_vmem):
            pltpu.sync_copy(x_vmem, o_hbm.at[i_vmem.at[0, pl.ds(0, SC_WINDOW)]])

        pltpu.emit_pipeline(
            body,
            grid=(n // SC_WINDOW,),
            in_specs=[pl.BlockSpec((SC_WINDOW, width), lambda i: (i % src_blocks, 0)),
                      pl.BlockSpec((1, LANES), lambda i: (i, 0))],
            out_specs=[],
            core_axis_name=("c", "s"),
            dimension_semantics=(pltpu.PARALLEL,),
        )(x_hbm, i_hbm)

    return scatter_kernel(data, _sc_index_rows(indices))


def _sc_gather_rows(data, indices):
    n = indices.shape[0]
    width = data.shape[1]

    @pl.kernel(out_type=jax.ShapeDtypeStruct((n, width), data.dtype), mesh=_sc_mesh())
    def gather_kernel(x_hbm, i_hbm, o_hbm):
        def body(i_vmem, o_vmem):
            pltpu.sync_copy(x_hbm.at[i_vmem.at[0, pl.ds(0, SC_WINDOW)]], o_vmem)

        pltpu.emit_pipeline(
            body,
            grid=(n // SC_WINDOW,),
            in_specs=[pl.BlockSpec((1, LANES), lambda i: (i, 0))],
            out_specs=[pl.BlockSpec((SC_WINDOW, width), lambda i: (i, 0))],
            core_axis_name=("c", "s"),
            dimension_semantics=(pltpu.PARALLEL,),
        )(i_hbm, o_hbm)

    return gather_kernel(data, _sc_index_rows(indices))


def _expert_kernel(block_e_ref, n_valid_ref, x_ref, wg_ref, wu_ref, wd_ref, y_ref,
                   wg_scr, wu_scr, wd_scr):
    i = pl.program_id(0)
    n_valid = n_valid_ref[i]

    @pl.when((i == 0) | (block_e_ref[i] != block_e_ref[jnp.maximum(i - 1, 0)]))
    def _():
        wg_scr[...] = wg_ref[0].astype(BF16)
        wu_scr[...] = wu_ref[0].astype(BF16)
        wd_scr[...] = wd_ref[0].astype(BF16)

    @pl.when(n_valid > 0)
    def _():
        row = lax.broadcasted_iota(jnp.int32, x_ref.shape, 0)
        packed = jnp.where(row < n_valid, x_ref[...], jnp.uint32(0))
        xb = _unpack_bf16_pairs(packed).astype(BF16)
        a = _dot(xb, wg_scr[...])
        u = _dot(xb, wu_scr[...])
        hmid = (a * jax.nn.sigmoid(a) * u).astype(BF16)
        y_ref[...] = _pack_bf16_pairs(_dot(hmid, wd_scr[...]))

    @pl.when(n_valid == 0)
    def _():
        y_ref[...] = jnp.zeros_like(y_ref)


def _experts(block_e, n_valid, x_rows, wg, wu, wd):
    n_rows, half = x_rows.shape
    n_blocks = n_rows // MOE_BLOCK

    def w_map(i, be, nv):
        return (be[i], 0, 0)

    row_blk = pl.BlockSpec((MOE_BLOCK, half), lambda i, be, nv: (i, 0))
    grid_spec = pltpu.PrefetchScalarGridSpec(
        num_scalar_prefetch=2,
        grid=(n_blocks,),
        in_specs=[row_blk,
                  pl.BlockSpec((1, D_MODEL, D_EXPERT), w_map),
                  pl.BlockSpec((1, D_MODEL, D_EXPERT), w_map),
                  pl.BlockSpec((1, D_EXPERT, D_MODEL), w_map)],
        out_specs=row_blk,
        scratch_shapes=[pltpu.VMEM((D_MODEL, D_EXPERT), BF16), pltpu.VMEM((D_MODEL, D_EXPERT), BF16),
                        pltpu.VMEM((D_EXPERT, D_MODEL), BF16)],
    )
    return pl.pallas_call(
        _expert_kernel,
        grid_spec=grid_spec,
        out_shape=jax.ShapeDtypeStruct((n_rows, half), jnp.uint32),
        compiler_params=pltpu.CompilerParams(
            dimension_semantics=("arbitrary",), vmem_limit_bytes=VMEM_LIMIT),
        name="moe_experts",
    )(block_e, n_valid, x_rows, wg, wu, wd)


def _combine_kernel(ya_ref, yb_ref, x1_ref, route_ref, g_ref, out_ref):
    route = route_ref[...]
    w1 = route[:, 2:3]
    w2 = route[:, 3:4]
    x2 = x1_ref[...] + (_unpack_bf16_pairs(ya_ref[...]) * w1 + _unpack_bf16_pairs(yb_ref[...]) * w2)
    ms = jnp.mean(x2 * x2, axis=-1, keepdims=True)
    out_ref[...] = x2 * lax.rsqrt(ms + EPS) * g_ref[...]


def _combine(y_pairs, x1, route, g):
    tokens = x1.shape[0]
    tc = MIX_ROWS
    n_tiles = tokens // tc
    row_blk = pl.BlockSpec((tc, D_MODEL), lambda i: (i, 0))
    return pl.pallas_call(
        _combine_kernel,
        grid=(n_tiles,),
        in_specs=[pl.BlockSpec((tc, D_MODEL // 2), lambda i: (i, 0)),
                  pl.BlockSpec((tc, D_MODEL // 2), lambda i: (i + n_tiles, 0)),
                  row_blk,
                  pl.BlockSpec((tc, LANES), lambda i: (i, 0)),
                  pl.BlockSpec((1, D_MODEL), lambda i: (0, 0))],
        out_specs=row_blk,
        out_shape=jax.ShapeDtypeStruct((tokens, D_MODEL), F32),
        compiler_params=pltpu.CompilerParams(
            dimension_semantics=("arbitrary",), vmem_limit_bytes=VMEM_LIMIT),
        name="moe_combine",
    )(y_pairs, y_pairs, x1, route, g)


def kernel(x, norm_mix_g, w_in, conv_w, conv_b, b_forget, w_conv_out, w_att_out, w_out,
           norm_ffn_g, w_router_group, w_router_expert, w_e_gate, w_e_up, w_e_down,
           norm_final_g):
    batch, seq, d = x.shape
    assert d == D_MODEL and seq % max(ATT_BLOCK, IN_PROJ_ROWS) == 0
    tokens = batch * seq
    assert tokens % MIX_ROWS == 0 and tokens % SC_WINDOW == 0
    x2d = x.reshape(tokens, D_MODEL)

    wb = w_in.astype(BF16)
    n_main = 6 * D_MODEL
    gate0 = n_main + N_HEADS
    pieces = [wb[:, i * D_MODEL:(i + 1) * D_MODEL] for i in range(6)]
    pieces += [wb[:, gate0:gate0 + D_MODEL], wb[:, gate0 + D_MODEL:gate0 + 2 * D_MODEL]]
    w_stack = jnp.stack(pieces, axis=0)
    n_rep = 6
    wf = jnp.pad(jnp.tile(wb[:, n_main:gate0], (1, n_rep)), ((0, 0), (0, LANES - n_rep * N_HEADS)))
    bfv = jnp.pad(jnp.tile(b_forget.astype(F32), n_rep), (0, LANES - n_rep * N_HEADS)).reshape(1, LANES)
    wr = jnp.concatenate(
        [w_router_group, jnp.transpose(w_router_expert, (1, 0, 2)).reshape(D_MODEL, N_EXPERTS)], axis=1)
    wr = jnp.pad(wr, ((0, 0), (0, LANES - wr.shape[1]))).astype(BF16)

    yc, q, k, v, sgc, sga, cq, ck = _in_proj(
        x2d, norm_mix_g.reshape(1, D_MODEL), w_stack, wf, bfv, conv_w,
        conv_b.reshape(1, D_MODEL), batch, seq)
    o = _attention(q, k, v, cq, ck, batch, seq)

    x1, h2, route, route_t, counts = _mixer_out(
        yc, o, sgc, sga, x2d, w_conv_out.astype(BF16), w_att_out.astype(BF16),
        w_out.astype(BF16), norm_ffn_g.reshape(1, D_MODEL), wr)

    counts = counts[0, N_GROUPS:N_GROUPS + N_EXPERTS].astype(jnp.int32)
    padded = (counts + MOE_BLOCK - 1) // MOE_BLOCK * MOE_BLOCK
    end_padded = jnp.cumsum(padded)
    start_padded = end_padded - padded
    expert = route_t[0:2].astype(jnp.int32)
    rank = route_t[4:6].astype(jnp.int32)
    expert_ids = jnp.arange(N_EXPERTS, dtype=jnp.int32)[:, None, None]
    dest = rank + jnp.sum(jnp.where(expert[None] == expert_ids, start_padded[:, None, None], 0), axis=0)
    dest_flat = dest.reshape(2 * tokens)
    n_rows = tokens * 2 + N_EXPERTS * MOE_BLOCK
    block_start = jnp.arange(n_rows // MOE_BLOCK, dtype=jnp.int32) * MOE_BLOCK
    block_e = jnp.minimum(jnp.sum(end_padded[None, :] <= block_start[:, None], axis=1),
                          N_EXPERTS - 1).astype(jnp.int32)
    used_lo = jnp.maximum(start_padded[None, :], block_start[:, None])
    used_hi = jnp.minimum((start_padded + counts)[None, :], block_start[:, None] + MOE_BLOCK)
    n_valid = jnp.sum(jnp.maximum(used_hi - used_lo, 0), axis=1).astype(jnp.int32)

    x_rows = _sc_scatter_rows(h2, dest_flat, n_rows)
    y_rows = _experts(block_e, n_valid, x_rows, w_e_gate, w_e_up, w_e_down)
    y_pairs = _sc_gather_rows(y_rows, dest_flat)
    out = _combine(y_pairs, x1, route, norm_final_g.reshape(1, D_MODEL))
    return out.reshape(batch, seq, D_MODEL)
```

```python
import jax
import jax.numpy as jnp
from jax import lax
from jax.experimental import pallas as pl
from jax.experimental.pallas import tpu as pltpu
from jax.experimental.pallas import tpu_sc as plsc

D_MODEL = 1024
HEAD_DIM = 64
N_HEADS = 16
N_GROUPS = 4
EXPERTS_PER_GROUP = 8
N_EXPERTS = N_GROUPS * EXPERTS_PER_GROUP
D_EXPERT = 512
MOE_BLOCK = 512
CONV_WIDTH = 3
EPS = 1e-6

LANES = 128
HEADS_PER_BLOCK = LANES // HEAD_DIM
NEG_BIG = -1e30
LOG2E = 1.4426950408889634

IN_PROJ_COLS = 256
IN_PROJ_ROWS = 256
ATT_BLOCK = 256
ATT_KEYS = 256
MIX_ROWS = 512
MIX_PART = 256
ROUTE_ROWS = 8
SC_WINDOW = 64
VMEM_LIMIT = 56 * 1024 * 1024

F32 = jnp.float32
BF16 = jnp.bfloat16


def _dot(a, b):
    return jnp.dot(a, b, preferred_element_type=F32)


def _pack_bf16_pairs(x):
    c = x.shape[1] // 2
    lo = lax.bitcast_convert_type(x[:, :c].astype(BF16).astype(F32), jnp.uint32) >> 16
    hi = lax.bitcast_convert_type(x[:, c:].astype(BF16).astype(F32), jnp.uint32) & jnp.uint32(0xFFFF0000)
    return lo | hi


def _unpack_bf16_pairs(w):
    lo = lax.bitcast_convert_type(w << 16, F32)
    hi = lax.bitcast_convert_type(w & jnp.uint32(0xFFFF0000), F32)
    return jnp.concatenate([lo, hi], axis=1)


def _in_proj_kernel(x_ref, g_ref, w_ref, wf_ref, bf_ref, cw_ref, cb_ref,
                    yc_ref, q_ref, k_ref, v_ref, sgc_ref, sga_ref, cq_ref, ck_ref, h_scr):
    j = pl.program_id(1)
    seq = x_ref.shape[0]
    n_chunks = seq // IN_PROJ_ROWS

    @pl.when(j == 0)
    def _():
        for r in range(n_chunks):
            rows = slice(r * IN_PROJ_ROWS, (r + 1) * IN_PROJ_ROWS)
            xs = x_ref[rows, :]
            ms = jnp.mean(xs * xs, axis=-1, keepdims=True)
            h_scr[rows, :] = (xs * lax.rsqrt(ms + EPS) * g_ref[...]).astype(BF16)
        f = _dot(h_scr[...], wf_ref[...]) + bf_ref[...]
        c = jnp.minimum(f, 0.0) - jnp.log(1.0 + jnp.exp(-jnp.abs(f)))
        row = lax.broadcasted_iota(jnp.int32, c.shape, 0)
        d = 1
        while d < seq:
            c = c + jnp.where(row >= d, pltpu.roll(c, d, axis=0), 0.0)
            d *= 2
        c = c * LOG2E
        hi = c.astype(BF16).astype(F32)
        mid = (c - hi).astype(BF16).astype(F32)
        lo = (c - hi - mid).astype(BF16).astype(F32)
        grp = lax.broadcasted_iota(jnp.int32, c.shape, 1) // N_HEADS
        terms = jnp.where(grp % 3 == 0, hi, jnp.where(grp % 3 == 1, mid, lo))
        one = jnp.float32(1.0)
        cq_ref[0] = jnp.where(grp < 3, terms, jnp.where(grp < 6, one, 0.0)).astype(BF16)
        ck_ref[0] = jnp.where(grp < 3, one, jnp.where(grp < 6, -terms, 0.0)).astype(BF16)

    cw0 = cw_ref[0:1, :]
    cw1 = cw_ref[1:2, :]
    cw2 = cw_ref[2:3, :]
    cb = cb_ref[...]
    rowc = lax.broadcasted_iota(jnp.int32, (IN_PROJ_ROWS, IN_PROJ_COLS), 0)
    zprev = None
    for r in range(n_chunks):
        rows = slice(r * IN_PROJ_ROWS, (r + 1) * IN_PROJ_ROWS)
        hs = h_scr[rows, :]
        cb_gate = _dot(hs, w_ref[0])
        z = _dot(hs, w_ref[1]) * _dot(hs, w_ref[2])
        z1 = pltpu.roll(z, 1, axis=0)
        z2 = pltpu.roll(z, 2, axis=0)
        if zprev is None:
            p1 = jnp.zeros_like(z)
            p2 = p1
        else:
            p1 = pltpu.roll(zprev, 1, axis=0)
            p2 = pltpu.roll(zprev, 2, axis=0)
        z1 = jnp.where(rowc < 1, p1, z1)
        z2 = jnp.where(rowc < 2, p2, z2)
        acc = cb + cw0 * z2 + cw1 * z1 + cw2 * z
        yc_ref[rows, :] = (cb_gate * acc).astype(BF16)
        zprev = z
        q_ref[rows, :] = (_dot(hs, w_ref[3]) * (LOG2E * HEAD_DIM ** -0.5)).astype(BF16)
        k_ref[rows, :] = _dot(hs, w_ref[4]).astype(BF16)
        v_ref[rows, :] = _dot(hs, w_ref[5]).astype(BF16)
        sgc_ref[rows, :] = jax.nn.sigmoid(_dot(hs, w_ref[6])).astype(BF16)
        sga_ref[rows, :] = jax.nn.sigmoid(_dot(hs, w_ref[7])).astype(BF16)


def _in_proj(x2d, g, w_stack, wf, bfv, conv_w, conv_b, batch, seq):
    tokens = batch * seq
    tn = IN_PROJ_COLS
    nj = D_MODEL // tn
    col_out = pl.BlockSpec((seq, tn), lambda b, j: (b, j))
    out_bf16 = jax.ShapeDtypeStruct((tokens, D_MODEL), BF16)
    return pl.pallas_call(
        _in_proj_kernel,
        grid=(batch, nj),
        in_specs=[
            pl.BlockSpec((seq, D_MODEL), lambda b, j: (b, 0)),
            pl.BlockSpec((1, D_MODEL), lambda b, j: (0, 0)),
            pl.BlockSpec((8, D_MODEL, tn), lambda b, j: (0, 0, j)),
            pl.BlockSpec((D_MODEL, LANES), lambda b, j: (0, 0)),
            pl.BlockSpec((1, LANES), lambda b, j: (0, 0)),
            pl.BlockSpec((CONV_WIDTH, tn), lambda b, j: (0, j)),
            pl.BlockSpec((1, tn), lambda b, j: (0, j)),
        ],
        out_specs=[col_out] * 6 + [pl.BlockSpec((1, seq, LANES), lambda b, j: (b, 0, 0))] * 2,
        out_shape=[out_bf16] * 6 + [jax.ShapeDtypeStruct((batch, seq, LANES), BF16)] * 2,
        scratch_shapes=[pltpu.VMEM((seq, D_MODEL), BF16)],
        compiler_params=pltpu.CompilerParams(
            dimension_semantics=("arbitrary", "arbitrary"), vmem_limit_bytes=VMEM_LIMIT),
        name="in_proj",
    )(x2d, g, w_stack, wf, bfv, conv_w, conv_b)


def _attention_kernel(q_ref, k_ref, v_ref, cq_ref, ck_ref, o_ref, va_scr):
    hp = pl.program_id(1)
    seq = q_ref.shape[0]
    blk = ATT_BLOCK
    lane = lax.broadcasted_iota(jnp.int32, (1, LANES), 1)
    zero = jnp.zeros((), BF16)

    own, base, q_mask = [], [], []
    for hh in range(HEADS_PER_BLOCK):
        own.append((lane // HEAD_DIM) == hh)
        base.append(((hh + 1) % HEADS_PER_BLOCK) * HEAD_DIM)
        gate = (lane % N_HEADS) == hp * HEADS_PER_BLOCK + hh
        q_mask.append(jnp.concatenate([own[hh], gate], axis=1))
        va_scr[hh] = jnp.where(own[hh], v_ref[...],
                               jnp.where(lane == base[hh], 1.0, 0.0).astype(BF16))

    def pv(p, keys):
        return jnp.concatenate([_dot(p[hh * blk:(hh + 1) * blk, :], va_scr[hh, keys, :])
                                for hh in range(HEADS_PER_BLOCK)], axis=0)

    for rg in range(seq // blk):
        rows = slice(rg * blk, (rg + 1) * blk)
        q_wide = jnp.concatenate([q_ref[rows, :], cq_ref[0, rows, :]], axis=1)
        qa = jnp.concatenate([jnp.where(q_mask[hh], q_wide, zero) for hh in range(HEADS_PER_BLOCK)], axis=0)
        m = acc = None
        k_end = (rg + 1) * blk
        for k0 in range(0, k_end, ATT_KEYS):
            k1 = min(k0 + ATT_KEYS, k_end)
            keys = slice(k0, k1)
            k_wide = jnp.concatenate([k_ref[keys, :], ck_ref[0, keys, :]], axis=1)
            s = lax.dot_general(qa, k_wide, (((1,), (1,)), ((), ())),
                                preferred_element_type=F32)
            if k1 > rg * blk:
                qpos = rg * blk + lax.broadcasted_iota(jnp.int32, s.shape, 0) % blk
                kpos = k0 + lax.broadcasted_iota(jnp.int32, s.shape, 1)
                s = jnp.where(kpos <= qpos, s, NEG_BIG)
            mx = jnp.max(s, axis=1, keepdims=True)
            if m is None:
                m = mx
                acc = pv(jnp.exp2(s - m).astype(BF16), keys)
            else:
                m_new = jnp.maximum(m, mx)
                acc = jnp.exp2(m - m_new) * acc + pv(jnp.exp2(s - m_new).astype(BF16), keys)
                m = m_new
        out = None
        for hh in range(HEADS_PER_BLOCK):
            a_h = acc[hh * blk:(hh + 1) * blk, :]
            res = a_h / a_h[:, base[hh]:base[hh] + 1]
            out = res if out is None else jnp.where(own[hh], res, out)
        o_ref[rows, :] = out.astype(o_ref.dtype)


def _attention(q, k, v, cq, ck, batch, seq):
    tokens = batch * seq
    n_hp = N_HEADS // HEADS_PER_BLOCK
    blk = pl.BlockSpec((seq, LANES), lambda b, hp: (b, hp))
    gate_blk = pl.BlockSpec((1, seq, LANES), lambda b, hp: (b, 0, 0))
    return pl.pallas_call(
        _attention_kernel,
        grid=(batch, n_hp),
        in_specs=[blk, blk, blk, gate_blk, gate_blk],
        out_specs=blk,
        out_shape=jax.ShapeDtypeStruct((tokens, D_MODEL), BF16),
        scratch_shapes=[pltpu.VMEM((HEADS_PER_BLOCK, seq, LANES), BF16)],
        compiler_params=pltpu.CompilerParams(
            dimension_semantics=("arbitrary", "arbitrary"), vmem_limit_bytes=VMEM_LIMIT),
        name="fox_attention",
    )(q, k, v, cq, ck)


def _mixer_out_kernel(yc_ref, o_ref, sgc_ref, sga_ref, x_ref, wco_ref, wao_ref, wo_ref,
                      g_ref, wr_ref, x1_ref, h2_ref, route_ref, route_t_ref, counts_ref,
                      carry_scr, logit_scr):
    i = pl.program_id(0)
    tm = x_ref.shape[0]

    @pl.when(i == 0)
    def _():
        carry_scr[...] = jnp.zeros_like(carry_scr)
        logit_scr[...] = jnp.zeros_like(logit_scr)

    logits = logit_scr[...]

    def route_previous_tile():
        lane = lax.broadcasted_iota(jnp.int32, (tm, LANES), 1).astype(F32)
        far = jnp.float32(4 * LANES)

        def first_lane_of_max(vals, vmax):
            return jnp.min(jnp.where(vals == vmax, lane, far), axis=1, keepdims=True)

        lg = jnp.where(lane < N_GROUPS, logits, NEG_BIG)
        gmax = jnp.max(lg, axis=1, keepdims=True)
        g_val = 1.0 / jnp.sum(jnp.exp(lg - gmax), axis=1, keepdims=True)
        g_idx = first_lane_of_max(lg, gmax)
        lo = N_GROUPS + EXPERTS_PER_GROUP * g_idx
        le = jnp.where((lane >= lo) & (lane < lo + EXPERTS_PER_GROUP), logits, NEG_BIG)
        e1max = jnp.max(le, axis=1, keepdims=True)
        e1 = first_lane_of_max(le, e1max)
        le2 = jnp.where(lane == e1, NEG_BIG, le)
        e2max = jnp.max(le2, axis=1, keepdims=True)
        e2 = first_lane_of_max(le2, e2max)
        ratio = jnp.exp(e2max - e1max)
        w1 = g_val / (1.0 + ratio)
        w2 = g_val * ratio / (1.0 + ratio)

        oh1 = lane == e1
        oh2 = lane == e2
        onehot = jnp.where(oh1 | oh2, 1.0, 0.0).astype(BF16)
        r_i = lax.broadcasted_iota(jnp.int32, (tm, tm), 0)
        c_i = lax.broadcasted_iota(jnp.int32, (tm, tm), 1)
        strict_lower = jnp.where(c_i < r_i, 1.0, 0.0).astype(BF16)
        carry = carry_scr[...]
        before = _dot(strict_lower, onehot) + carry
        rank1 = jnp.sum(jnp.where(oh1, before, 0.0), axis=1, keepdims=True)
        rank2 = jnp.sum(jnp.where(oh2, before, 0.0), axis=1, keepdims=True)
        is_tile = jnp.where(i > 0, 1.0, 0.0)
        carry = carry + is_tile * jnp.sum(onehot.astype(F32), axis=0, keepdims=True)
        carry_scr[...] = carry
        counts_ref[...] = carry

        route = jnp.where(lane == 0, e1 - N_GROUPS, 0.0)
        route = jnp.where(lane == 1, e2 - N_GROUPS, route)
        route = jnp.where(lane == 2, w1, route)
        route = jnp.where(lane == 3, w2, route)
        route = jnp.where(lane == 4, rank1, route)
        route = jnp.where(lane == 5, rank2, route)
        route_ref[...] = route
        route_t_ref[...] = route.T[0:ROUTE_ROWS, :]

    parts = [slice(c * MIX_PART, (c + 1) * MIX_PART) for c in range(tm // MIX_PART)]
    y_conv = [_dot(yc_ref[p, :], wco_ref[...]) for p in parts]
    y_att = [_dot(o_ref[p, :], wao_ref[...]) for p in parts]
    m = [(sgc_ref[p, :].astype(F32) * yc + sga_ref[p, :].astype(F32) * ya).astype(BF16)
         for p, yc, ya in zip(parts, y_conv, y_att)]
    x1 = [x_ref[p, :] + _dot(mm, wo_ref[...]) for p, mm in zip(parts, m)]
    for p, xx in zip(parts, x1):
        x1_ref[p, :] = xx
    route_previous_tile()
    h2 = [xx * lax.rsqrt(jnp.mean(xx * xx, axis=-1, keepdims=True) + EPS) * g_ref[...] for xx in x1]
    for p, hh in zip(parts, h2):
        h2_ref[p, :] = _pack_bf16_pairs(hh)
        logit_scr[p, :] = _dot(hh.astype(BF16), wr_ref[...])


def _mixer_out(yc, o, sgc, sga, x2d, wco, wao, wo, g, wr):
    tokens = x2d.shape[0]
    tm = MIX_ROWS
    n_tiles = tokens // tm

    def cur(i):
        return (jnp.minimum(i, n_tiles - 1), 0)

    def prev(i):
        return (jnp.maximum(i - 1, 0), 0)

    row_blk = pl.BlockSpec((tm, D_MODEL), cur)
    w_blk = pl.BlockSpec((D_MODEL, D_MODEL), lambda i: (0, 0))
    return pl.pallas_call(
        _mixer_out_kernel,
        grid=(n_tiles + 1,),
        in_specs=[row_blk, row_blk, row_blk, row_blk, row_blk, w_blk, w_blk, w_blk,
                  pl.BlockSpec((1, D_MODEL), lambda i: (0, 0)),
                  pl.BlockSpec((D_MODEL, LANES), lambda i: (0, 0))],
        out_specs=[row_blk,
                   pl.BlockSpec((tm, D_MODEL // 2), cur),
                   pl.BlockSpec((tm, LANES), prev),
                   pl.BlockSpec((ROUTE_ROWS, tm), lambda i: (0, jnp.maximum(i - 1, 0))),
                   pl.BlockSpec((1, LANES), lambda i: (0, 0))],
        out_shape=[jax.ShapeDtypeStruct((tokens, D_MODEL), F32),
                   jax.ShapeDtypeStruct((tokens, D_MODEL // 2), jnp.uint32),
                   jax.ShapeDtypeStruct((tokens, LANES), F32),
                   jax.ShapeDtypeStruct((ROUTE_ROWS, tokens), F32),
                   jax.ShapeDtypeStruct((1, LANES), F32)],
        scratch_shapes=[pltpu.VMEM((1, LANES), F32), pltpu.VMEM((tm, LANES), F32)],
        compiler_params=pltpu.CompilerParams(
            dimension_semantics=("arbitrary",), vmem_limit_bytes=VMEM_LIMIT),
        name="mixer_out",
    )(yc, o, sgc, sga, x2d, wco, wao, wo, g, wr)


def _sc_index_rows(indices):
    n = indices.shape[0]
    return jnp.pad(indices.reshape(n // SC_WINDOW, SC_WINDOW), ((0, 0), (0, LANES - SC_WINDOW)))


def _sc_mesh():
    return plsc.VectorSubcoreMesh(core_axis_name="c", subcore_axis_name="s")


def _sc_scatter_rows(data, indices, n_out):
    n = indices.shape[0]
    n_src, width = data.shape
    src_blocks = n_src // SC_WINDOW

    @pl.kernel(out_type=jax.ShapeDtypeStruct((n_out, width), data.dtype), mesh=_sc_mesh())
    def scatter_kernel(x_hbm, i_hbm, o_hbm):
        def body(x_vmem, i_vmem):
            pltpu.sync_copy(x_vmem, o_hbm.at[i_vmem.at[0, pl.ds(0, SC_WINDOW)]])

        pltpu.emit_pipeline(
            body,
            grid=(n // SC_WINDOW,),
            in_specs=[pl.BlockSpec((SC_WINDOW, width), lambda i: (i % src_blocks, 0)),
                      pl.BlockSpec((1, LANES), lambda i: (i, 0))],
            out_specs=[],
            core_axis_name=("c", "s"),
            dimension_semantics=(pltpu.PARALLEL,),
        )(x_hbm, i_hbm)

    return scatter_kernel(data, _sc_index_rows(indices))


def _sc_gather_rows(data, indices):
    n = indices.shape[0]
    width = data.shape[1]

    @pl.kernel(out_type=jax.ShapeDtypeStruct((n, width), data.dtype), mesh=_sc_mesh())
    def gather_kernel(x_hbm, i_hbm, o_hbm):
        def body(i_vmem, o_vmem):
            pltpu.sync_copy(x_hbm.at[i_vmem.at[0, pl.ds(0, SC_WINDOW)]], o_vmem)

        pltpu.emit_pipeline(
            body,
            grid=(n // SC_WINDOW,),
            in_specs=[pl.BlockSpec((1, LANES), lambda i: (i, 0))],
            out_specs=[pl.BlockSpec((SC_WINDOW, width), lambda i: (i, 0))],
            core_axis_name=("c", "s"),
            dimension_semantics=(pltpu.PARALLEL,),
        )(i_hbm, o_hbm)

    return gather_kernel(data, _sc_index_rows(indices))


def _expert_kernel(block_e_ref, n_valid_ref, x_ref, wg_ref, wu_ref, wd_ref, y_ref,
                   wg_scr, wu_scr, wd_scr):
    i = pl.program_id(0)
    n_valid = n_valid_ref[i]

    @pl.when((i == 0) | (block_e_ref[i] != block_e_ref[jnp.maximum(i - 1, 0)]))
    def _():
        wg_scr[...] = wg_ref[0].astype(BF16)
        wu_scr[...] = wu_ref[0].astype(BF16)
        wd_scr[...] = wd_ref[0].astype(BF16)

    @pl.when(n_valid > 0)
    def _():
        row = lax.broadcasted_iota(jnp.int32, x_ref.shape, 0)
        packed = jnp.where(row < n_valid, x_ref[...], jnp.uint32(0))
        xb = _unpack_bf16_pairs(packed).astype(BF16)
        a = _dot(xb, wg_scr[...])
        u = _dot(xb, wu_scr[...])
        hmid = (a * jax.nn.sigmoid(a) * u).astype(BF16)
        y_ref[...] = _pack_bf16_pairs(_dot(hmid, wd_scr[...]))

    @pl.when(n_valid == 0)
    def _():
        y_ref[...] = jnp.zeros_like(y_ref)


def _experts(block_e, n_valid, x_rows, wg, wu, wd):
    n_rows, half = x_rows.shape
    n_blocks = n_rows // MOE_BLOCK

    def w_map(i, be, nv):
        return (be[i], 0, 0)

    row_blk = pl.BlockSpec((MOE_BLOCK, half), lambda i, be, nv: (i, 0))
    grid_spec = pltpu.PrefetchScalarGridSpec(
        num_scalar_prefetch=2,
        grid=(n_blocks,),
        in_specs=[row_blk,
                  pl.BlockSpec((1, D_MODEL, D_EXPERT), w_map),
                  pl.BlockSpec((1, D_MODEL, D_EXPERT), w_map),
                  pl.BlockSpec((1, D_EXPERT, D_MODEL), w_map)],
        out_specs=row_blk,
        scratch_shapes=[pltpu.VMEM((D_MODEL, D_EXPERT), BF16), pltpu.VMEM((D_MODEL, D_EXPERT), BF16),
                        pltpu.VMEM((D_EXPERT, D_MODEL), BF16)],
    )
    return pl.pallas_call(
        _expert_kernel,
        grid_spec=grid_spec,
        out_shape=jax.ShapeDtypeStruct((n_rows, half), jnp.uint32),
        compiler_params=pltpu.CompilerParams(
            dimension_semantics=("arbitrary",), vmem_limit_bytes=VMEM_LIMIT),
        name="moe_experts",
    )(block_e, n_valid, x_rows, wg, wu, wd)


def _combine_kernel(ya_ref, yb_ref, x1_ref, route_ref, g_ref, out_ref):
    route = route_ref[...]
    w1 = route[:, 2:3]
    w2 = route[:, 3:4]
    x2 = x1_ref[...] + (_unpack_bf16_pairs(ya_ref[...]) * w1 + _unpack_bf16_pairs(yb_ref[...]) * w2)
    ms = jnp.mean(x2 * x2, axis=-1, keepdims=True)
    out_ref[...] = x2 * lax.rsqrt(ms + EPS) * g_ref[...]


def _combine(y_pairs, x1, route, g):
    tokens = x1.shape[0]
    tc = MIX_ROWS
    n_tiles = tokens // tc
    row_blk = pl.BlockSpec((tc, D_MODEL), lambda i: (i, 0))
    return pl.pallas_call(
        _combine_kernel,
        grid=(n_tiles,),
        in_specs=[pl.BlockSpec((tc, D_MODEL // 2), lambda i: (i, 0)),
                  pl.BlockSpec((tc, D_MODEL // 2), lambda i: (i + n_tiles, 0)),
                  row_blk,
                  pl.BlockSpec((tc, LANES), lambda i: (i, 0)),
                  pl.BlockSpec((1, D_MODEL), lambda i: (0, 0))],
        out_specs=row_blk,
        out_shape=jax.ShapeDtypeStruct((tokens, D_MODEL), F32),
        compiler_params=pltpu.CompilerParams(
            dimension_semantics=("arbitrary",), vmem_limit_bytes=VMEM_LIMIT),
        name="moe_combine",
    )(y_pairs, y_pairs, x1, route, g)


def kernel(x, norm_mix_g, w_in, conv_w, conv_b, b_forget, w_conv_out, w_att_out, w_out,
           norm_ffn_g, w_router_group, w_router_expert, w_e_gate, w_e_up, w_e_down,
           norm_final_g):
    batch, seq, d = x.shape
    assert d == D_MODEL and seq % max(ATT_BLOCK, IN_PROJ_ROWS) == 0
    tokens = batch * seq
    assert tokens % MIX_ROWS == 0 and tokens % SC_WINDOW == 0
    x2d = x.reshape(tokens, D_MODEL)

    wb = w_in.astype(BF16)
    n_main = 6 * D_MODEL
    gate0 = n_main + N_HEADS
    pieces = [wb[:, i * D_MODEL:(i + 1) * D_MODEL] for i in range(6)]
    pieces += [wb[:, gate0:gate0 + D_MODEL], wb[:, gate0 + D_MODEL:gate0 + 2 * D_MODEL]]
    w_stack = jnp.stack(pieces, axis=0)
    n_rep = 6
    wf = jnp.pad(jnp.tile(wb[:, n_main:gate0], (1, n_rep)), ((0, 0), (0, LANES - n_rep * N_HEADS)))
    bfv = jnp.pad(jnp.tile(b_forget.astype(F32), n_rep), (0, LANES - n_rep * N_HEADS)).reshape(1, LANES)
    wr = jnp.concatenate(
        [w_router_group, jnp.transpose(w_router_expert, (1, 0, 2)).reshape(D_MODEL, N_EXPERTS)], axis=1)
    wr = jnp.pad(wr, ((0, 0), (0, LANES - wr.shape[1]))).astype(BF16)

    yc, q, k, v, sgc, sga, cq, ck = _in_proj(
        x2d, norm_mix_g.reshape(1, D_MODEL), w_stack, wf, bfv, conv_w,
        conv_b.reshape(1, D_MODEL), batch, seq)
    o = _attention(q, k, v, cq, ck, batch, seq)

    x1, h2, route, route_t, counts = _mixer_out(
        yc, o, sgc, sga, x2d, w_conv_out.astype(BF16), w_att_out.astype(BF16),
        w_out.astype(BF16), norm_ffn_g.reshape(1, D_MODEL), wr)

    counts = counts[0, N_GROUPS:N_GROUPS + N_EXPERTS].astype(jnp.int32)
    padded = (counts + MOE_BLOCK - 1) // MOE_BLOCK * MOE_BLOCK
    end_padded = jnp.cumsum(padded)
    start_padded = end_padded - padded
    expert = route_t[0:2].astype(jnp.int32)
    rank = route_t[4:6].astype(jnp.int32)
    expert_ids = jnp.arange(N_EXPERTS, dtype=jnp.int32)[:, None, None]
    dest = rank + jnp.sum(jnp.where(expert[None] == expert_ids, start_padded[:, None, None], 0), axis=0)
    dest_flat = dest.reshape(2 * tokens)
    n_rows = tokens * 2 + N_EXPERTS * MOE_BLOCK
    block_start = jnp.arange(n_rows // MOE_BLOCK, dtype=jnp.int32) * MOE_BLOCK
    block_e = jnp.minimum(jnp.sum(end_padded[None, :] <= block_start[:, None], axis=1),
                          N_EXPERTS - 1).astype(jnp.int32)
    used_lo = jnp.maximum(start_padded[None, :], block_start[:, None])
    used_hi = jnp.minimum((start_padded + counts)[None, :], block_start[:, None] + MOE_BLOCK)
    n_valid = jnp.sum(jnp.maximum(used_hi - used_lo, 0), axis=1).astype(jnp.int32)

    x_rows = _sc_scatter_rows(h2, dest_flat, n_rows)
    y_rows = _experts(block_e, n_valid, x_rows, w_e_gate, w_e_up, w_e_down)
    y_pairs = _sc_gather_rows(y_rows, dest_flat)
    out = _combine(y_pairs, x1, route, norm_final_g.reshape(1, D_MODEL))
    return out.reshape(batch, seq, D_MODEL)
```

```python
import jax
import jax.numpy as jnp
from jax import lax
from jax.experimental import pallas as pl
from jax.experimental.pallas import tpu as pltpu
from jax.experimental.pallas import tpu_sc as plsc

D_MODEL = 1024
HEAD_DIM = 64
N_HEADS = 16
N_GROUPS = 4
EXPERTS_PER_GROUP = 8
N_EXPERTS = N_GROUPS * EXPERTS_PER_GROUP
D_EXPERT = 512
MOE_BLOCK = 1024
CONV_WIDTH = 3
EPS = 1e-6

LANES = 128
HEADS_PER_BLOCK = LANES // HEAD_DIM
NEG_BIG = -1e30
LOG2E = 1.4426950408889634

IN_PROJ_COLS = 256
IN_PROJ_ROWS = 256
ATT_BLOCK = 256
ATT_KEYS = 256
MIX_ROWS = 512
MIX_PART = 256
ROUTE_ROWS = 8
SC_WINDOW = 64
VMEM_LIMIT = 56 * 1024 * 1024

F32 = jnp.float32
BF16 = jnp.bfloat16


def _dot(a, b):
    return jnp.dot(a, b, preferred_element_type=F32)


def _pack_bf16_pairs(x):
    c = x.shape[1] // 2
    lo = lax.bitcast_convert_type(x[:, :c].astype(BF16).astype(F32), jnp.uint32) >> 16
    hi = lax.bitcast_convert_type(x[:, c:].astype(BF16).astype(F32), jnp.uint32) & jnp.uint32(0xFFFF0000)
    return lo | hi


def _unpack_bf16_pairs(w):
    lo = lax.bitcast_convert_type(w << 16, F32)
    hi = lax.bitcast_convert_type(w & jnp.uint32(0xFFFF0000), F32)
    return jnp.concatenate([lo, hi], axis=1)


def _in_proj_kernel(x_ref, g_ref, w_ref, wf_ref, bf_ref, cw_ref, cb_ref,
                    yc_ref, q_ref, k_ref, v_ref, sgc_ref, sga_ref, cq_ref, ck_ref, h_scr):
    j = pl.program_id(1)
    seq = x_ref.shape[0]
    n_chunks = seq // IN_PROJ_ROWS

    @pl.when(j == 0)
    def _():
        for r in range(n_chunks):
            rows = slice(r * IN_PROJ_ROWS, (r + 1) * IN_PROJ_ROWS)
            xs = x_ref[rows, :]
            ms = jnp.mean(xs * xs, axis=-1, keepdims=True)
            h_scr[rows, :] = (xs * lax.rsqrt(ms + EPS) * g_ref[...]).astype(BF16)
        f = _dot(h_scr[...], wf_ref[...]) + bf_ref[...]
        c = jnp.minimum(f, 0.0) - jnp.log(1.0 + jnp.exp(-jnp.abs(f)))
        row = lax.broadcasted_iota(jnp.int32, c.shape, 0)
        d = 1
        while d < seq:
            c = c + jnp.where(row >= d, pltpu.roll(c, d, axis=0), 0.0)
            d *= 2
        c = c * LOG2E
        hi = c.astype(BF16).astype(F32)
        mid = (c - hi).astype(BF16).astype(F32)
        lo = (c - hi - mid).astype(BF16).astype(F32)
        grp = lax.broadcasted_iota(jnp.int32, c.shape, 1) // N_HEADS
        terms = jnp.where(grp % 3 == 0, hi, jnp.where(grp % 3 == 1, mid, lo))
        one = jnp.float32(1.0)
        cq_ref[0] = jnp.where(grp < 3, terms, jnp.where(grp < 6, one, 0.0)).astype(BF16)
        ck_ref[0] = jnp.where(grp < 3, one, jnp.where(grp < 6, -terms, 0.0)).astype(BF16)

    cw0 = cw_ref[0:1, :]
    cw1 = cw_ref[1:2, :]
    cw2 = cw_ref[2:3, :]
    cb = cb_ref[...]
    rowc = lax.broadcasted_iota(jnp.int32, (IN_PROJ_ROWS, IN_PROJ_COLS), 0)
    zprev = None
    for r in range(n_chunks):
        rows = slice(r * IN_PROJ_ROWS, (r + 1) * IN_PROJ_ROWS)
        hs = h_scr[rows, :]
        cb_gate = _dot(hs, w_ref[0])
        z = _dot(hs, w_ref[1]) * _dot(hs, w_ref[2])
        z1 = pltpu.roll(z, 1, axis=0)
        z2 = pltpu.roll(z, 2, axis=0)
        if zprev is None:
            p1 = jnp.zeros_like(z)
            p2 = p1
        else:
            p1 = pltpu.roll(zprev, 1, axis=0)
            p2 = pltpu.roll(zprev, 2, axis=0)
        z1 = jnp.where(rowc < 1, p1, z1)
        z2 = jnp.where(rowc < 2, p2, z2)
        acc = cb + cw0 * z2 + cw1 * z1 + cw2 * z
        yc_ref[rows, :] = (cb_gate * acc).astype(BF16)
        zprev = z
        q_ref[rows, :] = (_dot(hs, w_ref[3]) * (LOG2E * HEAD_DIM ** -0.5)).astype(BF16)
        k_ref[rows, :] = _dot(hs, w_ref[4]).astype(BF16)
        v_ref[rows, :] = _dot(hs, w_ref[5]).astype(BF16)
        sgc_ref[rows, :] = jax.nn.sigmoid(_dot(hs, w_ref[6])).astype(BF16)
        sga_ref[rows, :] = jax.nn.sigmoid(_dot(hs, w_ref[7])).astype(BF16)


def _in_proj(x2d, g, w_stack, wf, bfv, conv_w, conv_b, batch, seq):
    tokens = batch * seq
    tn = IN_PROJ_COLS
    nj = D_MODEL // tn
    col_out = pl.BlockSpec((seq, tn), lambda b, j: (b, j))
    out_bf16 = jax.ShapeDtypeStruct((tokens, D_MODEL), BF16)
    return pl.pallas_call(
        _in_proj_kernel,
        grid=(batch, nj),
        in_specs=[
            pl.BlockSpec((seq, D_MODEL), lambda b, j: (b, 0)),
            pl.BlockSpec((1, D_MODEL), lambda b, j: (0, 0)),
            pl.BlockSpec((8, D_MODEL, tn), lambda b, j: (0, 0, j)),
            pl.BlockSpec((D_MODEL, LANES), lambda b, j: (0, 0)),
            pl.BlockSpec((1, LANES), lambda b, j: (0, 0)),
            pl.BlockSpec((CONV_WIDTH, tn), lambda b, j: (0, j)),
            pl.BlockSpec((1, tn), lambda b, j: (0, j)),
        ],
        out_specs=[col_out] * 6 + [pl.BlockSpec((1, seq, LANES), lambda b, j: (b, 0, 0))] * 2,
        out_shape=[out_bf16] * 6 + [jax.ShapeDtypeStruct((batch, seq, LANES), BF16)] * 2,
        scratch_shapes=[pltpu.VMEM((seq, D_MODEL), BF16)],
        compiler_params=pltpu.CompilerParams(
            dimension_semantics=("arbitrary", "arbitrary"), vmem_limit_bytes=VMEM_LIMIT),
        name="in_proj",
    )(x2d, g, w_stack, wf, bfv, conv_w, conv_b)


def _attention_kernel(q_ref, k_ref, v_ref, cq_ref, ck_ref, o_ref, va_scr):
    hp = pl.program_id(1)
    seq = q_ref.shape[0]
    blk = ATT_BLOCK
    lane = lax.broadcasted_iota(jnp.int32, (1, LANES), 1)
    zero = jnp.zeros((), BF16)

    own, base, q_mask = [], [], []
    for hh in range(HEADS_PER_BLOCK):
        own.append((lane // HEAD_DIM) == hh)
        base.append(((hh + 1) % HEADS_PER_BLOCK) * HEAD_DIM)
        gate = (lane % N_HEADS) == hp * HEADS_PER_BLOCK + hh
        q_mask.append(jnp.concatenate([own[hh], gate], axis=1))
        va_scr[hh] = jnp.where(own[hh], v_ref[...],
                               jnp.where(lane == base[hh], 1.0, 0.0).astype(BF16))

    def pv(p, keys):
        return jnp.concatenate([_dot(p[hh * blk:(hh + 1) * blk, :], va_scr[hh, keys, :])
                                for hh in range(HEADS_PER_BLOCK)], axis=0)

    for rg in range(seq // blk):
        rows = slice(rg * blk, (rg + 1) * blk)
        q_wide = jnp.concatenate([q_ref[rows, :], cq_ref[0, rows, :]], axis=1)
        qa = jnp.concatenate([jnp.where(q_mask[hh], q_wide, zero) for hh in range(HEADS_PER_BLOCK)], axis=0)
        m = acc = None
        k_end = (rg + 1) * blk
        for k0 in range(0, k_end, ATT_KEYS):
            k1 = min(k0 + ATT_KEYS, k_end)
            keys = slice(k0, k1)
            k_wide = jnp.concatenate([k_ref[keys, :], ck_ref[0, keys, :]], axis=1)
            s = lax.dot_general(qa, k_wide, (((1,), (1,)), ((), ())),
                                preferred_element_type=F32)
            if k1 > rg * blk:
                qpos = rg * blk + lax.broadcasted_iota(jnp.int32, s.shape, 0) % blk
                kpos = k0 + lax.broadcasted_iota(jnp.int32, s.shape, 1)
                s = jnp.where(kpos <= qpos, s, NEG_BIG)
            mx = jnp.max(s, axis=1, keepdims=True)
            if m is None:
                m = mx
                acc = pv(jnp.exp2(s - m).astype(BF16), keys)
            else:
                m_new = jnp.maximum(m, mx)
                acc = jnp.exp2(m - m_new) * acc + pv(jnp.exp2(s - m_new).astype(BF16), keys)
                m = m_new
        out = None
        for hh in range(HEADS_PER_BLOCK):
            a_h = acc[hh * blk:(hh + 1) * blk, :]
            res = a_h / a_h[:, base[hh]:base[hh] + 1]
            out = res if out is None else jnp.where(own[hh], res, out)
        o_ref[rows, :] = out.astype(o_ref.dtype)


def _attention(q, k, v, cq, ck, batch, seq):
    tokens = batch * seq
    n_hp = N_HEADS // HEADS_PER_BLOCK
    blk = pl.BlockSpec((seq, LANES), lambda b, hp: (b, hp))
    gate_blk = pl.BlockSpec((1, seq, LANES), lambda b, hp: (b, 0, 0))
    return pl.pallas_call(
        _attention_kernel,
        grid=(batch, n_hp),
        in_specs=[blk, blk, blk, gate_blk, gate_blk],
        out_specs=blk,
        out_shape=jax.ShapeDtypeStruct((tokens, D_MODEL), BF16),
        scratch_shapes=[pltpu.VMEM((HEADS_PER_BLOCK, seq, LANES), BF16)],
        compiler_params=pltpu.CompilerParams(
            dimension_semantics=("arbitrary", "arbitrary"), vmem_limit_bytes=VMEM_LIMIT),
        name="fox_attention",
    )(q, k, v, cq, ck)


def _mixer_out_kernel(yc_ref, o_ref, sgc_ref, sga_ref, x_ref, wco_ref, wao_ref, wo_ref,
                      g_ref, wr_ref, x1_ref, h2_ref, route_ref, route_t_ref, counts_ref,
                      carry_scr, logit_scr):
    i = pl.program_id(0)
    tm = x_ref.shape[0]

    @pl.when(i == 0)
    def _():
        carry_scr[...] = jnp.zeros_like(carry_scr)
        logit_scr[...] = jnp.zeros_like(logit_scr)

    logits = logit_scr[...]

    def route_previous_tile():
        lane = lax.broadcasted_iota(jnp.int32, (tm, LANES), 1).astype(F32)
        far = jnp.float32(4 * LANES)

        def first_lane_of_max(vals, vmax):
            return jnp.min(jnp.where(vals == vmax, lane, far), axis=1, keepdims=True)

        lg = jnp.where(lane < N_GROUPS, logits, NEG_BIG)
        gmax = jnp.max(lg, axis=1, keepdims=True)
        g_val = 1.0 / jnp.sum(jnp.exp(lg - gmax), axis=1, keepdims=True)
        g_idx = first_lane_of_max(lg, gmax)
        lo = N_GROUPS + EXPERTS_PER_GROUP * g_idx
        le = jnp.where((lane >= lo) & (lane < lo + EXPERTS_PER_GROUP), logits, NEG_BIG)
        e1max = jnp.max(le, axis=1, keepdims=True)
        e1 = first_lane_of_max(le, e1max)
        le2 = jnp.where(lane == e1, NEG_BIG, le)
        e2max = jnp.max(le2, axis=1, keepdims=True)
        e2 = first_lane_of_max(le2, e2max)
        ratio = jnp.exp(e2max - e1max)
        w1 = g_val / (1.0 + ratio)
        w2 = g_val * ratio / (1.0 + ratio)

        oh1 = lane == e1
        oh2 = lane == e2
        onehot = jnp.where(oh1 | oh2, 1.0, 0.0).astype(BF16)
        r_i = lax.broadcasted_iota(jnp.int32, (tm, tm), 0)
        c_i = lax.broadcasted_iota(jnp.int32, (tm, tm), 1)
        strict_lower = jnp.where(c_i < r_i, 1.0, 0.0).astype(BF16)
        carry = carry_scr[...]
        before = _dot(strict_lower, onehot) + carry
        rank1 = jnp.sum(jnp.where(oh1, before, 0.0), axis=1, keepdims=True)
        rank2 = jnp.sum(jnp.where(oh2, before, 0.0), axis=1, keepdims=True)
        is_tile = jnp.where(i > 0, 1.0, 0.0)
        carry = carry + is_tile * jnp.sum(onehot.astype(F32), axis=0, keepdims=True)
        carry_scr[...] = carry
        counts_ref[...] = carry

        route = jnp.where(lane == 0, e1 - N_GROUPS, 0.0)
        route = jnp.where(lane == 1, e2 - N_GROUPS, route)
        route = jnp.where(lane == 2, w1, route)
        route = jnp.where(lane == 3, w2, route)
        route = jnp.where(lane == 4, rank1, route)
        route = jnp.where(lane == 5, rank2, route)
        route_ref[...] = route
        route_t_ref[...] = route.T[0:ROUTE_ROWS, :]

    parts = [slice(c * MIX_PART, (c + 1) * MIX_PART) for c in range(tm // MIX_PART)]
    y_conv = [_dot(yc_ref[p, :], wco_ref[...]) for p in parts]
    y_att = [_dot(o_ref[p, :], wao_ref[...]) for p in parts]
    m = [(sgc_ref[p, :].astype(F32) * yc + sga_ref[p, :].astype(F32) * ya).astype(BF16)
         for p, yc, ya in zip(parts, y_conv, y_att)]
    x1 = [x_ref[p, :] + _dot(mm, wo_ref[...]) for p, mm in zip(parts, m)]
    for p, xx in zip(parts, x1):
        x1_ref[p, :] = xx
    route_previous_tile()
    h2 = [xx * lax.rsqrt(jnp.mean(xx * xx, axis=-1, keepdims=True) + EPS) * g_ref[...] for xx in x1]
    for p, hh in zip(parts, h2):
        h2_ref[p, :] = _pack_bf16_pairs(hh)
        logit_scr[p, :] = _dot(hh.astype(BF16), wr_ref[...])


def _mixer_out(yc, o, sgc, sga, x2d, wco, wao, wo, g, wr):
    tokens = x2d.shape[0]
    tm = MIX_ROWS
    n_tiles = tokens // tm

    def cur(i):
        return (jnp.minimum(i, n_tiles - 1), 0)

    def prev(i):
        return (jnp.maximum(i - 1, 0), 0)

    row_blk = pl.BlockSpec((tm, D_MODEL), cur)
    w_blk = pl.BlockSpec((D_MODEL, D_MODEL), lambda i: (0, 0))
    return pl.pallas_call(
        _mixer_out_kernel,
        grid=(n_tiles + 1,),
        in_specs=[row_blk, row_blk, row_blk, row_blk, row_blk, w_blk, w_blk, w_blk,
                  pl.BlockSpec((1, D_MODEL), lambda i: (0, 0)),
                  pl.BlockSpec((D_MODEL, LANES), lambda i: (0, 0))],
        out_specs=[row_blk,
                   pl.BlockSpec((tm, D_MODEL // 2), cur),
                   pl.BlockSpec((tm, LANES), prev),
                   pl.BlockSpec((ROUTE_ROWS, tm), lambda i: (0, jnp.maximum(i - 1, 0))),
                   pl.BlockSpec((1, LANES), lambda i: (0, 0))],
        out_shape=[jax.ShapeDtypeStruct((tokens, D_MODEL), F32),
                   jax.ShapeDtypeStruct((tokens, D_MODEL // 2), jnp.uint32),
                   jax.ShapeDtypeStruct((tokens, LANES), F32),
                   jax.ShapeDtypeStruct((ROUTE_ROWS, tokens), F32),
                   jax.ShapeDtypeStruct((1, LANES), F32)],
        scratch_shapes=[pltpu.VMEM((1, LANES), F32), pltpu.VMEM((tm, LANES), F32)],
        compiler_params=pltpu.CompilerParams(
            dimension_semantics=("arbitrary",), vmem_limit_bytes=VMEM_LIMIT),
        name="mixer_out",
    )(yc, o, sgc, sga, x2d, wco, wao, wo, g, wr)


def _sc_index_rows(indices):
    n = indices.shape[0]
    return jnp.pad(indices.reshape(n // SC_WINDOW, SC_WINDOW), ((0, 0), (0, LANES - SC_WINDOW)))


def _sc_mesh():
    return plsc.VectorSubcoreMesh(core_axis_name="c", subcore_axis_name="s")


def _sc_scatter_rows(data, indices, n_out):
    n = indices.shape[0]
    n_src, width = data.shape
    src_blocks = n_src // SC_WINDOW

    @pl.kernel(out_type=jax.ShapeDtypeStruct((n_out, width), data.dtype), mesh=_sc_mesh())
    def scatter_kernel(x_hbm, i_hbm, o_hbm):
        def body(x_vmem, i_vmem):
            pltpu.sync_copy(x_vmem, o_hbm.at[i_vmem.at[0, pl.ds(0, SC_WINDOW)]])

        pltpu.emit_pipeline(
            body,
            grid=(n // SC_WINDOW,),
            in_specs=[pl.BlockSpec((SC_WINDOW, width), lambda i: (i % src_blocks, 0)),
                      pl.BlockSpec((1, LANES), lambda i: (i, 0))],
            out_specs=[],
            core_axis_name=("c", "s"),
            dimension_semantics=(pltpu.PARALLEL,),
        )(x_hbm, i_hbm)

    return scatter_kernel(data, _sc_index_rows(indices))


def _sc_gather_rows(data, indices):
    n = indices.shape[0]
    width = data.shape[1]

    @pl.kernel(out_type=jax.ShapeDtypeStruct((n, width), data.dtype), mesh=_sc_mesh())
    def gather_kernel(x_hbm, i_hbm, o_hbm):
        def body(i_vmem, o_vmem):
            pltpu.sync_copy(x_hbm.at[i_vmem.at[0, pl.ds(0, SC_WINDOW)]], o_vmem)

        pltpu.emit_pipeline(
            body,
            grid=(n // SC_WINDOW,),
            in_specs=[pl.BlockSpec((1, LANES), lambda i: (i, 0))],
            out_specs=[pl.BlockSpec((SC_WINDOW, width), lambda i: (i, 0))],
            core_axis_name=("c", "s"),
            dimension_semantics=(pltpu.PARALLEL,),
        )(i_hbm, o_hbm)

    return gather_kernel(data, _sc_index_rows(indices))


def _expert_kernel(block_e_ref, n_valid_ref, x_ref, wg_ref, wu_ref, wd_ref, y_ref,
                   wg_scr, wu_scr, wd_scr):
    i = pl.program_id(0)
    n_valid = n_valid_ref[i]

    @pl.when((i == 0) | (block_e_ref[i] != block_e_ref[jnp.maximum(i - 1, 0)]))
    def _():
        wg_scr[...] = wg_ref[0].astype(BF16)
        wu_scr[...] = wu_ref[0].astype(BF16)
        wd_scr[...] = wd_ref[0].astype(BF16)

    @pl.when(n_valid > 0)
    def _():
        row = lax.broadcasted_iota(jnp.int32, x_ref.shape, 0)
        packed = jnp.where(row < n_valid, x_ref[...], jnp.uint32(0))
        xb = _unpack_bf16_pairs(packed).astype(BF16)
        a = _dot(xb, wg_scr[...])
        u = _dot(xb, wu_scr[...])
        hmid = (a * jax.nn.sigmoid(a) * u).astype(BF16)
        y_ref[...] = _pack_bf16_pairs(_dot(hmid, wd_scr[...]))

    @pl.when(n_valid == 0)
    def _():
        y_ref[...] = jnp.zeros_like(y_ref)


def _experts(block_e, n_valid, x_rows, wg, wu, wd):
    n_rows, half = x_rows.shape
    n_blocks = n_rows // MOE_BLOCK

    def w_map(i, be, nv):
        return (be[i], 0, 0)

    row_blk = pl.BlockSpec((MOE_BLOCK, half), lambda i, be, nv: (i, 0))
    grid_spec = pltpu.PrefetchScalarGridSpec(
        num_scalar_prefetch=2,
        grid=(n_blocks,),
        in_specs=[row_blk,
                  pl.BlockSpec((1, D_MODEL, D_EXPERT), w_map),
                  pl.BlockSpec((1, D_MODEL, D_EXPERT), w_map),
                  pl.BlockSpec((1, D_EXPERT, D_MODEL), w_map)],
        out_specs=row_blk,
        scratch_shapes=[pltpu.VMEM((D_MODEL, D_EXPERT), BF16), pltpu.VMEM((D_MODEL, D_EXPERT), BF16),
                        pltpu.VMEM((D_EXPERT, D_MODEL), BF16)],
    )
    return pl.pallas_call(
        _expert_kernel,
        grid_spec=grid_spec,
        out_shape=jax.ShapeDtypeStruct((n_rows, half), jnp.uint32),
        compiler_params=pltpu.CompilerParams(
            dimension_semantics=("arbitrary",), vmem_limit_bytes=VMEM_LIMIT),
        name="moe_experts",
    )(block_e, n_valid, x_rows, wg, wu, wd)


def _combine_kernel(ya_ref, yb_ref, x1_ref, route_ref, g_ref, out_ref):
    route = route_ref[...]
    w1 = route[:, 2:3]
    w2 = route[:, 3:4]
    x2 = x1_ref[...] + (_unpack_bf16_pairs(ya_ref[...]) * w1 + _unpack_bf16_pairs(yb_ref[...]) * w2)
    ms = jnp.mean(x2 * x2, axis=-1, keepdims=True)
    out_ref[...] = x2 * lax.rsqrt(ms + EPS) * g_ref[...]


def _combine(y_pairs, x1, route, g):
    tokens = x1.shape[0]
    tc = MIX_ROWS
    n_tiles = tokens // tc
    row_blk = pl.BlockSpec((tc, D_MODEL), lambda i: (i, 0))
    return pl.pallas_call(
        _combine_kernel,
        grid=(n_tiles,),
        in_specs=[pl.BlockSpec((tc, D_MODEL // 2), lambda i: (i, 0)),
                  pl.BlockSpec((tc, D_MODEL // 2), lambda i: (i + n_tiles, 0)),
                  row_blk,
                  pl.BlockSpec((tc, LANES), lambda i: (i, 0)),
                  pl.BlockSpec((1, D_MODEL), lambda i: (0, 0))],
        out_specs=row_blk,
        out_shape=jax.ShapeDtypeStruct((tokens, D_MODEL), F32),
        compiler_params=pltpu.CompilerParams(
            dimension_semantics=("arbitrary",), vmem_limit_bytes=VMEM_LIMIT),
        name="moe_combine",
    )(y_pairs, y_pairs, x1, route, g)


def kernel(x, norm_mix_g, w_in, conv_w, conv_b, b_forget, w_conv_out, w_att_out, w_out,
           norm_ffn_g, w_router_group, w_router_expert, w_e_gate, w_e_up, w_e_down,
           norm_final_g):
    batch, seq, d = x.shape
    assert d == D_MODEL and seq % max(ATT_BLOCK, IN_PROJ_ROWS) == 0
    tokens = batch * seq
    assert tokens % MIX_ROWS == 0 and tokens % SC_WINDOW == 0
    x2d = x.reshape(tokens, D_MODEL)

    wb = w_in.astype(BF16)
    n_main = 6 * D_MODEL
    gate0 = n_main + N_HEADS
    pieces = [wb[:, i * D_MODEL:(i + 1) * D_MODEL] for i in range(6)]
    pieces += [wb[:, gate0:gate0 + D_MODEL], wb[:, gate0 + D_MODEL:gate0 + 2 * D_MODEL]]
    w_stack = jnp.stack(pieces, axis=0)
    n_rep = 6
    wf = jnp.pad(jnp.tile(wb[:, n_main:gate0], (1, n_rep)), ((0, 0), (0, LANES - n_rep * N_HEADS)))
    bfv = jnp.pad(jnp.tile(b_forget.astype(F32), n_rep), (0, LANES - n_rep * N_HEADS)).reshape(1, LANES)
    wr = jnp.concatenate(
        [w_router_group, jnp.transpose(w_router_expert, (1, 0, 2)).reshape(D_MODEL, N_EXPERTS)], axis=1)
    wr = jnp.pad(wr, ((0, 0), (0, LANES - wr.shape[1]))).astype(BF16)

    yc, q, k, v, sgc, sga, cq, ck = _in_proj(
        x2d, norm_mix_g.reshape(1, D_MODEL), w_stack, wf, bfv, conv_w,
        conv_b.reshape(1, D_MODEL), batch, seq)
    o = _attention(q, k, v, cq, ck, batch, seq)

    x1, h2, route, route_t, counts = _mixer_out(
        yc, o, sgc, sga, x2d, w_conv_out.astype(BF16), w_att_out.astype(BF16),
        w_out.astype(BF16), norm_ffn_g.reshape(1, D_MODEL), wr)

    counts = counts[0, N_GROUPS:N_GROUPS + N_EXPERTS].astype(jnp.int32)
    padded = (counts + MOE_BLOCK - 1) // MOE_BLOCK * MOE_BLOCK
    end_padded = jnp.cumsum(padded)
    start_padded = end_padded - padded
    expert = route_t[0:2].astype(jnp.int32)
    rank = route_t[4:6].astype(jnp.int32)
    expert_ids = jnp.arange(N_EXPERTS, dtype=jnp.int32)[:, None, None]
    dest = rank + jnp.sum(jnp.where(expert[None] == expert_ids, start_padded[:, None, None], 0), axis=0)
    dest_flat = dest.reshape(2 * tokens)
    n_rows = tokens * 2 + N_EXPERTS * MOE_BLOCK
    block_start = jnp.arange(n_rows // MOE_BLOCK, dtype=jnp.int32) * MOE_BLOCK
    block_e = jnp.minimum(jnp.sum(end_padded[None, :] <= block_start[:, None], axis=1),
                          N_EXPERTS - 1).astype(jnp.int32)
    used_lo = jnp.maximum(start_padded[None, :], block_start[:, None])
    used_hi = jnp.minimum((start_padded + counts)[None, :], block_start[:, None] + MOE_BLOCK)
    n_valid = jnp.sum(jnp.maximum(used_hi - used_lo, 0), axis=1).astype(jnp.int32)

    x_rows = _sc_scatter_rows(h2, dest_flat, n_rows)
    y_rows = _experts(block_e, n_valid, x_rows, w_e_gate, w_e_up, w_e_down)
    y_pairs = _sc_gather_rows(y_rows, dest_flat)
    out = _combine(y_pairs, x1, route, norm_final_g.reshape(1, D_MODEL))
    return out.reshape(batch, seq, D_MODEL)
```

```python
import jax
import jax.numpy as jnp
from jax import lax
from jax.experimental import pallas as pl
from jax.experimental.pallas import tpu as pltpu
from jax.experimental.pallas import tpu_sc as plsc

D_MODEL = 1024
HEAD_DIM = 64
N_HEADS = 16
N_GROUPS = 4
EXPERTS_PER_GROUP = 8
N_EXPERTS = N_GROUPS * EXPERTS_PER_GROUP
D_EXPERT = 512
MOE_BLOCK = 1024
CONV_WIDTH = 3
EPS = 1e-6

LANES = 128
HEADS_PER_BLOCK = LANES // HEAD_DIM
NEG_BIG = -1e30
LOG2E = 1.4426950408889634

IN_PROJ_COLS = 256
IN_PROJ_ROWS = 256
ATT_BLOCK = 256
ATT_KEYS = 256
MIX_ROWS = 512
MIX_PART = 256
COMBINE_ROWS = 1024
ROUTE_ROWS = 8
SC_WINDOW = 64
VMEM_LIMIT = 56 * 1024 * 1024

F32 = jnp.float32
BF16 = jnp.bfloat16


def _dot(a, b):
    return jnp.dot(a, b, preferred_element_type=F32)


def _pack_bf16_pairs(x):
    c = x.shape[1] // 2
    lo = lax.bitcast_convert_type(x[:, :c].astype(BF16).astype(F32), jnp.uint32) >> 16
    hi = lax.bitcast_convert_type(x[:, c:].astype(BF16).astype(F32), jnp.uint32) & jnp.uint32(0xFFFF0000)
    return lo | hi


def _unpack_bf16_pairs(w):
    lo = lax.bitcast_convert_type(w << 16, F32)
    hi = lax.bitcast_convert_type(w & jnp.uint32(0xFFFF0000), F32)
    return jnp.concatenate([lo, hi], axis=1)


def _in_proj_kernel(x_ref, g_ref, w_ref, wf_ref, bf_ref, cw_ref, cb_ref,
                    yc_ref, q_ref, k_ref, v_ref, sgc_ref, sga_ref, cq_ref, ck_ref, h_scr):
    j = pl.program_id(1)
    seq = x_ref.shape[0]
    n_chunks = seq // IN_PROJ_ROWS

    @pl.when(j == 0)
    def _():
        for r in range(n_chunks):
            rows = slice(r * IN_PROJ_ROWS, (r + 1) * IN_PROJ_ROWS)
            xs = x_ref[rows, :]
            ms = jnp.mean(xs * xs, axis=-1, keepdims=True)
            h_scr[rows, :] = (xs * lax.rsqrt(ms + EPS) * g_ref[...]).astype(BF16)
        f = _dot(h_scr[...], wf_ref[...]) + bf_ref[...]
        c = jnp.minimum(f, 0.0) - jnp.log(1.0 + jnp.exp(-jnp.abs(f)))
        row = lax.broadcasted_iota(jnp.int32, c.shape, 0)
        d = 1
        while d < seq:
            c = c + jnp.where(row >= d, pltpu.roll(c, d, axis=0), 0.0)
            d *= 2
        c = c * LOG2E
        hi = c.astype(BF16).astype(F32)
        mid = (c - hi).astype(BF16).astype(F32)
        lo = (c - hi - mid).astype(BF16).astype(F32)
        grp = lax.broadcasted_iota(jnp.int32, c.shape, 1) // N_HEADS
        terms = jnp.where(grp % 3 == 0, hi, jnp.where(grp % 3 == 1, mid, lo))
        one = jnp.float32(1.0)
        cq_ref[0] = jnp.where(grp < 3, terms, jnp.where(grp < 6, one, 0.0)).astype(BF16)
        ck_ref[0] = jnp.where(grp < 3, one, jnp.where(grp < 6, -terms, 0.0)).astype(BF16)

    cw0 = cw_ref[0:1, :]
    cw1 = cw_ref[1:2, :]
    cw2 = cw_ref[2:3, :]
    cb = cb_ref[...]
    rowc = lax.broadcasted_iota(jnp.int32, (IN_PROJ_ROWS, IN_PROJ_COLS), 0)
    zprev = None
    for r in range(n_chunks):
        rows = slice(r * IN_PROJ_ROWS, (r + 1) * IN_PROJ_ROWS)
        hs = h_scr[rows, :]
        cb_gate = _dot(hs, w_ref[0])
        z = _dot(hs, w_ref[1]) * _dot(hs, w_ref[2])
        z1 = pltpu.roll(z, 1, axis=0)
        z2 = pltpu.roll(z, 2, axis=0)
        if zprev is None:
            p1 = jnp.zeros_like(z)
            p2 = p1
        else:
            p1 = pltpu.roll(zprev, 1, axis=0)
            p2 = pltpu.roll(zprev, 2, axis=0)
        z1 = jnp.where(rowc < 1, p1, z1)
        z2 = jnp.where(rowc < 2, p2, z2)
        acc = cb + cw0 * z2 + cw1 * z1 + cw2 * z
        yc_ref[rows, :] = (cb_gate * acc).astype(BF16)
        zprev = z
        q_ref[rows, :] = (_dot(hs, w_ref[3]) * (LOG2E * HEAD_DIM ** -0.5)).astype(BF16)
        k_ref[rows, :] = _dot(hs, w_ref[4]).astype(BF16)
        v_ref[rows, :] = _dot(hs, w_ref[5]).astype(BF16)
        sgc_ref[rows, :] = jax.nn.sigmoid(_dot(hs, w_ref[6])).astype(BF16)
        sga_ref[rows, :] = jax.nn.sigmoid(_dot(hs, w_ref[7])).astype(BF16)


def _in_proj(x2d, g, w_stack, wf, bfv, conv_w, conv_b, batch, seq):
    tokens = batch * seq
    tn = IN_PROJ_COLS
    nj = D_MODEL // tn
    col_out = pl.BlockSpec((seq, tn), lambda b, j: (b, j))
    out_bf16 = jax.ShapeDtypeStruct((tokens, D_MODEL), BF16)
    return pl.pallas_call(
        _in_proj_kernel,
        grid=(batch, nj),
        in_specs=[
            pl.BlockSpec((seq, D_MODEL), lambda b, j: (b, 0)),
            pl.BlockSpec((1, D_MODEL), lambda b, j: (0, 0)),
            pl.BlockSpec((8, D_MODEL, tn), lambda b, j: (0, 0, j)),
            pl.BlockSpec((D_MODEL, LANES), lambda b, j: (0, 0)),
            pl.BlockSpec((1, LANES), lambda b, j: (0, 0)),
            pl.BlockSpec((CONV_WIDTH, tn), lambda b, j: (0, j)),
            pl.BlockSpec((1, tn), lambda b, j: (0, j)),
        ],
        out_specs=[col_out] * 6 + [pl.BlockSpec((1, seq, LANES), lambda b, j: (b, 0, 0))] * 2,
        out_shape=[out_bf16] * 6 + [jax.ShapeDtypeStruct((batch, seq, LANES), BF16)] * 2,
        scratch_shapes=[pltpu.VMEM((seq, D_MODEL), BF16)],
        compiler_params=pltpu.CompilerParams(
            dimension_semantics=("arbitrary", "arbitrary"), vmem_limit_bytes=VMEM_LIMIT),
        name="in_proj",
    )(x2d, g, w_stack, wf, bfv, conv_w, conv_b)


def _attention_kernel(q_ref, k_ref, v_ref, cq_ref, ck_ref, o_ref, va_scr):
    hp = pl.program_id(1)
    seq = q_ref.shape[0]
    blk = ATT_BLOCK
    lane = lax.broadcasted_iota(jnp.int32, (1, LANES), 1)
    zero = jnp.zeros((), BF16)

    own, base, q_mask = [], [], []
    for hh in range(HEADS_PER_BLOCK):
        own.append((lane // HEAD_DIM) == hh)
        base.append(((hh + 1) % HEADS_PER_BLOCK) * HEAD_DIM)
        gate = (lane % N_HEADS) == hp * HEADS_PER_BLOCK + hh
        q_mask.append(jnp.concatenate([own[hh], gate], axis=1))
        va_scr[hh] = jnp.where(own[hh], v_ref[...],
                               jnp.where(lane == base[hh], 1.0, 0.0).astype(BF16))

    def pv(p, keys):
        return jnp.concatenate([_dot(p[hh * blk:(hh + 1) * blk, :], va_scr[hh, keys, :])
                                for hh in range(HEADS_PER_BLOCK)], axis=0)

    for rg in range(seq // blk):
        rows = slice(rg * blk, (rg + 1) * blk)
        q_wide = jnp.concatenate([q_ref[rows, :], cq_ref[0, rows, :]], axis=1)
        qa = jnp.concatenate([jnp.where(q_mask[hh], q_wide, zero) for hh in range(HEADS_PER_BLOCK)], axis=0)
        m = acc = None
        k_end = (rg + 1) * blk
        for k0 in range(0, k_end, ATT_KEYS):
            k1 = min(k0 + ATT_KEYS, k_end)
            keys = slice(k0, k1)
            k_wide = jnp.concatenate([k_ref[keys, :], ck_ref[0, keys, :]], axis=1)
            s = lax.dot_general(qa, k_wide, (((1,), (1,)), ((), ())),
                                preferred_element_type=F32)
            if k1 > rg * blk:
                qpos = rg * blk + lax.broadcasted_iota(jnp.int32, s.shape, 0) % blk
                kpos = k0 + lax.broadcasted_iota(jnp.int32, s.shape, 1)
                s = jnp.where(kpos <= qpos, s, NEG_BIG)
            mx = jnp.max(s, axis=1, keepdims=True)
            if m is None:
                m = mx
                acc = pv(jnp.exp2(s - m).astype(BF16), keys)
            else:
                m_new = jnp.maximum(m, mx)
                acc = jnp.exp2(m - m_new) * acc + pv(jnp.exp2(s - m_new).astype(BF16), keys)
                m = m_new
        out = None
        for hh in range(HEADS_PER_BLOCK):
            a_h = acc[hh * blk:(hh + 1) * blk, :]
            res = a_h / a_h[:, base[hh]:base[hh] + 1]
            out = res if out is None else jnp.where(own[hh], res, out)
        o_ref[rows, :] = out.astype(o_ref.dtype)


def _attention(q, k, v, cq, ck, batch, seq):
    tokens = batch * seq
    n_hp = N_HEADS // HEADS_PER_BLOCK
    blk = pl.BlockSpec((seq, LANES), lambda b, hp: (b, hp))
    gate_blk = pl.BlockSpec((1, seq, LANES), lambda b, hp: (b, 0, 0))
    return pl.pallas_call(
        _attention_kernel,
        grid=(batch, n_hp),
        in_specs=[blk, blk, blk, gate_blk, gate_blk],
        out_specs=blk,
        out_shape=jax.ShapeDtypeStruct((tokens, D_MODEL), BF16),
        scratch_shapes=[pltpu.VMEM((HEADS_PER_BLOCK, seq, LANES), BF16)],
        compiler_params=pltpu.CompilerParams(
            dimension_semantics=("arbitrary", "arbitrary"), vmem_limit_bytes=VMEM_LIMIT),
        name="fox_attention",
    )(q, k, v, cq, ck)


def _mixer_out_kernel(yc_ref, o_ref, sgc_ref, sga_ref, x_ref, wco_ref, wao_ref, wo_ref,
                      g_ref, wr_ref, x1_ref, h2_ref, route_ref, route_t_ref, counts_ref,
                      carry_scr, logit_scr):
    i = pl.program_id(0)
    tm = x_ref.shape[0]

    @pl.when(i == 0)
    def _():
        carry_scr[...] = jnp.zeros_like(carry_scr)
        logit_scr[...] = jnp.zeros_like(logit_scr)

    logits = logit_scr[...]

    def route_previous_tile():
        lane = lax.broadcasted_iota(jnp.int32, (tm, LANES), 1).astype(F32)
        far = jnp.float32(4 * LANES)

        def first_lane_of_max(vals, vmax):
            return jnp.min(jnp.where(vals == vmax, lane, far), axis=1, keepdims=True)

        lg = jnp.where(lane < N_GROUPS, logits, NEG_BIG)
        gmax = jnp.max(lg, axis=1, keepdims=True)
        g_val = 1.0 / jnp.sum(jnp.exp(lg - gmax), axis=1, keepdims=True)
        g_idx = first_lane_of_max(lg, gmax)
        lo = N_GROUPS + EXPERTS_PER_GROUP * g_idx
        le = jnp.where((lane >= lo) & (lane < lo + EXPERTS_PER_GROUP), logits, NEG_BIG)
        e1max = jnp.max(le, axis=1, keepdims=True)
        e1 = first_lane_of_max(le, e1max)
        le2 = jnp.where(lane == e1, NEG_BIG, le)
        e2max = jnp.max(le2, axis=1, keepdims=True)
        e2 = first_lane_of_max(le2, e2max)
        ratio = jnp.exp(e2max - e1max)
        w1 = g_val / (1.0 + ratio)
        w2 = g_val * ratio / (1.0 + ratio)

        oh1 = lane == e1
        oh2 = lane == e2
        onehot = jnp.where(oh1 | oh2, 1.0, 0.0).astype(BF16)
        r_i = lax.broadcasted_iota(jnp.int32, (tm, tm), 0)
        c_i = lax.broadcasted_iota(jnp.int32, (tm, tm), 1)
        strict_lower = jnp.where(c_i < r_i, 1.0, 0.0).astype(BF16)
        carry = carry_scr[...]
        before = _dot(strict_lower, onehot) + carry
        rank1 = jnp.sum(jnp.where(oh1, before, 0.0), axis=1, keepdims=True)
        rank2 = jnp.sum(jnp.where(oh2, before, 0.0), axis=1, keepdims=True)
        is_tile = jnp.where(i > 0, 1.0, 0.0)
        carry = carry + is_tile * jnp.sum(onehot.astype(F32), axis=0, keepdims=True)
        carry_scr[...] = carry
        counts_ref[...] = carry

        route = jnp.where(lane == 0, e1 - N_GROUPS, 0.0)
        route = jnp.where(lane == 1, e2 - N_GROUPS, route)
        route = jnp.where(lane == 2, w1, route)
        route = jnp.where(lane == 3, w2, route)
        route = jnp.where(lane == 4, rank1, route)
        route = jnp.where(lane == 5, rank2, route)
        route_ref[...] = route
        route_t_ref[...] = route.T[0:ROUTE_ROWS, :]

    parts = [slice(c * MIX_PART, (c + 1) * MIX_PART) for c in range(tm // MIX_PART)]
    y_conv = [_dot(yc_ref[p, :], wco_ref[...]) for p in parts]
    y_att = [_dot(o_ref[p, :], wao_ref[...]) for p in parts]
    m = [(sgc_ref[p, :].astype(F32) * yc + sga_ref[p, :].astype(F32) * ya).astype(BF16)
         for p, yc, ya in zip(parts, y_conv, y_att)]
    x1 = [x_ref[p, :] + _dot(mm, wo_ref[...]) for p, mm in zip(parts, m)]
    for p, xx in zip(parts, x1):
        x1_ref[p, :] = xx
    route_previous_tile()
    h2 = [xx * lax.rsqrt(jnp.mean(xx * xx, axis=-1, keepdims=True) + EPS) * g_ref[...] for xx in x1]
    for p, hh in zip(parts, h2):
        h2_ref[p, :] = _pack_bf16_pairs(hh)
        logit_scr[p, :] = _dot(hh.astype(BF16), wr_ref[...])


def _mixer_out(yc, o, sgc, sga, x2d, wco, wao, wo, g, wr):
    tokens = x2d.shape[0]
    tm = MIX_ROWS
    n_tiles = tokens // tm

    def cur(i):
        return (jnp.minimum(i, n_tiles - 1), 0)

    def prev(i):
        return (jnp.maximum(i - 1, 0), 0)

    row_blk = pl.BlockSpec((tm, D_MODEL), cur)
    w_blk = pl.BlockSpec((D_MODEL, D_MODEL), lambda i: (0, 0))
    return pl.pallas_call(
        _mixer_out_kernel,
        grid=(n_tiles + 1,),
        in_specs=[row_blk, row_blk, row_blk, row_blk, row_blk, w_blk, w_blk, w_blk,
                  pl.BlockSpec((1, D_MODEL), lambda i: (0, 0)),
                  pl.BlockSpec((D_MODEL, LANES), lambda i: (0, 0))],
        out_specs=[row_blk,
                   pl.BlockSpec((tm, D_MODEL // 2), cur),
                   pl.BlockSpec((tm, LANES), prev),
                   pl.BlockSpec((ROUTE_ROWS, tm), lambda i: (0, jnp.maximum(i - 1, 0))),
                   pl.BlockSpec((1, LANES), lambda i: (0, 0))],
        out_shape=[jax.ShapeDtypeStruct((tokens, D_MODEL), F32),
                   jax.ShapeDtypeStruct((tokens, D_MODEL // 2), jnp.uint32),
                   jax.ShapeDtypeStruct((tokens, LANES), F32),
                   jax.ShapeDtypeStruct((ROUTE_ROWS, tokens), F32),
                   jax.ShapeDtypeStruct((1, LANES), F32)],
        scratch_shapes=[pltpu.VMEM((1, LANES), F32), pltpu.VMEM((tm, LANES), F32)],
        compiler_params=pltpu.CompilerParams(
            dimension_semantics=("arbitrary",), vmem_limit_bytes=VMEM_LIMIT),
        name="mixer_out",
    )(yc, o, sgc, sga, x2d, wco, wao, wo, g, wr)


def _sc_index_rows(indices):
    n = indices.shape[0]
    return jnp.pad(indices.reshape(n // SC_WINDOW, SC_WINDOW), ((0, 0), (0, LANES - SC_WINDOW)))


def _sc_mesh():
    return plsc.VectorSubcoreMesh(core_axis_name="c", subcore_axis_name="s")


def _sc_scatter_rows(data, indices, n_out):
    n = indices.shape[0]
    n_src, width = data.shape
    src_blocks = n_src // SC_WINDOW

    @pl.kernel(out_type=jax.ShapeDtypeStruct((n_out, width), data.dtype), mesh=_sc_mesh())
    def scatter_kernel(x_hbm, i_hbm, o_hbm):
        def body(x_vmem, i_vmem):
            pltpu.sync_copy(x_vmem, o_hbm.at[i_vmem.at[0, pl.ds(0, SC_WINDOW)]])

        pltpu.emit_pipeline(
            body,
            grid=(n // SC_WINDOW,),
            in_specs=[pl.BlockSpec((SC_WINDOW, width), lambda i: (i % src_blocks, 0)),
                      pl.BlockSpec((1, LANES), lambda i: (i, 0))],
            out_specs=[],
            core_axis_name=("c", "s"),
            dimension_semantics=(pltpu.PARALLEL,),
        )(x_hbm, i_hbm)

    return scatter_kernel(data, _sc_index_rows(indices))


def _sc_gather_rows(data, indices):
    n = indices.shape[0]
    width = data.shape[1]

    @pl.kernel(out_type=jax.ShapeDtypeStruct((n, width), data.dtype), mesh=_sc_mesh())
    def gather_kernel(x_hbm, i_hbm, o_hbm):
        def body(i_vmem, o_vmem):
            pltpu.sync_copy(x_hbm.at[i_vmem.at[0, pl.ds(0, SC_WINDOW)]], o_vmem)

        pltpu.emit_pipeline(
            body,
            grid=(n // SC_WINDOW,),
            in_specs=[pl.BlockSpec((1, LANES), lambda i: (i, 0))],
            out_specs=[pl.BlockSpec((SC_WINDOW, width), lambda i: (i, 0))],
            core_axis_name=("c", "s"),
            dimension_semantics=(pltpu.PARALLEL,),
        )(i_hbm, o_hbm)

    return gather_kernel(data, _sc_index_rows(indices))


def _expert_kernel(block_e_ref, n_valid_ref, x_ref, wg_ref, wu_ref, wd_ref, y_ref,
                   wg_scr, wu_scr, wd_scr):
    i = pl.program_id(0)
    n_valid = n_valid_ref[i]

    @pl.when((i == 0) | (block_e_ref[i] != block_e_ref[jnp.maximum(i - 1, 0)]))
    def _():
        wg_scr[...] = wg_ref[0].astype(BF16)
        wu_scr[...] = wu_ref[0].astype(BF16)
        wd_scr[...] = wd_ref[0].astype(BF16)

    @pl.when(n_valid > 0)
    def _():
        row = lax.broadcasted_iota(jnp.int32, x_ref.shape, 0)
        packed = jnp.where(row < n_valid, x_ref[...], jnp.uint32(0))
        xb = _unpack_bf16_pairs(packed).astype(BF16)
        a = _dot(xb, wg_scr[...])
        u = _dot(xb, wu_scr[...])
        hmid = (a * jax.nn.sigmoid(a) * u).astype(BF16)
        y_ref[...] = _pack_bf16_pairs(_dot(hmid, wd_scr[...]))

    @pl.when(n_valid == 0)
    def _():
        y_ref[...] = jnp.zeros_like(y_ref)


def _experts(block_e, n_valid, x_rows, wg, wu, wd):
    n_rows, half = x_rows.shape
    n_blocks = n_rows // MOE_BLOCK

    def w_map(i, be, nv):
        return (be[i], 0, 0)

    row_blk = pl.BlockSpec((MOE_BLOCK, half), lambda i, be, nv: (i, 0))
    grid_spec = pltpu.PrefetchScalarGridSpec(
        num_scalar_prefetch=2,
        grid=(n_blocks,),
        in_specs=[row_blk,
                  pl.BlockSpec((1, D_MODEL, D_EXPERT), w_map),
                  pl.BlockSpec((1, D_MODEL, D_EXPERT), w_map),
                  pl.BlockSpec((1, D_EXPERT, D_MODEL), w_map)],
        out_specs=row_blk,
        scratch_shapes=[pltpu.VMEM((D_MODEL, D_EXPERT), BF16), pltpu.VMEM((D_MODEL, D_EXPERT), BF16),
                        pltpu.VMEM((D_EXPERT, D_MODEL), BF16)],
    )
    return pl.pallas_call(
        _expert_kernel,
        grid_spec=grid_spec,
        out_shape=jax.ShapeDtypeStruct((n_rows, half), jnp.uint32),
        compiler_params=pltpu.CompilerParams(
            dimension_semantics=("arbitrary",), vmem_limit_bytes=VMEM_LIMIT),
        name="moe_experts",
    )(block_e, n_valid, x_rows, wg, wu, wd)


def _combine_kernel(ya_ref, yb_ref, x1_ref, route_ref, g_ref, out_ref):
    route = route_ref[...]
    w1 = route[:, 2:3]
    w2 = route[:, 3:4]
    x2 = x1_ref[...] + (_unpack_bf16_pairs(ya_ref[...]) * w1 + _unpack_bf16_pairs(yb_ref[...]) * w2)
    ms = jnp.mean(x2 * x2, axis=-1, keepdims=True)
    out_ref[...] = x2 * lax.rsqrt(ms + EPS) * g_ref[...]


def _combine(y_pairs, x1, route, g):
    tokens = x1.shape[0]
    tc = COMBINE_ROWS
    n_tiles = tokens // tc
    row_blk = pl.BlockSpec((tc, D_MODEL), lambda i: (i, 0))
    return pl.pallas_call(
        _combine_kernel,
        grid=(n_tiles,),
        in_specs=[pl.BlockSpec((tc, D_MODEL // 2), lambda i: (i, 0)),
                  pl.BlockSpec((tc, D_MODEL // 2), lambda i: (i + n_tiles, 0)),
                  row_blk,
                  pl.BlockSpec((tc, LANES), lambda i: (i, 0)),
                  pl.BlockSpec((1, D_MODEL), lambda i: (0, 0))],
        out_specs=row_blk,
        out_shape=jax.ShapeDtypeStruct((tokens, D_MODEL), F32),
        compiler_params=pltpu.CompilerParams(
            dimension_semantics=("arbitrary",), vmem_limit_bytes=VMEM_LIMIT),
        name="moe_combine",
    )(y_pairs, y_pairs, x1, route, g)


def kernel(x, norm_mix_g, w_in, conv_w, conv_b, b_forget, w_conv_out, w_att_out, w_out,
           norm_ffn_g, w_router_group, w_router_expert, w_e_gate, w_e_up, w_e_down,
           norm_final_g):
    batch, seq, d = x.shape
    assert d == D_MODEL and seq % max(ATT_BLOCK, IN_PROJ_ROWS) == 0
    tokens = batch * seq
    assert tokens % MIX_ROWS == 0 and tokens % COMBINE_ROWS == 0 and tokens % SC_WINDOW == 0
    x2d = x.reshape(tokens, D_MODEL)

    wb = w_in.astype(BF16)
    n_main = 6 * D_MODEL
    gate0 = n_main + N_HEADS
    pieces = [wb[:, i * D_MODEL:(i + 1) * D_MODEL] for i in range(6)]
    pieces += [wb[:, gate0:gate0 + D_MODEL], wb[:, gate0 + D_MODEL:gate0 + 2 * D_MODEL]]
    w_stack = jnp.stack(pieces, axis=0)
    n_rep = 6
    wf = jnp.pad(jnp.tile(wb[:, n_main:gate0], (1, n_rep)), ((0, 0), (0, LANES - n_rep * N_HEADS)))
    bfv = jnp.pad(jnp.tile(b_forget.astype(F32), n_rep), (0, LANES - n_rep * N_HEADS)).reshape(1, LANES)
    wr = jnp.concatenate(
        [w_router_group, jnp.transpose(w_router_expert, (1, 0, 2)).reshape(D_MODEL, N_EXPERTS)], axis=1)
    wr = jnp.pad(wr, ((0, 0), (0, LANES - wr.shape[1]))).astype(BF16)

    yc, q, k, v, sgc, sga, cq, ck = _in_proj(
        x2d, norm_mix_g.reshape(1, D_MODEL), w_stack, wf, bfv, conv_w,
        conv_b.reshape(1, D_MODEL), batch, seq)
    o = _attention(q, k, v, cq, ck, batch, seq)

    x1, h2, route, route_t, counts = _mixer_out(
        yc, o, sgc, sga, x2d, w_conv_out.astype(BF16), w_att_out.astype(BF16),
        w_out.astype(BF16), norm_ffn_g.reshape(1, D_MODEL), wr)

    counts = counts[0, N_GROUPS:N_GROUPS + N_EXPERTS].astype(jnp.int32)
    padded = (counts + MOE_BLOCK - 1) // MOE_BLOCK * MOE_BLOCK
    end_padded = jnp.cumsum(padded)
    start_padded = end_padded - padded
    expert = route_t[0:2].astype(jnp.int32)
    rank = route_t[4:6].astype(jnp.int32)
    expert_ids = jnp.arange(N_EXPERTS, dtype=jnp.int32)[:, None, None]
    dest = rank + jnp.sum(jnp.where(expert[None] == expert_ids, start_padded[:, None, None], 0), axis=0)
    dest_flat = dest.reshape(2 * tokens)
    n_rows = tokens * 2 + N_EXPERTS * MOE_BLOCK
    block_start = jnp.arange(n_rows // MOE_BLOCK, dtype=jnp.int32) * MOE_BLOCK
    block_e = jnp.minimum(jnp.sum(end_padded[None, :] <= block_start[:, None], axis=1),
                          N_EXPERTS - 1).astype(jnp.int32)
    used_lo = jnp.maximum(start_padded[None, :], block_start[:, None])
    used_hi = jnp.minimum((start_padded + counts)[None, :], block_start[:, None] + MOE_BLOCK)
    n_valid = jnp.sum(jnp.maximum(used_hi - used_lo, 0), axis=1).astype(jnp.int32)

    x_rows = _sc_scatter_rows(h2, dest_flat, n_rows)
    y_rows = _experts(block_e, n_valid, x_rows, w_e_gate, w_e_up, w_e_down)
    y_pairs = _sc_gather_rows(y_rows, dest_flat)
    out = _combine(y_pairs, x1, route, norm_final_g.reshape(1, D_MODEL))
    return out.reshape(batch, seq, D_MODEL)
```

```python
import jax
import jax.numpy as jnp
from jax import lax
from jax.experimental import pallas as pl
from jax.experimental.pallas import tpu as pltpu
from jax.experimental.pallas import tpu_sc as plsc

D_MODEL = 1024
HEAD_DIM = 64
N_HEADS = 16
N_GROUPS = 4
EXPERTS_PER_GROUP = 8
N_EXPERTS = N_GROUPS * EXPERTS_PER_GROUP
D_EXPERT = 512
MOE_BLOCK = 1024
CONV_WIDTH = 3
EPS = 1e-6

LANES = 128
HEADS_PER_BLOCK = LANES // HEAD_DIM
NEG_BIG = -1e30
LOG2E = 1.4426950408889634

IN_PROJ_COLS = 256
IN_PROJ_ROWS = 256
ATT_BLOCK = 256
ATT_KEYS = 256
MIX_ROWS = 512
MIX_PART = 256
COMBINE_ROWS = 1024
ROUTE_ROWS = 8
SC_WINDOW = 64
VMEM_LIMIT = 56 * 1024 * 1024

F32 = jnp.float32
BF16 = jnp.bfloat16


def _dot(a, b):
    return jnp.dot(a, b, preferred_element_type=F32)


def _pack_bf16_pairs(x):
    c = x.shape[1] // 2
    lo = lax.bitcast_convert_type(x[:, :c].astype(BF16).astype(F32), jnp.uint32) >> 16
    hi = lax.bitcast_convert_type(x[:, c:].astype(BF16).astype(F32), jnp.uint32) & jnp.uint32(0xFFFF0000)
    return lo | hi


def _unpack_bf16_pairs(w):
    lo = lax.bitcast_convert_type(w << 16, F32)
    hi = lax.bitcast_convert_type(w & jnp.uint32(0xFFFF0000), F32)
    return jnp.concatenate([lo, hi], axis=1)


def _in_proj_kernel(x_ref, g_ref, w_ref, wf_ref, bf_ref, cw_ref, cb_ref,
                    yc_ref, q_ref, k_ref, v_ref, sgc_ref, sga_ref, cq_ref, ck_ref, h_scr):
    j = pl.program_id(1)
    seq = x_ref.shape[0]
    n_chunks = seq // IN_PROJ_ROWS

    @pl.when(j == 0)
    def _():
        for r in range(n_chunks):
            rows = slice(r * IN_PROJ_ROWS, (r + 1) * IN_PROJ_ROWS)
            xs = x_ref[rows, :]
            ms = jnp.mean(xs * xs, axis=-1, keepdims=True)
            h_scr[rows, :] = (xs * lax.rsqrt(ms + EPS) * g_ref[...]).astype(BF16)
        f = _dot(h_scr[...], wf_ref[...]) + bf_ref[...]
        c = jnp.minimum(f, 0.0) - jnp.log(1.0 + jnp.exp(-jnp.abs(f)))
        row = lax.broadcasted_iota(jnp.int32, c.shape, 0)
        d = 1
        while d < seq:
            c = c + jnp.where(row >= d, pltpu.roll(c, d, axis=0), 0.0)
            d *= 2
        c = c * LOG2E
        hi = c.astype(BF16).astype(F32)
        mid = (c - hi).astype(BF16).astype(F32)
        lo = (c - hi - mid).astype(BF16).astype(F32)
        grp = lax.broadcasted_iota(jnp.int32, c.shape, 1) // N_HEADS
        terms = jnp.where(grp % 3 == 0, hi, jnp.where(grp % 3 == 1, mid, lo))
        one = jnp.float32(1.0)
        cq_ref[0] = jnp.where(grp < 3, terms, jnp.where(grp < 6, one, 0.0)).astype(BF16)
        ck_ref[0] = jnp.where(grp < 3, one, jnp.where(grp < 6, -terms, 0.0)).astype(BF16)

    cw0 = cw_ref[0:1, :]
    cw1 = cw_ref[1:2, :]
    cw2 = cw_ref[2:3, :]
    cb = cb_ref[...]
    rowc = lax.broadcasted_iota(jnp.int32, (IN_PROJ_ROWS, IN_PROJ_COLS), 0)
    zprev = None
    for r in range(n_chunks):
        rows = slice(r * IN_PROJ_ROWS, (r + 1) * IN_PROJ_ROWS)
        hs = h_scr[rows, :]
        cb_gate = _dot(hs, w_ref[0])
        z = _dot(hs, w_ref[1]) * _dot(hs, w_ref[2])
        z1 = pltpu.roll(z, 1, axis=0)
        z2 = pltpu.roll(z, 2, axis=0)
        if zprev is None:
            p1 = jnp.zeros_like(z)
            p2 = p1
        else:
            p1 = pltpu.roll(zprev, 1, axis=0)
            p2 = pltpu.roll(zprev, 2, axis=0)
        z1 = jnp.where(rowc < 1, p1, z1)
        z2 = jnp.where(rowc < 2, p2, z2)
        acc = cb + cw0 * z2 + cw1 * z1 + cw2 * z
        yc_ref[rows, :] = (cb_gate * acc).astype(BF16)
        zprev = z
        q_ref[rows, :] = (_dot(hs, w_ref[3]) * (LOG2E * HEAD_DIM ** -0.5)).astype(BF16)
        k_ref[rows, :] = _dot(hs, w_ref[4]).astype(BF16)
        v_ref[rows, :] = _dot(hs, w_ref[5]).astype(BF16)
        sgc_ref[rows, :] = jax.nn.sigmoid(_dot(hs, w_ref[6])).astype(BF16)
        sga_ref[rows, :] = jax.nn.sigmoid(_dot(hs, w_ref[7])).astype(BF16)


def _in_proj(x2d, g, w_stack, wf, bfv, conv_w, conv_b, batch, seq):
    tokens = batch * seq
    tn = IN_PROJ_COLS
    nj = D_MODEL // tn
    col_out = pl.BlockSpec((seq, tn), lambda b, j: (b, j))
    out_bf16 = jax.ShapeDtypeStruct((tokens, D_MODEL), BF16)
    return pl.pallas_call(
        _in_proj_kernel,
        grid=(batch, nj),
        in_specs=[
            pl.BlockSpec((seq, D_MODEL), lambda b, j: (b, 0)),
            pl.BlockSpec((1, D_MODEL), lambda b, j: (0, 0)),
            pl.BlockSpec((8, D_MODEL, tn), lambda b, j: (0, 0, j)),
            pl.BlockSpec((D_MODEL, LANES), lambda b, j: (0, 0)),
            pl.BlockSpec((1, LANES), lambda b, j: (0, 0)),
            pl.BlockSpec((CONV_WIDTH, tn), lambda b, j: (0, j)),
            pl.BlockSpec((1, tn), lambda b, j: (0, j)),
        ],
        out_specs=[col_out] * 6 + [pl.BlockSpec((1, seq, LANES), lambda b, j: (b, 0, 0))] * 2,
        out_shape=[out_bf16] * 6 + [jax.ShapeDtypeStruct((batch, seq, LANES), BF16)] * 2,
        scratch_shapes=[pltpu.VMEM((seq, D_MODEL), BF16)],
        compiler_params=pltpu.CompilerParams(
            dimension_semantics=("arbitrary", "arbitrary"), vmem_limit_bytes=VMEM_LIMIT),
        name="in_proj",
    )(x2d, g, w_stack, wf, bfv, conv_w, conv_b)


def _attention_kernel(q_ref, k_ref, v_ref, cq_ref, ck_ref, o_ref, va_scr):
    hp = pl.program_id(1)
    seq = q_ref.shape[0]
    blk = ATT_BLOCK
    lane = lax.broadcasted_iota(jnp.int32, (1, LANES), 1)
    zero = jnp.zeros((), BF16)

    own, base, q_mask = [], [], []
    for hh in range(HEADS_PER_BLOCK):
        own.append((lane // HEAD_DIM) == hh)
        base.append(((hh + 1) % HEADS_PER_BLOCK) * HEAD_DIM)
        gate = (lane % N_HEADS) == hp * HEADS_PER_BLOCK + hh
        q_mask.append(jnp.concatenate([own[hh], gate], axis=1))
        va_scr[hh] = jnp.where(own[hh], v_ref[...],
                               jnp.where(lane == base[hh], 1.0, 0.0).astype(BF16))

    def pv(p, keys):
        return jnp.concatenate([_dot(p[hh * blk:(hh + 1) * blk, :], va_scr[hh, keys, :])
                                for hh in range(HEADS_PER_BLOCK)], axis=0)

    for rg in range(seq // blk):
        rows = slice(rg * blk, (rg + 1) * blk)
        q_wide = jnp.concatenate([q_ref[rows, :], cq_ref[0, rows, :]], axis=1)
        qa = jnp.concatenate([jnp.where(q_mask[hh], q_wide, zero) for hh in range(HEADS_PER_BLOCK)], axis=0)
        m = acc = None
        k_end = (rg + 1) * blk
        for k0 in range(0, k_end, ATT_KEYS):
            k1 = min(k0 + ATT_KEYS, k_end)
            keys = slice(k0, k1)
            k_wide = jnp.concatenate([k_ref[keys, :], ck_ref[0, keys, :]], axis=1)
            s = lax.dot_general(qa, k_wide, (((1,), (1,)), ((), ())),
                                preferred_element_type=F32)
            if k1 > rg * blk:
                qpos = rg * blk + lax.broadcasted_iota(jnp.int32, s.shape, 0) % blk
                kpos = k0 + lax.broadcasted_iota(jnp.int32, s.shape, 1)
                s = jnp.where(kpos <= qpos, s, NEG_BIG)
            mx = jnp.max(s, axis=1, keepdims=True)
            if m is None:
                m = mx
                acc = pv(jnp.exp2(s - m).astype(BF16), keys)
            else:
                m_new = jnp.maximum(m, mx)
                acc = jnp.exp2(m - m_new) * acc + pv(jnp.exp2(s - m_new).astype(BF16), keys)
                m = m_new
        out = None
        for hh in range(HEADS_PER_BLOCK):
            a_h = acc[hh * blk:(hh + 1) * blk, :]
            res = a_h / a_h[:, base[hh]:base[hh] + 1]
            out = res if out is None else jnp.where(own[hh], res, out)
        o_ref[rows, :] = out.astype(o_ref.dtype)


def _attention(q, k, v, cq, ck, batch, seq):
    tokens = batch * seq
    n_hp = N_HEADS // HEADS_PER_BLOCK
    blk = pl.BlockSpec((seq, LANES), lambda b, hp: (b, hp))
    gate_blk = pl.BlockSpec((1, seq, LANES), lambda b, hp: (b, 0, 0))
    return pl.pallas_call(
        _attention_kernel,
        grid=(batch, n_hp),
        in_specs=[blk, blk, blk, gate_blk, gate_blk],
        out_specs=blk,
        out_shape=jax.ShapeDtypeStruct((tokens, D_MODEL), BF16),
        scratch_shapes=[pltpu.VMEM((HEADS_PER_BLOCK, seq, LANES), BF16)],
        compiler_params=pltpu.CompilerParams(
            dimension_semantics=("arbitrary", "arbitrary"), vmem_limit_bytes=VMEM_LIMIT),
        name="fox_attention",
    )(q, k, v, cq, ck)


def _mixer_out_kernel(yc_ref, o_ref, sgc_ref, sga_ref, x_ref, wco_ref, wao_ref, wo_ref,
                      g_ref, wr_ref, x1_ref, h2_ref, route_ref, route_t_ref, counts_ref,
                      carry_scr, logit_scr):
    i = pl.program_id(0)
    tm = x_ref.shape[0]

    @pl.when(i == 0)
    def _():
        carry_scr[...] = jnp.zeros_like(carry_scr)
        logit_scr[...] = jnp.zeros_like(logit_scr)

    logits = logit_scr[...]

    def route_previous_tile():
        lane = lax.broadcasted_iota(jnp.int32, (tm, LANES), 1).astype(F32)
        far = jnp.float32(4 * LANES)

        def first_lane_of_max(vals, vmax):
            return jnp.min(jnp.where(vals == vmax, lane, far), axis=1, keepdims=True)

        lg = jnp.where(lane < N_GROUPS, logits, NEG_BIG)
        gmax = jnp.max(lg, axis=1, keepdims=True)
        g_val = 1.0 / jnp.sum(jnp.exp(lg - gmax), axis=1, keepdims=True)
        g_idx = first_lane_of_max(lg, gmax)
        lo = N_GROUPS + EXPERTS_PER_GROUP * g_idx
        le = jnp.where((lane >= lo) & (lane < lo + EXPERTS_PER_GROUP), logits, NEG_BIG)
        e1max = jnp.max(le, axis=1, keepdims=True)
        e1 = first_lane_of_max(le, e1max)
        le2 = jnp.where(lane == e1, NEG_BIG, le)
        e2max = jnp.max(le2, axis=1, keepdims=True)
        e2 = first_lane_of_max(le2, e2max)
        ratio = jnp.exp(e2max - e1max)
        w1 = g_val / (1.0 + ratio)
        w2 = g_val * ratio / (1.0 + ratio)

        oh1 = lane == e1
        oh2 = lane == e2
        onehot = jnp.where(oh1 | oh2, 1.0, 0.0).astype(BF16)
        r_i = lax.broadcasted_iota(jnp.int32, (tm, tm), 0)
        c_i = lax.broadcasted_iota(jnp.int32, (tm, tm), 1)
        strict_lower = jnp.where(c_i < r_i, 1.0, 0.0).astype(BF16)
        carry = carry_scr[...]
        before = _dot(strict_lower, onehot) + carry
        rank1 = jnp.sum(jnp.where(oh1, before, 0.0), axis=1, keepdims=True)
        rank2 = jnp.sum(jnp.where(oh2, before, 0.0), axis=1, keepdims=True)
        is_tile = jnp.where(i > 0, 1.0, 0.0)
        carry = carry + is_tile * jnp.sum(onehot.astype(F32), axis=0, keepdims=True)
        carry_scr[...] = carry
        counts_ref[...] = carry

        route = jnp.where(lane == 0, e1 - N_GROUPS, 0.0)
        route = jnp.where(lane == 1, e2 - N_GROUPS, route)
        route = jnp.where(lane == 2, w1, route)
        route = jnp.where(lane == 3, w2, route)
        route = jnp.where(lane == 4, rank1, route)
        route = jnp.where(lane == 5, rank2, route)
        route_ref[...] = route
        route_t_ref[...] = route.T[0:ROUTE_ROWS, :]

    parts = [slice(c * MIX_PART, (c + 1) * MIX_PART) for c in range(tm // MIX_PART)]
    y_conv = [_dot(yc_ref[p, :], wco_ref[...]) for p in parts]
    y_att = [_dot(o_ref[p, :], wao_ref[...]) for p in parts]
    m = [(sgc_ref[p, :].astype(F32) * yc + sga_ref[p, :].astype(F32) * ya).astype(BF16)
         for p, yc, ya in zip(parts, y_conv, y_att)]
    x1 = [x_ref[p, :] + _dot(mm, wo_ref[...]) for p, mm in zip(parts, m)]
    for p, xx in zip(parts, x1):
        x1_ref[p, :] = xx
    route_previous_tile()
    h2 = [xx * lax.rsqrt(jnp.mean(xx * xx, axis=-1, keepdims=True) + EPS) * g_ref[...] for xx in x1]
    for p, hh in zip(parts, h2):
        h2_ref[p, :] = _pack_bf16_pairs(hh)
        logit_scr[p, :] = _dot(hh.astype(BF16), wr_ref[...])


def _mixer_out(yc, o, sgc, sga, x2d, wco, wao, wo, g, wr):
    tokens = x2d.shape[0]
    tm = MIX_ROWS
    n_tiles = tokens // tm

    def cur(i):
        return (jnp.minimum(i, n_tiles - 1), 0)

    def prev(i):
        return (jnp.maximum(i - 1, 0), 0)

    row_blk = pl.BlockSpec((tm, D_MODEL), cur)
    w_blk = pl.BlockSpec((D_MODEL, D_MODEL), lambda i: (0, 0))
    return pl.pallas_call(
        _mixer_out_kernel,
        grid=(n_tiles + 1,),
        in_specs=[row_blk, row_blk, row_blk, row_blk, row_blk, w_blk, w_blk, w_blk,
                  pl.BlockSpec((1, D_MODEL), lambda i: (0, 0)),
                  pl.BlockSpec((D_MODEL, LANES), lambda i: (0, 0))],
        out_specs=[row_blk,
                   pl.BlockSpec((tm, D_MODEL // 2), cur),
                   pl.BlockSpec((tm, LANES), prev),
                   pl.BlockSpec((ROUTE_ROWS, tm), lambda i: (0, jnp.maximum(i - 1, 0))),
                   pl.BlockSpec((1, LANES), lambda i: (0, 0))],
        out_shape=[jax.ShapeDtypeStruct((tokens, D_MODEL), F32),
                   jax.ShapeDtypeStruct((tokens, D_MODEL // 2), jnp.uint32),
                   jax.ShapeDtypeStruct((tokens, LANES), F32),
                   jax.ShapeDtypeStruct((ROUTE_ROWS, tokens), F32),
                   jax.ShapeDtypeStruct((1, LANES), F32)],
        scratch_shapes=[pltpu.VMEM((1, LANES), F32), pltpu.VMEM((tm, LANES), F32)],
        compiler_params=pltpu.CompilerParams(
            dimension_semantics=("arbitrary",), vmem_limit_bytes=VMEM_LIMIT),
        name="mixer_out",
    )(yc, o, sgc, sga, x2d, wco, wao, wo, g, wr)


def _sc_index_rows(indices):
    n = indices.shape[0]
    return jnp.pad(indices.reshape(n // SC_WINDOW, SC_WINDOW), ((0, 0), (0, LANES - SC_WINDOW)))


def _sc_mesh():
    return plsc.VectorSubcoreMesh(core_axis_name="c", subcore_axis_name="s")


def _sc_scatter_rows(data, indices, n_out):
    n = indices.shape[0]
    n_src, width = data.shape
    src_blocks = n_src // SC_WINDOW

    @pl.kernel(out_type=jax.ShapeDtypeStruct((n_out, width), data.dtype), mesh=_sc_mesh())
    def scatter_kernel(x_hbm, i_hbm, o_hbm):
        def body(x_vmem, i_vmem):
            pltpu.sync_copy(x_vmem, o_hbm.at[i_vmem.at[0, pl.ds(0, SC_WINDOW)]])

        pltpu.emit_pipeline(
            body,
            grid=(n // SC_WINDOW,),
            in_specs=[pl.BlockSpec((SC_WINDOW, width), lambda i: (i % src_blocks, 0)),
                      pl.BlockSpec((1, LANES), lambda i: (i, 0))],
            out_specs=[],
            core_axis_name=("c", "s"),
            dimension_semantics=(pltpu.PARALLEL,),
        )(x_hbm, i_hbm)

    return scatter_kernel(data, _sc_index_rows(indices))


def _sc_gather_rows(data, indices):
    n = indices.shape[0]
    width = data.shape[1]

    @pl.kernel(out_type=jax.ShapeDtypeStruct((n, width), data.dtype), mesh=_sc_mesh())
    def gather_kernel(x_hbm, i_hbm, o_hbm):
        def body(i_vmem, o_vmem):
            pltpu.sync_copy(x_hbm.at[i_vmem.at[0, pl.ds(0, SC_WINDOW)]], o_vmem)

        pltpu.emit_pipeline(
            body,
            grid=(n // SC_WINDOW,),
            in_specs=[pl.BlockSpec((1, LANES), lambda i: (i, 0))],
            out_specs=[pl.BlockSpec((SC_WINDOW, width), lambda i: (i, 0))],
            core_axis_name=("c", "s"),
            dimension_semantics=(pltpu.PARALLEL,),
        )(i_hbm, o_hbm)

    return gather_kernel(data, _sc_index_rows(indices))


def _expert_kernel(block_e_ref, n_valid_ref, x_ref, wg_ref, wu_ref, wd_ref, y_ref,
                   wg_scr, wu_scr, wd_scr):
    i = pl.program_id(0)
    n_valid = n_valid_ref[i]

    @pl.when((i == 0) | (block_e_ref[i] != block_e_ref[jnp.maximum(i - 1, 0)]))
    def _():
        wg_scr[...] = wg_ref[0].astype(BF16)
        wu_scr[...] = wu_ref[0].astype(BF16)
        wd_scr[...] = wd_ref[0].astype(BF16)

    def swiglu(n_rows):
        row = lax.broadcasted_iota(jnp.int32, (n_rows, x_ref.shape[1]), 0)
        packed = jnp.where(row < n_valid, x_ref[0:n_rows, :], jnp.uint32(0))
        xb = _unpack_bf16_pairs(packed).astype(BF16)
        a = _dot(xb, wg_scr[...])
        u = _dot(xb, wu_scr[...])
        hmid = (a * jax.nn.sigmoid(a) * u).astype(BF16)
        y_ref[0:n_rows, :] = _pack_bf16_pairs(_dot(hmid, wd_scr[...]))

    half_rows = MOE_BLOCK // 2

    @pl.when(n_valid > half_rows)
    def _():
        swiglu(MOE_BLOCK)

    @pl.when((n_valid > 0) & (n_valid <= half_rows))
    def _():
        swiglu(half_rows)
        y_ref[half_rows:, :] = jnp.zeros((MOE_BLOCK - half_rows, y_ref.shape[1]), y_ref.dtype)

    @pl.when(n_valid == 0)
    def _():
        y_ref[...] = jnp.zeros_like(y_ref)


def _experts(block_e, n_valid, x_rows, wg, wu, wd):
    n_rows, half = x_rows.shape
    n_blocks = n_rows // MOE_BLOCK

    def w_map(i, be, nv):
        return (be[i], 0, 0)

    row_blk = pl.BlockSpec((MOE_BLOCK, half), lambda i, be, nv: (i, 0))
    grid_spec = pltpu.PrefetchScalarGridSpec(
        num_scalar_prefetch=2,
        grid=(n_blocks,),
        in_specs=[row_blk,
                  pl.BlockSpec((1, D_MODEL, D_EXPERT), w_map),
                  pl.BlockSpec((1, D_MODEL, D_EXPERT), w_map),
                  pl.BlockSpec((1, D_EXPERT, D_MODEL), w_map)],
        out_specs=row_blk,
        scratch_shapes=[pltpu.VMEM((D_MODEL, D_EXPERT), BF16), pltpu.VMEM((D_MODEL, D_EXPERT), BF16),
                        pltpu.VMEM((D_EXPERT, D_MODEL), BF16)],
    )
    return pl.pallas_call(
        _expert_kernel,
        grid_spec=grid_spec,
        out_shape=jax.ShapeDtypeStruct((n_rows, half), jnp.uint32),
        compiler_params=pltpu.CompilerParams(
            dimension_semantics=("arbitrary",), vmem_limit_bytes=VMEM_LIMIT),
        name="moe_experts",
    )(block_e, n_valid, x_rows, wg, wu, wd)


def _combine_kernel(ya_ref, yb_ref, x1_ref, route_ref, g_ref, out_ref):
    route = route_ref[...]
    w1 = route[:, 2:3]
    w2 = route[:, 3:4]
    x2 = x1_ref[...] + (_unpack_bf16_pairs(ya_ref[...]) * w1 + _unpack_bf16_pairs(yb_ref[...]) * w2)
    ms = jnp.mean(x2 * x2, axis=-1, keepdims=True)
    out_ref[...] = x2 * lax.rsqrt(ms + EPS) * g_ref[...]


def _combine(y_pairs, x1, route, g):
    tokens = x1.shape[0]
    tc = COMBINE_ROWS
    n_tiles = tokens // tc
    row_blk = pl.BlockSpec((tc, D_MODEL), lambda i: (i, 0))
    return pl.pallas_call(
        _combine_kernel,
        grid=(n_tiles,),
        in_specs=[pl.BlockSpec((tc, D_MODEL // 2), lambda i: (i, 0)),
                  pl.BlockSpec((tc, D_MODEL // 2), lambda i: (i + n_tiles, 0)),
                  row_blk,
                  pl.BlockSpec((tc, LANES), lambda i: (i, 0)),
                  pl.BlockSpec((1, D_MODEL), lambda i: (0, 0))],
        out_specs=row_blk,
        out_shape=jax.ShapeDtypeStruct((tokens, D_MODEL), F32),
        compiler_params=pltpu.CompilerParams(
            dimension_semantics=("arbitrary",), vmem_limit_bytes=VMEM_LIMIT),
        name="moe_combine",
    )(y_pairs, y_pairs, x1, route, g)


def kernel(x, norm_mix_g, w_in, conv_w, conv_b, b_forget, w_conv_out, w_att_out, w_out,
           norm_ffn_g, w_router_group, w_router_expert, w_e_gate, w_e_up, w_e_down,
           norm_final_g):
    batch, seq, d = x.shape
    assert d == D_MODEL and seq % max(ATT_BLOCK, IN_PROJ_ROWS) == 0
    tokens = batch * seq
    assert tokens % MIX_ROWS == 0 and tokens % COMBINE_ROWS == 0 and tokens % SC_WINDOW == 0
    x2d = x.reshape(tokens, D_MODEL)

    wb = w_in.astype(BF16)
    n_main = 6 * D_MODEL
    gate0 = n_main + N_HEADS
    pieces = [wb[:, i * D_MODEL:(i + 1) * D_MODEL] for i in range(6)]
    pieces += [wb[:, gate0:gate0 + D_MODEL], wb[:, gate0 + D_MODEL:gate0 + 2 * D_MODEL]]
    w_stack = jnp.stack(pieces, axis=0)
    n_rep = 6
    wf = jnp.pad(jnp.tile(wb[:, n_main:gate0], (1, n_rep)), ((0, 0), (0, LANES - n_rep * N_HEADS)))
    bfv = jnp.pad(jnp.tile(b_forget.astype(F32), n_rep), (0, LANES - n_rep * N_HEADS)).reshape(1, LANES)
    wr = jnp.concatenate(
        [w_router_group, jnp.transpose(w_router_expert, (1, 0, 2)).reshape(D_MODEL, N_EXPERTS)], axis=1)
    wr = jnp.pad(wr, ((0, 0), (0, LANES - wr.shape[1]))).astype(BF16)

    yc, q, k, v, sgc, sga, cq, ck = _in_proj(
        x2d, norm_mix_g.reshape(1, D_MODEL), w_stack, wf, bfv, conv_w,
        conv_b.reshape(1, D_MODEL), batch, seq)
    o = _attention(q, k, v, cq, ck, batch, seq)

    x1, h2, route, route_t, counts = _mixer_out(
        yc, o, sgc, sga, x2d, w_conv_out.astype(BF16), w_att_out.astype(BF16),
        w_out.astype(BF16), norm_ffn_g.reshape(1, D_MODEL), wr)

    counts = counts[0, N_GROUPS:N_GROUPS + N_EXPERTS].astype(jnp.int32)
    padded = (counts + MOE_BLOCK - 1) // MOE_BLOCK * MOE_BLOCK
    end_padded = jnp.cumsum(padded)
    start_padded = end_padded - padded
    expert = route_t[0:2].astype(jnp.int32)
    rank = route_t[4:6].astype(jnp.int32)
    expert_ids = jnp.arange(N_EXPERTS, dtype=jnp.int32)[:, None, None]
    dest = rank + jnp.sum(jnp.where(expert[None] == expert_ids, start_padded[:, None, None], 0), axis=0)
    dest_flat = dest.reshape(2 * tokens)
    n_rows = tokens * 2 + N_EXPERTS * MOE_BLOCK
    block_start = jnp.arange(n_rows // MOE_BLOCK, dtype=jnp.int32) * MOE_BLOCK
    block_e = jnp.minimum(jnp.sum(end_padded[None, :] <= block_start[:, None], axis=1),
                          N_EXPERTS - 1).astype(jnp.int32)
    used_lo = jnp.maximum(start_padded[None, :], block_start[:, None])
    used_hi = jnp.minimum((start_padded + counts)[None, :], block_start[:, None] + MOE_BLOCK)
    n_valid = jnp.sum(jnp.maximum(used_hi - used_lo, 0), axis=1).astype(jnp.int32)

    x_rows = _sc_scatter_rows(h2, dest_flat, n_rows)
    y_rows = _experts(block_e, n_valid, x_rows, w_e_gate, w_e_up, w_e_down)
    y_pairs = _sc_gather_rows(y_rows, dest_flat)
    out = _combine(y_pairs, x1, route, norm_final_g.reshape(1, D_MODEL))
    return out.reshape(batch, seq, D_MODEL)
```

```python
import jax
import jax.numpy as jnp
from jax import lax
from jax.experimental import pallas as pl
from jax.experimental.pallas import tpu as pltpu
from jax.experimental.pallas import tpu_sc as plsc

D_MODEL = 1024
HEAD_DIM = 64
N_HEADS = 16
N_GROUPS = 4
EXPERTS_PER_GROUP = 8
N_EXPERTS = N_GROUPS * EXPERTS_PER_GROUP
D_EXPERT = 512
MOE_BLOCK = 1024
CONV_WIDTH = 3
EPS = 1e-6

LANES = 128
HEADS_PER_BLOCK = LANES // HEAD_DIM
NEG_BIG = -1e30
LOG2E = 1.4426950408889634

IN_PROJ_COLS = 256
IN_PROJ_ROWS = 256
ATT_BLOCK = 256
ATT_KEYS = 256
MIX_ROWS = 512
MIX_PART = 256
COMBINE_ROWS = 1024
ROUTE_ROWS = 8
SC_WINDOW = 64
VMEM_LIMIT = 56 * 1024 * 1024

F32 = jnp.float32
BF16 = jnp.bfloat16


def _dot(a, b):
    return jnp.dot(a, b, preferred_element_type=F32)


def _pack_bf16_pairs(x):
    c = x.shape[1] // 2
    lo = lax.bitcast_convert_type(x[:, :c].astype(BF16).astype(F32), jnp.uint32) >> 16
    hi = lax.bitcast_convert_type(x[:, c:].astype(BF16).astype(F32), jnp.uint32) & jnp.uint32(0xFFFF0000)
    return lo | hi


def _unpack_bf16_pairs(w):
    lo = lax.bitcast_convert_type(w << 16, F32)
    hi = lax.bitcast_convert_type(w & jnp.uint32(0xFFFF0000), F32)
    return jnp.concatenate([lo, hi], axis=1)


def _in_proj_kernel(x_ref, g_ref, w_ref, wf_ref, bf_ref, cw_ref, cb_ref,
                    yc_ref, q_ref, k_ref, v_ref, sgc_ref, sga_ref, cq_ref, ck_ref, h_scr):
    j = pl.program_id(1)
    seq = x_ref.shape[0]
    n_chunks = seq // IN_PROJ_ROWS

    @pl.when(j == 0)
    def _():
        for r in range(n_chunks):
            rows = slice(r * IN_PROJ_ROWS, (r + 1) * IN_PROJ_ROWS)
            xs = x_ref[rows, :]
            ms = jnp.mean(xs * xs, axis=-1, keepdims=True)
            h_scr[rows, :] = (xs * lax.rsqrt(ms + EPS) * g_ref[...]).astype(BF16)
        f = _dot(h_scr[...], wf_ref[...]) + bf_ref[...]
        c = jnp.minimum(f, 0.0) - jnp.log(1.0 + jnp.exp(-jnp.abs(f)))
        row = lax.broadcasted_iota(jnp.int32, c.shape, 0)
        d = 1
        while d < seq:
            c = c + jnp.where(row >= d, pltpu.roll(c, d, axis=0), 0.0)
            d *= 2
        c = c * LOG2E
        hi = c.astype(BF16).astype(F32)
        mid = (c - hi).astype(BF16).astype(F32)
        lo = (c - hi - mid).astype(BF16).astype(F32)
        grp = lax.broadcasted_iota(jnp.int32, c.shape, 1) // N_HEADS
        terms = jnp.where(grp % 3 == 0, hi, jnp.where(grp % 3 == 1, mid, lo))
        one = jnp.float32(1.0)
        cq_ref[0] = jnp.where(grp < 3, terms, jnp.where(grp < 6, one, 0.0)).astype(BF16)
        ck_ref[0] = jnp.where(grp < 3, one, jnp.where(grp < 6, -terms, 0.0)).astype(BF16)

    cw0 = cw_ref[0:1, :]
    cw1 = cw_ref[1:2, :]
    cw2 = cw_ref[2:3, :]
    cb = cb_ref[...]
    rowc = lax.broadcasted_iota(jnp.int32, (IN_PROJ_ROWS, IN_PROJ_COLS), 0)
    zprev = None
    for r in range(n_chunks):
        rows = slice(r * IN_PROJ_ROWS, (r + 1) * IN_PROJ_ROWS)
        hs = h_scr[rows, :]
        cb_gate = _dot(hs, w_ref[0])
        z = _dot(hs, w_ref[1]) * _dot(hs, w_ref[2])
        z1 = pltpu.roll(z, 1, axis=0)
        z2 = pltpu.roll(z, 2, axis=0)
        if zprev is None:
            p1 = jnp.zeros_like(z)
            p2 = p1
        else:
            p1 = pltpu.roll(zprev, 1, axis=0)
            p2 = pltpu.roll(zprev, 2, axis=0)
        z1 = jnp.where(rowc < 1, p1, z1)
        z2 = jnp.where(rowc < 2, p2, z2)
        acc = cb + cw0 * z2 + cw1 * z1 + cw2 * z
        yc_ref[rows, :] = (cb_gate * acc).astype(BF16)
        zprev = z
        q_ref[rows, :] = (_dot(hs, w_ref[3]) * (LOG2E * HEAD_DIM ** -0.5)).astype(BF16)
        k_ref[rows, :] = _dot(hs, w_ref[4]).astype(BF16)
        v_ref[rows, :] = _dot(hs, w_ref[5]).astype(BF16)
        sgc_ref[rows, :] = jax.nn.sigmoid(_dot(hs, w_ref[6])).astype(BF16)
        sga_ref[rows, :] = jax.nn.sigmoid(_dot(hs, w_ref[7])).astype(BF16)


def _in_proj(x2d, g, w_stack, wf, bfv, conv_w, conv_b, batch, seq):
    tokens = batch * seq
    tn = IN_PROJ_COLS
    nj = D_MODEL // tn
    col_out = pl.BlockSpec((seq, tn), lambda b, j: (b, j))
    out_bf16 = jax.ShapeDtypeStruct((tokens, D_MODEL), BF16)
    return pl.pallas_call(
        _in_proj_kernel,
        grid=(batch, nj),
        in_specs=[
            pl.BlockSpec((seq, D_MODEL), lambda b, j: (b, 0)),
            pl.BlockSpec((1, D_MODEL), lambda b, j: (0, 0)),
            pl.BlockSpec((8, D_MODEL, tn), lambda b, j: (0, 0, j)),
            pl.BlockSpec((D_MODEL, LANES), lambda b, j: (0, 0)),
            pl.BlockSpec((1, LANES), lambda b, j: (0, 0)),
            pl.BlockSpec((CONV_WIDTH, tn), lambda b, j: (0, j)),
            pl.BlockSpec((1, tn), lambda b, j: (0, j)),
        ],
        out_specs=[col_out] * 6 + [pl.BlockSpec((1, seq, LANES), lambda b, j: (b, 0, 0))] * 2,
        out_shape=[out_bf16] * 6 + [jax.ShapeDtypeStruct((batch, seq, LANES), BF16)] * 2,
        scratch_shapes=[pltpu.VMEM((seq, D_MODEL), BF16)],
        compiler_params=pltpu.CompilerParams(
            dimension_semantics=("arbitrary", "arbitrary"), vmem_limit_bytes=VMEM_LIMIT),
        name="in_proj",
    )(x2d, g, w_stack, wf, bfv, conv_w, conv_b)


def _attention_kernel(q_ref, k_ref, v_ref, cq_ref, ck_ref, o_ref, va_scr):
    hp = pl.program_id(1)
    seq = q_ref.shape[0]
    blk = ATT_BLOCK
    lane = lax.broadcasted_iota(jnp.int32, (1, LANES), 1)
    zero = jnp.zeros((), BF16)

    own, base, q_mask = [], [], []
    for hh in range(HEADS_PER_BLOCK):
        own.append((lane // HEAD_DIM) == hh)
        base.append(((hh + 1) % HEADS_PER_BLOCK) * HEAD_DIM)
        gate = (lane % N_HEADS) == hp * HEADS_PER_BLOCK + hh
        q_mask.append(jnp.concatenate([own[hh], gate], axis=1))
        va_scr[hh] = jnp.where(own[hh], v_ref[...],
                               jnp.where(lane == base[hh], 1.0, 0.0).astype(BF16))

    def pv(p, keys):
        return jnp.concatenate([_dot(p[hh * blk:(hh + 1) * blk, :], va_scr[hh, keys, :])
                                for hh in range(HEADS_PER_BLOCK)], axis=0)

    for rg in range(seq // blk):
        rows = slice(rg * blk, (rg + 1) * blk)
        q_wide = jnp.concatenate([q_ref[rows, :], cq_ref[0, rows, :]], axis=1)
        qa = jnp.concatenate([jnp.where(q_mask[hh], q_wide, zero) for hh in range(HEADS_PER_BLOCK)], axis=0)
        m = acc = None
        k_end = (rg + 1) * blk
        for k0 in range(0, k_end, ATT_KEYS):
            k1 = min(k0 + ATT_KEYS, k_end)
            keys = slice(k0, k1)
            k_wide = jnp.concatenate([k_ref[keys, :], ck_ref[0, keys, :]], axis=1)
            s = lax.dot_general(qa, k_wide, (((1,), (1,)), ((), ())),
                                preferred_element_type=F32)
            if k1 > rg * blk:
                qpos = rg * blk + lax.broadcasted_iota(jnp.int32, s.shape, 0) % blk
                kpos = k0 + lax.broadcasted_iota(jnp.int32, s.shape, 1)
                s = jnp.where(kpos <= qpos, s, NEG_BIG)
            mx = jnp.max(s, axis=1, keepdims=True)
            if m is None:
                m = mx
                acc = pv(jnp.exp2(s - m).astype(BF16), keys)
            else:
                m_new = jnp.maximum(m, mx)
                acc = jnp.exp2(m - m_new) * acc + pv(jnp.exp2(s - m_new).astype(BF16), keys)
                m = m_new
        out = None
        for hh in range(HEADS_PER_BLOCK):
            a_h = acc[hh * blk:(hh + 1) * blk, :]
            res = a_h / a_h[:, base[hh]:base[hh] + 1]
            out = res if out is None else jnp.where(own[hh], res, out)
        o_ref[rows, :] = out.astype(o_ref.dtype)


def _attention(q, k, v, cq, ck, batch, seq):
    tokens = batch * seq
    n_hp = N_HEADS // HEADS_PER_BLOCK
    blk = pl.BlockSpec((seq, LANES), lambda b, hp: (b, hp))
    gate_blk = pl.BlockSpec((1, seq, LANES), lambda b, hp: (b, 0, 0))
    return pl.pallas_call(
        _attention_kernel,
        grid=(batch, n_hp),
        in_specs=[blk, blk, blk, gate_blk, gate_blk],
        out_specs=blk,
        out_shape=jax.ShapeDtypeStruct((tokens, D_MODEL), BF16),
        scratch_shapes=[pltpu.VMEM((HEADS_PER_BLOCK, seq, LANES), BF16)],
        compiler_params=pltpu.CompilerParams(
            dimension_semantics=("arbitrary", "arbitrary"), vmem_limit_bytes=VMEM_LIMIT),
        name="fox_attention",
    )(q, k, v, cq, ck)


def _mixer_out_kernel(yc_ref, o_ref, sgc_ref, sga_ref, x_ref, wco_ref, wao_ref, wo_ref,
                      g_ref, wr_ref, x1_ref, h2_ref, route_ref, route_t_ref, counts_ref,
                      carry_scr, logit_scr):
    i = pl.program_id(0)
    tm = x_ref.shape[0]

    @pl.when(i == 0)
    def _():
        carry_scr[...] = jnp.zeros_like(carry_scr)
        logit_scr[...] = jnp.zeros_like(logit_scr)

    logits = logit_scr[...]

    def route_previous_tile():
        lane = lax.broadcasted_iota(jnp.int32, (tm, LANES), 1).astype(F32)
        far = jnp.float32(4 * LANES)

        def first_lane_of_max(vals, vmax):
            return jnp.min(jnp.where(vals == vmax, lane, far), axis=1, keepdims=True)

        lg = jnp.where(lane < N_GROUPS, logits, NEG_BIG)
        gmax = jnp.max(lg, axis=1, keepdims=True)
        g_val = 1.0 / jnp.sum(jnp.exp(lg - gmax), axis=1, keepdims=True)
        g_idx = first_lane_of_max(lg, gmax)
        lo = N_GROUPS + EXPERTS_PER_GROUP * g_idx
        le = jnp.where((lane >= lo) & (lane < lo + EXPERTS_PER_GROUP), logits, NEG_BIG)
        e1max = jnp.max(le, axis=1, keepdims=True)
        e1 = first_lane_of_max(le, e1max)
        le2 = jnp.where(lane == e1, NEG_BIG, le)
        e2max = jnp.max(le2, axis=1, keepdims=True)
        e2 = first_lane_of_max(le2, e2max)
        ratio = jnp.exp(e2max - e1max)
        w1 = g_val / (1.0 + ratio)
        w2 = g_val * ratio / (1.0 + ratio)

        oh1 = lane == e1
        oh2 = lane == e2
        onehot = jnp.where(oh1 | oh2, 1.0, 0.0).astype(BF16)
        r_i = lax.broadcasted_iota(jnp.int32, (tm, tm), 0)
        c_i = lax.broadcasted_iota(jnp.int32, (tm, tm), 1)
        strict_lower = jnp.where(c_i < r_i, 1.0, 0.0).astype(BF16)
        carry = carry_scr[...]
        before = _dot(strict_lower, onehot) + carry
        rank1 = jnp.sum(jnp.where(oh1, before, 0.0), axis=1, keepdims=True)
        rank2 = jnp.sum(jnp.where(oh2, before, 0.0), axis=1, keepdims=True)
        is_tile = jnp.where(i > 0, 1.0, 0.0)
        carry = carry + is_tile * jnp.sum(onehot.astype(F32), axis=0, keepdims=True)
        carry_scr[...] = carry
        counts_ref[...] = carry

        route = jnp.where(lane == 0, e1 - N_GROUPS, 0.0)
        route = jnp.where(lane == 1, e2 - N_GROUPS, route)
        route = jnp.where(lane == 2, w1, route)
        route = jnp.where(lane == 3, w2, route)
        route = jnp.where(lane == 4, rank1, route)
        route = jnp.where(lane == 5, rank2, route)
        route_ref[...] = route
        route_t_ref[...] = route.T[0:ROUTE_ROWS, :]

    parts = [slice(c * MIX_PART, (c + 1) * MIX_PART) for c in range(tm // MIX_PART)]
    y_conv = [_dot(yc_ref[p, :], wco_ref[...]) for p in parts]
    y_att = [_dot(o_ref[p, :], wao_ref[...]) for p in parts]
    m = [(sgc_ref[p, :].astype(F32) * yc + sga_ref[p, :].astype(F32) * ya).astype(BF16)
         for p, yc, ya in zip(parts, y_conv, y_att)]
    x1 = [x_ref[p, :] + _dot(mm, wo_ref[...]) for p, mm in zip(parts, m)]
    for p, xx in zip(parts, x1):
        x1_ref[p, :] = xx
    route_previous_tile()
    h2 = [xx * lax.rsqrt(jnp.mean(xx * xx, axis=-1, keepdims=True) + EPS) * g_ref[...] for xx in x1]
    for p, hh in zip(parts, h2):
        h2_ref[p, :] = _pack_bf16_pairs(hh)
        logit_scr[p, :] = _dot(hh.astype(BF16), wr_ref[...])


def _mixer_out(yc, o, sgc, sga, x2d, wco, wao, wo, g, wr):
    tokens = x2d.shape[0]
    tm = MIX_ROWS
    n_tiles = tokens // tm

    def cur(i):
        return (jnp.minimum(i, n_tiles - 1), 0)

    def prev(i):
        return (jnp.maximum(i - 1, 0), 0)

    row_blk = pl.BlockSpec((tm, D_MODEL), cur)
    w_blk = pl.BlockSpec((D_MODEL, D_MODEL), lambda i: (0, 0))
    return pl.pallas_call(
        _mixer_out_kernel,
        grid=(n_tiles + 1,),
        in_specs=[row_blk, row_blk, row_blk, row_blk, row_blk, w_blk, w_blk, w_blk,
                  pl.BlockSpec((1, D_MODEL), lambda i: (0, 0)),
                  pl.BlockSpec((D_MODEL, LANES), lambda i: (0, 0))],
        out_specs=[row_blk,
                   pl.BlockSpec((tm, D_MODEL // 2), cur),
                   pl.BlockSpec((tm, LANES), prev),
                   pl.BlockSpec((ROUTE_ROWS, tm), lambda i: (0, jnp.maximum(i - 1, 0))),
                   pl.BlockSpec((1, LANES), lambda i: (0, 0))],
        out_shape=[jax.ShapeDtypeStruct((tokens, D_MODEL), F32),
                   jax.ShapeDtypeStruct((tokens, D_MODEL // 2), jnp.uint32),
                   jax.ShapeDtypeStruct((tokens, LANES), F32),
                   jax.ShapeDtypeStruct((ROUTE_ROWS, tokens), F32),
                   jax.ShapeDtypeStruct((1, LANES), F32)],
        scratch_shapes=[pltpu.VMEM((1, LANES), F32), pltpu.VMEM((tm, LANES), F32)],
        compiler_params=pltpu.CompilerParams(
            dimension_semantics=("arbitrary",), vmem_limit_bytes=VMEM_LIMIT),
        name="mixer_out",
    )(yc, o, sgc, sga, x2d, wco, wao, wo, g, wr)


def _sc_index_rows(indices):
    n = indices.shape[0]
    return jnp.pad(indices.reshape(n // SC_WINDOW, SC_WINDOW), ((0, 0), (0, LANES - SC_WINDOW)))


def _sc_mesh():
    return plsc.VectorSubcoreMesh(core_axis_name="c", subcore_axis_name="s")


def _sc_scatter_rows(data, indices, n_out):
    n = indices.shape[0]
    n_src, width = data.shape
    src_blocks = n_src // SC_WINDOW
    reps = n // n_src

    @pl.kernel(out_type=jax.ShapeDtypeStruct((n_out, width), data.dtype), mesh=_sc_mesh())
    def scatter_kernel(x_hbm, i_hbm, o_hbm):
        def body(x_vmem, *i_vmems):
            for i_vmem in i_vmems:
                pltpu.sync_copy(x_vmem, o_hbm.at[i_vmem.at[0, pl.ds(0, SC_WINDOW)]])

        pltpu.emit_pipeline(
            body,
            grid=(src_blocks,),
            in_specs=[pl.BlockSpec((SC_WINDOW, width), lambda i: (i, 0))]
                     + [pl.BlockSpec((1, LANES), lambda i, r=r: (i + r * src_blocks, 0)) for r in range(reps)],
            out_specs=[],
            core_axis_name=("c", "s"),
            dimension_semantics=(pltpu.PARALLEL,),
        )(x_hbm, *([i_hbm] * reps))

    return scatter_kernel(data, _sc_index_rows(indices))


def _sc_gather_rows(data, indices):
    n = indices.shape[0]
    width = data.shape[1]

    @pl.kernel(out_type=jax.ShapeDtypeStruct((n, width), data.dtype), mesh=_sc_mesh())
    def gather_kernel(x_hbm, i_hbm, o_hbm):
        def body(i_vmem, o_vmem):
            pltpu.sync_copy(x_hbm.at[i_vmem.at[0, pl.ds(0, SC_WINDOW)]], o_vmem)

        pltpu.emit_pipeline(
            body,
            grid=(n // SC_WINDOW,),
            in_specs=[pl.BlockSpec((1, LANES), lambda i: (i, 0))],
            out_specs=[pl.BlockSpec((SC_WINDOW, width), lambda i: (i, 0))],
            core_axis_name=("c", "s"),
            dimension_semantics=(pltpu.PARALLEL,),
        )(i_hbm, o_hbm)

    return gather_kernel(data, _sc_index_rows(indices))


def _expert_kernel(block_e_ref, n_valid_ref, x_ref, wg_ref, wu_ref, wd_ref, y_ref,
                   wg_scr, wu_scr, wd_scr):
    i = pl.program_id(0)
    n_valid = n_valid_ref[i]

    @pl.when((i == 0) | (block_e_ref[i] != block_e_ref[jnp.maximum(i - 1, 0)]))
    def _():
        wg_scr[...] = wg_ref[0].astype(BF16)
        wu_scr[...] = wu_ref[0].astype(BF16)
        wd_scr[...] = wd_ref[0].astype(BF16)

    @pl.when(n_valid > 0)
    def _():
        row = lax.broadcasted_iota(jnp.int32, x_ref.shape, 0)
        packed = jnp.where(row < n_valid, x_ref[...], jnp.uint32(0))
        xb = _unpack_bf16_pairs(packed).astype(BF16)
        a = _dot(xb, wg_scr[...])
        u = _dot(xb, wu_scr[...])
        hmid = (a * jax.nn.sigmoid(a) * u).astype(BF16)
        y_ref[...] = _pack_bf16_pairs(_dot(hmid, wd_scr[...]))

    @pl.when(n_valid == 0)
    def _():
        y_ref[...] = jnp.zeros_like(y_ref)


def _experts(block_e, n_valid, x_rows, wg, wu, wd):
    n_rows, half = x_rows.shape
    n_blocks = n_rows // MOE_BLOCK

    def w_map(i, be, nv):
        return (be[i], 0, 0)

    row_blk = pl.BlockSpec((MOE_BLOCK, half), lambda i, be, nv: (i, 0))
    grid_spec = pltpu.PrefetchScalarGridSpec(
        num_scalar_prefetch=2,
        grid=(n_blocks,),
        in_specs=[row_blk,
                  pl.BlockSpec((1, D_MODEL, D_EXPERT), w_map),
                  pl.BlockSpec((1, D_MODEL, D_EXPERT), w_map),
                  pl.BlockSpec((1, D_EXPERT, D_MODEL), w_map)],
        out_specs=row_blk,
        scratch_shapes=[pltpu.VMEM((D_MODEL, D_EXPERT), BF16), pltpu.VMEM((D_MODEL, D_EXPERT), BF16),
                        pltpu.VMEM((D_EXPERT, D_MODEL), BF16)],
    )
    return pl.pallas_call(
        _expert_kernel,
        grid_spec=grid_spec,
        out_shape=jax.ShapeDtypeStruct((n_rows, half), jnp.uint32),
        compiler_params=pltpu.CompilerParams(
            dimension_semantics=("arbitrary",), vmem_limit_bytes=VMEM_LIMIT),
        name="moe_experts",
    )(block_e, n_valid, x_rows, wg, wu, wd)


def _combine_kernel(ya_ref, yb_ref, x1_ref, route_ref, g_ref, out_ref):
    route = route_ref[...]
    w1 = route[:, 2:3]
    w2 = route[:, 3:4]
    x2 = x1_ref[...] + (_unpack_bf16_pairs(ya_ref[...]) * w1 + _unpack_bf16_pairs(yb_ref[...]) * w2)
    ms = jnp.mean(x2 * x2, axis=-1, keepdims=True)
    out_ref[...] = x2 * lax.rsqrt(ms + EPS) * g_ref[...]


def _combine(y_pairs, x1, route, g):
    tokens = x1.shape[0]
    tc = COMBINE_ROWS
    n_tiles = tokens // tc
    row_blk = pl.BlockSpec((tc, D_MODEL), lambda i: (i, 0))
    return pl.pallas_call(
        _combine_kernel,
        grid=(n_tiles,),
        in_specs=[pl.BlockSpec((tc, D_MODEL // 2), lambda i: (i, 0)),
                  pl.BlockSpec((tc, D_MODEL // 2), lambda i: (i + n_tiles, 0)),
                  row_blk,
                  pl.BlockSpec((tc, LANES), lambda i: (i, 0)),
                  pl.BlockSpec((1, D_MODEL), lambda i: (0, 0))],
        out_specs=row_blk,
        out_shape=jax.ShapeDtypeStruct((tokens, D_MODEL), F32),
        compiler_params=pltpu.CompilerParams(
            dimension_semantics=("arbitrary",), vmem_limit_bytes=VMEM_LIMIT),
        name="moe_combine",
    )(y_pairs, y_pairs, x1, route, g)


def kernel(x, norm_mix_g, w_in, conv_w, conv_b, b_forget, w_conv_out, w_att_out, w_out,
           norm_ffn_g, w_router_group, w_router_expert, w_e_gate, w_e_up, w_e_down,
           norm_final_g):
    batch, seq, d = x.shape
    assert d == D_MODEL and seq % max(ATT_BLOCK, IN_PROJ_ROWS) == 0
    tokens = batch * seq
    assert tokens % MIX_ROWS == 0 and tokens % COMBINE_ROWS == 0 and tokens % SC_WINDOW == 0
    x2d = x.reshape(tokens, D_MODEL)

    wb = w_in.astype(BF16)
    n_main = 6 * D_MODEL
    gate0 = n_main + N_HEADS
    pieces = [wb[:, i * D_MODEL:(i + 1) * D_MODEL] for i in range(6)]
    pieces += [wb[:, gate0:gate0 + D_MODEL], wb[:, gate0 + D_MODEL:gate0 + 2 * D_MODEL]]
    w_stack = jnp.stack(pieces, axis=0)
    n_rep = 6
    wf = jnp.pad(jnp.tile(wb[:, n_main:gate0], (1, n_rep)), ((0, 0), (0, LANES - n_rep * N_HEADS)))
    bfv = jnp.pad(jnp.tile(b_forget.astype(F32), n_rep), (0, LANES - n_rep * N_HEADS)).reshape(1, LANES)
    wr = jnp.concatenate(
        [w_router_group, jnp.transpose(w_router_expert, (1, 0, 2)).reshape(D_MODEL, N_EXPERTS)], axis=1)
    wr = jnp.pad(wr, ((0, 0), (0, LANES - wr.shape[1]))).astype(BF16)

    yc, q, k, v, sgc, sga, cq, ck = _in_proj(
        x2d, norm_mix_g.reshape(1, D_MODEL), w_stack, wf, bfv, conv_w,
        conv_b.reshape(1, D_MODEL), batch, seq)
    o = _attention(q, k, v, cq, ck, batch, seq)

    x1, h2, route, route_t, counts = _mixer_out(
        yc, o, sgc, sga, x2d, w_conv_out.astype(BF16), w_att_out.astype(BF16),
        w_out.astype(BF16), norm_ffn_g.reshape(1, D_MODEL), wr)

    counts = counts[0, N_GROUPS:N_GROUPS + N_EXPERTS].astype(jnp.int32)
    padded = (counts + MOE_BLOCK - 1) // MOE_BLOCK * MOE_BLOCK
    end_padded = jnp.cumsum(padded)
    start_padded = end_padded - padded
    expert = route_t[0:2].astype(jnp.int32)
    rank = route_t[4:6].astype(jnp.int32)
    expert_ids = jnp.arange(N_EXPERTS, dtype=jnp.int32)[:, None, None]
    dest = rank + jnp.sum(jnp.where(expert[None] == expert_ids, start_padded[:, None, None], 0), axis=0)
    dest_flat = dest.reshape(2 * tokens)
    n_rows = tokens * 2 + N_EXPERTS * MOE_BLOCK
    block_start = jnp.arange(n_rows // MOE_BLOCK, dtype=jnp.int32) * MOE_BLOCK
    block_e = jnp.minimum(jnp.sum(end_padded[None, :] <= block_start[:, None], axis=1),
                          N_EXPERTS - 1).astype(jnp.int32)
    used_lo = jnp.maximum(start_padded[None, :], block_start[:, None])
    used_hi = jnp.minimum((start_padded + counts)[None, :], block_start[:, None] + MOE_BLOCK)
    n_valid = jnp.sum(jnp.maximum(used_hi - used_lo, 0), axis=1).astype(jnp.int32)

    x_rows = _sc_scatter_rows(h2, dest_flat, n_rows)
    y_rows = _experts(block_e, n_valid, x_rows, w_e_gate, w_e_up, w_e_down)
    y_pairs = _sc_gather_rows(y_rows, dest_flat)
    out = _combine(y_pairs, x1, route, norm_final_g.reshape(1, D_MODEL))
    return out.reshape(batch, seq, D_MODEL)
```

```python
import jax
import jax.numpy as jnp
from jax import lax
from jax.experimental import pallas as pl
from jax.experimental.pallas import tpu as pltpu
from jax.experimental.pallas import tpu_sc as plsc

D_MODEL = 1024
HEAD_DIM = 64
N_HEADS = 16
N_GROUPS = 4
EXPERTS_PER_GROUP = 8
N_EXPERTS = N_GROUPS * EXPERTS_PER_GROUP
D_EXPERT = 512
MOE_BLOCK = 1024
CONV_WIDTH = 3
EPS = 1e-6

LANES = 128
HEADS_PER_BLOCK = LANES // HEAD_DIM
NEG_BIG = -1e30
LOG2E = 1.4426950408889634

IN_PROJ_COLS = 256
IN_PROJ_ROWS = 256
ATT_BLOCK = 256
ATT_KEYS = 256
MIX_ROWS = 512
MIX_PART = 256
COMBINE_ROWS = 1024
ROUTE_ROWS = 8
SC_WINDOW = 64
VMEM_LIMIT = 56 * 1024 * 1024

F32 = jnp.float32
BF16 = jnp.bfloat16


def _dot(a, b):
    return jnp.dot(a, b, preferred_element_type=F32)


def _pack_bf16_pairs(x):
    c = x.shape[1] // 2
    lo = lax.bitcast_convert_type(x[:, :c].astype(BF16).astype(F32), jnp.uint32) >> 16
    hi = lax.bitcast_convert_type(x[:, c:].astype(BF16).astype(F32), jnp.uint32) & jnp.uint32(0xFFFF0000)
    return lo | hi


def _unpack_bf16_pairs(w):
    lo = lax.bitcast_convert_type(w << 16, F32)
    hi = lax.bitcast_convert_type(w & jnp.uint32(0xFFFF0000), F32)
    return jnp.concatenate([lo, hi], axis=1)


def _in_proj_kernel(x_ref, g_ref, w_ref, wf_ref, bf_ref, cw_ref, cb_ref,
                    yc_ref, q_ref, k_ref, v_ref, sgc_ref, sga_ref, cq_ref, ck_ref, h_scr):
    j = pl.program_id(1)
    seq = x_ref.shape[0]
    n_chunks = seq // IN_PROJ_ROWS

    @pl.when(j == 0)
    def _():
        for r in range(n_chunks):
            rows = slice(r * IN_PROJ_ROWS, (r + 1) * IN_PROJ_ROWS)
            xs = x_ref[rows, :]
            ms = jnp.mean(xs * xs, axis=-1, keepdims=True)
            h_scr[rows, :] = (xs * lax.rsqrt(ms + EPS) * g_ref[...]).astype(BF16)
        f = _dot(h_scr[...], wf_ref[...]) + bf_ref[...]
        c = jnp.minimum(f, 0.0) - jnp.log(1.0 + jnp.exp(-jnp.abs(f)))
        row = lax.broadcasted_iota(jnp.int32, c.shape, 0)
        d = 1
        while d < seq:
            c = c + jnp.where(row >= d, pltpu.roll(c, d, axis=0), 0.0)
            d *= 2
        c = c * LOG2E
        hi = c.astype(BF16).astype(F32)
        mid = (c - hi).astype(BF16).astype(F32)
        lo = (c - hi - mid).astype(BF16).astype(F32)
        grp = lax.broadcasted_iota(jnp.int32, c.shape, 1) // N_HEADS
        terms = jnp.where(grp % 3 == 0, hi, jnp.where(grp % 3 == 1, mid, lo))
        one = jnp.float32(1.0)
        cq_ref[0] = jnp.where(grp < 3, terms, jnp.where(grp < 6, one, 0.0)).astype(BF16)
        ck_ref[0] = jnp.where(grp < 3, one, jnp.where(grp < 6, -terms, 0.0)).astype(BF16)

    cw0 = cw_ref[0:1, :]
    cw1 = cw_ref[1:2, :]
    cw2 = cw_ref[2:3, :]
    cb = cb_ref[...]
    rowc = lax.broadcasted_iota(jnp.int32, (IN_PROJ_ROWS, IN_PROJ_COLS), 0)
    zprev = None
    for r in range(n_chunks):
        rows = slice(r * IN_PROJ_ROWS, (r + 1) * IN_PROJ_ROWS)
        hs = h_scr[rows, :]
        cb_gate = _dot(hs, w_ref[0])
        z = _dot(hs, w_ref[1]) * _dot(hs, w_ref[2])
        z1 = pltpu.roll(z, 1, axis=0)
        z2 = pltpu.roll(z, 2, axis=0)
        if zprev is None:
            p1 = jnp.zeros_like(z)
            p2 = p1
        else:
            p1 = pltpu.roll(zprev, 1, axis=0)
            p2 = pltpu.roll(zprev, 2, axis=0)
        z1 = jnp.where(rowc < 1, p1, z1)
        z2 = jnp.where(rowc < 2, p2, z2)
        acc = cb + cw0 * z2 + cw1 * z1 + cw2 * z
        yc_ref[rows, :] = (cb_gate * acc).astype(BF16)
        zprev = z
        q_ref[rows, :] = (_dot(hs, w_ref[3]) * (LOG2E * HEAD_DIM ** -0.5)).astype(BF16)
        k_ref[rows, :] = _dot(hs, w_ref[4]).astype(BF16)
        v_ref[rows, :] = _dot(hs, w_ref[5]).astype(BF16)
        sgc_ref[rows, :] = jax.nn.sigmoid(_dot(hs, w_ref[6])).astype(BF16)
        sga_ref[rows, :] = jax.nn.sigmoid(_dot(hs, w_ref[7])).astype(BF16)


def _in_proj(x2d, g, w_stack, wf, bfv, conv_w, conv_b, batch, seq):
    tokens = batch * seq
    tn = IN_PROJ_COLS
    nj = D_MODEL // tn
    col_out = pl.BlockSpec((seq, tn), lambda b, j: (b, j))
    out_bf16 = jax.ShapeDtypeStruct((tokens, D_MODEL), BF16)
    return pl.pallas_call(
        _in_proj_kernel,
        grid=(batch, nj),
        in_specs=[
            pl.BlockSpec((seq, D_MODEL), lambda b, j: (b, 0)),
            pl.BlockSpec((1, D_MODEL), lambda b, j: (0, 0)),
            pl.BlockSpec((8, D_MODEL, tn), lambda b, j: (0, 0, j)),
            pl.BlockSpec((D_MODEL, LANES), lambda b, j: (0, 0)),
            pl.BlockSpec((1, LANES), lambda b, j: (0, 0)),
            pl.BlockSpec((CONV_WIDTH, tn), lambda b, j: (0, j)),
            pl.BlockSpec((1, tn), lambda b, j: (0, j)),
        ],
        out_specs=[col_out] * 6 + [pl.BlockSpec((1, seq, LANES), lambda b, j: (b, 0, 0))] * 2,
        out_shape=[out_bf16] * 6 + [jax.ShapeDtypeStruct((batch, seq, LANES), BF16)] * 2,
        scratch_shapes=[pltpu.VMEM((seq, D_MODEL), BF16)],
        compiler_params=pltpu.CompilerParams(
            dimension_semantics=("arbitrary", "arbitrary"), vmem_limit_bytes=VMEM_LIMIT),
        name="in_proj",
    )(x2d, g, w_stack, wf, bfv, conv_w, conv_b)


def _attention_kernel(q_ref, k_ref, v_ref, cq_ref, ck_ref, o_ref, va_scr):
    hp = pl.program_id(1)
    seq = q_ref.shape[0]
    blk = ATT_BLOCK
    lane = lax.broadcasted_iota(jnp.int32, (1, LANES), 1)
    zero = jnp.zeros((), BF16)

    own, base, q_mask = [], [], []
    for hh in range(HEADS_PER_BLOCK):
        own.append((lane // HEAD_DIM) == hh)
        base.append(((hh + 1) % HEADS_PER_BLOCK) * HEAD_DIM)
        gate = (lane % N_HEADS) == hp * HEADS_PER_BLOCK + hh
        q_mask.append(jnp.concatenate([own[hh], gate], axis=1))
        va_scr[hh] = jnp.where(own[hh], v_ref[...],
                               jnp.where(lane == base[hh], 1.0, 0.0).astype(BF16))

    def pv(p, keys):
        return jnp.concatenate([_dot(p[hh * blk:(hh + 1) * blk, :], va_scr[hh, keys, :])
                                for hh in range(HEADS_PER_BLOCK)], axis=0)

    for rg in range(seq // blk):
        rows = slice(rg * blk, (rg + 1) * blk)
        q_wide = jnp.concatenate([q_ref[rows, :], cq_ref[0, rows, :]], axis=1)
        qa = jnp.concatenate([jnp.where(q_mask[hh], q_wide, zero) for hh in range(HEADS_PER_BLOCK)], axis=0)
        m = acc = None
        k_end = (rg + 1) * blk
        for k0 in range(0, k_end, ATT_KEYS):
            k1 = min(k0 + ATT_KEYS, k_end)
            keys = slice(k0, k1)
            k_wide = jnp.concatenate([k_ref[keys, :], ck_ref[0, keys, :]], axis=1)
            s = lax.dot_general(qa, k_wide, (((1,), (1,)), ((), ())),
                                preferred_element_type=F32)
            if k1 > rg * blk:
                qpos = rg * blk + lax.broadcasted_iota(jnp.int32, s.shape, 0) % blk
                kpos = k0 + lax.broadcasted_iota(jnp.int32, s.shape, 1)
                s = jnp.where(kpos <= qpos, s, NEG_BIG)
            mx = jnp.max(s, axis=1, keepdims=True)
            if m is None:
                m = mx
                acc = pv(jnp.exp2(s - m).astype(BF16), keys)
            else:
                m_new = jnp.maximum(m, mx)
                acc = jnp.exp2(m - m_new) * acc + pv(jnp.exp2(s - m_new).astype(BF16), keys)
                m = m_new
        out = None
        for hh in range(HEADS_PER_BLOCK):
            a_h = acc[hh * blk:(hh + 1) * blk, :]
            res = a_h / a_h[:, base[hh]:base[hh] + 1]
            out = res if out is None else jnp.where(own[hh], res, out)
        o_ref[rows, :] = out.astype(o_ref.dtype)


def _attention(q, k, v, cq, ck, batch, seq):
    tokens = batch * seq
    n_hp = N_HEADS // HEADS_PER_BLOCK
    blk = pl.BlockSpec((seq, LANES), lambda b, hp: (b, hp))
    gate_blk = pl.BlockSpec((1, seq, LANES), lambda b, hp: (b, 0, 0))
    return pl.pallas_call(
        _attention_kernel,
        grid=(batch, n_hp),
        in_specs=[blk, blk, blk, gate_blk, gate_blk],
        out_specs=blk,
        out_shape=jax.ShapeDtypeStruct((tokens, D_MODEL), BF16),
        scratch_shapes=[pltpu.VMEM((HEADS_PER_BLOCK, seq, LANES), BF16)],
        compiler_params=pltpu.CompilerParams(
            dimension_semantics=("arbitrary", "arbitrary"), vmem_limit_bytes=VMEM_LIMIT),
        name="fox_attention",
    )(q, k, v, cq, ck)


def _mixer_out_kernel(yc_ref, o_ref, sgc_ref, sga_ref, x_ref, wco_ref, wao_ref, wo_ref,
                      g_ref, wr_ref, x1_ref, h2_ref, route_ref, route_t_ref, counts_ref,
                      carry_scr, logit_scr):
    i = pl.program_id(0)
    tm = x_ref.shape[0]

    @pl.when(i == 0)
    def _():
        carry_scr[...] = jnp.zeros_like(carry_scr)
        logit_scr[...] = jnp.zeros_like(logit_scr)

    logits = logit_scr[...]

    def route_previous_tile():
        lane = lax.broadcasted_iota(jnp.int32, (tm, LANES), 1).astype(F32)
        far = jnp.float32(4 * LANES)

        def first_lane_of_max(vals, vmax):
            return jnp.min(jnp.where(vals == vmax, lane, far), axis=1, keepdims=True)

        lg = jnp.where(lane < N_GROUPS, logits, NEG_BIG)
        gmax = jnp.max(lg, axis=1, keepdims=True)
        g_val = 1.0 / jnp.sum(jnp.exp(lg - gmax), axis=1, keepdims=True)
        g_idx = first_lane_of_max(lg, gmax)
        lo = N_GROUPS + EXPERTS_PER_GROUP * g_idx
        le = jnp.where((lane >= lo) & (lane < lo + EXPERTS_PER_GROUP), logits, NEG_BIG)
        e1max = jnp.max(le, axis=1, keepdims=True)
        e1 = first_lane_of_max(le, e1max)
        le2 = jnp.where(lane == e1, NEG_BIG, le)
        e2max = jnp.max(le2, axis=1, keepdims=True)
        e2 = first_lane_of_max(le2, e2max)
        ratio = jnp.exp(e2max - e1max)
        w1 = g_val / (1.0 + ratio)
        w2 = g_val * ratio / (1.0 + ratio)

        oh1 = lane == e1
        oh2 = lane == e2
        onehot = jnp.where(oh1 | oh2, 1.0, 0.0).astype(BF16)
        r_i = lax.broadcasted_iota(jnp.int32, (tm, tm), 0)
        c_i = lax.broadcasted_iota(jnp.int32, (tm, tm), 1)
        strict_lower = jnp.where(c_i < r_i, 1.0, 0.0).astype(BF16)
        carry = carry_scr[...]
        before = _dot(strict_lower, onehot) + carry
        rank1 = jnp.sum(jnp.where(oh1, before, 0.0), axis=1, keepdims=True)
        rank2 = jnp.sum(jnp.where(oh2, before, 0.0), axis=1, keepdims=True)
        is_tile = jnp.where(i > 0, 1.0, 0.0)
        carry = carry + is_tile * jnp.sum(onehot.astype(F32), axis=0, keepdims=True)
        carry_scr[...] = carry
        counts_ref[...] = carry

        route = jnp.where(lane == 0, e1 - N_GROUPS, 0.0)
        route = jnp.where(lane == 1, e2 - N_GROUPS, route)
        route = jnp.where(lane == 2, w1, route)
        route = jnp.where(lane == 3, w2, route)
        route = jnp.where(lane == 4, rank1, route)
        route = jnp.where(lane == 5, rank2, route)
        route_ref[...] = route
        route_t_ref[...] = route.T[0:ROUTE_ROWS, :]

    parts = [slice(c * MIX_PART, (c + 1) * MIX_PART) for c in range(tm // MIX_PART)]
    y_conv = [_dot(yc_ref[p, :], wco_ref[...]) for p in parts]
    y_att = [_dot(o_ref[p, :], wao_ref[...]) for p in parts]
    m = [(sgc_ref[p, :].astype(F32) * yc + sga_ref[p, :].astype(F32) * ya).astype(BF16)
         for p, yc, ya in zip(parts, y_conv, y_att)]
    x1 = [x_ref[p, :] + _dot(mm, wo_ref[...]) for p, mm in zip(parts, m)]
    for p, xx in zip(parts, x1):
        x1_ref[p, :] = xx
    route_previous_tile()
    h2 = [xx * lax.rsqrt(jnp.mean(xx * xx, axis=-1, keepdims=True) + EPS) * g_ref[...] for xx in x1]
    for p, hh in zip(parts, h2):
        h2_ref[p, :] = _pack_bf16_pairs(hh)
        logit_scr[p, :] = _dot(hh.astype(BF16), wr_ref[...])


def _mixer_out(yc, o, sgc, sga, x2d, wco, wao, wo, g, wr):
    tokens = x2d.shape[0]
    tm = MIX_ROWS
    n_tiles = tokens // tm

    def cur(i):
        return (jnp.minimum(i, n_tiles - 1), 0)

    def prev(i):
        return (jnp.maximum(i - 1, 0), 0)

    row_blk = pl.BlockSpec((tm, D_MODEL), cur)
    w_blk = pl.BlockSpec((D_MODEL, D_MODEL), lambda i: (0, 0))
    return pl.pallas_call(
        _mixer_out_kernel,
        grid=(n_tiles + 1,),
        in_specs=[row_blk, row_blk, row_blk, row_blk, row_blk, w_blk, w_blk, w_blk,
                  pl.BlockSpec((1, D_MODEL), lambda i: (0, 0)),
                  pl.BlockSpec((D_MODEL, LANES), lambda i: (0, 0))],
        out_specs=[row_blk,
                   pl.BlockSpec((tm, D_MODEL // 2), cur),
                   pl.BlockSpec((tm, LANES), prev),
                   pl.BlockSpec((ROUTE_ROWS, tm), lambda i: (0, jnp.maximum(i - 1, 0))),
                   pl.BlockSpec((1, LANES), lambda i: (0, 0))],
        out_shape=[jax.ShapeDtypeStruct((tokens, D_MODEL), F32),
                   jax.ShapeDtypeStruct((tokens, D_MODEL // 2), jnp.uint32),
                   jax.ShapeDtypeStruct((tokens, LANES), F32),
                   jax.ShapeDtypeStruct((ROUTE_ROWS, tokens), F32),
                   jax.ShapeDtypeStruct((1, LANES), F32)],
        scratch_shapes=[pltpu.VMEM((1, LANES), F32), pltpu.VMEM((tm, LANES), F32)],
        compiler_params=pltpu.CompilerParams(
            dimension_semantics=("arbitrary",), vmem_limit_bytes=VMEM_LIMIT),
        name="mixer_out",
    )(yc, o, sgc, sga, x2d, wco, wao, wo, g, wr)


def _sc_index_rows(indices):
    n = indices.shape[0]
    return jnp.pad(indices.reshape(n // SC_WINDOW, SC_WINDOW), ((0, 0), (0, LANES - SC_WINDOW)))


def _sc_mesh():
    return plsc.VectorSubcoreMesh(core_axis_name="c", subcore_axis_name="s")


def _sc_scatter_rows(data, indices, n_out):
    n = indices.shape[0]
    n_src, width = data.shape
    src_blocks = n_src // SC_WINDOW
    reps = n // n_src

    @pl.kernel(out_type=jax.ShapeDtypeStruct((n_out, width), data.dtype), mesh=_sc_mesh())
    def scatter_kernel(x_hbm, i_hbm, o_hbm):
        def body(x_vmem, *i_vmems):
            for i_vmem in i_vmems:
                pltpu.sync_copy(x_vmem, o_hbm.at[i_vmem.at[0, pl.ds(0, SC_WINDOW)]])

        pltpu.emit_pipeline(
            body,
            grid=(src_blocks,),
            in_specs=[pl.BlockSpec((SC_WINDOW, width), lambda i: (i, 0))]
                     + [pl.BlockSpec((1, LANES), lambda i, r=r: (i + r * src_blocks, 0)) for r in range(reps)],
            out_specs=[],
            core_axis_name=("c", "s"),
            dimension_semantics=(pltpu.PARALLEL,),
        )(x_hbm, *([i_hbm] * reps))

    return scatter_kernel(data, _sc_index_rows(indices))


def _sc_gather_rows(data, indices):
    n = indices.shape[0]
    width = data.shape[1]

    @pl.kernel(out_type=jax.ShapeDtypeStruct((n, width), data.dtype), mesh=_sc_mesh())
    def gather_kernel(x_hbm, i_hbm, o_hbm):
        def body(i_vmem, o_vmem):
            pltpu.sync_copy(x_hbm.at[i_vmem.at[0, pl.ds(0, SC_WINDOW)]], o_vmem)

        pltpu.emit_pipeline(
            body,
            grid=(n // SC_WINDOW,),
            in_specs=[pl.BlockSpec((1, LANES), lambda i: (i, 0))],
            out_specs=[pl.BlockSpec((SC_WINDOW, width), lambda i: (i, 0))],
            core_axis_name=("c", "s"),
            dimension_semantics=(pltpu.PARALLEL,),
        )(i_hbm, o_hbm)

    return gather_kernel(data, _sc_index_rows(indices))


def _expert_kernel(block_e_ref, n_valid_ref, n_used_ref, x_ref, wg_ref, wu_ref, wd_ref, y_ref,
                   wg_scr, wu_scr, wd_scr):
    i = pl.program_id(0)
    n_valid = n_valid_ref[i]

    @pl.when((i == 0) | (block_e_ref[i] != block_e_ref[jnp.maximum(i - 1, 0)]))
    def _():
        wg_scr[...] = wg_ref[0].astype(BF16)
        wu_scr[...] = wu_ref[0].astype(BF16)
        wd_scr[...] = wd_ref[0].astype(BF16)

    @pl.when(n_valid > 0)
    def _():
        row = lax.broadcasted_iota(jnp.int32, x_ref.shape, 0)
        packed = jnp.where(row < n_valid, x_ref[...], jnp.uint32(0))
        xb = _unpack_bf16_pairs(packed).astype(BF16)
        a = _dot(xb, wg_scr[...])
        u = _dot(xb, wu_scr[...])
        hmid = (a * jax.nn.sigmoid(a) * u).astype(BF16)
        y_ref[...] = _pack_bf16_pairs(_dot(hmid, wd_scr[...]))


def _experts(block_e, n_valid, n_used, x_rows, wg, wu, wd):
    n_rows, half = x_rows.shape
    n_blocks = n_rows // MOE_BLOCK

    def w_map(i, be, nv, nu):
        return (be[i], 0, 0)

    def row_map(i, be, nv, nu):
        return (jnp.minimum(i, jnp.maximum(nu[0] - 1, 0)), 0)

    row_blk = pl.BlockSpec((MOE_BLOCK, half), row_map)
    grid_spec = pltpu.PrefetchScalarGridSpec(
        num_scalar_prefetch=3,
        grid=(n_blocks,),
        in_specs=[row_blk,
                  pl.BlockSpec((1, D_MODEL, D_EXPERT), w_map),
                  pl.BlockSpec((1, D_MODEL, D_EXPERT), w_map),
                  pl.BlockSpec((1, D_EXPERT, D_MODEL), w_map)],
        out_specs=row_blk,
        scratch_shapes=[pltpu.VMEM((D_MODEL, D_EXPERT), BF16), pltpu.VMEM((D_MODEL, D_EXPERT), BF16),
                        pltpu.VMEM((D_EXPERT, D_MODEL), BF16)],
    )
    return pl.pallas_call(
        _expert_kernel,
        grid_spec=grid_spec,
        out_shape=jax.ShapeDtypeStruct((n_rows, half), jnp.uint32),
        compiler_params=pltpu.CompilerParams(
            dimension_semantics=("arbitrary",), vmem_limit_bytes=VMEM_LIMIT),
        name="moe_experts",
    )(block_e, n_valid, n_used, x_rows, wg, wu, wd)


def _combine_kernel(ya_ref, yb_ref, x1_ref, route_ref, g_ref, out_ref):
    route = route_ref[...]
    w1 = route[:, 2:3]
    w2 = route[:, 3:4]
    x2 = x1_ref[...] + (_unpack_bf16_pairs(ya_ref[...]) * w1 + _unpack_bf16_pairs(yb_ref[...]) * w2)
    ms = jnp.mean(x2 * x2, axis=-1, keepdims=True)
    out_ref[...] = x2 * lax.rsqrt(ms + EPS) * g_ref[...]


def _combine(y_pairs, x1, route, g):
    tokens = x1.shape[0]
    tc = COMBINE_ROWS
    n_tiles = tokens // tc
    row_blk = pl.BlockSpec((tc, D_MODEL), lambda i: (i, 0))
    return pl.pallas_call(
        _combine_kernel,
        grid=(n_tiles,),
        in_specs=[pl.BlockSpec((tc, D_MODEL // 2), lambda i: (i, 0)),
                  pl.BlockSpec((tc, D_MODEL // 2), lambda i: (i + n_tiles, 0)),
                  row_blk,
                  pl.BlockSpec((tc, LANES), lambda i: (i, 0)),
                  pl.BlockSpec((1, D_MODEL), lambda i: (0, 0))],
        out_specs=row_blk,
        out_shape=jax.ShapeDtypeStruct((tokens, D_MODEL), F32),
        compiler_params=pltpu.CompilerParams(
            dimension_semantics=("arbitrary",), vmem_limit_bytes=VMEM_LIMIT),
        name="moe_combine",
    )(y_pairs, y_pairs, x1, route, g)


def kernel(x, norm_mix_g, w_in, conv_w, conv_b, b_forget, w_conv_out, w_att_out, w_out,
           norm_ffn_g, w_router_group, w_router_expert, w_e_gate, w_e_up, w_e_down,
           norm_final_g):
    batch, seq, d = x.shape
    assert d == D_MODEL and seq % max(ATT_BLOCK, IN_PROJ_ROWS) == 0
    tokens = batch * seq
    assert tokens % MIX_ROWS == 0 and tokens % COMBINE_ROWS == 0 and tokens % SC_WINDOW == 0
    x2d = x.reshape(tokens, D_MODEL)

    wb = w_in.astype(BF16)
    n_main = 6 * D_MODEL
    gate0 = n_main + N_HEADS
    pieces = [wb[:, i * D_MODEL:(i + 1) * D_MODEL] for i in range(6)]
    pieces += [wb[:, gate0:gate0 + D_MODEL], wb[:, gate0 + D_MODEL:gate0 + 2 * D_MODEL]]
    w_stack = jnp.stack(pieces, axis=0)
    n_rep = 6
    wf = jnp.pad(jnp.tile(wb[:, n_main:gate0], (1, n_rep)), ((0, 0), (0, LANES - n_rep * N_HEADS)))
    bfv = jnp.pad(jnp.tile(b_forget.astype(F32), n_rep), (0, LANES - n_rep * N_HEADS)).reshape(1, LANES)
    wr = jnp.concatenate(
        [w_router_group, jnp.transpose(w_router_expert, (1, 0, 2)).reshape(D_MODEL, N_EXPERTS)], axis=1)
    wr = jnp.pad(wr, ((0, 0), (0, LANES - wr.shape[1]))).astype(BF16)

    yc, q, k, v, sgc, sga, cq, ck = _in_proj(
        x2d, norm_mix_g.reshape(1, D_MODEL), w_stack, wf, bfv, conv_w,
        conv_b.reshape(1, D_MODEL), batch, seq)
    o = _attention(q, k, v, cq, ck, batch, seq)

    x1, h2, route, route_t, counts = _mixer_out(
        yc, o, sgc, sga, x2d, w_conv_out.astype(BF16), w_att_out.astype(BF16),
        w_out.astype(BF16), norm_ffn_g.reshape(1, D_MODEL), wr)

    counts = counts[0, N_GROUPS:N_GROUPS + N_EXPERTS].astype(jnp.int32)
    padded = (counts + MOE_BLOCK - 1) // MOE_BLOCK * MOE_BLOCK
    end_padded = jnp.cumsum(padded)
    start_padded = end_padded - padded
    expert = route_t[0:2].astype(jnp.int32)
    rank = route_t[4:6].astype(jnp.int32)
    expert_ids = jnp.arange(N_EXPERTS, dtype=jnp.int32)[:, None, None]
    dest = rank + jnp.sum(jnp.where(expert[None] == expert_ids, start_padded[:, None, None], 0), axis=0)
    dest_flat = dest.reshape(2 * tokens)
    n_rows = tokens * 2 + N_EXPERTS * MOE_BLOCK
    block_start = jnp.arange(n_rows // MOE_BLOCK, dtype=jnp.int32) * MOE_BLOCK
    block_e = jnp.minimum(jnp.sum(end_padded[None, :] <= block_start[:, None], axis=1),
                          N_EXPERTS - 1).astype(jnp.int32)
    used_lo = jnp.maximum(start_padded[None, :], block_start[:, None])
    used_hi = jnp.minimum((start_padded + counts)[None, :], block_start[:, None] + MOE_BLOCK)
    n_valid = jnp.sum(jnp.maximum(used_hi - used_lo, 0), axis=1).astype(jnp.int32)

    x_rows = _sc_scatter_rows(h2, dest_flat, n_rows)
    n_used = (end_padded[-1:] // MOE_BLOCK).astype(jnp.int32)
    y_rows = _experts(block_e, n_valid, n_used, x_rows, w_e_gate, w_e_up, w_e_down)
    y_pairs = _sc_gather_rows(y_rows, dest_flat)
    out = _combine(y_pairs, x1, route, norm_final_g.reshape(1, D_MODEL))
    return out.reshape(batch, seq, D_MODEL)
```

```python
import jax
import jax.numpy as jnp
from jax import lax
from jax.experimental import pallas as pl
from jax.experimental.pallas import tpu as pltpu
from jax.experimental.pallas import tpu_sc as plsc

D_MODEL = 1024
HEAD_DIM = 64
N_HEADS = 16
N_GROUPS = 4
EXPERTS_PER_GROUP = 8
N_EXPERTS = N_GROUPS * EXPERTS_PER_GROUP
D_EXPERT = 512
MOE_BLOCK = 1024
CONV_WIDTH = 3
EPS = 1e-6

LANES = 128
HEADS_PER_BLOCK = LANES // HEAD_DIM
NEG_BIG = -1e30
LOG2E = 1.4426950408889634

IN_PROJ_COLS = 256
IN_PROJ_ROWS = 256
ATT_BLOCK = 256
ATT_KEYS = 256
MIX_ROWS = 512
MIX_PART = 256
COMBINE_ROWS = 1024
ROUTE_ROWS = 8
SC_WINDOW = 64
VMEM_LIMIT = 56 * 1024 * 1024

F32 = jnp.float32
BF16 = jnp.bfloat16


def _dot(a, b):
    return jnp.dot(a, b, preferred_element_type=F32)


def _pack_bf16_pairs(x):
    c = x.shape[1] // 2
    lo = lax.bitcast_convert_type(x[:, :c].astype(BF16).astype(F32), jnp.uint32) >> 16
    hi = lax.bitcast_convert_type(x[:, c:].astype(BF16).astype(F32), jnp.uint32) & jnp.uint32(0xFFFF0000)
    return lo | hi


def _unpack_bf16_pairs(w):
    lo = lax.bitcast_convert_type(w << 16, F32)
    hi = lax.bitcast_convert_type(w & jnp.uint32(0xFFFF0000), F32)
    return jnp.concatenate([lo, hi], axis=1)


def _in_proj_kernel(x_ref, g_ref, w_ref, wf_ref, bf_ref, cw_ref, cb_ref,
                    yc_ref, q_ref, k_ref, v_ref, sgc_ref, sga_ref, cq_ref, ck_ref, h_scr):
    j = pl.program_id(1)
    seq = x_ref.shape[0]
    n_chunks = seq // IN_PROJ_ROWS

    @pl.when(j == 0)
    def _():
        for r in range(n_chunks):
            rows = slice(r * IN_PROJ_ROWS, (r + 1) * IN_PROJ_ROWS)
            xs = x_ref[rows, :]
            ms = jnp.mean(xs * xs, axis=-1, keepdims=True)
            h_scr[rows, :] = (xs * lax.rsqrt(ms + EPS) * g_ref[...]).astype(BF16)
        f = _dot(h_scr[...], wf_ref[...]) + bf_ref[...]
        c = jnp.minimum(f, 0.0) - jnp.log(1.0 + jnp.exp(-jnp.abs(f)))
        row = lax.broadcasted_iota(jnp.int32, c.shape, 0)
        d = 1
        while d < seq:
            c = c + jnp.where(row >= d, pltpu.roll(c, d, axis=0), 0.0)
            d *= 2
        c = c * LOG2E
        hi = c.astype(BF16).astype(F32)
        mid = (c - hi).astype(BF16).astype(F32)
        lo = (c - hi - mid).astype(BF16).astype(F32)
        grp = lax.broadcasted_iota(jnp.int32, c.shape, 1) // N_HEADS
        terms = jnp.where(grp % 3 == 0, hi, jnp.where(grp % 3 == 1, mid, lo))
        one = jnp.float32(1.0)
        cq_ref[0] = jnp.where(grp < 3, terms, jnp.where(grp < 6, one, 0.0)).astype(BF16)
        ck_ref[0] = jnp.where(grp < 3, one, jnp.where(grp < 6, -terms, 0.0)).astype(BF16)

    cw0 = cw_ref[0:1, :]
    cw1 = cw_ref[1:2, :]
    cw2 = cw_ref[2:3, :]
    cb = cb_ref[...]
    rowc = lax.broadcasted_iota(jnp.int32, (IN_PROJ_ROWS, IN_PROJ_COLS), 0)
    zprev = None
    for r in range(n_chunks):
        rows = slice(r * IN_PROJ_ROWS, (r + 1) * IN_PROJ_ROWS)
        hs = h_scr[rows, :]
        cb_gate = _dot(hs, w_ref[0])
        z = _dot(hs, w_ref[1]) * _dot(hs, w_ref[2])
        z1 = pltpu.roll(z, 1, axis=0)
        z2 = pltpu.roll(z, 2, axis=0)
        if zprev is None:
            p1 = jnp.zeros_like(z)
            p2 = p1
        else:
            p1 = pltpu.roll(zprev, 1, axis=0)
            p2 = pltpu.roll(zprev, 2, axis=0)
        z1 = jnp.where(rowc < 1, p1, z1)
        z2 = jnp.where(rowc < 2, p2, z2)
        acc = cb + cw0 * z2 + cw1 * z1 + cw2 * z
        yc_ref[rows, :] = (cb_gate * acc).astype(BF16)
        zprev = z
        q_ref[rows, :] = (_dot(hs, w_ref[3]) * (LOG2E * HEAD_DIM ** -0.5)).astype(BF16)
        k_ref[rows, :] = _dot(hs, w_ref[4]).astype(BF16)
        v_ref[rows, :] = _dot(hs, w_ref[5]).astype(BF16)
        sgc_ref[rows, :] = jax.nn.sigmoid(_dot(hs, w_ref[6])).astype(BF16)
        sga_ref[rows, :] = jax.nn.sigmoid(_dot(hs, w_ref[7])).astype(BF16)


def _in_proj(x2d, g, w_stack, wf, bfv, conv_w, conv_b, batch, seq):
    tokens = batch * seq
    tn = IN_PROJ_COLS
    nj = D_MODEL // tn
    col_out = pl.BlockSpec((seq, tn), lambda b, j: (b, j))
    out_bf16 = jax.ShapeDtypeStruct((tokens, D_MODEL), BF16)
    return pl.pallas_call(
        _in_proj_kernel,
        grid=(batch, nj),
        in_specs=[
            pl.BlockSpec((seq, D_MODEL), lambda b, j: (b, 0)),
            pl.BlockSpec((1, D_MODEL), lambda b, j: (0, 0)),
            pl.BlockSpec((8, D_MODEL, tn), lambda b, j: (0, 0, j)),
            pl.BlockSpec((D_MODEL, LANES), lambda b, j: (0, 0)),
            pl.BlockSpec((1, LANES), lambda b, j: (0, 0)),
            pl.BlockSpec((CONV_WIDTH, tn), lambda b, j: (0, j)),
            pl.BlockSpec((1, tn), lambda b, j: (0, j)),
        ],
        out_specs=[col_out] * 6 + [pl.BlockSpec((1, seq, LANES), lambda b, j: (b, 0, 0))] * 2,
        out_shape=[out_bf16] * 6 + [jax.ShapeDtypeStruct((batch, seq, LANES), BF16)] * 2,
        scratch_shapes=[pltpu.VMEM((seq, D_MODEL), BF16)],
        compiler_params=pltpu.CompilerParams(
            dimension_semantics=("arbitrary", "arbitrary"), vmem_limit_bytes=VMEM_LIMIT),
        name="in_proj",
    )(x2d, g, w_stack, wf, bfv, conv_w, conv_b)


def _attention_kernel(q_ref, k_ref, v_ref, cq_ref, ck_ref, o_ref, va_scr):
    hp = pl.program_id(1)
    seq = q_ref.shape[0]
    blk = ATT_BLOCK
    lane = lax.broadcasted_iota(jnp.int32, (1, LANES), 1)
    zero = jnp.zeros((), BF16)

    own, base, q_mask = [], [], []
    for hh in range(HEADS_PER_BLOCK):
        own.append((lane // HEAD_DIM) == hh)
        base.append(((hh + 1) % HEADS_PER_BLOCK) * HEAD_DIM)
        gate = (lane % N_HEADS) == hp * HEADS_PER_BLOCK + hh
        q_mask.append(jnp.concatenate([own[hh], gate], axis=1))
        va_scr[hh] = jnp.where(own[hh], v_ref[...],
                               jnp.where(lane == base[hh], 1.0, 0.0).astype(BF16))

    def pv(p, keys):
        return jnp.concatenate([_dot(p[hh * blk:(hh + 1) * blk, :], va_scr[hh, keys, :])
                                for hh in range(HEADS_PER_BLOCK)], axis=0)

    for rg in range(seq // blk):
        rows = slice(rg * blk, (rg + 1) * blk)
        q_wide = jnp.concatenate([q_ref[rows, :], cq_ref[0, rows, :]], axis=1)
        qa = jnp.concatenate([jnp.where(q_mask[hh], q_wide, zero) for hh in range(HEADS_PER_BLOCK)], axis=0)
        m = acc = None
        k_end = (rg + 1) * blk
        for k0 in range(0, k_end, ATT_KEYS):
            k1 = min(k0 + ATT_KEYS, k_end)
            keys = slice(k0, k1)
            k_wide = jnp.concatenate([k_ref[keys, :], ck_ref[0, keys, :]], axis=1)
            s = lax.dot_general(qa, k_wide, (((1,), (1,)), ((), ())),
                                preferred_element_type=F32)
            if k1 > rg * blk:
                qpos = rg * blk + lax.broadcasted_iota(jnp.int32, s.shape, 0) % blk
                kpos = k0 + lax.broadcasted_iota(jnp.int32, s.shape, 1)
                s = jnp.where(kpos <= qpos, s, NEG_BIG)
            mx = jnp.max(s, axis=1, keepdims=True)
            if m is None:
                m = mx
                acc = pv(jnp.exp2(s - m).astype(BF16), keys)
            else:
                m_new = jnp.maximum(m, mx)
                acc = jnp.exp2(m - m_new) * acc + pv(jnp.exp2(s - m_new).astype(BF16), keys)
                m = m_new
        out = None
        for hh in range(HEADS_PER_BLOCK):
            a_h = acc[hh * blk:(hh + 1) * blk, :]
            res = a_h / a_h[:, base[hh]:base[hh] + 1]
            out = res if out is None else jnp.where(own[hh], res, out)
        o_ref[rows, :] = out.astype(o_ref.dtype)


def _attention(q, k, v, cq, ck, batch, seq):
    tokens = batch * seq
    n_hp = N_HEADS // HEADS_PER_BLOCK
    blk = pl.BlockSpec((seq, LANES), lambda b, hp: (b, hp))
    gate_blk = pl.BlockSpec((1, seq, LANES), lambda b, hp: (b, 0, 0))
    return pl.pallas_call(
        _attention_kernel,
        grid=(batch, n_hp),
        in_specs=[blk, blk, blk, gate_blk, gate_blk],
        out_specs=blk,
        out_shape=jax.ShapeDtypeStruct((tokens, D_MODEL), BF16),
        scratch_shapes=[pltpu.VMEM((HEADS_PER_BLOCK, seq, LANES), BF16)],
        compiler_params=pltpu.CompilerParams(
            dimension_semantics=("arbitrary", "arbitrary"), vmem_limit_bytes=VMEM_LIMIT),
        name="fox_attention",
    )(q, k, v, cq, ck)


def _mixer_out_kernel(yc_ref, o_ref, sgc_ref, sga_ref, x_ref, wco_ref, wao_ref, wo_ref,
                      g_ref, wr_ref, x1_ref, h2_ref, route_ref, route_t_ref, counts_ref,
                      carry_scr, logit_scr):
    i = pl.program_id(0)
    tm = x_ref.shape[0]

    logits = jnp.where(i > 0, logit_scr[...], 0.0)

    def route_previous_tile():
        lane = lax.broadcasted_iota(jnp.int32, (tm, LANES), 1).astype(F32)
        far = jnp.float32(4 * LANES)

        def first_lane_of_max(vals, vmax):
            return jnp.min(jnp.where(vals == vmax, lane, far), axis=1, keepdims=True)

        lg = jnp.where(lane < N_GROUPS, logits, NEG_BIG)
        gmax = jnp.max(lg, axis=1, keepdims=True)
        g_val = 1.0 / jnp.sum(jnp.exp(lg - gmax), axis=1, keepdims=True)
        g_idx = first_lane_of_max(lg, gmax)
        lo = N_GROUPS + EXPERTS_PER_GROUP * g_idx
        le = jnp.where((lane >= lo) & (lane < lo + EXPERTS_PER_GROUP), logits, NEG_BIG)
        e1max = jnp.max(le, axis=1, keepdims=True)
        e1 = first_lane_of_max(le, e1max)
        le2 = jnp.where(lane == e1, NEG_BIG, le)
        e2max = jnp.max(le2, axis=1, keepdims=True)
        e2 = first_lane_of_max(le2, e2max)
        ratio = jnp.exp(e2max - e1max)
        w1 = g_val / (1.0 + ratio)
        w2 = g_val * ratio / (1.0 + ratio)

        oh1 = lane == e1
        oh2 = lane == e2
        onehot = jnp.where(oh1 | oh2, 1.0, 0.0).astype(BF16)
        r_i = lax.broadcasted_iota(jnp.int32, (tm, tm), 0)
        c_i = lax.broadcasted_iota(jnp.int32, (tm, tm), 1)
        strict_lower = jnp.where(c_i < r_i, 1.0, 0.0).astype(BF16)
        carry = jnp.where(i > 0, carry_scr[...], 0.0)
        before = _dot(strict_lower, onehot) + carry
        rank1 = jnp.sum(jnp.where(oh1, before, 0.0), axis=1, keepdims=True)
        rank2 = jnp.sum(jnp.where(oh2, before, 0.0), axis=1, keepdims=True)
        is_tile = jnp.where(i > 0, 1.0, 0.0)
        carry = carry + is_tile * jnp.sum(onehot.astype(F32), axis=0, keepdims=True)
        carry_scr[...] = carry
        counts_ref[...] = carry

        route = jnp.where(lane == 0, e1 - N_GROUPS, 0.0)
        route = jnp.where(lane == 1, e2 - N_GROUPS, route)
        route = jnp.where(lane == 2, w1, route)
        route = jnp.where(lane == 3, w2, route)
        route = jnp.where(lane == 4, rank1, route)
        route = jnp.where(lane == 5, rank2, route)
        route_ref[...] = route
        route_t_ref[...] = route.T[0:ROUTE_ROWS, :]

    parts = [slice(c * MIX_PART, (c + 1) * MIX_PART) for c in range(tm // MIX_PART)]
    y_conv = [_dot(yc_ref[p, :], wco_ref[...]) for p in parts]
    y_att = [_dot(o_ref[p, :], wao_ref[...]) for p in parts]
    m = [(sgc_ref[p, :].astype(F32) * yc + sga_ref[p, :].astype(F32) * ya).astype(BF16)
         for p, yc, ya in zip(parts, y_conv, y_att)]
    x1 = [x_ref[p, :] + _dot(mm, wo_ref[...]) for p, mm in zip(parts, m)]
    for p, xx in zip(parts, x1):
        x1_ref[p, :] = xx
    route_previous_tile()
    h2 = [xx * lax.rsqrt(jnp.mean(xx * xx, axis=-1, keepdims=True) + EPS) * g_ref[...] for xx in x1]
    for p, hh in zip(parts, h2):
        h2_ref[p, :] = _pack_bf16_pairs(hh)
        logit_scr[p, :] = _dot(hh.astype(BF16), wr_ref[...])


def _mixer_out(yc, o, sgc, sga, x2d, wco, wao, wo, g, wr):
    tokens = x2d.shape[0]
    tm = MIX_ROWS
    n_tiles = tokens // tm

    def cur(i):
        return (jnp.minimum(i, n_tiles - 1), 0)

    def prev(i):
        return (jnp.maximum(i - 1, 0), 0)

    row_blk = pl.BlockSpec((tm, D_MODEL), cur)
    w_blk = pl.BlockSpec((D_MODEL, D_MODEL), lambda i: (0, 0))
    return pl.pallas_call(
        _mixer_out_kernel,
        grid=(n_tiles + 1,),
        in_specs=[row_blk, row_blk, row_blk, row_blk, row_blk, w_blk, w_blk, w_blk,
                  pl.BlockSpec((1, D_MODEL), lambda i: (0, 0)),
                  pl.BlockSpec((D_MODEL, LANES), lambda i: (0, 0))],
        out_specs=[row_blk,
                   pl.BlockSpec((tm, D_MODEL // 2), cur),
                   pl.BlockSpec((tm, LANES), prev),
                   pl.BlockSpec((ROUTE_ROWS, tm), lambda i: (0, jnp.maximum(i - 1, 0))),
                   pl.BlockSpec((1, LANES), lambda i: (0, 0))],
        out_shape=[jax.ShapeDtypeStruct((tokens, D_MODEL), F32),
                   jax.ShapeDtypeStruct((tokens, D_MODEL // 2), jnp.uint32),
                   jax.ShapeDtypeStruct((tokens, LANES), F32),
                   jax.ShapeDtypeStruct((ROUTE_ROWS, tokens), F32),
                   jax.ShapeDtypeStruct((1, LANES), F32)],
        scratch_shapes=[pltpu.VMEM((1, LANES), F32), pltpu.VMEM((tm, LANES), F32)],
        compiler_params=pltpu.CompilerParams(
            dimension_semantics=("arbitrary",), vmem_limit_bytes=VMEM_LIMIT),
        name="mixer_out",
    )(yc, o, sgc, sga, x2d, wco, wao, wo, g, wr)


def _sc_index_rows(indices):
    n = indices.shape[0]
    return jnp.pad(indices.reshape(n // SC_WINDOW, SC_WINDOW), ((0, 0), (0, LANES - SC_WINDOW)))


def _sc_mesh():
    return plsc.VectorSubcoreMesh(core_axis_name="c", subcore_axis_name="s")


def _sc_scatter_rows(data, indices, n_out):
    n = indices.shape[0]
    n_src, width = data.shape
    src_blocks = n_src // SC_WINDOW
    reps = n // n_src

    @pl.kernel(out_type=jax.ShapeDtypeStruct((n_out, width), data.dtype), mesh=_sc_mesh())
    def scatter_kernel(x_hbm, i_hbm, o_hbm):
        def body(x_vmem, *i_vmems):
            for i_vmem in i_vmems:
                pltpu.sync_copy(x_vmem, o_hbm.at[i_vmem.at[0, pl.ds(0, SC_WINDOW)]])

        pltpu.emit_pipeline(
            body,
            grid=(src_blocks,),
            in_specs=[pl.BlockSpec((SC_WINDOW, width), lambda i: (i, 0))]
                     + [pl.BlockSpec((1, LANES), lambda i, r=r: (i + r * src_blocks, 0)) for r in range(reps)],
            out_specs=[],
            core_axis_name=("c", "s"),
            dimension_semantics=(pltpu.PARALLEL,),
        )(x_hbm, *([i_hbm] * reps))

    return scatter_kernel(data, _sc_index_rows(indices))


def _sc_gather_rows(data, indices):
    n = indices.shape[0]
    width = data.shape[1]

    @pl.kernel(out_type=jax.ShapeDtypeStruct((n, width), data.dtype), mesh=_sc_mesh())
    def gather_kernel(x_hbm, i_hbm, o_hbm):
        def body(i_vmem, o_vmem):
            pltpu.sync_copy(x_hbm.at[i_vmem.at[0, pl.ds(0, SC_WINDOW)]], o_vmem)

        pltpu.emit_pipeline(
            body,
            grid=(n // SC_WINDOW,),
            in_specs=[pl.BlockSpec((1, LANES), lambda i: (i, 0))],
            out_specs=[pl.BlockSpec((SC_WINDOW, width), lambda i: (i, 0))],
            core_axis_name=("c", "s"),
            dimension_semantics=(pltpu.PARALLEL,),
        )(i_hbm, o_hbm)

    return gather_kernel(data, _sc_index_rows(indices))


def _expert_kernel(block_e_ref, n_valid_ref, n_used_ref, x_ref, wg_ref, wu_ref, wd_ref, y_ref,
                   wg_scr, wu_scr, wd_scr):
    i = pl.program_id(0)
    n_valid = n_valid_ref[i]

    @pl.when((i == 0) | (block_e_ref[i] != block_e_ref[jnp.maximum(i - 1, 0)]))
    def _():
        wg_scr[...] = wg_ref[0].astype(BF16)
        wu_scr[...] = wu_ref[0].astype(BF16)
        wd_scr[...] = wd_ref[0].astype(BF16)

    @pl.when(n_valid > 0)
    def _():
        row = lax.broadcasted_iota(jnp.int32, x_ref.shape, 0)
        packed = jnp.where(row < n_valid, x_ref[...], jnp.uint32(0))
        xb = _unpack_bf16_pairs(packed).astype(BF16)
        a = _dot(xb, wg_scr[...])
        u = _dot(xb, wu_scr[...])
        hmid = (a * jax.nn.sigmoid(a) * u).astype(BF16)
        y_ref[...] = _pack_bf16_pairs(_dot(hmid, wd_scr[...]))


def _experts(block_e, n_valid, n_used, x_rows, wg, wu, wd):
    n_rows, half = x_rows.shape
    n_blocks = n_rows // MOE_BLOCK

    def w_map(i, be, nv, nu):
        return (be[i], 0, 0)

    def row_map(i, be, nv, nu):
        return (jnp.minimum(i, jnp.maximum(nu[0] - 1, 0)), 0)

    row_blk = pl.BlockSpec((MOE_BLOCK, half), row_map)
    grid_spec = pltpu.PrefetchScalarGridSpec(
        num_scalar_prefetch=3,
        grid=(n_blocks,),
        in_specs=[row_blk,
                  pl.BlockSpec((1, D_MODEL, D_EXPERT), w_map),
                  pl.BlockSpec((1, D_MODEL, D_EXPERT), w_map),
                  pl.BlockSpec((1, D_EXPERT, D_MODEL), w_map)],
        out_specs=row_blk,
        scratch_shapes=[pltpu.VMEM((D_MODEL, D_EXPERT), BF16), pltpu.VMEM((D_MODEL, D_EXPERT), BF16),
                        pltpu.VMEM((D_EXPERT, D_MODEL), BF16)],
    )
    return pl.pallas_call(
        _expert_kernel,
        grid_spec=grid_spec,
        out_shape=jax.ShapeDtypeStruct((n_rows, half), jnp.uint32),
        compiler_params=pltpu.CompilerParams(
            dimension_semantics=("arbitrary",), vmem_limit_bytes=VMEM_LIMIT),
        name="moe_experts",
    )(block_e, n_valid, n_used, x_rows, wg, wu, wd)


def _combine_kernel(ya_ref, yb_ref, x1_ref, route_ref, g_ref, out_ref):
    route = route_ref[...]
    w1 = route[:, 2:3]
    w2 = route[:, 3:4]
    x2 = x1_ref[...] + (_unpack_bf16_pairs(ya_ref[...]) * w1 + _unpack_bf16_pairs(yb_ref[...]) * w2)
    ms = jnp.mean(x2 * x2, axis=-1, keepdims=True)
    out_ref[...] = x2 * lax.rsqrt(ms + EPS) * g_ref[...]


def _combine(y_pairs, x1, route, g):
    tokens = x1.shape[0]
    tc = COMBINE_ROWS
    n_tiles = tokens // tc
    row_blk = pl.BlockSpec((tc, D_MODEL), lambda i: (i, 0))
    return pl.pallas_call(
        _combine_kernel,
        grid=(n_tiles,),
        in_specs=[pl.BlockSpec((tc, D_MODEL // 2), lambda i: (i, 0)),
                  pl.BlockSpec((tc, D_MODEL // 2), lambda i: (i + n_tiles, 0)),
                  row_blk,
                  pl.BlockSpec((tc, LANES), lambda i: (i, 0)),
                  pl.BlockSpec((1, D_MODEL), lambda i: (0, 0))],
        out_specs=row_blk,
        out_shape=jax.ShapeDtypeStruct((tokens, D_MODEL), F32),
        compiler_params=pltpu.CompilerParams(
            dimension_semantics=("arbitrary",), vmem_limit_bytes=VMEM_LIMIT),
        name="moe_combine",
    )(y_pairs, y_pairs, x1, route, g)


def kernel(x, norm_mix_g, w_in, conv_w, conv_b, b_forget, w_conv_out, w_att_out, w_out,
           norm_ffn_g, w_router_group, w_router_expert, w_e_gate, w_e_up, w_e_down,
           norm_final_g):
    batch, seq, d = x.shape
    assert d == D_MODEL and seq % max(ATT_BLOCK, IN_PROJ_ROWS) == 0
    tokens = batch * seq
    assert tokens % MIX_ROWS == 0 and tokens % COMBINE_ROWS == 0 and tokens % SC_WINDOW == 0
    x2d = x.reshape(tokens, D_MODEL)

    wb = w_in.astype(BF16)
    n_main = 6 * D_MODEL
    gate0 = n_main + N_HEADS
    pieces = [wb[:, i * D_MODEL:(i + 1) * D_MODEL] for i in range(6)]
    pieces += [wb[:, gate0:gate0 + D_MODEL], wb[:, gate0 + D_MODEL:gate0 + 2 * D_MODEL]]
    w_stack = jnp.stack(pieces, axis=0)
    n_rep = 6
    wf = jnp.pad(jnp.tile(wb[:, n_main:gate0], (1, n_rep)), ((0, 0), (0, LANES - n_rep * N_HEADS)))
    bfv = jnp.pad(jnp.tile(b_forget.astype(F32), n_rep), (0, LANES - n_rep * N_HEADS)).reshape(1, LANES)
    wr = jnp.concatenate(
        [w_router_group, jnp.transpose(w_router_expert, (1, 0, 2)).reshape(D_MODEL, N_EXPERTS)], axis=1)
    wr = jnp.pad(wr, ((0, 0), (0, LANES - wr.shape[1]))).astype(BF16)

    yc, q, k, v, sgc, sga, cq, ck = _in_proj(
        x2d, norm_mix_g.reshape(1, D_MODEL), w_stack, wf, bfv, conv_w,
        conv_b.reshape(1, D_MODEL), batch, seq)
    o = _attention(q, k, v, cq, ck, batch, seq)

    x1, h2, route, route_t, counts = _mixer_out(
        yc, o, sgc, sga, x2d, w_conv_out.astype(BF16), w_att_out.astype(BF16),
        w_out.astype(BF16), norm_ffn_g.reshape(1, D_MODEL), wr)

    counts = counts[0, N_GROUPS:N_GROUPS + N_EXPERTS].astype(jnp.int32)
    padded = (counts + MOE_BLOCK - 1) // MOE_BLOCK * MOE_BLOCK
    end_padded = jnp.cumsum(padded)
    start_padded = end_padded - padded
    expert = route_t[0:2].astype(jnp.int32)
    rank = route_t[4:6].astype(jnp.int32)
    expert_ids = jnp.arange(N_EXPERTS, dtype=jnp.int32)[:, None, None]
    dest = rank + jnp.sum(jnp.where(expert[None] == expert_ids, start_padded[:, None, None], 0), axis=0)
    dest_flat = dest.reshape(2 * tokens)
    n_rows = tokens * 2 + N_EXPERTS * MOE_BLOCK
    block_start = jnp.arange(n_rows // MOE_BLOCK, dtype=jnp.int32) * MOE_BLOCK
    block_e = jnp.minimum(jnp.sum(end_padded[None, :] <= block_start[:, None], axis=1),
                          N_EXPERTS - 1).astype(jnp.int32)
    used_lo = jnp.maximum(start_padded[None, :], block_start[:, None])
    used_hi = jnp.minimum((start_padded + counts)[None, :], block_start[:, None] + MOE_BLOCK)
    n_valid = jnp.sum(jnp.maximum(used_hi - used_lo, 0), axis=1).astype(jnp.int32)

    x_rows = _sc_scatter_rows(h2, dest_flat, n_rows)
    n_used = (end_padded[-1:] // MOE_BLOCK).astype(jnp.int32)
    y_rows = _experts(block_e, n_valid, n_used, x_rows, w_e_gate, w_e_up, w_e_down)
    y_pairs = _sc_gather_rows(y_rows, dest_flat)
    out = _combine(y_pairs, x1, route, norm_final_g.reshape(1, D_MODEL))
    return out.reshape(batch, seq, D_MODEL)
```
